```python
import math
import jax
import jax.numpy as jnp
from jax import lax
import numpy as np

D_MODEL = 2048
BATCH = 4
SEQ = 4096
DEPTH = 1

HEAD_DIM = 128
N_FOX_HEADS = 8
N_MOBA_HEADS = 8
FOX_WIDTH = N_FOX_HEADS * HEAD_DIM
MOBA_WIDTH = N_MOBA_HEADS * HEAD_DIM
MIX_WIDTH = FOX_WIDTH + MOBA_WIDTH
FOX_Q_BLOCK = 128
MOBA_BLOCK = 256
MOBA_TOPK = 3
MOBA_Q_CHUNK = 32
REL_BUCKETS = 32
REL_MAX_DIST = 128
N_GROUPS = 4
EXPERTS_PER_GROUP = 4
N_EXPERTS = N_GROUPS * EXPERTS_PER_GROUP
EXPERT_TOPK = 2
D_EXPERT = 512
EPS = 1e-6
NEG = -1e30

kernel_name = 'hymba_fox_moba_hmoe_layer'


def rmsnorm(x, g):
    xf = x.astype(jnp.float32)
    y = xf * lax.rsqrt(jnp.mean(xf * xf, axis=-1, keepdims=True) + EPS)
    return (y * g.astype(jnp.float32)).astype(x.dtype)


def rel_bucket(dist):
    max_exact = REL_BUCKETS // 2
    d = jnp.maximum(dist, 0)
    log_ratio = jnp.log(jnp.maximum(d, 1).astype(jnp.float32) / max_exact) / math.log(REL_MAX_DIST / max_exact)
    large = max_exact + (log_ratio * (REL_BUCKETS - max_exact)).astype(jnp.int32)
    large = jnp.minimum(large, REL_BUCKETS - 1)
    return jnp.where(d < max_exact, d, large)


def fox_attention(q, k, v, log_f):
    B, H, S, dh = q.shape
    c = jnp.cumsum(log_f, axis=-1)
    key_pos = jnp.arange(S)
    scale = dh ** -0.5

    def block(qb):
        start = qb * FOX_Q_BLOCK
        q_blk = lax.dynamic_slice_in_dim(q, start, FOX_Q_BLOCK, axis=2)
        c_blk = lax.dynamic_slice_in_dim(c, start, FOX_Q_BLOCK, axis=2)
        q_pos = start + jnp.arange(FOX_Q_BLOCK)
        s = jnp.einsum('bhqd,bhkd->bhqk', q_blk, k).astype(jnp.float32) * scale
        s = s + (c_blk[..., :, None] - c[..., None, :])
        s = jnp.where(key_pos[None, :] <= q_pos[:, None], s, NEG)
        p = jax.nn.softmax(s, axis=-1)
        return jnp.einsum('bhqk,bhkd->bhqd', p.astype(v.dtype), v)

    out = lax.map(block, jnp.arange(S // FOX_Q_BLOCK))
    return jnp.moveaxis(out, 0, 2).reshape(B, H, S, dh)


def moba_attention(q, k, v, rel_table):
    B, H, S, dh = q.shape
    nb = -(-S // MOBA_BLOCK)
    pad = nb * MOBA_BLOCK - S
    k_blocks = jnp.pad(k, ((0, 0), (0, 0), (0, pad), (0, 0))).reshape(B, H, nb, MOBA_BLOCK, dh)
    v_blocks = jnp.pad(v, ((0, 0), (0, 0), (0, pad), (0, 0))).reshape(B, H, nb, MOBA_BLOCK, dh)
    k_mean = jnp.mean(k_blocks, axis=3)
    n_sel = min(MOBA_TOPK, nb)
    scale = dh ** -0.5
    blk_off = jnp.arange(MOBA_BLOCK)
    b_ix = jnp.arange(B)[:, None, None, None]
    h_ix = jnp.arange(H)[None, :, None, None]

    def chunk(ci):
        start = ci * MOBA_Q_CHUNK
        own = start // MOBA_BLOCK
        q_c = lax.dynamic_slice_in_dim(q, start, MOBA_Q_CHUNK, axis=2)
        q_pos = start + jnp.arange(MOBA_Q_CHUNK)
        gate = jnp.einsum('bhqd,bhnd->bhqn', q_c, k_mean).astype(jnp.float32)
        gate = jnp.where(jnp.arange(nb) < own, gate, NEG)
        _, idx = lax.top_k(gate, n_sel)
        valid = idx < own
        k_sel = k_blocks[b_ix, h_ix, idx]
        v_sel = v_blocks[b_ix, h_ix, idx]
        k_own = lax.dynamic_index_in_dim(k_blocks, own, axis=2, keepdims=False)
        v_own = lax.dynamic_index_in_dim(v_blocks, own, axis=2, keepdims=False)
        pos_sel = idx[..., None] * MOBA_BLOCK + blk_off
        pos_own = own * MOBA_BLOCK + blk_off
        s_sel = jnp.einsum('bhqd,bhqkld->bhqkl', q_c, k_sel).astype(jnp.float32) * scale
        s_sel = s_sel + rel_table[h_ix[..., None], rel_bucket(q_pos[:, None, None] - pos_sel)].astype(jnp.float32)
        s_sel = jnp.where(valid[..., None], s_sel, NEG)
        s_own = jnp.einsum('bhqd,bhld->bhql', q_c, k_own).astype(jnp.float32) * scale
        s_own = s_own + rel_table[:, rel_bucket(q_pos[:, None] - pos_own[None, :])][None].astype(jnp.float32)
        s_own = jnp.where(pos_own[None, :] <= q_pos[:, None], s_own, NEG)
        s_all = jnp.concatenate([s_sel.reshape(B, H, MOBA_Q_CHUNK, n_sel * MOBA_BLOCK), s_own], axis=-1)
        p = jax.nn.softmax(s_all, axis=-1).astype(v.dtype)
        p_sel = p[..., :n_sel * MOBA_BLOCK].reshape(B, H, MOBA_Q_CHUNK, n_sel, MOBA_BLOCK)
        p_own = p[..., n_sel * MOBA_BLOCK:]
        return (jnp.einsum('bhqkl,bhqkld->bhqd', p_sel, v_sel)
                + jnp.einsum('bhql,bhld->bhqd', p_own, v_own))

    out = lax.map(chunk, jnp.arange(S // MOBA_Q_CHUNK))
    return jnp.moveaxis(out, 0, 2).reshape(B, H, S, dh)


def hier_moe(h, w_group_router, b_group_router, w_expert_router, b_expert_router, w_gate, w_up, w_down):
    B, S, D = h.shape
    t = h.reshape(B * S, D)
    g_logits = (t @ w_group_router).astype(jnp.float32) + b_group_router.astype(jnp.float32)
    p_group = jax.nn.softmax(g_logits, axis=-1)
    group = jnp.argmax(g_logits, axis=-1)
    p_top_group = jnp.take_along_axis(p_group, group[:, None], axis=-1)
    e_logits = jnp.einsum('td,dge->tge', t, w_expert_router).astype(jnp.float32) + b_expert_router.astype(jnp.float32)
    e_logits = jnp.take_along_axis(e_logits, group[:, None, None], axis=1)[:, 0]
    p_exp = jax.nn.softmax(e_logits, axis=-1)
    top_p, top_i = lax.top_k(p_exp, EXPERT_TOPK)
    top_p = top_p / jnp.sum(top_p, axis=-1, keepdims=True)
    within = jnp.sum(jax.nn.one_hot(top_i, EXPERTS_PER_GROUP, dtype=jnp.float32) * top_p[..., None], axis=1)
    combine = jax.nn.one_hot(group, N_GROUPS, dtype=jnp.float32)[:, :, None] * within[:, None, :] * p_top_group[:, :, None]
    combine = combine.reshape(-1, N_EXPERTS).astype(h.dtype)
    hid = jax.nn.silu(jnp.einsum('td,edf->tef', t, w_gate)) * jnp.einsum('td,edf->tef', t, w_up)
    y = jnp.einsum('tef,efd->td', hid * combine[:, :, None], w_down)
    return y.reshape(B, S, D)


def setup_inputs(seed: int = 0) -> dict:
    key = jax.random.key(seed)
    ks = jax.random.split(key, 20)
    f32 = jnp.float32
    L = DEPTH
    nrm = jax.random.normal
    x = nrm(ks[0], (BATCH, SEQ, D_MODEL), f32)
    attn_norm_g = 1.0 + 0.02 * nrm(ks[1], (L, D_MODEL), f32)
    w_qkv = nrm(ks[2], (L, D_MODEL, 3 * MIX_WIDTH), f32) * D_MODEL ** -0.5
    w_fg = nrm(ks[3], (L, D_MODEL, N_FOX_HEADS), f32) * (0.1 * D_MODEL ** -0.5)
    w_in = jnp.concatenate([w_qkv, w_fg], axis=-1)
    b_forget = 4.0 + 0.5 * nrm(ks[4], (L, N_FOX_HEADS), f32)
    fox_out_norm_g = 1.0 + 0.02 * nrm(ks[5], (L, FOX_WIDTH), f32)
    moba_out_norm_g = 1.0 + 0.02 * nrm(ks[6], (L, MOBA_WIDTH), f32)
    rel_bias = 0.5 * nrm(ks[7], (N_MOBA_HEADS, REL_BUCKETS), f32)
    w_out = nrm(ks[8], (L, MIX_WIDTH, D_MODEL), f32) * MIX_WIDTH ** -0.5
    ffn_norm_g = 1.0 + 0.02 * nrm(ks[9], (L, D_MODEL), f32)
    w_group_router = nrm(ks[10], (L, D_MODEL, N_GROUPS), f32) * D_MODEL ** -0.5
    b_group_router = 0.01 * nrm(ks[11], (L, N_GROUPS), f32)
    w_expert_router = nrm(ks[12], (L, D_MODEL, N_GROUPS, EXPERTS_PER_GROUP), f32) * D_MODEL ** -0.5
    b_expert_router = 0.01 * nrm(ks[13], (L, N_GROUPS, EXPERTS_PER_GROUP), f32)
    w_gate = nrm(ks[14], (L, N_EXPERTS, D_MODEL, D_EXPERT), f32) * D_MODEL ** -0.5
    w_up = nrm(ks[15], (L, N_EXPERTS, D_MODEL, D_EXPERT), f32) * D_MODEL ** -0.5
    w_down = nrm(ks[16], (L, N_EXPERTS, D_EXPERT, D_MODEL), f32) * D_EXPERT ** -0.5
    final_norm_g = 1.0 + 0.02 * nrm(ks[17], (D_MODEL,), f32)
    return {'x': x, 'attn_norm_g': attn_norm_g, 'w_in': w_in, 'b_forget': b_forget,
            'fox_out_norm_g': fox_out_norm_g, 'moba_out_norm_g': moba_out_norm_g, 'rel_bias': rel_bias,
            'w_out': w_out, 'ffn_norm_g': ffn_norm_g, 'w_group_router': w_group_router,
            'b_group_router': b_group_router, 'w_expert_router': w_expert_router,
            'b_expert_router': b_expert_router, 'w_gate': w_gate, 'w_up': w_up, 'w_down': w_down,
            'final_norm_g': final_norm_g}


def reference(x, attn_norm_g, w_in, b_forget, fox_out_norm_g, moba_out_norm_g, rel_bias, w_out, ffn_norm_g,
              w_group_router, b_group_router, w_expert_router, b_expert_router, w_gate, w_up, w_down,
              final_norm_g):
    B, S, _ = x.shape

    def to_heads(t, n):
        return t.reshape(B, S, n, HEAD_DIM).transpose(0, 2, 1, 3)

    def from_heads(t):
        return t.transpose(0, 2, 1, 3).reshape(B, S, -1)

    for l in range(DEPTH):
        h = rmsnorm(x, attn_norm_g[l])
        proj = h @ w_in[l]
        fox_qkv = proj[..., :3 * FOX_WIDTH]
        moba_qkv = proj[..., 3 * FOX_WIDTH:3 * MIX_WIDTH]
        f_logit = proj[..., 3 * MIX_WIDTH:]
        fq, fk, fv = [to_heads(t, N_FOX_HEADS) for t in jnp.split(fox_qkv, 3, axis=-1)]
        mq, mk, mv = [to_heads(t, N_MOBA_HEADS) for t in jnp.split(moba_qkv, 3, axis=-1)]
        log_f = jax.nn.log_sigmoid(f_logit.astype(jnp.float32) + b_forget[l].astype(jnp.float32)).transpose(0, 2, 1)
        fo = rmsnorm(from_heads(fox_attention(fq, fk, fv, log_f)), fox_out_norm_g[l])
        mo = rmsnorm(from_heads(moba_attention(mq, mk, mv, rel_bias)), moba_out_norm_g[l])
        x = x + jnp.concatenate([fo, mo], axis=-1) @ w_out[l]
        x = x + hier_moe(rmsnorm(x, ffn_norm_g[l]), w_group_router[l], b_group_router[l], w_expert_router[l],
                         b_expert_router[l], w_gate[l], w_up[l], w_down[l])
    return rmsnorm(x, final_norm_g)
```

```python
import functools
import math

import numpy as np
import jax
import jax.numpy as jnp
from jax import lax
from jax.experimental import pallas as pl
from jax.experimental.pallas import tpu as pltpu

HEAD_DIM = 128
N_FOX_HEADS = 8
N_MOBA_HEADS = 8
MOBA_BLOCK = 256
MOBA_TOPK = 3
REL_BUCKETS = 32
REL_MAX_DIST = 128
N_GROUPS = 4
EXPERTS_PER_GROUP = 4
N_EXPERTS = N_GROUPS * EXPERTS_PER_GROUP
EPS = 1e-6
NEG = -1e30
LANES = 128
ROUTER_COLS = LANES
VMEM_LIMIT = 56 * 1024 * 1024

f32 = jnp.float32
bf16 = jnp.bfloat16


def _params(*sem):
    return pltpu.CompilerParams(dimension_semantics=sem, vmem_limit_bytes=VMEM_LIMIT)


def _pick(n, pref):
    t = min(n, pref)
    assert n % t == 0, (n, pref)
    return t


def _in_proj_kernel(x_ref, g_ref, w_ref, wfg_ref, qkv_ref, f_ref, h_scr):
    @pl.when(pl.program_id(1) == 0)
    def _():
        x = x_ref[...]
        y = x * lax.rsqrt(jnp.mean(x * x, axis=-1, keepdims=True) + EPS)
        hb = (y * g_ref[...]).astype(bf16)
        h_scr[...] = hb
        f_ref[...] = lax.dot_general(wfg_ref[...], hb, (((1,), (1,)), ((), ())),
                                     preferred_element_type=f32)

    acc = jnp.dot(h_scr[...], w_ref[...], preferred_element_type=f32)
    for c in range(acc.shape[1] // HEAD_DIM):
        qkv_ref[c] = acc[:, c * HEAD_DIM:(c + 1) * HEAD_DIM].astype(bf16)


def _in_proj(x2, g, w_qkv, w_fg_t):
    T, D = x2.shape
    N = w_qkv.shape[1]
    tm = _pick(T, 1024)
    tn = _pick(N, 1024)
    nh = w_fg_t.shape[0]
    return pl.pallas_call(
        _in_proj_kernel,
        grid=(T // tm, N // tn),
        in_specs=[
            pl.BlockSpec((tm, D), lambda i, j: (i, 0)),
            pl.BlockSpec((1, D), lambda i, j: (0, 0)),
            pl.BlockSpec((D, tn), lambda i, j: (0, j)),
            pl.BlockSpec((nh, D), lambda i, j: (0, 0)),
        ],
        out_specs=[
            pl.BlockSpec((tn // HEAD_DIM, tm, HEAD_DIM), lambda i, j: (j, i, 0)),
            pl.BlockSpec((nh, tm), lambda i, j: (0, i)),
        ],
        out_shape=[
            jax.ShapeDtypeStruct((N // HEAD_DIM, T, HEAD_DIM), bf16),
            jax.ShapeDtypeStruct((nh, T), f32),
        ],
        scratch_shapes=[pltpu.VMEM((tm, D), bf16)],
        compiler_params=_params("arbitrary", "arbitrary"),
        name="in_proj",
    )(x2, g, w_qkv, w_fg_t)


def _fox_decay_kernel(f_ref, b_ref, o_ref):
    lf = jax.nn.log_sigmoid(f_ref[...] + b_ref[...])
    S = lf.shape[1]
    lane = lax.broadcasted_iota(jnp.int32, lf.shape, 1)
    c = lf
    sh = 1
    while sh < S:
        c = c + jnp.where(lane >= sh, pltpu.roll(c, sh, axis=1), 0.0)
        sh *= 2
    o_ref[0] = -c


def _fox_decay(f_t, b_forget, B, S):
    nh = f_t.shape[0]
    return pl.pallas_call(
        _fox_decay_kernel,
        grid=(B,),
        in_specs=[
            pl.BlockSpec((nh, S), lambda b: (0, b)),
            pl.BlockSpec((nh, 1), lambda b: (0, 0)),
        ],
        out_specs=pl.BlockSpec((1, nh, S), lambda b: (b, 0, 0)),
        out_shape=jax.ShapeDtypeStruct((B, nh, S), f32),
        compiler_params=_params("arbitrary"),
        name="fox_decay",
    )(f_t, b_forget)


def _softmax_step(carry, s, v):
    m, l, acc = carry
    m_new = jnp.maximum(m, jnp.max(s, axis=-1, keepdims=True))
    alpha = jnp.exp(m - m_new)
    p = jnp.exp(s - m_new)
    l = alpha * l + jnp.sum(p, axis=-1, keepdims=True)
    acc = alpha * acc + jnp.dot(p.astype(bf16), v, preferred_element_type=f32)
    return m_new, l, acc


def _qk(q, k):
    return lax.dot_general(q, k, (((1,), (1,)), ((), ())), preferred_element_type=f32)


def _fox_attn_kernel(q_ref, k_ref, v_ref, nc_ref, o_ref, *, t, scale):
    qi = pl.program_id(2)
    q = q_ref[0]

    def scores(kb):
        off = pl.multiple_of(kb * t, t)
        s = _qk(q, k_ref[0, pl.ds(off, t), :]) * scale + nc_ref[0, kb]
        return s, v_ref[0, pl.ds(off, t), :]

    def body(kb, carry):
        s, v = scores(kb)
        return _softmax_step(carry, s, v)

    init = (jnp.full((t, 1), NEG, f32), jnp.zeros((t, 1), f32), jnp.zeros((t, HEAD_DIM), f32))
    carry = lax.fori_loop(0, qi, body, init)
    s, v = scores(qi)
    row = lax.broadcasted_iota(jnp.int32, (t, t), 0)
    col = lax.broadcasted_iota(jnp.int32, (t, t), 1)
    s = jnp.where(col <= row, s, NEG)
    _, l, acc = _softmax_step(carry, s, v)
    o_ref[...] = (acc / l).astype(o_ref.dtype)


def _fox_attn(qkv, negc, B, S, slab0):
    T = B * S
    H = N_FOX_HEADS
    t = _pick(S, 512)
    nq = S // t
    negc4 = negc.reshape(B * H, nq, 1, t)
    kern = functools.partial(_fox_attn_kernel, t=t, scale=HEAD_DIM ** -0.5)
    return pl.pallas_call(
        kern,
        grid=(B, H, nq),
        in_specs=[
            pl.BlockSpec((1, t, HEAD_DIM), lambda b, h, i: (slab0 + h, b * nq + i, 0)),
            pl.BlockSpec((1, S, HEAD_DIM), lambda b, h, i: (slab0 + H + h, b, 0)),
            pl.BlockSpec((1, S, HEAD_DIM), lambda b, h, i: (slab0 + 2 * H + h, b, 0)),
            pl.BlockSpec((1, nq, 1, t), lambda b, h, i: (b * H + h, 0, 0, 0)),
        ],
        out_specs=pl.BlockSpec((t, HEAD_DIM), lambda b, h, i: (b * nq + i, h)),
        out_shape=jax.ShapeDtypeStruct((T, H * HEAD_DIM), bf16),
        compiler_params=_params("arbitrary", "arbitrary", "arbitrary"),
        name="fox_attn",
    )(qkv, qkv, qkv, negc4)


def _rel_bucket_table(n):
    max_exact = REL_BUCKETS // 2
    d = np.arange(n)
    ratio = np.log(np.maximum(d, 1).astype(np.float32) / np.float32(max_exact)) / np.float32(
        math.log(REL_MAX_DIST / max_exact))
    large = max_exact + (ratio * np.float32(REL_BUCKETS - max_exact)).astype(np.int32)
    large = np.minimum(large, REL_BUCKETS - 1)
    return np.where(d < max_exact, d, large).astype(np.int32)


def _moba_attn_kernel(rel_ref, q_ref, k_ref, v_ref, avg_ref, bk_ref, o_ref, bias_scr, km_scr,
                      *, scale):
    h = pl.program_id(1)
    own = pl.program_id(2)
    blk = MOBA_BLOCK
    far_bias = rel_ref[h, REL_BUCKETS - 1]

    @pl.when(own == 0)
    def _():
        for which in range(2):
            bk = bk_ref[which]
            tile = jnp.full((blk, blk), far_bias, f32)
            for b in range(REL_BUCKETS - 1):
                tile = jnp.where(bk == b, rel_ref[h, b], tile)
            bias_scr[which] = tile
        km_scr[...] = jnp.dot(avg_ref[...], k_ref[0], preferred_element_type=f32).astype(bf16)

    q = q_ref[0]
    lane = lax.broadcasted_iota(jnp.int32, (blk, LANES), 1)
    valid = lane < own
    gate = jnp.where(valid, _qk(q, km_scr[...]), NEG)
    sel = jnp.zeros((blk, LANES), jnp.bool_)
    for _ in range(MOBA_TOPK):
        mx = jnp.max(gate, axis=-1, keepdims=True)
        first = jnp.min(jnp.where(gate == mx, lane, LANES), axis=-1, keepdims=True)
        pick = lane == first
        sel = jnp.logical_or(sel, jnp.logical_and(pick, valid))
        gate = jnp.where(pick, -jnp.inf, gate)
    sel_bias = jnp.where(sel, 0.0, NEG)

    def kv(n):
        off = pl.multiple_of(n * blk, blk)
        return k_ref[0, pl.ds(off, blk), :], v_ref[0, pl.ds(off, blk), :]

    k, v = kv(own)
    row = lax.broadcasted_iota(jnp.int32, (blk, blk), 0)
    col = lax.broadcasted_iota(jnp.int32, (blk, blk), 1)
    s = jnp.where(col <= row, _qk(q, k) * scale + bias_scr[0], NEG)
    init = (jnp.full((blk, 1), NEG, f32), jnp.zeros((blk, 1), f32), jnp.zeros((blk, HEAD_DIM), f32))
    carry = _softmax_step(init, s, v)

    def body(n, carry):
        k, v = kv(n)
        chosen = jnp.sum(jnp.where(lane == n, sel_bias, 0.0), axis=-1, keepdims=True)
        bias = jnp.where(n == own - 1, bias_scr[1], far_bias)
        s = _qk(q, k) * scale + bias + chosen
        return _softmax_step(carry, s, v)

    _, l, acc = lax.fori_loop(0, own, body, carry)
    o_ref[...] = (acc / l).astype(o_ref.dtype)


def _moba_attn(qkv, rel_bias, B, S, slab0):
    T = B * S
    H = N_MOBA_HEADS
    blk = MOBA_BLOCK
    assert S % blk == 0 and S // blk <= LANES
    nb = S // blk
    avg = np.zeros((LANES, S), np.float32)
    for n in range(nb):
        avg[n, n * blk:(n + 1) * blk] = 1.0 / blk
    table = _rel_bucket_table(2 * blk)
    dist = np.arange(blk)[:, None] - np.arange(blk)[None, :]
    buckets = np.stack([table[np.maximum(dist, 0)], table[dist + blk]]).astype(np.int32)
    kern = functools.partial(_moba_attn_kernel, scale=HEAD_DIM ** -0.5)
    grid_spec = pltpu.PrefetchScalarGridSpec(
        num_scalar_prefetch=1,
        grid=(B, H, nb),
        in_specs=[
            pl.BlockSpec((1, blk, HEAD_DIM), lambda b, h, i, r: (slab0 + h, b * nb + i, 0)),
            pl.BlockSpec((1, S, HEAD_DIM), lambda b, h, i, r: (slab0 + H + h, b, 0)),
            pl.BlockSpec((1, S, HEAD_DIM), lambda b, h, i, r: (slab0 + 2 * H + h, b, 0)),
            pl.BlockSpec((LANES, S), lambda b, h, i, r: (0, 0)),
            pl.BlockSpec((2, blk, blk), lambda b, h, i, r: (0, 0, 0)),
        ],
        out_specs=pl.BlockSpec((blk, HEAD_DIM), lambda b, h, i, r: (b * nb + i, h)),
        scratch_shapes=[pltpu.VMEM((2, blk, blk), f32), pltpu.VMEM((LANES, HEAD_DIM), bf16)],
    )
    return pl.pallas_call(
        kern,
        grid_spec=grid_spec,
        out_shape=jax.ShapeDtypeStruct((T, H * HEAD_DIM), bf16),
        compiler_params=_params("arbitrary", "arbitrary", "arbitrary"),
        name="moba_attn",
    )(rel_bias, qkv, qkv, qkv, jnp.asarray(avg, bf16), jnp.asarray(buckets))


def _rms(x, g):
    return x * lax.rsqrt(jnp.mean(x * x, axis=-1, keepdims=True) + EPS) * g


def _route(logits):
    lane = lax.broadcasted_iota(jnp.int32, logits.shape, 1)

    def first_max(x):
        mx = jnp.max(x, axis=-1, keepdims=True)
        return mx, jnp.min(jnp.where(x == mx, lane, LANES), axis=-1, keepdims=True)

    gl = jnp.where(lane < N_GROUPS, logits, -jnp.inf)
    gmax, grp = first_max(gl)
    p_top_group = 1.0 / jnp.sum(jnp.exp(gl - gmax), axis=-1, keepdims=True)
    lo = N_GROUPS + grp * EXPERTS_PER_GROUP
    in_grp = jnp.logical_and(lane >= lo, lane < lo + EXPERTS_PER_GROUP)
    el = jnp.where(in_grp, logits, -jnp.inf)
    emax = jnp.max(el, axis=-1, keepdims=True)
    pe = jnp.exp(el - emax)
    p_exp = pe / jnp.sum(pe, axis=-1, keepdims=True)
    p_exp = jnp.where(in_grp, p_exp, -1.0)
    p1, i1 = first_max(p_exp)
    p2, i2 = first_max(jnp.where(lane == i1, -1.0, p_exp))
    tot = p1 + p2
    within = jnp.where(lane == i1, p1 / tot, jnp.where(lane == i2, p2 / tot, 0.0))
    return within * p_top_group


def _out_proj_kernel(fo_ref, mo_ref, x_ref, gf_ref, gm_ref, w_ref, gffn_ref, wr_ref, br_ref,
                     x1_ref, h2_ref, comb_ref):
    fo = _rms(fo_ref[...].astype(f32), gf_ref[...]).astype(bf16)
    mo = _rms(mo_ref[...].astype(f32), gm_ref[...]).astype(bf16)
    mix = jnp.concatenate([fo, mo], axis=-1)
    x1 = x_ref[...] + jnp.dot(mix, w_ref[...], preferred_element_type=f32)
    x1_ref[...] = x1
    h2 = _rms(x1, gffn_ref[...]).astype(bf16)
    h2_ref[...] = h2
    logits = jnp.dot(h2, wr_ref[...], preferred_element_type=f32) + br_ref[...]
    comb_ref[...] = _route(logits)


def _out_proj(fo, mo, x2, gf, gm, w_out, g_ffn, w_r, b_r):
    T, D = x2.shape
    Wf = fo.shape[1]
    Wm = mo.shape[1]
    tm = _pick(T, 512)
    const = lambda i: (0, 0)
    rows = lambda i: (i, 0)
    return pl.pallas_call(
        _out_proj_kernel,
        grid=(T // tm,),
        in_specs=[
            pl.BlockSpec((tm, Wf), rows),
            pl.BlockSpec((tm, Wm), rows),
            pl.BlockSpec((tm, D), rows),
            pl.BlockSpec((1, Wf), const),
            pl.BlockSpec((1, Wm), const),
            pl.BlockSpec((Wf + Wm, D), const),
            pl.BlockSpec((1, D), const),
            pl.BlockSpec((D, ROUTER_COLS), const),
            pl.BlockSpec((1, ROUTER_COLS), const),
        ],
        out_specs=[
            pl.BlockSpec((tm, D), rows),
            pl.BlockSpec((tm, D), rows),
            pl.BlockSpec((tm, ROUTER_COLS), rows),
        ],
        out_shape=[
            jax.ShapeDtypeStruct((T, D), f32),
            jax.ShapeDtypeStruct((T, D), bf16),
            jax.ShapeDtypeStruct((T, ROUTER_COLS), f32),
        ],
        compiler_params=_params("arbitrary"),
        name="out_proj",
    )(fo, mo, x2, gf, gm, w_out, g_ffn, w_r, b_r)


def _moe_kernel(h_ref, comb_ref, x1_ref, wg_ref, wu_ref, wd_ref, gfin_ref, o_ref, acc_scr):
    e = pl.program_id(1)

    @pl.when(e == 0)
    def _():
        acc_scr[...] = jnp.zeros_like(acc_scr)

    h = h_ref[...]
    lane = lax.broadcasted_iota(jnp.int32, comb_ref.shape, 1)
    c = jnp.sum(jnp.where(lane == N_GROUPS + e, comb_ref[...], 0.0), axis=-1, keepdims=True)
    gate = jnp.dot(h, wg_ref[0], preferred_element_type=f32)
    up = jnp.dot(h, wu_ref[0], preferred_element_type=f32)
    hid = (jax.nn.silu(gate) * up * c).astype(bf16)
    acc_scr[...] += jnp.dot(hid, wd_ref[0], preferred_element_type=f32)

    @pl.when(e == pl.num_programs(1) - 1)
    def _():
        o_ref[...] = _rms(x1_ref[...] + acc_scr[...], gfin_ref[...])


def _moe(h2, comb, x1, w_gate, w_up, w_down, g_final):
    T, D = x1.shape
    E, _, F = w_gate.shape
    tm = _pick(T, 512)
    rows = lambda i, e: (i, 0)
    return pl.pallas_call(
        _moe_kernel,
        grid=(T // tm, E),
        in_specs=[
            pl.BlockSpec((tm, D), rows),
            pl.BlockSpec((tm, ROUTER_COLS), rows),
            pl.BlockSpec((tm, D), rows),
            pl.BlockSpec((1, D, F), lambda i, e: (e, 0, 0)),
            pl.BlockSpec((1, D, F), lambda i, e: (e, 0, 0)),
            pl.BlockSpec((1, F, D), lambda i, e: (e, 0, 0)),
            pl.BlockSpec((1, D), lambda i, e: (0, 0)),
        ],
        out_specs=pl.BlockSpec((tm, D), rows),
        out_shape=jax.ShapeDtypeStruct((T, D), f32),
        scratch_shapes=[pltpu.VMEM((tm, D), f32)],
        compiler_params=_params("arbitrary", "arbitrary"),
        name="moe",
    )(h2, comb, x1, w_gate, w_up, w_down, g_final)


def kernel(x, attn_norm_g, w_in, b_forget, fox_out_norm_g, moba_out_norm_g, rel_bias, w_out, ffn_norm_g,
           w_group_router, b_group_router, w_expert_router, b_expert_router, w_gate, w_up, w_down,
           final_norm_g):
    B, S, D = x.shape
    T = B * S
    depth = w_in.shape[0]
    fox_w = N_FOX_HEADS * HEAD_DIM
    moba_w = N_MOBA_HEADS * HEAD_DIM
    qkv_w = 3 * (fox_w + moba_w)
    assert w_in.shape[2] == qkv_w + N_FOX_HEADS
    assert N_GROUPS + N_EXPERTS <= ROUTER_COLS

    x2 = x.reshape(T, D)
    out = None
    for l in range(depth):
        w_qkv = w_in[l, :, :qkv_w].astype(bf16)
        w_fg_t = w_in[l, :, qkv_w:].T.astype(bf16)
        qkv, f_t = _in_proj(x2, attn_norm_g[l].reshape(1, D), w_qkv, w_fg_t)
        negc = _fox_decay(f_t, b_forget[l].reshape(N_FOX_HEADS, 1), B, S)
        fo = _fox_attn(qkv, negc, B, S, 0)
        mo = _moba_attn(qkv, rel_bias, B, S, 3 * N_FOX_HEADS)

        w_r = jnp.concatenate([w_group_router[l], w_expert_router[l].reshape(D, N_EXPERTS)], axis=1)
        w_r = jnp.pad(w_r, ((0, 0), (0, ROUTER_COLS - w_r.shape[1]))).astype(bf16)
        b_r = jnp.concatenate([b_group_router[l], b_expert_router[l].reshape(N_EXPERTS)])
        b_r = jnp.pad(b_r, (0, ROUTER_COLS - b_r.shape[0])).reshape(1, ROUTER_COLS)
        x1, h2, comb = _out_proj(fo, mo, x2, fox_out_norm_g[l].reshape(1, fox_w),
                                 moba_out_norm_g[l].reshape(1, moba_w), w_out[l].astype(bf16),
                                 ffn_norm_g[l].reshape(1, D), w_r, b_r)
        last = l == depth - 1
        g_fin = final_norm_g.reshape(1, D)
        assert last, "the fused residual + final norm epilogue expects a single layer"
        out = _moe(h2, comb, x1, w_gate[l].astype(bf16), w_up[l].astype(bf16), w_down[l].astype(bf16), g_fin)
        x2 = out
    return out.reshape(B, S, D)
```

```python
import functools
import math

import numpy as np
import jax
import jax.numpy as jnp
from jax import lax
from jax.experimental import pallas as pl
from jax.experimental.pallas import tpu as pltpu

HEAD_DIM = 128
N_FOX_HEADS = 8
N_MOBA_HEADS = 8
MOBA_BLOCK = 256
MOBA_TOPK = 3
REL_BUCKETS = 32
REL_MAX_DIST = 128
N_GROUPS = 4
EXPERTS_PER_GROUP = 4
N_EXPERTS = N_GROUPS * EXPERTS_PER_GROUP
EPS = 1e-6
NEG = -1e30
LOG2E = math.log2(math.e)
LANES = 128
ROUTER_COLS = LANES
VMEM_LIMIT = 56 * 1024 * 1024

f32 = jnp.float32
bf16 = jnp.bfloat16


def _params(*sem):
    return pltpu.CompilerParams(dimension_semantics=sem, vmem_limit_bytes=VMEM_LIMIT)


def _pick(n, pref):
    t = min(n, pref)
    assert n % t == 0, (n, pref)
    return t


def _in_proj_kernel(cs_ref, x_ref, g_ref, w_ref, wfg_ref, qkv_ref, f_ref, h_scr):
    @pl.when(pl.program_id(1) == 0)
    def _():
        x = x_ref[...]
        y = x * lax.rsqrt(jnp.mean(x * x, axis=-1, keepdims=True) + EPS)
        hb = (y * g_ref[...]).astype(bf16)
        h_scr[...] = hb
        f_ref[...] = lax.dot_general(wfg_ref[...], hb, (((1,), (1,)), ((), ())),
                                     preferred_element_type=f32)

    acc = jnp.dot(h_scr[...], w_ref[...], preferred_element_type=f32) * cs_ref[pl.program_id(1)]
    for c in range(acc.shape[1] // HEAD_DIM):
        qkv_ref[c] = acc[:, c * HEAD_DIM:(c + 1) * HEAD_DIM].astype(bf16)


def _in_proj(x2, g, w_qkv, w_fg_t, col_scale, tn):
    T, D = x2.shape
    N = w_qkv.shape[1]
    tm = _pick(T, 1024)
    assert N % tn == 0 and col_scale.shape == (N // tn,)
    nh = w_fg_t.shape[0]
    return pl.pallas_call(
        _in_proj_kernel,
        grid=(T // tm, N // tn),
        in_specs=[
            pl.BlockSpec(memory_space=pltpu.SMEM),
            pl.BlockSpec((tm, D), lambda i, j: (i, 0)),
            pl.BlockSpec((1, D), lambda i, j: (0, 0)),
            pl.BlockSpec((D, tn), lambda i, j: (0, j)),
            pl.BlockSpec((nh, D), lambda i, j: (0, 0)),
        ],
        out_specs=[
            pl.BlockSpec((tn // HEAD_DIM, tm, HEAD_DIM), lambda i, j: (j, i, 0)),
            pl.BlockSpec((nh, tm), lambda i, j: (0, i)),
        ],
        out_shape=[
            jax.ShapeDtypeStruct((N // HEAD_DIM, T, HEAD_DIM), bf16),
            jax.ShapeDtypeStruct((nh, T), f32),
        ],
        scratch_shapes=[pltpu.VMEM((tm, D), bf16)],
        compiler_params=_params("arbitrary", "arbitrary"),
        name="in_proj",
    )(col_scale, x2, g, w_qkv, w_fg_t)


def _fox_decay_kernel(f_ref, b_ref, o_ref):
    lf = jax.nn.log_sigmoid(f_ref[...] + b_ref[...])
    S = lf.shape[1]
    lane = lax.broadcasted_iota(jnp.int32, lf.shape, 1)
    c = lf
    sh = 1
    while sh < S:
        c = c + jnp.where(lane >= sh, pltpu.roll(c, sh, axis=1), 0.0)
        sh *= 2
    o_ref[0] = -c


def _fox_decay(f_t, b_forget, B, S):
    nh = f_t.shape[0]
    return pl.pallas_call(
        _fox_decay_kernel,
        grid=(B,),
        in_specs=[
            pl.BlockSpec((nh, S), lambda b: (0, b)),
            pl.BlockSpec((nh, 1), lambda b: (0, 0)),
        ],
        out_specs=pl.BlockSpec((1, nh, S), lambda b: (b, 0, 0)),
        out_shape=jax.ShapeDtypeStruct((B, nh, S), f32),
        compiler_params=_params("arbitrary"),
        name="fox_decay",
    )(f_t, b_forget)


def _qk(q, k):
    return lax.dot_general(q, k, (((1,), (1,)), ((), ())), preferred_element_type=f32)


def _softmax_step2(carry, s2, v):
    m, l, acc = carry
    m_new = jnp.maximum(m, jnp.max(s2, axis=-1, keepdims=True))
    alpha = jnp.exp2(m - m_new)
    p = jnp.exp2(s2 - m_new)
    l = alpha * l + jnp.sum(p, axis=-1, keepdims=True)
    acc = alpha * acc + jnp.dot(p.astype(bf16), v, preferred_element_type=f32)
    return m_new, l, acc


def _fox_attn_kernel(q_ref, k_ref, v_ref, nc_ref, o_ref, *, t):
    nq = q_ref.shape[1] // t
    row = lax.broadcasted_iota(jnp.int32, (t, t), 0)
    col = lax.broadcasted_iota(jnp.int32, (t, t), 1)
    for qi in range(nq):
        q = q_ref[0, qi * t:(qi + 1) * t, :]
        carry = (jnp.full((t, 1), NEG, f32), jnp.zeros((t, 1), f32), jnp.zeros((t, HEAD_DIM), f32))
        for kb in range(qi + 1):
            keys = slice(kb * t, (kb + 1) * t)
            s2 = _qk(q, k_ref[0, keys, :]) + nc_ref[0, :, keys] * LOG2E
            if kb == qi:
                s2 = jnp.where(col <= row, s2, NEG)
            carry = _softmax_step2(carry, s2, v_ref[0, keys, :])
        _, l, acc = carry
        o_ref[qi * t:(qi + 1) * t, :] = (acc / l).astype(o_ref.dtype)


def _fox_attn(qkv, negc, B, S, slab0):
    T = B * S
    H = N_FOX_HEADS
    t = _pick(S, 512)
    kern = functools.partial(_fox_attn_kernel, t=t)
    return pl.pallas_call(
        kern,
        grid=(B, H),
        in_specs=[
            pl.BlockSpec((1, S, HEAD_DIM), lambda b, h: (slab0 + h, b, 0)),
            pl.BlockSpec((1, S, HEAD_DIM), lambda b, h: (slab0 + H + h, b, 0)),
            pl.BlockSpec((1, S, HEAD_DIM), lambda b, h: (slab0 + 2 * H + h, b, 0)),
            pl.BlockSpec((1, 1, S), lambda b, h: (b * H + h, 0, 0)),
        ],
        out_specs=pl.BlockSpec((S, HEAD_DIM), lambda b, h: (b, h)),
        out_shape=jax.ShapeDtypeStruct((T, H * HEAD_DIM), bf16),
        compiler_params=_params("arbitrary", "arbitrary"),
        name="fox_attn",
    )(qkv, qkv, qkv, negc.reshape(B * H, 1, S))


def _rel_bucket_table(n):
    max_exact = REL_BUCKETS // 2
    d = np.arange(n)
    ratio = np.log(np.maximum(d, 1).astype(np.float32) / np.float32(max_exact)) / np.float32(
        math.log(REL_MAX_DIST / max_exact))
    large = max_exact + (ratio * np.float32(REL_BUCKETS - max_exact)).astype(np.int32)
    large = np.minimum(large, REL_BUCKETS - 1)
    return np.where(d < max_exact, d, large).astype(np.int32)


def _moba_attn_kernel(rel_ref, q_ref, k_ref, v_ref, avg_ref, hot_ref, bk_ref, o_ref, bias_scr, kaug_scr):
    h = pl.program_id(1)
    blk = MOBA_BLOCK
    nb = q_ref.shape[1] // blk
    far_bias = rel_ref[h, REL_BUCKETS - 1]
    row = lax.broadcasted_iota(jnp.int32, (blk, blk), 0)
    col = lax.broadcasted_iota(jnp.int32, (blk, blk), 1)
    for which in range(2):
        bk = bk_ref[which]
        tile = jnp.zeros((blk, blk), f32)
        for b in range(REL_BUCKETS - 1):
            tile = jnp.where(bk == b, (rel_ref[h, b] - far_bias) * LOG2E, tile)
        bias_scr[which] = jnp.where(col <= row, tile, NEG) if which == 0 else tile
    kaug_scr[:, :HEAD_DIM] = k_ref[0]
    kaug_scr[:, HEAD_DIM:] = hot_ref[...]
    kmean = jnp.dot(avg_ref[...], k_ref[0], preferred_element_type=f32).astype(bf16)
    lane = lax.broadcasted_iota(jnp.int32, (blk, LANES), 1)

    for own in range(nb):
        rows = slice(own * blk, (own + 1) * blk)
        q = q_ref[0, rows, :]
        init = (jnp.full((blk, 1), NEG, f32), jnp.zeros((blk, 1), f32), jnp.zeros((blk, HEAD_DIM), f32))
        carry = _softmax_step2(init, _qk(q, k_ref[0, rows, :]) + bias_scr[0], v_ref[0, rows, :])
        if own > 0:
            valid = lane < own
            gate = jnp.where(valid, _qk(q, kmean), NEG)
            sel = jnp.zeros((blk, LANES), jnp.bool_)
            for _ in range(MOBA_TOPK):
                mx = jnp.max(gate, axis=-1, keepdims=True)
                first = jnp.min(jnp.where(gate == mx, lane, LANES), axis=-1, keepdims=True)
                pick = lane == first
                sel = jnp.logical_or(sel, jnp.logical_and(pick, valid))
                gate = jnp.where(pick, -jnp.inf, gate)
            q_aug = jnp.concatenate([q, jnp.where(sel, 0.0, NEG).astype(bf16)], axis=1)
            prev = slice((own - 1) * blk, own * blk)
            carry = _softmax_step2(carry, _qk(q_aug, kaug_scr[prev, :]) + bias_scr[1], v_ref[0, prev, :])
            n = 0
            while n < own - 1:
                width = min(2, own - 1 - n)
                keys = slice(n * blk, (n + width) * blk)
                carry = _softmax_step2(carry, _qk(q_aug, kaug_scr[keys, :]), v_ref[0, keys, :])
                n += width
        _, l, acc = carry
        o_ref[rows, :] = (acc / l).astype(o_ref.dtype)


def _moba_attn(qkv, rel_bias, B, S, slab0):
    T = B * S
    H = N_MOBA_HEADS
    blk = MOBA_BLOCK
    assert S % blk == 0 and S // blk <= LANES
    nb = S // blk
    avg = np.zeros((LANES, S), np.float32)
    for n in range(nb):
        avg[n, n * blk:(n + 1) * blk] = 1.0 / blk
    hot = (avg.T > 0).astype(np.float32)
    table = _rel_bucket_table(2 * blk)
    dist = np.arange(blk)[:, None] - np.arange(blk)[None, :]
    buckets = np.stack([table[np.maximum(dist, 0)], table[dist + blk]]).astype(np.int32)
    grid_spec = pltpu.PrefetchScalarGridSpec(
        num_scalar_prefetch=1,
        grid=(B, H),
        in_specs=[
            pl.BlockSpec((1, S, HEAD_DIM), lambda b, h, r: (slab0 + h, b, 0)),
            pl.BlockSpec((1, S, HEAD_DIM), lambda b, h, r: (slab0 + H + h, b, 0)),
            pl.BlockSpec((1, S, HEAD_DIM), lambda b, h, r: (slab0 + 2 * H + h, b, 0)),
            pl.BlockSpec((LANES, S), lambda b, h, r: (0, 0)),
            pl.BlockSpec((S, LANES), lambda b, h, r: (0, 0)),
            pl.BlockSpec((2, blk, blk), lambda b, h, r: (0, 0, 0)),
        ],
        out_specs=pl.BlockSpec((S, HEAD_DIM), lambda b, h, r: (b, h)),
        scratch_shapes=[pltpu.VMEM((2, blk, blk), f32), pltpu.VMEM((S, 2 * HEAD_DIM), bf16)],
    )
    return pl.pallas_call(
        _moba_attn_kernel,
        grid_spec=grid_spec,
        out_shape=jax.ShapeDtypeStruct((T, H * HEAD_DIM), bf16),
        compiler_params=_params("arbitrary", "arbitrary"),
        name="moba_attn",
    )(rel_bias, qkv, qkv, qkv, jnp.asarray(avg, bf16), jnp.asarray(hot, bf16), jnp.asarray(buckets))


def _rms(x, g):
    return x * lax.rsqrt(jnp.mean(x * x, axis=-1, keepdims=True) + EPS) * g


def _route(logits):
    lane = lax.broadcasted_iota(jnp.int32, logits.shape, 1)

    def first_max(x):
        mx = jnp.max(x, axis=-1, keepdims=True)
        return mx, jnp.min(jnp.where(x == mx, lane, LANES), axis=-1, keepdims=True)

    gl = jnp.where(lane < N_GROUPS, logits, -jnp.inf)
    gmax, grp = first_max(gl)
    p_top_group = 1.0 / jnp.sum(jnp.exp(gl - gmax), axis=-1, keepdims=True)
    lo = N_GROUPS + grp * EXPERTS_PER_GROUP
    in_grp = jnp.logical_and(lane >= lo, lane < lo + EXPERTS_PER_GROUP)
    el = jnp.where(in_grp, logits, -jnp.inf)
    emax = jnp.max(el, axis=-1, keepdims=True)
    pe = jnp.exp(el - emax)
    p_exp = pe / jnp.sum(pe, axis=-1, keepdims=True)
    p_exp = jnp.where(in_grp, p_exp, -1.0)
    p1, i1 = first_max(p_exp)
    p2, i2 = first_max(jnp.where(lane == i1, -1.0, p_exp))
    tot = p1 + p2
    within = jnp.where(lane == i1, p1 / tot, jnp.where(lane == i2, p2 / tot, 0.0))
    return within * p_top_group


def _out_proj_kernel(fo_ref, mo_ref, x_ref, gf_ref, gm_ref, w_ref, gffn_ref, wr_ref, br_ref,
                     x1_ref, h2_ref, comb_ref):
    fo = _rms(fo_ref[...].astype(f32), gf_ref[...]).astype(bf16)
    mo = _rms(mo_ref[...].astype(f32), gm_ref[...]).astype(bf16)
    mix = jnp.concatenate([fo, mo], axis=-1)
    x1 = x_ref[...] + jnp.dot(mix, w_ref[...], preferred_element_type=f32)
    x1_ref[...] = x1
    h2 = _rms(x1, gffn_ref[...]).astype(bf16)
    h2_ref[...] = h2
    logits = jnp.dot(h2, wr_ref[...], preferred_element_type=f32) + br_ref[...]
    comb_ref[...] = _route(logits)


def _out_proj(fo, mo, x2, gf, gm, w_out, g_ffn, w_r, b_r):
    T, D = x2.shape
    Wf = fo.shape[1]
    Wm = mo.shape[1]
    tm = _pick(T, 512)
    const = lambda i: (0, 0)
    rows = lambda i: (i, 0)
    return pl.pallas_call(
        _out_proj_kernel,
        grid=(T // tm,),
        in_specs=[
            pl.BlockSpec((tm, Wf), rows),
            pl.BlockSpec((tm, Wm), rows),
            pl.BlockSpec((tm, D), rows),
            pl.BlockSpec((1, Wf), const),
            pl.BlockSpec((1, Wm), const),
            pl.BlockSpec((Wf + Wm, D), const),
            pl.BlockSpec((1, D), const),
            pl.BlockSpec((D, ROUTER_COLS), const),
            pl.BlockSpec((1, ROUTER_COLS), const),
        ],
        out_specs=[
            pl.BlockSpec((tm, D), rows),
            pl.BlockSpec((tm, D), rows),
            pl.BlockSpec((tm, ROUTER_COLS), rows),
        ],
        out_shape=[
            jax.ShapeDtypeStruct((T, D), f32),
            jax.ShapeDtypeStruct((T, D), bf16),
            jax.ShapeDtypeStruct((T, ROUTER_COLS), f32),
        ],
        compiler_params=_params("arbitrary"),
        name="out_proj",
    )(fo, mo, x2, gf, gm, w_out, g_ffn, w_r, b_r)


def _moe_kernel(h_ref, comb_ref, x1_ref, wg_ref, wu_ref, wd_ref, gfin_ref, o_ref, acc_scr):
    e = pl.program_id(1)

    @pl.when(e == 0)
    def _():
        acc_scr[...] = jnp.zeros_like(acc_scr)

    h = h_ref[...]
    lane = lax.broadcasted_iota(jnp.int32, comb_ref.shape, 1)
    c = jnp.sum(jnp.where(lane == N_GROUPS + e, comb_ref[...], 0.0), axis=-1, keepdims=True)
    gate = jnp.dot(h, wg_ref[0], preferred_element_type=f32)
    up = jnp.dot(h, wu_ref[0], preferred_element_type=f32)
    hid = (jax.nn.silu(gate) * up * c).astype(bf16)
    acc_scr[...] += jnp.dot(hid, wd_ref[0], preferred_element_type=f32)

    @pl.when(e == pl.num_programs(1) - 1)
    def _():
        o_ref[...] = _rms(x1_ref[...] + acc_scr[...], gfin_ref[...])


def _moe(h2, comb, x1, w_gate, w_up, w_down, g_final):
    T, D = x1.shape
    E, _, F = w_gate.shape
    tm = _pick(T, 512)
    rows = lambda i, e: (i, 0)
    return pl.pallas_call(
        _moe_kernel,
        grid=(T // tm, E),
        in_specs=[
            pl.BlockSpec((tm, D), rows),
            pl.BlockSpec((tm, ROUTER_COLS), rows),
            pl.BlockSpec((tm, D), rows),
            pl.BlockSpec((1, D, F), lambda i, e: (e, 0, 0)),
            pl.BlockSpec((1, D, F), lambda i, e: (e, 0, 0)),
            pl.BlockSpec((1, F, D), lambda i, e: (e, 0, 0)),
            pl.BlockSpec((1, D), lambda i, e: (0, 0)),
        ],
        out_specs=pl.BlockSpec((tm, D), rows),
        out_shape=jax.ShapeDtypeStruct((T, D), f32),
        scratch_shapes=[pltpu.VMEM((tm, D), f32)],
        compiler_params=_params("arbitrary", "arbitrary"),
        name="moe",
    )(h2, comb, x1, w_gate, w_up, w_down, g_final)


def kernel(x, attn_norm_g, w_in, b_forget, fox_out_norm_g, moba_out_norm_g, rel_bias, w_out, ffn_norm_g,
           w_group_router, b_group_router, w_expert_router, b_expert_router, w_gate, w_up, w_down,
           final_norm_g):
    B, S, D = x.shape
    T = B * S
    depth = w_in.shape[0]
    fox_w = N_FOX_HEADS * HEAD_DIM
    moba_w = N_MOBA_HEADS * HEAD_DIM
    qkv_w = 3 * (fox_w + moba_w)
    assert w_in.shape[2] == qkv_w + N_FOX_HEADS
    assert N_GROUPS + N_EXPERTS <= ROUTER_COLS

    x2 = x.reshape(T, D)
    out = None
    for l in range(depth):
        w_qkv = w_in[l, :, :qkv_w].astype(bf16)
        w_fg_t = w_in[l, :, qkv_w:].T.astype(bf16)
        assert fox_w == moba_w
        col_scale = np.ones((qkv_w // fox_w,), np.float32)
        col_scale[[0, 3]] = HEAD_DIM ** -0.5 * LOG2E
        qkv, f_t = _in_proj(x2, attn_norm_g[l].reshape(1, D), w_qkv, w_fg_t, jnp.asarray(col_scale), fox_w)
        negc = _fox_decay(f_t, b_forget[l].reshape(N_FOX_HEADS, 1), B, S)
        fo = _fox_attn(qkv, negc, B, S, 0)
        mo = _moba_attn(qkv, rel_bias, B, S, 3 * N_FOX_HEADS)

        w_r = jnp.concatenate([w_group_router[l], w_expert_router[l].reshape(D, N_EXPERTS)], axis=1)
        w_r = jnp.pad(w_r, ((0, 0), (0, ROUTER_COLS - w_r.shape[1]))).astype(bf16)
        b_r = jnp.concatenate([b_group_router[l], b_expert_router[l].reshape(N_EXPERTS)])
        b_r = jnp.pad(b_r, (0, ROUTER_COLS - b_r.shape[0])).reshape(1, ROUTER_COLS)
        x1, h2, comb = _out_proj(fo, mo, x2, fox_out_norm_g[l].reshape(1, fox_w),
                                 moba_out_norm_g[l].reshape(1, moba_w), w_out[l].astype(bf16),
                                 ffn_norm_g[l].reshape(1, D), w_r, b_r)
        last = l == depth - 1
        g_fin = final_norm_g.reshape(1, D)
        assert last, "the fused residual + final norm epilogue expects a single layer"
        out = _moe(h2, comb, x1, w_gate[l].astype(bf16), w_up[l].astype(bf16), w_down[l].astype(bf16), g_fin)
        x2 = out
    return out.reshape(B, S, D)
```

```python
import functools
import math

import numpy as np
import jax
import jax.numpy as jnp
from jax import lax
from jax.experimental import pallas as pl
from jax.experimental.pallas import tpu as pltpu

HEAD_DIM = 128
N_FOX_HEADS = 8
N_MOBA_HEADS = 8
MOBA_BLOCK = 256
MOBA_TOPK = 3
REL_BUCKETS = 32
REL_MAX_DIST = 128
N_GROUPS = 4
EXPERTS_PER_GROUP = 4
N_EXPERTS = N_GROUPS * EXPERTS_PER_GROUP
EPS = 1e-6
NEG = -1e30
LOG2E = math.log2(math.e)
LANES = 128
ROUTER_COLS = LANES
MOE_TILE = 512
VMEM_LIMIT = 56 * 1024 * 1024

f32 = jnp.float32
bf16 = jnp.bfloat16


def _params(*sem):
    return pltpu.CompilerParams(dimension_semantics=sem, vmem_limit_bytes=VMEM_LIMIT)


def _pick(n, pref):
    t = min(n, pref)
    assert n % t == 0, (n, pref)
    return t


def _in_proj_kernel(cs_ref, x_ref, g_ref, w_ref, wfg_ref, qkv_ref, f_ref, h_scr):
    @pl.when(pl.program_id(1) == 0)
    def _():
        x = x_ref[...]
        y = x * lax.rsqrt(jnp.mean(x * x, axis=-1, keepdims=True) + EPS)
        hb = (y * g_ref[...]).astype(bf16)
        h_scr[...] = hb
        f_ref[...] = lax.dot_general(wfg_ref[...], hb, (((1,), (1,)), ((), ())),
                                     preferred_element_type=f32)

    acc = jnp.dot(h_scr[...], w_ref[...], preferred_element_type=f32) * cs_ref[pl.program_id(1)]
    for c in range(acc.shape[1] // HEAD_DIM):
        qkv_ref[c] = acc[:, c * HEAD_DIM:(c + 1) * HEAD_DIM].astype(bf16)


def _in_proj(x2, g, w_qkv, w_fg_t, col_scale, tn):
    T, D = x2.shape
    N = w_qkv.shape[1]
    tm = _pick(T, 1024)
    assert N % tn == 0 and col_scale.shape == (N // tn,)
    nh = w_fg_t.shape[0]
    return pl.pallas_call(
        _in_proj_kernel,
        grid=(T // tm, N // tn),
        in_specs=[
            pl.BlockSpec(memory_space=pltpu.SMEM),
            pl.BlockSpec((tm, D), lambda i, j: (i, 0)),
            pl.BlockSpec((1, D), lambda i, j: (0, 0)),
            pl.BlockSpec((D, tn), lambda i, j: (0, j)),
            pl.BlockSpec((nh, D), lambda i, j: (0, 0)),
        ],
        out_specs=[
            pl.BlockSpec((tn // HEAD_DIM, tm, HEAD_DIM), lambda i, j: (j, i, 0)),
            pl.BlockSpec((nh, tm), lambda i, j: (0, i)),
        ],
        out_shape=[
            jax.ShapeDtypeStruct((N // HEAD_DIM, T, HEAD_DIM), bf16),
            jax.ShapeDtypeStruct((nh, T), f32),
        ],
        scratch_shapes=[pltpu.VMEM((tm, D), bf16)],
        compiler_params=_params("arbitrary", "arbitrary"),
        name="in_proj",
    )(col_scale, x2, g, w_qkv, w_fg_t)


def _fox_decay_kernel(f_ref, b_ref, o_ref):
    lf = jax.nn.log_sigmoid(f_ref[...] + b_ref[...])
    S = lf.shape[1]
    lane = lax.broadcasted_iota(jnp.int32, lf.shape, 1)
    c = lf
    sh = 1
    while sh < S:
        c = c + jnp.where(lane >= sh, pltpu.roll(c, sh, axis=1), 0.0)
        sh *= 2
    o_ref[0] = -c


def _fox_decay(f_t, b_forget, B, S):
    nh = f_t.shape[0]
    return pl.pallas_call(
        _fox_decay_kernel,
        grid=(B,),
        in_specs=[
            pl.BlockSpec((nh, S), lambda b: (0, b)),
            pl.BlockSpec((nh, 1), lambda b: (0, 0)),
        ],
        out_specs=pl.BlockSpec((1, nh, S), lambda b: (b, 0, 0)),
        out_shape=jax.ShapeDtypeStruct((B, nh, S), f32),
        compiler_params=_params("arbitrary"),
        name="fox_decay",
    )(f_t, b_forget)


def _qk(q, k):
    return lax.dot_general(q, k, (((1,), (1,)), ((), ())), preferred_element_type=f32)


def _softmax_step2(carry, s2, v):
    m, l, acc = carry
    m_new = jnp.maximum(m, jnp.max(s2, axis=-1, keepdims=True))
    alpha = jnp.exp2(m - m_new)
    p = jnp.exp2(s2 - m_new)
    l = alpha * l + jnp.sum(p, axis=-1, keepdims=True)
    acc = alpha * acc + jnp.dot(p.astype(bf16), v, preferred_element_type=f32)
    return m_new, l, acc


def _fox_attn_kernel(q_ref, k_ref, v_ref, nc_ref, o_ref, *, t):
    nq = q_ref.shape[1] // t
    row = lax.broadcasted_iota(jnp.int32, (t, t), 0)
    col = lax.broadcasted_iota(jnp.int32, (t, t), 1)
    for qi in range(nq):
        q = q_ref[0, qi * t:(qi + 1) * t, :]
        carry = (jnp.full((t, 1), NEG, f32), jnp.zeros((t, 1), f32), jnp.zeros((t, HEAD_DIM), f32))
        for kb in range(qi + 1):
            keys = slice(kb * t, (kb + 1) * t)
            s2 = _qk(q, k_ref[0, keys, :]) + nc_ref[0, :, keys] * LOG2E
            if kb == qi:
                s2 = jnp.where(col <= row, s2, NEG)
            carry = _softmax_step2(carry, s2, v_ref[0, keys, :])
        _, l, acc = carry
        o_ref[qi * t:(qi + 1) * t, :] = (acc / l).astype(o_ref.dtype)


def _fox_attn(qkv, negc, B, S, slab0):
    T = B * S
    H = N_FOX_HEADS
    t = _pick(S, 512)
    kern = functools.partial(_fox_attn_kernel, t=t)
    return pl.pallas_call(
        kern,
        grid=(B, H),
        in_specs=[
            pl.BlockSpec((1, S, HEAD_DIM), lambda b, h: (slab0 + h, b, 0)),
            pl.BlockSpec((1, S, HEAD_DIM), lambda b, h: (slab0 + H + h, b, 0)),
            pl.BlockSpec((1, S, HEAD_DIM), lambda b, h: (slab0 + 2 * H + h, b, 0)),
            pl.BlockSpec((1, 1, S), lambda b, h: (b * H + h, 0, 0)),
        ],
        out_specs=pl.BlockSpec((S, HEAD_DIM), lambda b, h: (b, h)),
        out_shape=jax.ShapeDtypeStruct((T, H * HEAD_DIM), bf16),
        compiler_params=_params("arbitrary", "arbitrary"),
        name="fox_attn",
    )(qkv, qkv, qkv, negc.reshape(B * H, 1, S))


def _rel_bucket_table(n):
    max_exact = REL_BUCKETS // 2
    d = np.arange(n)
    ratio = np.log(np.maximum(d, 1).astype(np.float32) / np.float32(max_exact)) / np.float32(
        math.log(REL_MAX_DIST / max_exact))
    large = max_exact + (ratio * np.float32(REL_BUCKETS - max_exact)).astype(np.int32)
    large = np.minimum(large, REL_BUCKETS - 1)
    return np.where(d < max_exact, d, large).astype(np.int32)


def _moba_attn_kernel(rel_ref, q_ref, k_ref, v_ref, avg_ref, hot_ref, bk_ref, o_ref, bias_scr, kaug_scr):
    h = pl.program_id(1)
    blk = MOBA_BLOCK
    nb = q_ref.shape[1] // blk
    far_bias = rel_ref[h, REL_BUCKETS - 1]
    row = lax.broadcasted_iota(jnp.int32, (blk, blk), 0)
    col = lax.broadcasted_iota(jnp.int32, (blk, blk), 1)
    for which in range(2):
        bk = bk_ref[which]
        tile = jnp.zeros((blk, blk), f32)
        for b in range(REL_BUCKETS - 1):
            tile = jnp.where(bk == b, (rel_ref[h, b] - far_bias) * LOG2E, tile)
        bias_scr[which] = jnp.where(col <= row, tile, NEG) if which == 0 else tile
    kaug_scr[:, :HEAD_DIM] = k_ref[0]
    kaug_scr[:, HEAD_DIM:] = hot_ref[...]
    kmean = jnp.dot(avg_ref[...], k_ref[0], preferred_element_type=f32).astype(bf16)
    lane = lax.broadcasted_iota(jnp.int32, (blk, LANES), 1)

    for own in range(nb):
        rows = slice(own * blk, (own + 1) * blk)
        q = q_ref[0, rows, :]
        init = (jnp.full((blk, 1), NEG, f32), jnp.zeros((blk, 1), f32), jnp.zeros((blk, HEAD_DIM), f32))
        carry = _softmax_step2(init, _qk(q, k_ref[0, rows, :]) + bias_scr[0], v_ref[0, rows, :])
        if own > 0:
            valid = lane < own
            gate = jnp.where(valid, _qk(q, kmean), NEG)
            sel = jnp.zeros((blk, LANES), jnp.bool_)
            for _ in range(MOBA_TOPK):
                mx = jnp.max(gate, axis=-1, keepdims=True)
                first = jnp.min(jnp.where(gate == mx, lane, LANES), axis=-1, keepdims=True)
                pick = lane == first
                sel = jnp.logical_or(sel, jnp.logical_and(pick, valid))
                gate = jnp.where(pick, -jnp.inf, gate)
            q_aug = jnp.concatenate([q, jnp.where(sel, 0.0, NEG).astype(bf16)], axis=1)
            prev = slice((own - 1) * blk, own * blk)
            carry = _softmax_step2(carry, _qk(q_aug, kaug_scr[prev, :]) + bias_scr[1], v_ref[0, prev, :])
            n = 0
            while n < own - 1:
                width = min(2, own - 1 - n)
                keys = slice(n * blk, (n + width) * blk)
                carry = _softmax_step2(carry, _qk(q_aug, kaug_scr[keys, :]), v_ref[0, keys, :])
                n += width
        _, l, acc = carry
        o_ref[rows, :] = (acc / l).astype(o_ref.dtype)


def _moba_attn(qkv, rel_bias, B, S, slab0):
    T = B * S
    H = N_MOBA_HEADS
    blk = MOBA_BLOCK
    assert S % blk == 0 and S // blk <= LANES
    nb = S // blk
    avg = np.zeros((LANES, S), np.float32)
    for n in range(nb):
        avg[n, n * blk:(n + 1) * blk] = 1.0 / blk
    hot = (avg.T > 0).astype(np.float32)
    table = _rel_bucket_table(2 * blk)
    dist = np.arange(blk)[:, None] - np.arange(blk)[None, :]
    buckets = np.stack([table[np.maximum(dist, 0)], table[dist + blk]]).astype(np.int32)
    grid_spec = pltpu.PrefetchScalarGridSpec(
        num_scalar_prefetch=1,
        grid=(B, H),
        in_specs=[
            pl.BlockSpec((1, S, HEAD_DIM), lambda b, h, r: (slab0 + h, b, 0)),
            pl.BlockSpec((1, S, HEAD_DIM), lambda b, h, r: (slab0 + H + h, b, 0)),
            pl.BlockSpec((1, S, HEAD_DIM), lambda b, h, r: (slab0 + 2 * H + h, b, 0)),
            pl.BlockSpec((LANES, S), lambda b, h, r: (0, 0)),
            pl.BlockSpec((S, LANES), lambda b, h, r: (0, 0)),
            pl.BlockSpec((2, blk, blk), lambda b, h, r: (0, 0, 0)),
        ],
        out_specs=pl.BlockSpec((S, HEAD_DIM), lambda b, h, r: (b, h)),
        scratch_shapes=[pltpu.VMEM((2, blk, blk), f32), pltpu.VMEM((S, 2 * HEAD_DIM), bf16)],
    )
    return pl.pallas_call(
        _moba_attn_kernel,
        grid_spec=grid_spec,
        out_shape=jax.ShapeDtypeStruct((T, H * HEAD_DIM), bf16),
        compiler_params=_params("arbitrary", "arbitrary"),
        name="moba_attn",
    )(rel_bias, qkv, qkv, qkv, jnp.asarray(avg, bf16), jnp.asarray(hot, bf16), jnp.asarray(buckets))


def _rms(x, g):
    return x * lax.rsqrt(jnp.mean(x * x, axis=-1, keepdims=True) + EPS) * g


def _first_max(x, lane):
    mx = jnp.max(x, axis=-1, keepdims=True)
    return mx, jnp.min(jnp.where(x == mx, lane, LANES), axis=-1, keepdims=True)


def _top_group(logits, lane):
    gl = jnp.where(lane < N_GROUPS, logits, -jnp.inf)
    gmax, grp = _first_max(gl, lane)
    pg = jnp.exp(gl - gmax)
    return grp, pg, jnp.sum(pg, axis=-1, keepdims=True)


def _out_proj_kernel(fo_ref, mo_ref, x_ref, gf_ref, gm_ref, w_ref, gffn_ref, wr_ref, br_ref,
                     x1_ref, route_ref, count_ref, count_scr):
    @pl.when(pl.program_id(0) == 0)
    def _():
        count_scr[...] = jnp.zeros_like(count_scr)

    fo = _rms(fo_ref[...].astype(f32), gf_ref[...]).astype(bf16)
    mo = _rms(mo_ref[...].astype(f32), gm_ref[...]).astype(bf16)
    mix = jnp.concatenate([fo, mo], axis=-1)
    x1 = x_ref[...] + jnp.dot(mix, w_ref[...], preferred_element_type=f32)
    x1_ref[...] = x1
    h2 = _rms(x1, gffn_ref[...]).astype(bf16)
    logits = jnp.dot(h2, wr_ref[...], preferred_element_type=f32) + br_ref[...]
    tm = logits.shape[0]
    lane = lax.broadcasted_iota(jnp.int32, logits.shape, 1)
    grp, _, _ = _top_group(logits, lane)
    onehot = jnp.where(lane == grp, 1.0, 0.0)
    earlier = lax.broadcasted_iota(jnp.int32, (tm, tm), 1) < lax.broadcasted_iota(jnp.int32, (tm, tm), 0)
    before = jnp.dot(jnp.where(earlier, 1.0, 0.0).astype(bf16), onehot.astype(bf16),
                     preferred_element_type=f32) + count_scr[...]
    rank = jnp.sum(onehot * before, axis=-1, keepdims=True).astype(jnp.int32)
    route_ref[...] = jnp.where(lane == 0, grp, jnp.where(lane == 1, rank, 0))
    count_scr[...] += jnp.sum(onehot, axis=0, keepdims=True)
    count_ref[...] = count_scr[...].astype(jnp.int32)


def _out_proj(fo, mo, x2, gf, gm, w_out, g_ffn, w_r, b_r):
    T, D = x2.shape
    Wf = fo.shape[1]
    Wm = mo.shape[1]
    tm = _pick(T, 512)
    const = lambda i: (0, 0)
    rows = lambda i: (i, 0)
    return pl.pallas_call(
        _out_proj_kernel,
        grid=(T // tm,),
        in_specs=[
            pl.BlockSpec((tm, Wf), rows),
            pl.BlockSpec((tm, Wm), rows),
            pl.BlockSpec((tm, D), rows),
            pl.BlockSpec((1, Wf), const),
            pl.BlockSpec((1, Wm), const),
            pl.BlockSpec((Wf + Wm, D), const),
            pl.BlockSpec((1, D), const),
            pl.BlockSpec((D, ROUTER_COLS), const),
            pl.BlockSpec((1, ROUTER_COLS), const),
        ],
        out_specs=[
            pl.BlockSpec((tm, D), rows),
            pl.BlockSpec((tm, ROUTER_COLS), rows),
            pl.BlockSpec((1, ROUTER_COLS), const),
        ],
        out_shape=[
            jax.ShapeDtypeStruct((T, D), f32),
            jax.ShapeDtypeStruct((T, ROUTER_COLS), jnp.int32),
            jax.ShapeDtypeStruct((1, ROUTER_COLS), jnp.int32),
        ],
        scratch_shapes=[pltpu.VMEM((1, ROUTER_COLS), f32)],
        compiler_params=_params("arbitrary"),
        name="out_proj",
    )(fo, mo, x2, gf, gm, w_out, g_ffn, w_r, b_r)


def _slot(grp_ref, rank_ref, base_ref, t):
    return base_ref[grp_ref[t]] + rank_ref[t]


def _dispatch_kernel(grp_ref, rank_ref, base_ref, fill_ref, x_ref, xs_ref, zero_scr, sem, zsem):
    i = pl.program_id(0)
    tm = x_ref.shape[0]

    def row_copy(r):
        d = _slot(grp_ref, rank_ref, base_ref, i * tm + r)
        return pltpu.make_async_copy(x_ref.at[pl.ds(r, 1)], xs_ref.at[pl.ds(d, 1)], sem)

    def zero_copy(r):
        return pltpu.make_async_copy(zero_scr, xs_ref.at[pl.ds(r, 1)], zsem)

    def start(copy):
        def body(r, carry):
            copy(r).start()
            return carry
        return body

    def wait(copy):
        def body(r, carry):
            copy(r).wait()
            return carry
        return body

    lax.fori_loop(0, tm, start(row_copy), 0)

    @pl.when(i == 0)
    def _():
        zero_scr[...] = jnp.zeros_like(zero_scr)
        for g in range(N_GROUPS):
            lax.fori_loop(fill_ref[2 * g], fill_ref[2 * g + 1], start(zero_copy), 0)
            lax.fori_loop(fill_ref[2 * g], fill_ref[2 * g + 1], wait(zero_copy), 0)

    lax.fori_loop(0, tm, wait(row_copy), 0)


def _dispatch(x1, grp, rank, base, fill, rows_out):
    T, D = x1.shape
    tm = _pick(T, 512)
    grid_spec = pltpu.PrefetchScalarGridSpec(
        num_scalar_prefetch=4,
        grid=(T // tm,),
        in_specs=[pl.BlockSpec((tm, D), lambda i, *_: (i, 0))],
        out_specs=pl.BlockSpec(memory_space=pl.ANY),
        scratch_shapes=[pltpu.VMEM((1, D), f32), pltpu.SemaphoreType.DMA, pltpu.SemaphoreType.DMA],
    )
    return pl.pallas_call(
        _dispatch_kernel,
        grid_spec=grid_spec,
        out_shape=jax.ShapeDtypeStruct((rows_out, D), f32),
        compiler_params=_params("arbitrary"),
        name="moe_dispatch",
    )(grp, rank, base, fill, x1)


def _collect_kernel(grp_ref, rank_ref, base_ref, ys_ref, o_ref, sem):
    i = pl.program_id(0)
    tm = o_ref.shape[0]

    def row_copy(r):
        d = _slot(grp_ref, rank_ref, base_ref, i * tm + r)
        return pltpu.make_async_copy(ys_ref.at[pl.ds(d, 1)], o_ref.at[pl.ds(r, 1)], sem)

    def start(r, carry):
        row_copy(r).start()
        return carry

    def wait(r, carry):
        row_copy(r).wait()
        return carry

    lax.fori_loop(0, tm, start, 0)
    lax.fori_loop(0, tm, wait, 0)


def _collect(ys, grp, rank, base, T):
    D = ys.shape[1]
    tm = _pick(T, 512)
    grid_spec = pltpu.PrefetchScalarGridSpec(
        num_scalar_prefetch=3,
        grid=(T // tm,),
        in_specs=[pl.BlockSpec(memory_space=pl.ANY)],
        out_specs=pl.BlockSpec((tm, D), lambda i, *_: (i, 0)),
        scratch_shapes=[pltpu.SemaphoreType.DMA],
    )
    return pl.pallas_call(
        _collect_kernel,
        grid_spec=grid_spec,
        out_shape=jax.ShapeDtypeStruct((T, D), f32),
        compiler_params=_params("arbitrary"),
        name="moe_collect",
    )(grp, rank, base, ys)


def _moe_kernel(tg_ref, nu_ref, xs_ref, gffn_ref, wr_ref, br_ref, wg_ref, wu_ref, wd_ref, gfin_ref, o_ref):
    i = pl.program_id(0)

    @pl.when(i < nu_ref[0])
    def _():
        g = tg_ref[i]
        x = xs_ref[...]
        h = _rms(x, gffn_ref[...]).astype(bf16)
        logits = jnp.dot(h, wr_ref[...], preferred_element_type=f32) + br_ref[...]
        lane = lax.broadcasted_iota(jnp.int32, logits.shape, 1)
        _, pg, pg_sum = _top_group(logits, lane)
        p_top_group = jnp.sum(jnp.where(lane == g, pg, 0.0), axis=-1, keepdims=True) / pg_sum
        lo = N_GROUPS + g * EXPERTS_PER_GROUP
        in_grp = jnp.logical_and(lane >= lo, lane < lo + EXPERTS_PER_GROUP)
        el = jnp.where(in_grp, logits, -jnp.inf)
        pe = jnp.exp(el - jnp.max(el, axis=-1, keepdims=True))
        p_exp = jnp.where(in_grp, pe / jnp.sum(pe, axis=-1, keepdims=True), -1.0)
        p1, i1 = _first_max(p_exp, lane)
        p2, i2 = _first_max(jnp.where(lane == i1, -1.0, p_exp), lane)
        tot = p1 + p2
        combine = jnp.where(lane == i1, p1 / tot, jnp.where(lane == i2, p2 / tot, 0.0)) * p_top_group
        y = jnp.zeros(x.shape, f32)
        for e in range(EXPERTS_PER_GROUP):
            c = jnp.sum(jnp.where(lane == lo + e, combine, 0.0), axis=-1, keepdims=True)
            gate = jnp.dot(h, wg_ref[e], preferred_element_type=f32)
            up = jnp.dot(h, wu_ref[e], preferred_element_type=f32)
            hid = (jax.nn.silu(gate) * up * c).astype(bf16)
            y = y + jnp.dot(hid, wd_ref[e], preferred_element_type=f32)
        o_ref[...] = _rms(x + y, gfin_ref[...])

    @pl.when(i >= nu_ref[0])
    def _():
        o_ref[...] = jnp.zeros_like(o_ref)


def _moe(xs, tile_group, n_used, g_ffn, w_r, b_r, w_gate, w_up, w_down, g_final, ts):
    R, D = xs.shape
    E, _, F = w_gate.shape
    G = EXPERTS_PER_GROUP
    const = lambda i, tg, nu: (0, 0)
    rows = lambda i, tg, nu: (jnp.minimum(i, nu[0] - 1), 0)
    experts = lambda i, tg, nu: (tg[i], 0, 0)
    once = pl.Buffered(1)
    grid_spec = pltpu.PrefetchScalarGridSpec(
        num_scalar_prefetch=2,
        grid=(R // ts,),
        in_specs=[
            pl.BlockSpec((ts, D), rows),
            pl.BlockSpec((1, D), const),
            pl.BlockSpec((D, ROUTER_COLS), const),
            pl.BlockSpec((1, ROUTER_COLS), const),
            pl.BlockSpec((G, D, F), experts, pipeline_mode=once),
            pl.BlockSpec((G, D, F), experts, pipeline_mode=once),
            pl.BlockSpec((G, F, D), experts, pipeline_mode=once),
            pl.BlockSpec((1, D), const),
        ],
        out_specs=pl.BlockSpec((ts, D), lambda i, tg, nu: (i, 0)),
    )
    return pl.pallas_call(
        _moe_kernel,
        grid_spec=grid_spec,
        out_shape=jax.ShapeDtypeStruct((R, D), f32),
        compiler_params=_params("arbitrary"),
        name="moe",
    )(tile_group, n_used, xs, g_ffn, w_r, b_r, w_gate, w_up, w_down, g_final)


def kernel(x, attn_norm_g, w_in, b_forget, fox_out_norm_g, moba_out_norm_g, rel_bias, w_out, ffn_norm_g,
           w_group_router, b_group_router, w_expert_router, b_expert_router, w_gate, w_up, w_down,
           final_norm_g):
    B, S, D = x.shape
    T = B * S
    depth = w_in.shape[0]
    fox_w = N_FOX_HEADS * HEAD_DIM
    moba_w = N_MOBA_HEADS * HEAD_DIM
    qkv_w = 3 * (fox_w + moba_w)
    assert w_in.shape[2] == qkv_w + N_FOX_HEADS
    assert N_GROUPS + N_EXPERTS <= ROUTER_COLS

    x2 = x.reshape(T, D)
    out = None
    for l in range(depth):
        w_qkv = w_in[l, :, :qkv_w].astype(bf16)
        w_fg_t = w_in[l, :, qkv_w:].T.astype(bf16)
        assert fox_w == moba_w
        col_scale = np.ones((qkv_w // fox_w,), np.float32)
        col_scale[[0, 3]] = HEAD_DIM ** -0.5 * LOG2E
        qkv, f_t = _in_proj(x2, attn_norm_g[l].reshape(1, D), w_qkv, w_fg_t, jnp.asarray(col_scale), fox_w)
        negc = _fox_decay(f_t, b_forget[l].reshape(N_FOX_HEADS, 1), B, S)
        fo = _fox_attn(qkv, negc, B, S, 0)
        mo = _moba_attn(qkv, rel_bias, B, S, 3 * N_FOX_HEADS)

        w_r = jnp.concatenate([w_group_router[l], w_expert_router[l].reshape(D, N_EXPERTS)], axis=1)
        w_r = jnp.pad(w_r, ((0, 0), (0, ROUTER_COLS - w_r.shape[1]))).astype(bf16)
        b_r = jnp.concatenate([b_group_router[l], b_expert_router[l].reshape(N_EXPERTS)])
        b_r = jnp.pad(b_r, (0, ROUTER_COLS - b_r.shape[0])).reshape(1, ROUTER_COLS)
        g_ffn = ffn_norm_g[l].reshape(1, D)
        x1, route, count = _out_proj(fo, mo, x2, fox_out_norm_g[l].reshape(1, fox_w),
                                     moba_out_norm_g[l].reshape(1, moba_w), w_out[l].astype(bf16),
                                     g_ffn, w_r, b_r)
        ts = MOE_TILE
        grp, rank = route[:, 0], route[:, 1]
        counts = count[0, :N_GROUPS]
        padded = (counts + ts - 1) // ts * ts
        ends = jnp.cumsum(padded)
        base = ends - padded
        rows_out = T + N_GROUPS * ts
        fill = jnp.stack([base + counts, ends.at[-1].set(rows_out)], axis=1).reshape(-1)
        tile_start = jnp.arange(rows_out // ts, dtype=jnp.int32) * ts
        tile_group = jnp.minimum(jnp.sum(tile_start[:, None] >= ends[None, :], axis=1), N_GROUPS - 1)
        n_used = (ends[-1:] // ts).astype(jnp.int32)

        assert l == depth - 1, "the fused residual + final norm epilogue expects a single layer"
        xs = _dispatch(x1, grp, rank, base, fill, rows_out)
        ys = _moe(xs, tile_group.astype(jnp.int32), n_used, g_ffn, w_r, b_r, w_gate[l].astype(bf16),
                  w_up[l].astype(bf16), w_down[l].astype(bf16), final_norm_g.reshape(1, D), ts)
        out = _collect(ys, grp, rank, base, T)
        x2 = out
    return out.reshape(B, S, D)
```

```python
import functools
import math

import numpy as np
import jax
import jax.numpy as jnp
from jax import lax
from jax.experimental import pallas as pl
from jax.experimental.pallas import tpu as pltpu

HEAD_DIM = 128
N_FOX_HEADS = 8
N_MOBA_HEADS = 8
MOBA_BLOCK = 256
MOBA_TOPK = 3
REL_BUCKETS = 32
REL_MAX_DIST = 128
N_GROUPS = 4
EXPERTS_PER_GROUP = 4
N_EXPERTS = N_GROUPS * EXPERTS_PER_GROUP
EPS = 1e-6
NEG = -1e30
LOG2E = math.log2(math.e)
LANES = 128
ROUTER_COLS = LANES
MOE_TILE = 512
ZERO_ROWS = 64
DMA_UNROLL = 8
VMEM_LIMIT = 56 * 1024 * 1024

f32 = jnp.float32
bf16 = jnp.bfloat16


def _params(*sem):
    return pltpu.CompilerParams(dimension_semantics=sem, vmem_limit_bytes=VMEM_LIMIT)


def _pick(n, pref):
    t = min(n, pref)
    assert n % t == 0, (n, pref)
    return t


def _in_proj_kernel(cs_ref, x_ref, g_ref, w_ref, wfg_ref, qkv_ref, f_ref, h_scr):
    @pl.when(pl.program_id(1) == 0)
    def _():
        x = x_ref[...]
        y = x * lax.rsqrt(jnp.mean(x * x, axis=-1, keepdims=True) + EPS)
        hb = (y * g_ref[...]).astype(bf16)
        h_scr[...] = hb
        f_ref[...] = lax.dot_general(wfg_ref[...], hb, (((1,), (1,)), ((), ())),
                                     preferred_element_type=f32)

    acc = jnp.dot(h_scr[...], w_ref[...], preferred_element_type=f32) * cs_ref[pl.program_id(1)]
    for c in range(acc.shape[1] // HEAD_DIM):
        qkv_ref[c] = acc[:, c * HEAD_DIM:(c + 1) * HEAD_DIM].astype(bf16)


def _in_proj(x2, g, w_qkv, w_fg_t, col_scale, tn):
    T, D = x2.shape
    N = w_qkv.shape[1]
    tm = _pick(T, 1024)
    assert N % tn == 0 and col_scale.shape == (N // tn,)
    nh = w_fg_t.shape[0]
    return pl.pallas_call(
        _in_proj_kernel,
        grid=(T // tm, N // tn),
        in_specs=[
            pl.BlockSpec(memory_space=pltpu.SMEM),
            pl.BlockSpec((tm, D), lambda i, j: (i, 0)),
            pl.BlockSpec((1, D), lambda i, j: (0, 0)),
            pl.BlockSpec((D, tn), lambda i, j: (0, j)),
            pl.BlockSpec((nh, D), lambda i, j: (0, 0)),
        ],
        out_specs=[
            pl.BlockSpec((tn // HEAD_DIM, tm, HEAD_DIM), lambda i, j: (j, i, 0)),
            pl.BlockSpec((nh, tm), lambda i, j: (0, i)),
        ],
        out_shape=[
            jax.ShapeDtypeStruct((N // HEAD_DIM, T, HEAD_DIM), bf16),
            jax.ShapeDtypeStruct((nh, T), f32),
        ],
        scratch_shapes=[pltpu.VMEM((tm, D), bf16)],
        compiler_params=_params("arbitrary", "arbitrary"),
        name="in_proj",
    )(col_scale, x2, g, w_qkv, w_fg_t)


def _fox_decay_kernel(f_ref, b_ref, o_ref):
    lf = jax.nn.log_sigmoid(f_ref[...] + b_ref[...])
    S = lf.shape[1]
    lane = lax.broadcasted_iota(jnp.int32, lf.shape, 1)
    c = lf
    sh = 1
    while sh < S:
        c = c + jnp.where(lane >= sh, pltpu.roll(c, sh, axis=1), 0.0)
        sh *= 2
    o_ref[0] = -c


def _fox_decay(f_t, b_forget, B, S):
    nh = f_t.shape[0]
    return pl.pallas_call(
        _fox_decay_kernel,
        grid=(B,),
        in_specs=[
            pl.BlockSpec((nh, S), lambda b: (0, b)),
            pl.BlockSpec((nh, 1), lambda b: (0, 0)),
        ],
        out_specs=pl.BlockSpec((1, nh, S), lambda b: (b, 0, 0)),
        out_shape=jax.ShapeDtypeStruct((B, nh, S), f32),
        compiler_params=_params("arbitrary"),
        name="fox_decay",
    )(f_t, b_forget)


def _qk(q, k):
    return lax.dot_general(q, k, (((1,), (1,)), ((), ())), preferred_element_type=f32)


def _softmax_step2(carry, s2, v):
    m, l, acc = carry
    m_new = jnp.maximum(m, jnp.max(s2, axis=-1, keepdims=True))
    alpha = jnp.exp2(m - m_new)
    p = jnp.exp2(s2 - m_new)
    l = alpha * l + jnp.sum(p, axis=-1, keepdims=True)
    acc = alpha * acc + jnp.dot(p.astype(bf16), v, preferred_element_type=f32)
    return m_new, l, acc


def _fox_attn_kernel(q_ref, k_ref, v_ref, nc_ref, o_ref, *, t):
    nq = q_ref.shape[1] // t
    row = lax.broadcasted_iota(jnp.int32, (t, t), 0)
    col = lax.broadcasted_iota(jnp.int32, (t, t), 1)
    for qi in range(nq):
        q = q_ref[0, qi * t:(qi + 1) * t, :]
        carry = (jnp.full((t, 1), NEG, f32), jnp.zeros((t, 1), f32), jnp.zeros((t, HEAD_DIM), f32))
        for kb in range(qi + 1):
            keys = slice(kb * t, (kb + 1) * t)
            s2 = _qk(q, k_ref[0, keys, :]) + nc_ref[0, :, keys] * LOG2E
            if kb == qi:
                s2 = jnp.where(col <= row, s2, NEG)
            carry = _softmax_step2(carry, s2, v_ref[0, keys, :])
        _, l, acc = carry
        o_ref[qi * t:(qi + 1) * t, :] = (acc / l).astype(o_ref.dtype)


def _fox_attn(qkv, negc, B, S, slab0):
    T = B * S
    H = N_FOX_HEADS
    t = _pick(S, 512)
    kern = functools.partial(_fox_attn_kernel, t=t)
    return pl.pallas_call(
        kern,
        grid=(B, H),
        in_specs=[
            pl.BlockSpec((1, S, HEAD_DIM), lambda b, h: (slab0 + h, b, 0)),
            pl.BlockSpec((1, S, HEAD_DIM), lambda b, h: (slab0 + H + h, b, 0)),
            pl.BlockSpec((1, S, HEAD_DIM), lambda b, h: (slab0 + 2 * H + h, b, 0)),
            pl.BlockSpec((1, 1, S), lambda b, h: (b * H + h, 0, 0)),
        ],
        out_specs=pl.BlockSpec((S, HEAD_DIM), lambda b, h: (b, h)),
        out_shape=jax.ShapeDtypeStruct((T, H * HEAD_DIM), bf16),
        compiler_params=_params("arbitrary", "arbitrary"),
        name="fox_attn",
    )(qkv, qkv, qkv, negc.reshape(B * H, 1, S))


def _rel_bucket_table(n):
    max_exact = REL_BUCKETS // 2
    d = np.arange(n)
    ratio = np.log(np.maximum(d, 1).astype(np.float32) / np.float32(max_exact)) / np.float32(
        math.log(REL_MAX_DIST / max_exact))
    large = max_exact + (ratio * np.float32(REL_BUCKETS - max_exact)).astype(np.int32)
    large = np.minimum(large, REL_BUCKETS - 1)
    return np.where(d < max_exact, d, large).astype(np.int32)


def _moba_attn_kernel(rel_ref, q_ref, k_ref, v_ref, avg_ref, hot_ref, bk_ref, o_ref, bias_scr, kaug_scr,
                      qaug_scr):
    h = pl.program_id(1)
    blk = MOBA_BLOCK
    t = 2 * blk
    S = q_ref.shape[1]
    far_bias = rel_ref[h, REL_BUCKETS - 1]
    row = lax.broadcasted_iota(jnp.int32, (blk, blk), 0)
    col = lax.broadcasted_iota(jnp.int32, (blk, blk), 1)
    tiles = []
    for which in range(2):
        bk = bk_ref[which]
        tile = jnp.zeros((blk, blk), f32)
        for b in range(REL_BUCKETS - 1):
            tile = jnp.where(bk == b, (rel_ref[h, b] - far_bias) * LOG2E, tile)
        tiles.append(jnp.where(col <= row, tile, NEG) if which == 0 else tile)
    own_tile, prev_tile = tiles
    zero_tile = jnp.zeros((blk, blk), f32)
    for which, quads in enumerate((((zero_tile, prev_tile), (zero_tile, zero_tile)),
                                   ((own_tile, zero_tile), (prev_tile, own_tile)))):
        for r in range(2):
            for c in range(2):
                bias_scr[which, r * blk:(r + 1) * blk, c * blk:(c + 1) * blk] = quads[r][c]
    kaug_scr[:, :HEAD_DIM] = k_ref[0]
    kaug_scr[:, HEAD_DIM:] = hot_ref[...]
    kmean = jnp.dot(avg_ref[...], k_ref[0], preferred_element_type=f32).astype(bf16)

    lane = lax.broadcasted_iota(jnp.int32, (S, LANES), 1)
    own = lax.broadcasted_iota(jnp.int32, (S, LANES), 0) // blk
    valid = lane < own
    gate = jnp.where(valid, _qk(q_ref[0], kmean), NEG)
    sel = lane == own
    for _ in range(MOBA_TOPK):
        mx = jnp.max(gate, axis=-1, keepdims=True)
        first = jnp.min(jnp.where(gate == mx, lane, LANES), axis=-1, keepdims=True)
        pick = lane == first
        sel = jnp.logical_or(sel, jnp.logical_and(pick, valid))
        gate = jnp.where(pick, -jnp.inf, gate)
    qaug_scr[:, :HEAD_DIM] = q_ref[0]
    qaug_scr[:, HEAD_DIM:] = jnp.where(sel, 0.0, NEG).astype(bf16)

    for j in range(S // t):
        rows = slice(j * t, (j + 1) * t)
        qa = qaug_scr[rows, :]
        carry = (jnp.full((t, 1), NEG, f32), jnp.zeros((t, 1), f32), jnp.zeros((t, HEAD_DIM), f32))
        carry = _softmax_step2(carry, _qk(qa, kaug_scr[rows, :]) + bias_scr[1], v_ref[0, rows, :])
        if j > 0:
            keys = slice((j - 1) * t, j * t)
            carry = _softmax_step2(carry, _qk(qa, kaug_scr[keys, :]) + bias_scr[0], v_ref[0, keys, :])
        for c in range(j - 1):
            keys = slice(c * t, (c + 1) * t)
            carry = _softmax_step2(carry, _qk(qa, kaug_scr[keys, :]), v_ref[0, keys, :])
        _, l, acc = carry
        o_ref[rows, :] = (acc / l).astype(o_ref.dtype)


def _moba_attn(qkv, rel_bias, B, S, slab0):
    T = B * S
    H = N_MOBA_HEADS
    blk = MOBA_BLOCK
    assert S % (2 * blk) == 0 and S // blk <= LANES
    nb = S // blk
    avg = np.zeros((LANES, S), np.float32)
    for n in range(nb):
        avg[n, n * blk:(n + 1) * blk] = 1.0 / blk
    hot = (avg.T > 0).astype(np.float32)
    table = _rel_bucket_table(2 * blk)
    dist = np.arange(blk)[:, None] - np.arange(blk)[None, :]
    buckets = np.stack([table[np.maximum(dist, 0)], table[dist + blk]]).astype(np.int32)
    grid_spec = pltpu.PrefetchScalarGridSpec(
        num_scalar_prefetch=1,
        grid=(B, H),
        in_specs=[
            pl.BlockSpec((1, S, HEAD_DIM), lambda b, h, r: (slab0 + h, b, 0)),
            pl.BlockSpec((1, S, HEAD_DIM), lambda b, h, r: (slab0 + H + h, b, 0)),
            pl.BlockSpec((1, S, HEAD_DIM), lambda b, h, r: (slab0 + 2 * H + h, b, 0)),
            pl.BlockSpec((LANES, S), lambda b, h, r: (0, 0)),
            pl.BlockSpec((S, LANES), lambda b, h, r: (0, 0)),
            pl.BlockSpec((2, blk, blk), lambda b, h, r: (0, 0, 0)),
        ],
        out_specs=pl.BlockSpec((S, HEAD_DIM), lambda b, h, r: (b, h)),
        scratch_shapes=[pltpu.VMEM((2, 2 * blk, 2 * blk), f32), pltpu.VMEM((S, 2 * HEAD_DIM), bf16),
                        pltpu.VMEM((S, 2 * HEAD_DIM), bf16)],
    )
    return pl.pallas_call(
        _moba_attn_kernel,
        grid_spec=grid_spec,
        out_shape=jax.ShapeDtypeStruct((T, H * HEAD_DIM), bf16),
        compiler_params=_params("arbitrary", "arbitrary"),
        name="moba_attn",
    )(rel_bias, qkv, qkv, qkv, jnp.asarray(avg, bf16), jnp.asarray(hot, bf16), jnp.asarray(buckets))


def _rms(x, g):
    return x * lax.rsqrt(jnp.mean(x * x, axis=-1, keepdims=True) + EPS) * g


def _first_max(x, lane):
    mx = jnp.max(x, axis=-1, keepdims=True)
    return mx, jnp.min(jnp.where(x == mx, lane, LANES), axis=-1, keepdims=True)


def _top_group(logits, lane):
    gl = jnp.where(lane < N_GROUPS, logits, -jnp.inf)
    gmax, grp = _first_max(gl, lane)
    pg = jnp.exp(gl - gmax)
    return grp, pg, jnp.sum(pg, axis=-1, keepdims=True)


def _out_proj_kernel(fo_ref, mo_ref, x_ref, gf_ref, gm_ref, w_ref, gffn_ref, wr_ref, br_ref,
                     x1_ref, route_ref, count_ref, count_scr):
    @pl.when(pl.program_id(0) == 0)
    def _():
        count_scr[...] = jnp.zeros_like(count_scr)

    fo = _rms(fo_ref[...].astype(f32), gf_ref[...]).astype(bf16)
    mo = _rms(mo_ref[...].astype(f32), gm_ref[...]).astype(bf16)
    mix = jnp.concatenate([fo, mo], axis=-1)
    x1 = x_ref[...] + jnp.dot(mix, w_ref[...], preferred_element_type=f32)
    x1_ref[...] = x1
    h2 = _rms(x1, gffn_ref[...]).astype(bf16)
    logits = jnp.dot(h2, wr_ref[...], preferred_element_type=f32) + br_ref[...]
    tm = logits.shape[0]
    lane = lax.broadcasted_iota(jnp.int32, logits.shape, 1)
    grp, _, _ = _top_group(logits, lane)
    onehot = jnp.where(lane == grp, 1.0, 0.0)
    earlier = lax.broadcasted_iota(jnp.int32, (tm, tm), 1) < lax.broadcasted_iota(jnp.int32, (tm, tm), 0)
    before = jnp.dot(jnp.where(earlier, 1.0, 0.0).astype(bf16), onehot.astype(bf16),
                     preferred_element_type=f32) + count_scr[...]
    rank = jnp.sum(onehot * before, axis=-1, keepdims=True).astype(jnp.int32)
    route_ref[...] = jnp.where(lane == 0, grp, jnp.where(lane == 1, rank, 0))
    count_scr[...] += jnp.sum(onehot, axis=0, keepdims=True)
    count_ref[...] = count_scr[...].astype(jnp.int32)


def _out_proj(fo, mo, x2, gf, gm, w_out, g_ffn, w_r, b_r):
    T, D = x2.shape
    Wf = fo.shape[1]
    Wm = mo.shape[1]
    tm = _pick(T, 512)
    const = lambda i: (0, 0)
    rows = lambda i: (i, 0)
    return pl.pallas_call(
        _out_proj_kernel,
        grid=(T // tm,),
        in_specs=[
            pl.BlockSpec((tm, Wf), rows),
            pl.BlockSpec((tm, Wm), rows),
            pl.BlockSpec((tm, D), rows),
            pl.BlockSpec((1, Wf), const),
            pl.BlockSpec((1, Wm), const),
            pl.BlockSpec((Wf + Wm, D), const),
            pl.BlockSpec((1, D), const),
            pl.BlockSpec((D, ROUTER_COLS), const),
            pl.BlockSpec((1, ROUTER_COLS), const),
        ],
        out_specs=[
            pl.BlockSpec((tm, D), rows),
            pl.BlockSpec((tm, ROUTER_COLS), rows),
            pl.BlockSpec((1, ROUTER_COLS), const),
        ],
        out_shape=[
            jax.ShapeDtypeStruct((T, D), f32),
            jax.ShapeDtypeStruct((T, ROUTER_COLS), jnp.int32),
            jax.ShapeDtypeStruct((1, ROUTER_COLS), jnp.int32),
        ],
        scratch_shapes=[pltpu.VMEM((1, ROUTER_COLS), f32)],
        compiler_params=_params("arbitrary"),
        name="out_proj",
    )(fo, mo, x2, gf, gm, w_out, g_ffn, w_r, b_r)


def _slot(grp_ref, rank_ref, base_ref, t):
    return base_ref[grp_ref[t]] + rank_ref[t]


def _dispatch_kernel(grp_ref, rank_ref, base_ref, fill_ref, x_ref, xs_ref, zero_scr, sem, zsem):
    i = pl.program_id(0)
    tm = x_ref.shape[0]

    zrows = zero_scr.shape[0]

    def row_copy(r):
        d = _slot(grp_ref, rank_ref, base_ref, i * tm + r)
        return pltpu.make_async_copy(x_ref.at[pl.ds(r, 1)], xs_ref.at[pl.ds(d, 1)], sem)

    def zero_row(r):
        return pltpu.make_async_copy(zero_scr.at[pl.ds(0, 1)], xs_ref.at[pl.ds(r, 1)], zsem)

    def zero_chunk(c):
        return pltpu.make_async_copy(zero_scr, xs_ref.at[pl.ds(pl.multiple_of(c * zrows, zrows), zrows)], zsem)

    def start(copy):
        def body(r, carry):
            copy(r).start()
            return carry
        return body

    def wait(copy):
        def body(r, carry):
            copy(r).wait()
            return carry
        return body

    lax.fori_loop(0, tm, start(row_copy), 0, unroll=DMA_UNROLL)

    @pl.when(i == 0)
    def _():
        zero_scr[...] = jnp.zeros_like(zero_scr)
        for g in range(N_GROUPS):
            lax.fori_loop(fill_ref[2 * g], fill_ref[2 * g + 1], start(zero_row), 0)
            lax.fori_loop(fill_ref[2 * g], fill_ref[2 * g + 1], wait(zero_row), 0)
        lo, hi = fill_ref[2 * N_GROUPS] // zrows, fill_ref[2 * N_GROUPS + 1] // zrows
        lax.fori_loop(lo, hi, start(zero_chunk), 0)
        lax.fori_loop(lo, hi, wait(zero_chunk), 0)

    pltpu.make_async_copy(x_ref, xs_ref.at[pl.ds(0, tm)], sem).wait()


def _dispatch(x1, grp, rank, base, fill, rows_out):
    T, D = x1.shape
    tm = _pick(T, 512)
    grid_spec = pltpu.PrefetchScalarGridSpec(
        num_scalar_prefetch=4,
        grid=(T // tm,),
        in_specs=[pl.BlockSpec((tm, D), lambda i, *_: (i, 0))],
        out_specs=pl.BlockSpec(memory_space=pl.ANY),
        scratch_shapes=[pltpu.VMEM((ZERO_ROWS, D), f32), pltpu.SemaphoreType.DMA, pltpu.SemaphoreType.DMA],
    )
    return pl.pallas_call(
        _dispatch_kernel,
        grid_spec=grid_spec,
        out_shape=jax.ShapeDtypeStruct((rows_out, D), f32),
        compiler_params=_params("arbitrary"),
        name="moe_dispatch",
    )(grp, rank, base, fill, x1)


def _collect_kernel(grp_ref, rank_ref, base_ref, ys_ref, o_ref, sem):
    i = pl.program_id(0)
    tm = o_ref.shape[0]

    def row_copy(r):
        d = _slot(grp_ref, rank_ref, base_ref, i * tm + r)
        return pltpu.make_async_copy(ys_ref.at[pl.ds(d, 1)], o_ref.at[pl.ds(r, 1)], sem)

    def start(r, carry):
        row_copy(r).start()
        return carry

    lax.fori_loop(0, tm, start, 0, unroll=DMA_UNROLL)
    pltpu.make_async_copy(ys_ref.at[pl.ds(0, tm)], o_ref, sem).wait()


def _collect(ys, grp, rank, base, T):
    D = ys.shape[1]
    tm = _pick(T, 512)
    grid_spec = pltpu.PrefetchScalarGridSpec(
        num_scalar_prefetch=3,
        grid=(T // tm,),
        in_specs=[pl.BlockSpec(memory_space=pl.ANY)],
        out_specs=pl.BlockSpec((tm, D), lambda i, *_: (i, 0)),
        scratch_shapes=[pltpu.SemaphoreType.DMA],
    )
    return pl.pallas_call(
        _collect_kernel,
        grid_spec=grid_spec,
        out_shape=jax.ShapeDtypeStruct((T, D), f32),
        compiler_params=_params("arbitrary"),
        name="moe_collect",
    )(grp, rank, base, ys)


def _moe_kernel(tg_ref, nu_ref, xs_ref, gffn_ref, wr_ref, br_ref, wg_ref, wu_ref, wd_ref, gfin_ref, o_ref):
    i = pl.program_id(0)

    @pl.when(i < nu_ref[0])
    def _():
        g = tg_ref[i]
        x = xs_ref[...]
        h = _rms(x, gffn_ref[...]).astype(bf16)
        logits = jnp.dot(h, wr_ref[...], preferred_element_type=f32) + br_ref[...]
        lane = lax.broadcasted_iota(jnp.int32, logits.shape, 1)
        _, pg, pg_sum = _top_group(logits, lane)
        p_top_group = jnp.sum(jnp.where(lane == g, pg, 0.0), axis=-1, keepdims=True) / pg_sum
        lo = N_GROUPS + g * EXPERTS_PER_GROUP
        in_grp = jnp.logical_and(lane >= lo, lane < lo + EXPERTS_PER_GROUP)
        el = jnp.where(in_grp, logits, -jnp.inf)
        pe = jnp.exp(el - jnp.max(el, axis=-1, keepdims=True))
        p_exp = jnp.where(in_grp, pe / jnp.sum(pe, axis=-1, keepdims=True), -1.0)
        p1, i1 = _first_max(p_exp, lane)
        p2, i2 = _first_max(jnp.where(lane == i1, -1.0, p_exp), lane)
        tot = p1 + p2
        combine = jnp.where(lane == i1, p1 / tot, jnp.where(lane == i2, p2 / tot, 0.0)) * p_top_group
        y = jnp.zeros(x.shape, f32)
        for e in range(EXPERTS_PER_GROUP):
            c = jnp.sum(jnp.where(lane == lo + e, combine, 0.0), axis=-1, keepdims=True)
            gate = jnp.dot(h, wg_ref[e], preferred_element_type=f32)
            up = jnp.dot(h, wu_ref[e], preferred_element_type=f32)
            hid = (jax.nn.silu(gate) * up * c).astype(bf16)
            y = y + jnp.dot(hid, wd_ref[e], preferred_element_type=f32)
        o_ref[...] = _rms(x + y, gfin_ref[...])

    @pl.when(i >= nu_ref[0])
    def _():
        o_ref[...] = jnp.zeros_like(o_ref)


def _moe(xs, tile_group, n_used, g_ffn, w_r, b_r, w_gate, w_up, w_down, g_final, ts):
    R, D = xs.shape
    E, _, F = w_gate.shape
    G = EXPERTS_PER_GROUP
    const = lambda i, tg, nu: (0, 0)
    rows = lambda i, tg, nu: (jnp.minimum(i, nu[0] - 1), 0)
    experts = lambda i, tg, nu: (tg[i], 0, 0)
    once = pl.Buffered(1)
    grid_spec = pltpu.PrefetchScalarGridSpec(
        num_scalar_prefetch=2,
        grid=(R // ts,),
        in_specs=[
            pl.BlockSpec((ts, D), rows),
            pl.BlockSpec((1, D), const),
            pl.BlockSpec((D, ROUTER_COLS), const),
            pl.BlockSpec((1, ROUTER_COLS), const),
            pl.BlockSpec((G, D, F), experts, pipeline_mode=once),
            pl.BlockSpec((G, D, F), experts, pipeline_mode=once),
            pl.BlockSpec((G, F, D), experts, pipeline_mode=once),
            pl.BlockSpec((1, D), const),
        ],
        out_specs=pl.BlockSpec((ts, D), lambda i, tg, nu: (i, 0)),
    )
    return pl.pallas_call(
        _moe_kernel,
        grid_spec=grid_spec,
        out_shape=jax.ShapeDtypeStruct((R, D), f32),
        compiler_params=_params("arbitrary"),
        name="moe",
    )(tile_group, n_used, xs, g_ffn, w_r, b_r, w_gate, w_up, w_down, g_final)


def kernel(x, attn_norm_g, w_in, b_forget, fox_out_norm_g, moba_out_norm_g, rel_bias, w_out, ffn_norm_g,
           w_group_router, b_group_router, w_expert_router, b_expert_router, w_gate, w_up, w_down,
           final_norm_g):
    B, S, D = x.shape
    T = B * S
    depth = w_in.shape[0]
    fox_w = N_FOX_HEADS * HEAD_DIM
    moba_w = N_MOBA_HEADS * HEAD_DIM
    qkv_w = 3 * (fox_w + moba_w)
    assert w_in.shape[2] == qkv_w + N_FOX_HEADS
    assert N_GROUPS + N_EXPERTS <= ROUTER_COLS

    x2 = x.reshape(T, D)
    out = None
    for l in range(depth):
        w_qkv = w_in[l, :, :qkv_w].astype(bf16)
        w_fg_t = w_in[l, :, qkv_w:].T.astype(bf16)
        assert fox_w == moba_w
        col_scale = np.ones((qkv_w // fox_w,), np.float32)
        col_scale[[0, 3]] = HEAD_DIM ** -0.5 * LOG2E
        qkv, f_t = _in_proj(x2, attn_norm_g[l].reshape(1, D), w_qkv, w_fg_t, jnp.asarray(col_scale), fox_w)
        negc = _fox_decay(f_t, b_forget[l].reshape(N_FOX_HEADS, 1), B, S)
        fo = _fox_attn(qkv, negc, B, S, 0)
        mo = _moba_attn(qkv, rel_bias, B, S, 3 * N_FOX_HEADS)

        w_r = jnp.concatenate([w_group_router[l], w_expert_router[l].reshape(D, N_EXPERTS)], axis=1)
        w_r = jnp.pad(w_r, ((0, 0), (0, ROUTER_COLS - w_r.shape[1]))).astype(bf16)
        b_r = jnp.concatenate([b_group_router[l], b_expert_router[l].reshape(N_EXPERTS)])
        b_r = jnp.pad(b_r, (0, ROUTER_COLS - b_r.shape[0])).reshape(1, ROUTER_COLS)
        g_ffn = ffn_norm_g[l].reshape(1, D)
        x1, route, count = _out_proj(fo, mo, x2, fox_out_norm_g[l].reshape(1, fox_w),
                                     moba_out_norm_g[l].reshape(1, moba_w), w_out[l].astype(bf16),
                                     g_ffn, w_r, b_r)
        ts = MOE_TILE
        grp, rank = route[:, 0], route[:, 1]
        counts = count[0, :N_GROUPS]
        padded = (counts + ts - 1) // ts * ts
        ends = jnp.cumsum(padded)
        base = ends - padded
        rows_out = T + N_GROUPS * ts
        fill = jnp.concatenate([jnp.stack([base + counts, ends], axis=1).reshape(-1),
                                jnp.stack([ends[-1], jnp.asarray(rows_out, ends.dtype)])])
        tile_start = jnp.arange(rows_out // ts, dtype=jnp.int32) * ts
        tile_group = jnp.minimum(jnp.sum(tile_start[:, None] >= ends[None, :], axis=1), N_GROUPS - 1)
        n_used = (ends[-1:] // ts).astype(jnp.int32)

        assert l == depth - 1, "the fused residual + final norm epilogue expects a single layer"
        xs = _dispatch(x1, grp, rank, base, fill, rows_out)
        ys = _moe(xs, tile_group.astype(jnp.int32), n_used, g_ffn, w_r, b_r, w_gate[l].astype(bf16),
                  w_up[l].astype(bf16), w_down[l].astype(bf16), final_norm_g.reshape(1, D), ts)
        out = _collect(ys, grp, rank, base, T)
        x2 = out
    return out.reshape(B, S, D)
```

```python
import functools
import math

import numpy as np
import jax
import jax.numpy as jnp
from jax import lax
from jax.experimental import pallas as pl
from jax.experimental.pallas import tpu as pltpu

HEAD_DIM = 128
N_FOX_HEADS = 8
N_MOBA_HEADS = 8
MOBA_BLOCK = 256
MOBA_TOPK = 3
REL_BUCKETS = 32
REL_MAX_DIST = 128
N_GROUPS = 4
EXPERTS_PER_GROUP = 4
N_EXPERTS = N_GROUPS * EXPERTS_PER_GROUP
EPS = 1e-6
NEG = -1e30
LOG2E = math.log2(math.e)
LANES = 128
ROUTER_COLS = LANES
MOE_TILE = 512
KEY_BIAS_PARTS = 3
ZERO_ROWS = 64
DMA_UNROLL = 8
VMEM_LIMIT = 56 * 1024 * 1024

f32 = jnp.float32
bf16 = jnp.bfloat16


def _params(*sem):
    return pltpu.CompilerParams(dimension_semantics=sem, vmem_limit_bytes=VMEM_LIMIT)


def _pick(n, pref):
    t = min(n, pref)
    assert n % t == 0, (n, pref)
    return t


def _in_proj_kernel(cs_ref, x_ref, g_ref, w_ref, wfg_ref, qkv_ref, f_ref, h_scr):
    @pl.when(pl.program_id(1) == 0)
    def _():
        x = x_ref[...]
        y = x * lax.rsqrt(jnp.mean(x * x, axis=-1, keepdims=True) + EPS)
        hb = (y * g_ref[...]).astype(bf16)
        h_scr[...] = hb
        f_ref[...] = lax.dot_general(wfg_ref[...], hb, (((1,), (1,)), ((), ())),
                                     preferred_element_type=f32)

    acc = jnp.dot(h_scr[...], w_ref[...], preferred_element_type=f32) * cs_ref[pl.program_id(1)]
    for c in range(acc.shape[1] // HEAD_DIM):
        qkv_ref[c] = acc[:, c * HEAD_DIM:(c + 1) * HEAD_DIM].astype(bf16)


def _in_proj(x2, g, w_qkv, w_fg_t, col_scale, tn):
    T, D = x2.shape
    N = w_qkv.shape[1]
    tm = _pick(T, 1024)
    assert N % tn == 0 and col_scale.shape == (N // tn,)
    nh = w_fg_t.shape[0]
    return pl.pallas_call(
        _in_proj_kernel,
        grid=(T // tm, N // tn),
        in_specs=[
            pl.BlockSpec(memory_space=pltpu.SMEM),
            pl.BlockSpec((tm, D), lambda i, j: (i, 0)),
            pl.BlockSpec((1, D), lambda i, j: (0, 0)),
            pl.BlockSpec((D, tn), lambda i, j: (0, j)),
            pl.BlockSpec((nh, D), lambda i, j: (0, 0)),
        ],
        out_specs=[
            pl.BlockSpec((tn // HEAD_DIM, tm, HEAD_DIM), lambda i, j: (j, i, 0)),
            pl.BlockSpec((nh, tm), lambda i, j: (0, i)),
        ],
        out_shape=[
            jax.ShapeDtypeStruct((N // HEAD_DIM, T, HEAD_DIM), bf16),
            jax.ShapeDtypeStruct((nh, T), f32),
        ],
        scratch_shapes=[pltpu.VMEM((tm, D), bf16)],
        compiler_params=_params("arbitrary", "arbitrary"),
        name="in_proj",
    )(col_scale, x2, g, w_qkv, w_fg_t)


def _fox_decay_kernel(f_ref, b_ref, o_ref):
    lf = jax.nn.log_sigmoid(f_ref[...] + b_ref[...])
    S = lf.shape[1]
    lane = lax.broadcasted_iota(jnp.int32, lf.shape, 1)
    c = lf
    sh = 1
    while sh < S:
        c = c + jnp.where(lane >= sh, pltpu.roll(c, sh, axis=1), 0.0)
        sh *= 2
    rest = -c * LOG2E
    parts = []
    for _ in range(KEY_BIAS_PARTS):
        part = rest.astype(bf16).astype(f32)
        parts.append(part)
        rest = rest - part
    nh = lf.shape[0]
    cols = jnp.concatenate(parts + [jnp.zeros((LANES - KEY_BIAS_PARTS * nh, S), f32)], axis=0)
    o_ref[0] = jnp.transpose(cols).astype(bf16)


def _fox_decay(f_t, b_forget, B, S):
    nh = f_t.shape[0]
    assert KEY_BIAS_PARTS * nh <= LANES
    return pl.pallas_call(
        _fox_decay_kernel,
        grid=(B,),
        in_specs=[
            pl.BlockSpec((nh, S), lambda b: (0, b)),
            pl.BlockSpec((nh, 1), lambda b: (0, 0)),
        ],
        out_specs=pl.BlockSpec((1, S, LANES), lambda b: (b, 0, 0)),
        out_shape=jax.ShapeDtypeStruct((B, S, LANES), bf16),
        compiler_params=_params("arbitrary"),
        name="fox_decay",
    )(f_t, b_forget)


def _qk(q, k):
    return lax.dot_general(q, k, (((1,), (1,)), ((), ())), preferred_element_type=f32)


def _softmax_init(t):
    return jnp.full((t, 1), NEG, f32), jnp.zeros((t, 2 * HEAD_DIM), f32)


def _softmax_step2(carry, s2, v_ones):
    m, acc = carry
    m_new = jnp.maximum(m, jnp.max(s2, axis=-1, keepdims=True))
    p = jnp.exp2(s2 - m_new).astype(bf16)
    acc = jnp.exp2(m - m_new) * acc + jnp.dot(p, v_ones, preferred_element_type=f32)
    return m_new, acc


def _softmax_result(carry):
    _, acc = carry
    return acc[:, :HEAD_DIM] / acc[:, HEAD_DIM:]


def _with_ones(vo_scr, v_ref):
    vo_scr[:, :HEAD_DIM] = v_ref[0]
    vo_scr[:, HEAD_DIM:] = jnp.ones((vo_scr.shape[0], HEAD_DIM), vo_scr.dtype)


def _fox_attn_kernel(q_ref, k_ref, v_ref, kb_ref, o_ref, qaug_scr, kaug_scr, vo_scr, *, t):
    h = pl.program_id(1)
    S = q_ref.shape[1]
    lane = lax.broadcasted_iota(jnp.int32, (S, LANES), 1)
    mine = jnp.logical_and(lane % N_FOX_HEADS == h, lane < KEY_BIAS_PARTS * N_FOX_HEADS)
    qaug_scr[:, :HEAD_DIM] = q_ref[0]
    qaug_scr[:, HEAD_DIM:] = jnp.where(mine, 1.0, 0.0).astype(bf16)
    kaug_scr[:, :HEAD_DIM] = k_ref[0]
    kaug_scr[:, HEAD_DIM:] = kb_ref[0]
    _with_ones(vo_scr, v_ref)
    row = lax.broadcasted_iota(jnp.int32, (t, t), 0)
    col = lax.broadcasted_iota(jnp.int32, (t, t), 1)
    for qi in range(S // t):
        qa = qaug_scr[qi * t:(qi + 1) * t, :]
        carry = _softmax_init(t)
        for kb in range(qi + 1):
            keys = slice(kb * t, (kb + 1) * t)
            s2 = _qk(qa, kaug_scr[keys, :])
            if kb == qi:
                s2 = jnp.where(col <= row, s2, NEG)
            carry = _softmax_step2(carry, s2, vo_scr[keys, :])
        o_ref[qi * t:(qi + 1) * t, :] = _softmax_result(carry).astype(o_ref.dtype)


def _fox_attn(qkv, key_bias, B, S, slab0):
    T = B * S
    H = N_FOX_HEADS
    t = _pick(S, 512)
    kern = functools.partial(_fox_attn_kernel, t=t)
    return pl.pallas_call(
        kern,
        grid=(B, H),
        in_specs=[
            pl.BlockSpec((1, S, HEAD_DIM), lambda b, h: (slab0 + h, b, 0)),
            pl.BlockSpec((1, S, HEAD_DIM), lambda b, h: (slab0 + H + h, b, 0)),
            pl.BlockSpec((1, S, HEAD_DIM), lambda b, h: (slab0 + 2 * H + h, b, 0)),
            pl.BlockSpec((1, S, LANES), lambda b, h: (b, 0, 0)),
        ],
        out_specs=pl.BlockSpec((S, HEAD_DIM), lambda b, h: (b, h)),
        out_shape=jax.ShapeDtypeStruct((T, H * HEAD_DIM), bf16),
        scratch_shapes=[pltpu.VMEM((S, 2 * HEAD_DIM), bf16)] * 3,
        compiler_params=_params("arbitrary", "arbitrary"),
        name="fox_attn",
    )(qkv, qkv, qkv, key_bias)


def _rel_bucket_table(n):
    max_exact = REL_BUCKETS // 2
    d = np.arange(n)
    ratio = np.log(np.maximum(d, 1).astype(np.float32) / np.float32(max_exact)) / np.float32(
        math.log(REL_MAX_DIST / max_exact))
    large = max_exact + (ratio * np.float32(REL_BUCKETS - max_exact)).astype(np.int32)
    large = np.minimum(large, REL_BUCKETS - 1)
    return np.where(d < max_exact, d, large).astype(np.int32)


def _moba_attn_kernel(rel_ref, q_ref, k_ref, v_ref, avg_ref, hot_ref, bk_ref, o_ref, bias_scr, kaug_scr,
                      qaug_scr, vo_scr):
    h = pl.program_id(1)
    blk = MOBA_BLOCK
    t = 2 * blk
    S = q_ref.shape[1]
    far_bias = rel_ref[h, REL_BUCKETS - 1]
    row = lax.broadcasted_iota(jnp.int32, (blk, blk), 0)
    col = lax.broadcasted_iota(jnp.int32, (blk, blk), 1)
    tiles = []
    for which in range(2):
        bk = bk_ref[which]
        tile = jnp.zeros((blk, blk), f32)
        for b in range(REL_BUCKETS - 1):
            tile = jnp.where(bk == b, (rel_ref[h, b] - far_bias) * LOG2E, tile)
        tiles.append(jnp.where(col <= row, tile, NEG) if which == 0 else tile)
    own_tile, prev_tile = tiles
    zero_tile = jnp.zeros((blk, blk), f32)
    for which, quads in enumerate((((zero_tile, prev_tile), (zero_tile, zero_tile)),
                                   ((own_tile, zero_tile), (prev_tile, own_tile)))):
        for r in range(2):
            for c in range(2):
                bias_scr[which, r * blk:(r + 1) * blk, c * blk:(c + 1) * blk] = quads[r][c]
    kaug_scr[:, :HEAD_DIM] = k_ref[0]
    kaug_scr[:, HEAD_DIM:] = hot_ref[...]
    _with_ones(vo_scr, v_ref)
    kmean = jnp.dot(avg_ref[...], k_ref[0], preferred_element_type=f32).astype(bf16)

    lane = lax.broadcasted_iota(jnp.int32, (S, LANES), 1)
    own = lax.broadcasted_iota(jnp.int32, (S, LANES), 0) // blk
    valid = lane < own
    gate = jnp.where(valid, _qk(q_ref[0], kmean), NEG)
    sel = lane == own
    for _ in range(MOBA_TOPK):
        mx = jnp.max(gate, axis=-1, keepdims=True)
        first = jnp.min(jnp.where(gate == mx, lane, LANES), axis=-1, keepdims=True)
        pick = lane == first
        sel = jnp.logical_or(sel, jnp.logical_and(pick, valid))
        gate = jnp.where(pick, -jnp.inf, gate)
    qaug_scr[:, :HEAD_DIM] = q_ref[0]
    qaug_scr[:, HEAD_DIM:] = jnp.where(sel, 0.0, NEG).astype(bf16)

    for j in range(S // t):
        rows = slice(j * t, (j + 1) * t)
        qa = qaug_scr[rows, :]
        carry = _softmax_step2(_softmax_init(t), _qk(qa, kaug_scr[rows, :]) + bias_scr[1], vo_scr[rows, :])
        if j > 0:
            keys = slice((j - 1) * t, j * t)
            carry = _softmax_step2(carry, _qk(qa, kaug_scr[keys, :]) + bias_scr[0], vo_scr[keys, :])
        for c in range(j - 1):
            keys = slice(c * t, (c + 1) * t)
            carry = _softmax_step2(carry, _qk(qa, kaug_scr[keys, :]), vo_scr[keys, :])
        o_ref[rows, :] = _softmax_result(carry).astype(o_ref.dtype)


def _moba_attn(qkv, rel_bias, B, S, slab0):
    T = B * S
    H = N_MOBA_HEADS
    blk = MOBA_BLOCK
    assert S % (2 * blk) == 0 and S // blk <= LANES
    nb = S // blk
    avg = np.zeros((LANES, S), np.float32)
    for n in range(nb):
        avg[n, n * blk:(n + 1) * blk] = 1.0 / blk
    hot = (avg.T > 0).astype(np.float32)
    table = _rel_bucket_table(2 * blk)
    dist = np.arange(blk)[:, None] - np.arange(blk)[None, :]
    buckets = np.stack([table[np.maximum(dist, 0)], table[dist + blk]]).astype(np.int32)
    grid_spec = pltpu.PrefetchScalarGridSpec(
        num_scalar_prefetch=1,
        grid=(B, H),
        in_specs=[
            pl.BlockSpec((1, S, HEAD_DIM), lambda b, h, r: (slab0 + h, b, 0)),
            pl.BlockSpec((1, S, HEAD_DIM), lambda b, h, r: (slab0 + H + h, b, 0)),
            pl.BlockSpec((1, S, HEAD_DIM), lambda b, h, r: (slab0 + 2 * H + h, b, 0)),
            pl.BlockSpec((LANES, S), lambda b, h, r: (0, 0)),
            pl.BlockSpec((S, LANES), lambda b, h, r: (0, 0)),
            pl.BlockSpec((2, blk, blk), lambda b, h, r: (0, 0, 0)),
        ],
        out_specs=pl.BlockSpec((S, HEAD_DIM), lambda b, h, r: (b, h)),
        scratch_shapes=[pltpu.VMEM((2, 2 * blk, 2 * blk), f32)] + [pltpu.VMEM((S, 2 * HEAD_DIM), bf16)] * 3,
    )
    return pl.pallas_call(
        _moba_attn_kernel,
        grid_spec=grid_spec,
        out_shape=jax.ShapeDtypeStruct((T, H * HEAD_DIM), bf16),
        compiler_params=_params("arbitrary", "arbitrary"),
        name="moba_attn",
    )(rel_bias, qkv, qkv, qkv, jnp.asarray(avg, bf16), jnp.asarray(hot, bf16), jnp.asarray(buckets))


def _rms(x, g):
    return x * lax.rsqrt(jnp.mean(x * x, axis=-1, keepdims=True) + EPS) * g


def _first_max(x, lane):
    mx = jnp.max(x, axis=-1, keepdims=True)
    return mx, jnp.min(jnp.where(x == mx, lane, LANES), axis=-1, keepdims=True)


def _top_group(logits, lane):
    gl = jnp.where(lane < N_GROUPS, logits, -jnp.inf)
    gmax, grp = _first_max(gl, lane)
    pg = jnp.exp(gl - gmax)
    return grp, pg, jnp.sum(pg, axis=-1, keepdims=True)


def _out_proj_kernel(fo_ref, mo_ref, x_ref, gf_ref, gm_ref, w_ref, gffn_ref, wr_ref, br_ref,
                     x1_ref, route_ref, count_ref, count_scr):
    @pl.when(pl.program_id(0) == 0)
    def _():
        count_scr[...] = jnp.zeros_like(count_scr)

    fo = _rms(fo_ref[...].astype(f32), gf_ref[...]).astype(bf16)
    mo = _rms(mo_ref[...].astype(f32), gm_ref[...]).astype(bf16)
    mix = jnp.concatenate([fo, mo], axis=-1)
    x1 = x_ref[...] + jnp.dot(mix, w_ref[...], preferred_element_type=f32)
    x1_ref[...] = x1
    h2 = _rms(x1, gffn_ref[...]).astype(bf16)
    logits = jnp.dot(h2, wr_ref[...], preferred_element_type=f32) + br_ref[...]
    tm = logits.shape[0]
    lane = lax.broadcasted_iota(jnp.int32, logits.shape, 1)
    grp, _, _ = _top_group(logits, lane)
    onehot = jnp.where(lane == grp, 1.0, 0.0)
    earlier = lax.broadcasted_iota(jnp.int32, (tm, tm), 1) < lax.broadcasted_iota(jnp.int32, (tm, tm), 0)
    before = jnp.dot(jnp.where(earlier, 1.0, 0.0).astype(bf16), onehot.astype(bf16),
                     preferred_element_type=f32) + count_scr[...]
    rank = jnp.sum(onehot * before, axis=-1, keepdims=True).astype(jnp.int32)
    route_ref[...] = jnp.where(lane == 0, grp, jnp.where(lane == 1, rank, 0))
    count_scr[...] += jnp.sum(onehot, axis=0, keepdims=True)
    count_ref[...] = count_scr[...].astype(jnp.int32)


def _out_proj(fo, mo, x2, gf, gm, w_out, g_ffn, w_r, b_r):
    T, D = x2.shape
    Wf = fo.shape[1]
    Wm = mo.shape[1]
    tm = _pick(T, 512)
    const = lambda i: (0, 0)
    rows = lambda i: (i, 0)
    return pl.pallas_call(
        _out_proj_kernel,
        grid=(T // tm,),
        in_specs=[
            pl.BlockSpec((tm, Wf), rows),
            pl.BlockSpec((tm, Wm), rows),
            pl.BlockSpec((tm, D), rows),
            pl.BlockSpec((1, Wf), const),
            pl.BlockSpec((1, Wm), const),
            pl.BlockSpec((Wf + Wm, D), const),
            pl.BlockSpec((1, D), const),
            pl.BlockSpec((D, ROUTER_COLS), const),
            pl.BlockSpec((1, ROUTER_COLS), const),
        ],
        out_specs=[
            pl.BlockSpec((tm, D), rows),
            pl.BlockSpec((tm, ROUTER_COLS), rows),
            pl.BlockSpec((1, ROUTER_COLS), const),
        ],
        out_shape=[
            jax.ShapeDtypeStruct((T, D), f32),
            jax.ShapeDtypeStruct((T, ROUTER_COLS), jnp.int32),
            jax.ShapeDtypeStruct((1, ROUTER_COLS), jnp.int32),
        ],
        scratch_shapes=[pltpu.VMEM((1, ROUTER_COLS), f32)],
        compiler_params=_params("arbitrary"),
        name="out_proj",
    )(fo, mo, x2, gf, gm, w_out, g_ffn, w_r, b_r)


def _slot(grp_ref, rank_ref, base_ref, t):
    return base_ref[grp_ref[t]] + rank_ref[t]


def _dispatch_kernel(grp_ref, rank_ref, base_ref, fill_ref, x_ref, xs_ref, zero_scr, sem, zsem):
    i = pl.program_id(0)
    tm = x_ref.shape[0]

    zrows = zero_scr.shape[0]

    def row_copy(r):
        d = _slot(grp_ref, rank_ref, base_ref, i * tm + r)
        return pltpu.make_async_copy(x_ref.at[pl.ds(r, 1)], xs_ref.at[pl.ds(d, 1)], sem)

    def zero_row(r):
        return pltpu.make_async_copy(zero_scr.at[pl.ds(0, 1)], xs_ref.at[pl.ds(r, 1)], zsem)

    def zero_chunk(c):
        return pltpu.make_async_copy(zero_scr, xs_ref.at[pl.ds(pl.multiple_of(c * zrows, zrows), zrows)], zsem)

    def start(copy):
        def body(r, carry):
            copy(r).start()
            return carry
        return body

    def wait(copy):
        def body(r, carry):
            copy(r).wait()
            return carry
        return body

    lax.fori_loop(0, tm, start(row_copy), 0, unroll=DMA_UNROLL)

    @pl.when(i == 0)
    def _():
        zero_scr[...] = jnp.zeros_like(zero_scr)
        for g in range(N_GROUPS):
            lax.fori_loop(fill_ref[2 * g], fill_ref[2 * g + 1], start(zero_row), 0)
            lax.fori_loop(fill_ref[2 * g], fill_ref[2 * g + 1], wait(zero_row), 0)
        lo, hi = fill_ref[2 * N_GROUPS] // zrows, fill_ref[2 * N_GROUPS + 1] // zrows
        lax.fori_loop(lo, hi, start(zero_chunk), 0)
        lax.fori_loop(lo, hi, wait(zero_chunk), 0)

    pltpu.make_async_copy(x_ref, xs_ref.at[pl.ds(0, tm)], sem).wait()


def _dispatch(x1, grp, rank, base, fill, rows_out):
    T, D = x1.shape
    tm = _pick(T, 512)
    grid_spec = pltpu.PrefetchScalarGridSpec(
        num_scalar_prefetch=4,
        grid=(T // tm,),
        in_specs=[pl.BlockSpec((tm, D), lambda i, *_: (i, 0))],
        out_specs=pl.BlockSpec(memory_space=pl.ANY),
        scratch_shapes=[pltpu.VMEM((ZERO_ROWS, D), f32), pltpu.SemaphoreType.DMA, pltpu.SemaphoreType.DMA],
    )
    return pl.pallas_call(
        _dispatch_kernel,
        grid_spec=grid_spec,
        out_shape=jax.ShapeDtypeStruct((rows_out, D), f32),
        compiler_params=_params("arbitrary"),
        name="moe_dispatch",
    )(grp, rank, base, fill, x1)


def _collect_kernel(grp_ref, rank_ref, base_ref, ys_ref, o_ref, sem):
    i = pl.program_id(0)
    tm = o_ref.shape[0]

    def row_copy(r):
        d = _slot(grp_ref, rank_ref, base_ref, i * tm + r)
        return pltpu.make_async_copy(ys_ref.at[pl.ds(d, 1)], o_ref.at[pl.ds(r, 1)], sem)

    def start(r, carry):
        row_copy(r).start()
        return carry

    lax.fori_loop(0, tm, start, 0, unroll=DMA_UNROLL)
    pltpu.make_async_copy(ys_ref.at[pl.ds(0, tm)], o_ref, sem).wait()


def _collect(ys, grp, rank, base, T):
    D = ys.shape[1]
    tm = _pick(T, 512)
    grid_spec = pltpu.PrefetchScalarGridSpec(
        num_scalar_prefetch=3,
        grid=(T // tm,),
        in_specs=[pl.BlockSpec(memory_space=pl.ANY)],
        out_specs=pl.BlockSpec((tm, D), lambda i, *_: (i, 0)),
        scratch_shapes=[pltpu.SemaphoreType.DMA],
    )
    return pl.pallas_call(
        _collect_kernel,
        grid_spec=grid_spec,
        out_shape=jax.ShapeDtypeStruct((T, D), f32),
        compiler_params=_params("arbitrary"),
        name="moe_collect",
    )(grp, rank, base, ys)


def _moe_kernel(tg_ref, nu_ref, xs_ref, gffn_ref, wr_ref, br_ref, wg_ref, wu_ref, wd_ref, gfin_ref, o_ref):
    i = pl.program_id(0)

    @pl.when(i < nu_ref[0])
    def _():
        g = tg_ref[i]
        x = xs_ref[...]
        h = _rms(x, gffn_ref[...]).astype(bf16)
        logits = jnp.dot(h, wr_ref[...], preferred_element_type=f32) + br_ref[...]
        lane = lax.broadcasted_iota(jnp.int32, logits.shape, 1)
        _, pg, pg_sum = _top_group(logits, lane)
        p_top_group = jnp.sum(jnp.where(lane == g, pg, 0.0), axis=-1, keepdims=True) / pg_sum
        lo = N_GROUPS + g * EXPERTS_PER_GROUP
        in_grp = jnp.logical_and(lane >= lo, lane < lo + EXPERTS_PER_GROUP)
        el = jnp.where(in_grp, logits, -jnp.inf)
        pe = jnp.exp(el - jnp.max(el, axis=-1, keepdims=True))
        p_exp = jnp.where(in_grp, pe / jnp.sum(pe, axis=-1, keepdims=True), -1.0)
        p1, i1 = _first_max(p_exp, lane)
        p2, i2 = _first_max(jnp.where(lane == i1, -1.0, p_exp), lane)
        tot = p1 + p2
        combine = jnp.where(lane == i1, p1 / tot, jnp.where(lane == i2, p2 / tot, 0.0)) * p_top_group
        y = jnp.zeros(x.shape, f32)
        for e in range(EXPERTS_PER_GROUP):
            c = jnp.sum(jnp.where(lane == lo + e, combine, 0.0), axis=-1, keepdims=True)
            gate = jnp.dot(h, wg_ref[e], preferred_element_type=f32)
            up = jnp.dot(h, wu_ref[e], preferred_element_type=f32)
            hid = (jax.nn.silu(gate) * up * c).astype(bf16)
            y = y + jnp.dot(hid, wd_ref[e], preferred_element_type=f32)
        o_ref[...] = _rms(x + y, gfin_ref[...])

    @pl.when(i >= nu_ref[0])
    def _():
        o_ref[...] = jnp.zeros_like(o_ref)


def _moe(xs, tile_group, n_used, g_ffn, w_r, b_r, w_gate, w_up, w_down, g_final, ts):
    R, D = xs.shape
    E, _, F = w_gate.shape
    G = EXPERTS_PER_GROUP
    const = lambda i, tg, nu: (0, 0)
    rows = lambda i, tg, nu: (jnp.minimum(i, nu[0] - 1), 0)
    experts = lambda i, tg, nu: (tg[i], 0, 0)
    once = pl.Buffered(1)
    grid_spec = pltpu.PrefetchScalarGridSpec(
        num_scalar_prefetch=2,
        grid=(R // ts,),
        in_specs=[
            pl.BlockSpec((ts, D), rows),
            pl.BlockSpec((1, D), const),
            pl.BlockSpec((D, ROUTER_COLS), const),
            pl.BlockSpec((1, ROUTER_COLS), const),
            pl.BlockSpec((G, D, F), experts, pipeline_mode=once),
            pl.BlockSpec((G, D, F), experts, pipeline_mode=once),
            pl.BlockSpec((G, F, D), experts, pipeline_mode=once),
            pl.BlockSpec((1, D), const),
        ],
        out_specs=pl.BlockSpec((ts, D), lambda i, tg, nu: (i, 0)),
    )
    return pl.pallas_call(
        _moe_kernel,
        grid_spec=grid_spec,
        out_shape=jax.ShapeDtypeStruct((R, D), f32),
        compiler_params=_params("arbitrary"),
        name="moe",
    )(tile_group, n_used, xs, g_ffn, w_r, b_r, w_gate, w_up, w_down, g_final)


def kernel(x, attn_norm_g, w_in, b_forget, fox_out_norm_g, moba_out_norm_g, rel_bias, w_out, ffn_norm_g,
           w_group_router, b_group_router, w_expert_router, b_expert_router, w_gate, w_up, w_down,
           final_norm_g):
    B, S, D = x.shape
    T = B * S
    depth = w_in.shape[0]
    fox_w = N_FOX_HEADS * HEAD_DIM
    moba_w = N_MOBA_HEADS * HEAD_DIM
    qkv_w = 3 * (fox_w + moba_w)
    assert w_in.shape[2] == qkv_w + N_FOX_HEADS
    assert N_GROUPS + N_EXPERTS <= ROUTER_COLS

    x2 = x.reshape(T, D)
    out = None
    for l in range(depth):
        w_qkv = w_in[l, :, :qkv_w].astype(bf16)
        w_fg_t = w_in[l, :, qkv_w:].T.astype(bf16)
        assert fox_w == moba_w
        col_scale = np.ones((qkv_w // fox_w,), np.float32)
        col_scale[[0, 3]] = HEAD_DIM ** -0.5 * LOG2E
        qkv, f_t = _in_proj(x2, attn_norm_g[l].reshape(1, D), w_qkv, w_fg_t, jnp.asarray(col_scale), fox_w)
        key_bias = _fox_decay(f_t, b_forget[l].reshape(N_FOX_HEADS, 1), B, S)
        fo = _fox_attn(qkv, key_bias, B, S, 0)
        mo = _moba_attn(qkv, rel_bias, B, S, 3 * N_FOX_HEADS)

        w_r = jnp.concatenate([w_group_router[l], w_expert_router[l].reshape(D, N_EXPERTS)], axis=1)
        w_r = jnp.pad(w_r, ((0, 0), (0, ROUTER_COLS - w_r.shape[1]))).astype(bf16)
        b_r = jnp.concatenate([b_group_router[l], b_expert_router[l].reshape(N_EXPERTS)])
        b_r = jnp.pad(b_r, (0, ROUTER_COLS - b_r.shape[0])).reshape(1, ROUTER_COLS)
        g_ffn = ffn_norm_g[l].reshape(1, D)
        x1, route, count = _out_proj(fo, mo, x2, fox_out_norm_g[l].reshape(1, fox_w),
                                     moba_out_norm_g[l].reshape(1, moba_w), w_out[l].astype(bf16),
                                     g_ffn, w_r, b_r)
        ts = MOE_TILE
        grp, rank = route[:, 0], route[:, 1]
        counts = count[0, :N_GROUPS]
        padded = (counts + ts - 1) // ts * ts
        ends = jnp.cumsum(padded)
        base = ends - padded
        rows_out = T + N_GROUPS * ts
        fill = jnp.concatenate([jnp.stack([base + counts, ends], axis=1).reshape(-1),
                                jnp.stack([ends[-1], jnp.asarray(rows_out, ends.dtype)])])
        tile_start = jnp.arange(rows_out // ts, dtype=jnp.int32) * ts
        tile_group = jnp.minimum(jnp.sum(tile_start[:, None] >= ends[None, :], axis=1), N_GROUPS - 1)
        n_used = (ends[-1:] // ts).astype(jnp.int32)

        assert l == depth - 1, "the fused residual + final norm epilogue expects a single layer"
        xs = _dispatch(x1, grp, rank, base, fill, rows_out)
        ys = _moe(xs, tile_group.astype(jnp.int32), n_used, g_ffn, w_r, b_r, w_gate[l].astype(bf16),
                  w_up[l].astype(bf16), w_down[l].astype(bf16), final_norm_g.reshape(1, D), ts)
        out = _collect(ys, grp, rank, base, T)
        x2 = out
    return out.reshape(B, S, D)
```

```python
import functools
import math

import numpy as np
import jax
import jax.numpy as jnp
from jax import lax
from jax.experimental import pallas as pl
from jax.experimental.pallas import tpu as pltpu

HEAD_DIM = 128
N_FOX_HEADS = 8
N_MOBA_HEADS = 8
MOBA_BLOCK = 256
MOBA_TOPK = 3
REL_BUCKETS = 32
REL_MAX_DIST = 128
N_GROUPS = 4
EXPERTS_PER_GROUP = 4
N_EXPERTS = N_GROUPS * EXPERTS_PER_GROUP
EXPERT_PAIRS = EXPERTS_PER_GROUP * (EXPERTS_PER_GROUP - 1) // 2
N_CLASSES = N_GROUPS * EXPERT_PAIRS
EPS = 1e-6
NEG = -1e30
LOG2E = math.log2(math.e)
LANES = 128
ROUTER_COLS = LANES
MOE_TILE = 512
KEY_BIAS_PARTS = 3
ZERO_ROWS = 64
DMA_UNROLL = 8
VMEM_LIMIT = 56 * 1024 * 1024

f32 = jnp.float32
bf16 = jnp.bfloat16


def _params(*sem):
    return pltpu.CompilerParams(dimension_semantics=sem, vmem_limit_bytes=VMEM_LIMIT)


def _pick(n, pref):
    t = min(n, pref)
    assert n % t == 0, (n, pref)
    return t


def _in_proj_kernel(cs_ref, x_ref, g_ref, w_ref, wfg_ref, qkv_ref, f_ref, h_scr):
    @pl.when(pl.program_id(1) == 0)
    def _():
        x = x_ref[...]
        y = x * lax.rsqrt(jnp.mean(x * x, axis=-1, keepdims=True) + EPS)
        hb = (y * g_ref[...]).astype(bf16)
        h_scr[...] = hb
        f_ref[...] = lax.dot_general(wfg_ref[...], hb, (((1,), (1,)), ((), ())),
                                     preferred_element_type=f32)

    acc = jnp.dot(h_scr[...], w_ref[...], preferred_element_type=f32) * cs_ref[pl.program_id(1)]
    for c in range(acc.shape[1] // HEAD_DIM):
        qkv_ref[c] = acc[:, c * HEAD_DIM:(c + 1) * HEAD_DIM].astype(bf16)


def _in_proj(x2, g, w_qkv, w_fg_t, col_scale, tn):
    T, D = x2.shape
    N = w_qkv.shape[1]
    tm = _pick(T, 1024)
    assert N % tn == 0 and col_scale.shape == (N // tn,)
    nh = w_fg_t.shape[0]
    return pl.pallas_call(
        _in_proj_kernel,
        grid=(T // tm, N // tn),
        in_specs=[
            pl.BlockSpec(memory_space=pltpu.SMEM),
            pl.BlockSpec((tm, D), lambda i, j: (i, 0)),
            pl.BlockSpec((1, D), lambda i, j: (0, 0)),
            pl.BlockSpec((D, tn), lambda i, j: (0, j)),
            pl.BlockSpec((nh, D), lambda i, j: (0, 0)),
        ],
        out_specs=[
            pl.BlockSpec((tn // HEAD_DIM, tm, HEAD_DIM), lambda i, j: (j, i, 0)),
            pl.BlockSpec((nh, tm), lambda i, j: (0, i)),
        ],
        out_shape=[
            jax.ShapeDtypeStruct((N // HEAD_DIM, T, HEAD_DIM), bf16),
            jax.ShapeDtypeStruct((nh, T), f32),
        ],
        scratch_shapes=[pltpu.VMEM((tm, D), bf16)],
        compiler_params=_params("arbitrary", "arbitrary"),
        name="in_proj",
    )(col_scale, x2, g, w_qkv, w_fg_t)


def _fox_decay_kernel(f_ref, b_ref, o_ref):
    lf = jax.nn.log_sigmoid(f_ref[...] + b_ref[...])
    S = lf.shape[1]
    lane = lax.broadcasted_iota(jnp.int32, lf.shape, 1)
    c = lf
    sh = 1
    while sh < S:
        c = c + jnp.where(lane >= sh, pltpu.roll(c, sh, axis=1), 0.0)
        sh *= 2
    rest = -c * LOG2E
    parts = []
    for _ in range(KEY_BIAS_PARTS):
        part = rest.astype(bf16).astype(f32)
        parts.append(part)
        rest = rest - part
    nh = lf.shape[0]
    cols = jnp.concatenate(parts + [jnp.zeros((LANES - KEY_BIAS_PARTS * nh, S), f32)], axis=0)
    o_ref[0] = jnp.transpose(cols).astype(bf16)


def _fox_decay(f_t, b_forget, B, S):
    nh = f_t.shape[0]
    assert KEY_BIAS_PARTS * nh <= LANES
    return pl.pallas_call(
        _fox_decay_kernel,
        grid=(B,),
        in_specs=[
            pl.BlockSpec((nh, S), lambda b: (0, b)),
            pl.BlockSpec((nh, 1), lambda b: (0, 0)),
        ],
        out_specs=pl.BlockSpec((1, S, LANES), lambda b: (b, 0, 0)),
        out_shape=jax.ShapeDtypeStruct((B, S, LANES), bf16),
        compiler_params=_params("arbitrary"),
        name="fox_decay",
    )(f_t, b_forget)


def _qk(q, k):
    return lax.dot_general(q, k, (((1,), (1,)), ((), ())), preferred_element_type=f32)


def _softmax_init(t):
    return jnp.full((t, 1), NEG, f32), jnp.zeros((t, 2 * HEAD_DIM), f32)


def _softmax_step2(carry, s2, v_ones):
    m, acc = carry
    m_new = jnp.maximum(m, jnp.max(s2, axis=-1, keepdims=True))
    p = jnp.exp2(s2 - m_new).astype(bf16)
    acc = jnp.exp2(m - m_new) * acc + jnp.dot(p, v_ones, preferred_element_type=f32)
    return m_new, acc


def _softmax_result(carry):
    _, acc = carry
    return acc[:, :HEAD_DIM] / acc[:, HEAD_DIM:]


def _with_ones(vo_scr, v_ref):
    vo_scr[:, :HEAD_DIM] = v_ref[0]
    vo_scr[:, HEAD_DIM:] = jnp.ones((vo_scr.shape[0], HEAD_DIM), vo_scr.dtype)


def _fox_attn_kernel(q_ref, k_ref, v_ref, kb_ref, o_ref, qaug_scr, kaug_scr, vo_scr, *, t):
    h = pl.program_id(1)
    S = q_ref.shape[1]
    lane = lax.broadcasted_iota(jnp.int32, (S, LANES), 1)
    mine = jnp.logical_and(lane % N_FOX_HEADS == h, lane < KEY_BIAS_PARTS * N_FOX_HEADS)
    qaug_scr[:, :HEAD_DIM] = q_ref[0]
    qaug_scr[:, HEAD_DIM:] = jnp.where(mine, 1.0, 0.0).astype(bf16)
    kaug_scr[:, :HEAD_DIM] = k_ref[0]
    kaug_scr[:, HEAD_DIM:] = kb_ref[0]
    _with_ones(vo_scr, v_ref)
    row = lax.broadcasted_iota(jnp.int32, (t, t), 0)
    col = lax.broadcasted_iota(jnp.int32, (t, t), 1)
    for qi in range(S // t):
        qa = qaug_scr[qi * t:(qi + 1) * t, :]
        carry = _softmax_init(t)
        for kb in range(qi + 1):
            keys = slice(kb * t, (kb + 1) * t)
            s2 = _qk(qa, kaug_scr[keys, :])
            if kb == qi:
                s2 = jnp.where(col <= row, s2, NEG)
            carry = _softmax_step2(carry, s2, vo_scr[keys, :])
        o_ref[qi * t:(qi + 1) * t, :] = _softmax_result(carry).astype(o_ref.dtype)


def _fox_attn(qkv, key_bias, B, S, slab0):
    T = B * S
    H = N_FOX_HEADS
    t = _pick(S, 512)
    kern = functools.partial(_fox_attn_kernel, t=t)
    return pl.pallas_call(
        kern,
        grid=(B, H),
        in_specs=[
            pl.BlockSpec((1, S, HEAD_DIM), lambda b, h: (slab0 + h, b, 0)),
            pl.BlockSpec((1, S, HEAD_DIM), lambda b, h: (slab0 + H + h, b, 0)),
            pl.BlockSpec((1, S, HEAD_DIM), lambda b, h: (slab0 + 2 * H + h, b, 0)),
            pl.BlockSpec((1, S, LANES), lambda b, h: (b, 0, 0)),
        ],
        out_specs=pl.BlockSpec((S, HEAD_DIM), lambda b, h: (b, h)),
        out_shape=jax.ShapeDtypeStruct((T, H * HEAD_DIM), bf16),
        scratch_shapes=[pltpu.VMEM((S, 2 * HEAD_DIM), bf16)] * 3,
        compiler_params=_params("arbitrary", "arbitrary"),
        name="fox_attn",
    )(qkv, qkv, qkv, key_bias)


def _rel_bucket_table(n):
    max_exact = REL_BUCKETS // 2
    d = np.arange(n)
    ratio = np.log(np.maximum(d, 1).astype(np.float32) / np.float32(max_exact)) / np.float32(
        math.log(REL_MAX_DIST / max_exact))
    large = max_exact + (ratio * np.float32(REL_BUCKETS - max_exact)).astype(np.int32)
    large = np.minimum(large, REL_BUCKETS - 1)
    return np.where(d < max_exact, d, large).astype(np.int32)


def _moba_attn_kernel(rel_ref, q_ref, k_ref, v_ref, avg_ref, hot_ref, bk_ref, o_ref, bias_scr, kaug_scr,
                      qaug_scr, vo_scr):
    h = pl.program_id(1)
    blk = MOBA_BLOCK
    t = 2 * blk
    S = q_ref.shape[1]
    far_bias = rel_ref[h, REL_BUCKETS - 1]
    row = lax.broadcasted_iota(jnp.int32, (blk, blk), 0)
    col = lax.broadcasted_iota(jnp.int32, (blk, blk), 1)
    tiles = []
    for which in range(2):
        bk = bk_ref[which]
        tile = jnp.zeros((blk, blk), f32)
        for b in range(REL_BUCKETS - 1):
            tile = jnp.where(bk == b, (rel_ref[h, b] - far_bias) * LOG2E, tile)
        tiles.append(jnp.where(col <= row, tile, NEG) if which == 0 else tile)
    own_tile, prev_tile = tiles
    zero_tile = jnp.zeros((blk, blk), f32)
    for which, quads in enumerate((((zero_tile, prev_tile), (zero_tile, zero_tile)),
                                   ((own_tile, zero_tile), (prev_tile, own_tile)))):
        for r in range(2):
            for c in range(2):
                bias_scr[which, r * blk:(r + 1) * blk, c * blk:(c + 1) * blk] = quads[r][c]
    kaug_scr[:, :HEAD_DIM] = k_ref[0]
    kaug_scr[:, HEAD_DIM:] = hot_ref[...]
    _with_ones(vo_scr, v_ref)
    kmean = jnp.dot(avg_ref[...], k_ref[0], preferred_element_type=f32).astype(bf16)

    lane = lax.broadcasted_iota(jnp.int32, (S, LANES), 1)
    own = lax.broadcasted_iota(jnp.int32, (S, LANES), 0) // blk
    valid = lane < own
    gate = jnp.where(valid, _qk(q_ref[0], kmean), NEG)
    sel = lane == own
    for _ in range(MOBA_TOPK):
        mx = jnp.max(gate, axis=-1, keepdims=True)
        first = jnp.min(jnp.where(gate == mx, lane, LANES), axis=-1, keepdims=True)
        pick = lane == first
        sel = jnp.logical_or(sel, jnp.logical_and(pick, valid))
        gate = jnp.where(pick, -jnp.inf, gate)
    qaug_scr[:, :HEAD_DIM] = q_ref[0]
    qaug_scr[:, HEAD_DIM:] = jnp.where(sel, 0.0, NEG).astype(bf16)

    for j in range(S // t):
        rows = slice(j * t, (j + 1) * t)
        qa = qaug_scr[rows, :]
        carry = _softmax_step2(_softmax_init(t), _qk(qa, kaug_scr[rows, :]) + bias_scr[1], vo_scr[rows, :])
        if j > 0:
            keys = slice((j - 1) * t, j * t)
            carry = _softmax_step2(carry, _qk(qa, kaug_scr[keys, :]) + bias_scr[0], vo_scr[keys, :])
        for c in range(j - 1):
            keys = slice(c * t, (c + 1) * t)
            carry = _softmax_step2(carry, _qk(qa, kaug_scr[keys, :]), vo_scr[keys, :])
        o_ref[rows, :] = _softmax_result(carry).astype(o_ref.dtype)


def _moba_attn(qkv, rel_bias, B, S, slab0):
    T = B * S
    H = N_MOBA_HEADS
    blk = MOBA_BLOCK
    assert S % (2 * blk) == 0 and S // blk <= LANES
    nb = S // blk
    avg = np.zeros((LANES, S), np.float32)
    for n in range(nb):
        avg[n, n * blk:(n + 1) * blk] = 1.0 / blk
    hot = (avg.T > 0).astype(np.float32)
    table = _rel_bucket_table(2 * blk)
    dist = np.arange(blk)[:, None] - np.arange(blk)[None, :]
    buckets = np.stack([table[np.maximum(dist, 0)], table[dist + blk]]).astype(np.int32)
    grid_spec = pltpu.PrefetchScalarGridSpec(
        num_scalar_prefetch=1,
        grid=(B, H),
        in_specs=[
            pl.BlockSpec((1, S, HEAD_DIM), lambda b, h, r: (slab0 + h, b, 0)),
            pl.BlockSpec((1, S, HEAD_DIM), lambda b, h, r: (slab0 + H + h, b, 0)),
            pl.BlockSpec((1, S, HEAD_DIM), lambda b, h, r: (slab0 + 2 * H + h, b, 0)),
            pl.BlockSpec((LANES, S), lambda b, h, r: (0, 0)),
            pl.BlockSpec((S, LANES), lambda b, h, r: (0, 0)),
            pl.BlockSpec((2, blk, blk), lambda b, h, r: (0, 0, 0)),
        ],
        out_specs=pl.BlockSpec((S, HEAD_DIM), lambda b, h, r: (b, h)),
        scratch_shapes=[pltpu.VMEM((2, 2 * blk, 2 * blk), f32)] + [pltpu.VMEM((S, 2 * HEAD_DIM), bf16)] * 3,
    )
    return pl.pallas_call(
        _moba_attn_kernel,
        grid_spec=grid_spec,
        out_shape=jax.ShapeDtypeStruct((T, H * HEAD_DIM), bf16),
        compiler_params=_params("arbitrary", "arbitrary"),
        name="moba_attn",
    )(rel_bias, qkv, qkv, qkv, jnp.asarray(avg, bf16), jnp.asarray(hot, bf16), jnp.asarray(buckets))


def _rms(x, g):
    return x * lax.rsqrt(jnp.mean(x * x, axis=-1, keepdims=True) + EPS) * g


def _first_max(x, lane):
    mx = jnp.max(x, axis=-1, keepdims=True)
    return mx, jnp.min(jnp.where(x == mx, lane, LANES), axis=-1, keepdims=True)


def _route(logits, lane, grp=None):
    gl = jnp.where(lane < N_GROUPS, logits, -jnp.inf)
    gmax, top = _first_max(gl, lane)
    grp = top if grp is None else grp
    pg = jnp.exp(gl - gmax)
    p_top_group = jnp.sum(jnp.where(lane == grp, pg, 0.0), axis=-1, keepdims=True) / jnp.sum(
        pg, axis=-1, keepdims=True)
    lo = N_GROUPS + grp * EXPERTS_PER_GROUP
    in_grp = jnp.logical_and(lane >= lo, lane < lo + EXPERTS_PER_GROUP)
    el = jnp.where(in_grp, logits, -jnp.inf)
    pe = jnp.exp(el - jnp.max(el, axis=-1, keepdims=True))
    p_exp = jnp.where(in_grp, pe / jnp.sum(pe, axis=-1, keepdims=True), -1.0)
    p1, i1 = _first_max(p_exp, lane)
    p2, i2 = _first_max(jnp.where(lane == i1, -1.0, p_exp), lane)
    tot = p1 + p2
    combine = jnp.where(lane == i1, p1 / tot, jnp.where(lane == i2, p2 / tot, 0.0)) * p_top_group
    return grp, jnp.minimum(i1, i2) - lo, jnp.maximum(i1, i2) - lo, combine


def _out_proj_kernel(fo_ref, mo_ref, x_ref, gf_ref, gm_ref, w_ref, gffn_ref, wr_ref, br_ref,
                     x1_ref, route_ref, count_ref, count_scr):
    @pl.when(pl.program_id(0) == 0)
    def _():
        count_scr[...] = jnp.zeros_like(count_scr)

    fo = _rms(fo_ref[...].astype(f32), gf_ref[...]).astype(bf16)
    mo = _rms(mo_ref[...].astype(f32), gm_ref[...]).astype(bf16)
    mix = jnp.concatenate([fo, mo], axis=-1)
    x1 = x_ref[...] + jnp.dot(mix, w_ref[...], preferred_element_type=f32)
    x1_ref[...] = x1
    h2 = _rms(x1, gffn_ref[...]).astype(bf16)
    logits = jnp.dot(h2, wr_ref[...], preferred_element_type=f32) + br_ref[...]
    tm = logits.shape[0]
    lane = lax.broadcasted_iota(jnp.int32, logits.shape, 1)
    grp, ea, eb, _ = _route(logits, lane)
    pair = (ea * (2 * EXPERTS_PER_GROUP - 1 - ea)) // 2 + eb - ea - 1
    cls = grp * EXPERT_PAIRS + pair
    onehot = jnp.where(lane == cls, 1.0, 0.0)
    earlier = lax.broadcasted_iota(jnp.int32, (tm, tm), 1) < lax.broadcasted_iota(jnp.int32, (tm, tm), 0)
    before = jnp.dot(jnp.where(earlier, 1.0, 0.0).astype(bf16), onehot.astype(bf16),
                     preferred_element_type=f32) + count_scr[...]
    rank = jnp.sum(onehot * before, axis=-1, keepdims=True).astype(jnp.int32)
    route_ref[...] = jnp.where(lane == 0, cls, jnp.where(lane == 1, rank, 0))
    count_scr[...] += jnp.sum(onehot, axis=0, keepdims=True)
    count_ref[...] = count_scr[...].astype(jnp.int32)


def _out_proj(fo, mo, x2, gf, gm, w_out, g_ffn, w_r, b_r):
    T, D = x2.shape
    Wf = fo.shape[1]
    Wm = mo.shape[1]
    tm = _pick(T, 512)
    const = lambda i: (0, 0)
    rows = lambda i: (i, 0)
    return pl.pallas_call(
        _out_proj_kernel,
        grid=(T // tm,),
        in_specs=[
            pl.BlockSpec((tm, Wf), rows),
            pl.BlockSpec((tm, Wm), rows),
            pl.BlockSpec((tm, D), rows),
            pl.BlockSpec((1, Wf), const),
            pl.BlockSpec((1, Wm), const),
            pl.BlockSpec((Wf + Wm, D), const),
            pl.BlockSpec((1, D), const),
            pl.BlockSpec((D, ROUTER_COLS), const),
            pl.BlockSpec((1, ROUTER_COLS), const),
        ],
        out_specs=[
            pl.BlockSpec((tm, D), rows),
            pl.BlockSpec((tm, ROUTER_COLS), rows),
            pl.BlockSpec((1, ROUTER_COLS), const),
        ],
        out_shape=[
            jax.ShapeDtypeStruct((T, D), f32),
            jax.ShapeDtypeStruct((T, ROUTER_COLS), jnp.int32),
            jax.ShapeDtypeStruct((1, ROUTER_COLS), jnp.int32),
        ],
        scratch_shapes=[pltpu.VMEM((1, ROUTER_COLS), f32)],
        compiler_params=_params("arbitrary"),
        name="out_proj",
    )(fo, mo, x2, gf, gm, w_out, g_ffn, w_r, b_r)


def _slot(grp_ref, rank_ref, base_ref, t):
    return base_ref[grp_ref[t]] + rank_ref[t]


def _dispatch_kernel(grp_ref, rank_ref, base_ref, fill_ref, x_ref, xs_ref, zero_scr, sem, zsem):
    i = pl.program_id(0)
    tm = x_ref.shape[0]

    zrows = zero_scr.shape[0]

    def row_copy(r):
        d = _slot(grp_ref, rank_ref, base_ref, i * tm + r)
        return pltpu.make_async_copy(x_ref.at[pl.ds(r, 1)], xs_ref.at[pl.ds(d, 1)], sem)

    def zero_row(r):
        return pltpu.make_async_copy(zero_scr.at[pl.ds(0, 1)], xs_ref.at[pl.ds(r, 1)], zsem)

    def zero_chunk(c):
        return pltpu.make_async_copy(zero_scr, xs_ref.at[pl.ds(pl.multiple_of(c * zrows, zrows), zrows)], zsem)

    def start(copy):
        def body(r, carry):
            copy(r).start()
            return carry
        return body

    def wait(copy):
        def body(r, carry):
            copy(r).wait()
            return carry
        return body

    lax.fori_loop(0, tm, start(row_copy), 0, unroll=DMA_UNROLL)

    @pl.when(i == 0)
    def _():
        zero_scr[...] = jnp.zeros_like(zero_scr)

        def fill_class(c, carry):
            lax.fori_loop(fill_ref[2 * c], fill_ref[2 * c + 1], start(zero_row), 0)
            lax.fori_loop(fill_ref[2 * c], fill_ref[2 * c + 1], wait(zero_row), 0)
            return carry

        lax.fori_loop(0, N_CLASSES, fill_class, 0)
        lo, hi = fill_ref[2 * N_CLASSES] // zrows, fill_ref[2 * N_CLASSES + 1] // zrows
        lax.fori_loop(lo, hi, start(zero_chunk), 0)
        lax.fori_loop(lo, hi, wait(zero_chunk), 0)

    pltpu.make_async_copy(x_ref, xs_ref.at[pl.ds(0, tm)], sem).wait()


def _dispatch(x1, grp, rank, base, fill, rows_out):
    T, D = x1.shape
    tm = _pick(T, 512)
    grid_spec = pltpu.PrefetchScalarGridSpec(
        num_scalar_prefetch=4,
        grid=(T // tm,),
        in_specs=[pl.BlockSpec((tm, D), lambda i, *_: (i, 0))],
        out_specs=pl.BlockSpec(memory_space=pl.ANY),
        scratch_shapes=[pltpu.VMEM((ZERO_ROWS, D), f32), pltpu.SemaphoreType.DMA, pltpu.SemaphoreType.DMA],
    )
    return pl.pallas_call(
        _dispatch_kernel,
        grid_spec=grid_spec,
        out_shape=jax.ShapeDtypeStruct((rows_out, D), f32),
        compiler_params=_params("arbitrary"),
        name="moe_dispatch",
    )(grp, rank, base, fill, x1)


def _collect_kernel(grp_ref, rank_ref, base_ref, ys_ref, o_ref, sem):
    i = pl.program_id(0)
    tm = o_ref.shape[0]

    def row_copy(r):
        d = _slot(grp_ref, rank_ref, base_ref, i * tm + r)
        return pltpu.make_async_copy(ys_ref.at[pl.ds(d, 1)], o_ref.at[pl.ds(r, 1)], sem)

    def start(r, carry):
        row_copy(r).start()
        return carry

    lax.fori_loop(0, tm, start, 0, unroll=DMA_UNROLL)
    pltpu.make_async_copy(ys_ref.at[pl.ds(0, tm)], o_ref, sem).wait()


def _collect(ys, grp, rank, base, T):
    D = ys.shape[1]
    tm = _pick(T, 512)
    grid_spec = pltpu.PrefetchScalarGridSpec(
        num_scalar_prefetch=3,
        grid=(T // tm,),
        in_specs=[pl.BlockSpec(memory_space=pl.ANY)],
        out_specs=pl.BlockSpec((tm, D), lambda i, *_: (i, 0)),
        scratch_shapes=[pltpu.SemaphoreType.DMA],
    )
    return pl.pallas_call(
        _collect_kernel,
        grid_spec=grid_spec,
        out_shape=jax.ShapeDtypeStruct((T, D), f32),
        compiler_params=_params("arbitrary"),
        name="moe_collect",
    )(grp, rank, base, ys)


def _moe_kernel(ea_ref, eb_ref, nu_ref, xs_ref, gffn_ref, wr_ref, br_ref, wga_ref, wua_ref, wda_ref,
                wgb_ref, wub_ref, wdb_ref, gfin_ref, o_ref):
    i = pl.program_id(0)

    @pl.when(i < nu_ref[0])
    def _():
        x = xs_ref[...]
        h = _rms(x, gffn_ref[...]).astype(bf16)
        logits = jnp.dot(h, wr_ref[...], preferred_element_type=f32) + br_ref[...]
        lane = lax.broadcasted_iota(jnp.int32, logits.shape, 1)
        _, _, _, combine = _route(logits, lane, ea_ref[i] // EXPERTS_PER_GROUP)
        y = jnp.zeros(x.shape, f32)
        for e_ref, wg_ref, wu_ref, wd_ref in ((ea_ref, wga_ref, wua_ref, wda_ref),
                                               (eb_ref, wgb_ref, wub_ref, wdb_ref)):
            c = jnp.sum(jnp.where(lane == N_GROUPS + e_ref[i], combine, 0.0), axis=-1, keepdims=True)
            gate = jnp.dot(h, wg_ref[0], preferred_element_type=f32)
            up = jnp.dot(h, wu_ref[0], preferred_element_type=f32)
            hid = (jax.nn.silu(gate) * up * c).astype(bf16)
            y = y + jnp.dot(hid, wd_ref[0], preferred_element_type=f32)
        o_ref[...] = _rms(x + y, gfin_ref[...])

    @pl.when(i >= nu_ref[0])
    def _():
        o_ref[...] = jnp.zeros_like(o_ref)


def _moe(xs, tile_ea, tile_eb, n_used, g_ffn, w_r, b_r, w_gate, w_up, w_down, g_final, ts):
    R, D = xs.shape
    E, _, F = w_gate.shape
    const = lambda i, ea, eb, nu: (0, 0)
    rows = lambda i, ea, eb, nu: (jnp.minimum(i, nu[0] - 1), 0)
    first = lambda i, ea, eb, nu: (ea[i], 0, 0)
    second = lambda i, ea, eb, nu: (eb[i], 0, 0)
    grid_spec = pltpu.PrefetchScalarGridSpec(
        num_scalar_prefetch=3,
        grid=(R // ts,),
        in_specs=[
            pl.BlockSpec((ts, D), rows),
            pl.BlockSpec((1, D), const),
            pl.BlockSpec((D, ROUTER_COLS), const),
            pl.BlockSpec((1, ROUTER_COLS), const),
            pl.BlockSpec((1, D, F), first),
            pl.BlockSpec((1, D, F), first),
            pl.BlockSpec((1, F, D), first),
            pl.BlockSpec((1, D, F), second),
            pl.BlockSpec((1, D, F), second),
            pl.BlockSpec((1, F, D), second),
            pl.BlockSpec((1, D), const),
        ],
        out_specs=pl.BlockSpec((ts, D), lambda i, ea, eb, nu: (i, 0)),
    )
    return pl.pallas_call(
        _moe_kernel,
        grid_spec=grid_spec,
        out_shape=jax.ShapeDtypeStruct((R, D), f32),
        compiler_params=_params("arbitrary"),
        name="moe",
    )(tile_ea, tile_eb, n_used, xs, g_ffn, w_r, b_r, w_gate, w_up, w_down, w_gate, w_up, w_down, g_final)


def kernel(x, attn_norm_g, w_in, b_forget, fox_out_norm_g, moba_out_norm_g, rel_bias, w_out, ffn_norm_g,
           w_group_router, b_group_router, w_expert_router, b_expert_router, w_gate, w_up, w_down,
           final_norm_g):
    B, S, D = x.shape
    T = B * S
    depth = w_in.shape[0]
    fox_w = N_FOX_HEADS * HEAD_DIM
    moba_w = N_MOBA_HEADS * HEAD_DIM
    qkv_w = 3 * (fox_w + moba_w)
    assert w_in.shape[2] == qkv_w + N_FOX_HEADS
    assert N_GROUPS + N_EXPERTS <= ROUTER_COLS

    x2 = x.reshape(T, D)
    out = None
    for l in range(depth):
        w_qkv = w_in[l, :, :qkv_w].astype(bf16)
        w_fg_t = w_in[l, :, qkv_w:].T.astype(bf16)
        assert fox_w == moba_w
        col_scale = np.ones((qkv_w // fox_w,), np.float32)
        col_scale[[0, 3]] = HEAD_DIM ** -0.5 * LOG2E
        qkv, f_t = _in_proj(x2, attn_norm_g[l].reshape(1, D), w_qkv, w_fg_t, jnp.asarray(col_scale), fox_w)
        key_bias = _fox_decay(f_t, b_forget[l].reshape(N_FOX_HEADS, 1), B, S)
        fo = _fox_attn(qkv, key_bias, B, S, 0)
        mo = _moba_attn(qkv, rel_bias, B, S, 3 * N_FOX_HEADS)

        w_r = jnp.concatenate([w_group_router[l], w_expert_router[l].reshape(D, N_EXPERTS)], axis=1)
        w_r = jnp.pad(w_r, ((0, 0), (0, ROUTER_COLS - w_r.shape[1]))).astype(bf16)
        b_r = jnp.concatenate([b_group_router[l], b_expert_router[l].reshape(N_EXPERTS)])
        b_r = jnp.pad(b_r, (0, ROUTER_COLS - b_r.shape[0])).reshape(1, ROUTER_COLS)
        g_ffn = ffn_norm_g[l].reshape(1, D)
        x1, route, count = _out_proj(fo, mo, x2, fox_out_norm_g[l].reshape(1, fox_w),
                                     moba_out_norm_g[l].reshape(1, moba_w), w_out[l].astype(bf16),
                                     g_ffn, w_r, b_r)
        ts = MOE_TILE
        cls, rank = route[:, 0], route[:, 1]
        counts = count[0, :N_CLASSES]
        padded = (counts + ts - 1) // ts * ts
        ends = jnp.cumsum(padded)
        base = ends - padded
        rows_out = (T // ts + N_CLASSES) * ts
        fill = jnp.concatenate([jnp.stack([base + counts, ends], axis=1).reshape(-1),
                                jnp.stack([ends[-1], jnp.asarray(rows_out, ends.dtype)])])
        tile_start = jnp.arange(rows_out // ts, dtype=jnp.int32) * ts
        tile_cls = jnp.minimum(jnp.sum(tile_start[:, None] >= ends[None, :], axis=1), N_CLASSES - 1)
        pairs = [(a, b) for a in range(EXPERTS_PER_GROUP) for b in range(a + 1, EXPERTS_PER_GROUP)]
        first_of_cls = np.array([g * EXPERTS_PER_GROUP + a for g in range(N_GROUPS) for a, _ in pairs], np.int32)
        second_of_cls = np.array([g * EXPERTS_PER_GROUP + b for g in range(N_GROUPS) for _, b in pairs], np.int32)
        tile_ea = jnp.asarray(first_of_cls)[tile_cls]
        tile_eb = jnp.asarray(second_of_cls)[tile_cls]
        n_used = (ends[-1:] // ts).astype(jnp.int32)

        assert l == depth - 1, "the fused residual + final norm epilogue expects a single layer"
        xs = _dispatch(x1, cls, rank, base, fill, rows_out)
        ys = _moe(xs, tile_ea, tile_eb, n_used, g_ffn, w_r, b_r, w_gate[l].astype(bf16),
                  w_up[l].astype(bf16), w_down[l].astype(bf16), final_norm_g.reshape(1, D), ts)
        out = _collect(ys, cls, rank, base, T)
        x2 = out
    return out.reshape(B, S, D)
```

```python
import functools
import math

import numpy as np
import jax
import jax.numpy as jnp
from jax import lax
from jax.experimental import pallas as pl
from jax.experimental.pallas import tpu as pltpu

HEAD_DIM = 128
N_FOX_HEADS = 8
N_MOBA_HEADS = 8
MOBA_BLOCK = 256
MOBA_TOPK = 3
REL_BUCKETS = 32
REL_MAX_DIST = 128
N_GROUPS = 4
EXPERTS_PER_GROUP = 4
N_EXPERTS = N_GROUPS * EXPERTS_PER_GROUP
EXPERT_PAIRS = EXPERTS_PER_GROUP * (EXPERTS_PER_GROUP - 1) // 2
N_CLASSES = N_GROUPS * EXPERT_PAIRS
EPS = 1e-6
NEG = -1e30
LOG2E = math.log2(math.e)
LANES = 128
SUBLANES = 8
ROUTER_COLS = LANES
MOE_TILE = 512
KEY_BIAS_PARTS = 3
ZERO_ROWS = 64
DMA_UNROLL = 8
VMEM_LIMIT = 56 * 1024 * 1024

f32 = jnp.float32
bf16 = jnp.bfloat16


def _params(*sem):
    return pltpu.CompilerParams(dimension_semantics=sem, vmem_limit_bytes=VMEM_LIMIT)


def _pick(n, pref):
    t = min(n, pref)
    assert n % t == 0, (n, pref)
    return t


def _in_proj_kernel(cs_ref, x_ref, g_ref, w_ref, wfg_ref, qkv_ref, f_ref, h_scr):
    @pl.when(pl.program_id(1) == 0)
    def _():
        x = x_ref[...]
        y = x * lax.rsqrt(jnp.mean(x * x, axis=-1, keepdims=True) + EPS)
        hb = (y * g_ref[...]).astype(bf16)
        h_scr[...] = hb
        f_ref[...] = lax.dot_general(wfg_ref[...], hb, (((1,), (1,)), ((), ())),
                                     preferred_element_type=f32)

    acc = jnp.dot(h_scr[...], w_ref[...], preferred_element_type=f32) * cs_ref[pl.program_id(1)]
    for c in range(acc.shape[1] // HEAD_DIM):
        qkv_ref[c] = acc[:, c * HEAD_DIM:(c + 1) * HEAD_DIM].astype(bf16)


def _in_proj(x2, g, w_qkv, w_fg_t, col_scale, tn):
    T, D = x2.shape
    N = w_qkv.shape[1]
    tm = _pick(T, 1024)
    assert N % tn == 0 and col_scale.shape == (N // tn,)
    nh = w_fg_t.shape[0]
    return pl.pallas_call(
        _in_proj_kernel,
        grid=(T // tm, N // tn),
        in_specs=[
            pl.BlockSpec(memory_space=pltpu.SMEM),
            pl.BlockSpec((tm, D), lambda i, j: (i, 0)),
            pl.BlockSpec((1, D), lambda i, j: (0, 0)),
            pl.BlockSpec((D, tn), lambda i, j: (0, j)),
            pl.BlockSpec((nh, D), lambda i, j: (0, 0)),
        ],
        out_specs=[
            pl.BlockSpec((tn // HEAD_DIM, tm, HEAD_DIM), lambda i, j: (j, i, 0)),
            pl.BlockSpec((nh, tm), lambda i, j: (0, i)),
        ],
        out_shape=[
            jax.ShapeDtypeStruct((N // HEAD_DIM, T, HEAD_DIM), bf16),
            jax.ShapeDtypeStruct((nh, T), f32),
        ],
        scratch_shapes=[pltpu.VMEM((tm, D), bf16)],
        compiler_params=_params("arbitrary", "arbitrary"),
        name="in_proj",
    )(col_scale, x2, g, w_qkv, w_fg_t)


def _fox_decay_kernel(f_ref, b_ref, o_ref):
    lf = jax.nn.log_sigmoid(f_ref[...] + b_ref[...])
    S = lf.shape[1]
    lane = lax.broadcasted_iota(jnp.int32, lf.shape, 1)
    c = lf
    sh = 1
    while sh < S:
        c = c + jnp.where(lane >= sh, pltpu.roll(c, sh, axis=1), 0.0)
        sh *= 2
    rest = -c * LOG2E
    parts = []
    for _ in range(KEY_BIAS_PARTS):
        part = rest.astype(bf16).astype(f32)
        parts.append(part)
        rest = rest - part
    nh = lf.shape[0]
    cols = jnp.concatenate(parts + [jnp.zeros((LANES - KEY_BIAS_PARTS * nh, S), f32)], axis=0)
    o_ref[0] = jnp.transpose(cols).astype(bf16)


def _fox_decay(f_t, b_forget, B, S):
    nh = f_t.shape[0]
    assert KEY_BIAS_PARTS * nh <= LANES
    return pl.pallas_call(
        _fox_decay_kernel,
        grid=(B,),
        in_specs=[
            pl.BlockSpec((nh, S), lambda b: (0, b)),
            pl.BlockSpec((nh, 1), lambda b: (0, 0)),
        ],
        out_specs=pl.BlockSpec((1, S, LANES), lambda b: (b, 0, 0)),
        out_shape=jax.ShapeDtypeStruct((B, S, LANES), bf16),
        compiler_params=_params("arbitrary"),
        name="fox_decay",
    )(f_t, b_forget)


def _qk(q, k):
    return lax.dot_general(q, k, (((1,), (1,)), ((), ())), preferred_element_type=f32)


def _softmax_init(t):
    return jnp.full((t, 1), NEG, f32), jnp.zeros((t, 2 * HEAD_DIM), f32)


def _softmax_step2(carry, s2, v_ones):
    m, acc = carry
    m_new = jnp.maximum(m, jnp.max(s2, axis=-1, keepdims=True))
    p = jnp.exp2(s2 - m_new).astype(bf16)
    acc = jnp.exp2(m - m_new) * acc + jnp.dot(p, v_ones, preferred_element_type=f32)
    return m_new, acc


def _softmax_result(carry):
    _, acc = carry
    return acc[:, :HEAD_DIM] / acc[:, HEAD_DIM:]


def _with_ones(vo_scr, v_ref):
    vo_scr[:, :HEAD_DIM] = v_ref[0]
    vo_scr[:, HEAD_DIM:] = jnp.ones((vo_scr.shape[0], HEAD_DIM), vo_scr.dtype)


def _fox_attn_kernel(q_ref, k_ref, v_ref, kb_ref, o_ref, qaug_scr, kaug_scr, vo_scr, *, t):
    h = pl.program_id(1)
    S = q_ref.shape[1]
    lane = lax.broadcasted_iota(jnp.int32, (S, LANES), 1)
    mine = jnp.logical_and(lane % N_FOX_HEADS == h, lane < KEY_BIAS_PARTS * N_FOX_HEADS)
    qaug_scr[:, :HEAD_DIM] = q_ref[0]
    qaug_scr[:, HEAD_DIM:] = jnp.where(mine, 1.0, 0.0).astype(bf16)
    kaug_scr[:, :HEAD_DIM] = k_ref[0]
    kaug_scr[:, HEAD_DIM:] = kb_ref[0]
    _with_ones(vo_scr, v_ref)
    row = lax.broadcasted_iota(jnp.int32, (t, t), 0)
    col = lax.broadcasted_iota(jnp.int32, (t, t), 1)
    for qi in range(S // t):
        qa = qaug_scr[qi * t:(qi + 1) * t, :]
        carry = _softmax_init(t)
        for kb in range(qi + 1):
            keys = slice(kb * t, (kb + 1) * t)
            s2 = _qk(qa, kaug_scr[keys, :])
            if kb == qi:
                s2 = jnp.where(col <= row, s2, NEG)
            carry = _softmax_step2(carry, s2, vo_scr[keys, :])
        o_ref[qi * t:(qi + 1) * t, :] = _softmax_result(carry).astype(o_ref.dtype)


def _fox_attn(qkv, key_bias, B, S, slab0):
    T = B * S
    H = N_FOX_HEADS
    t = _pick(S, 512)
    kern = functools.partial(_fox_attn_kernel, t=t)
    return pl.pallas_call(
        kern,
        grid=(B, H),
        in_specs=[
            pl.BlockSpec((1, S, HEAD_DIM), lambda b, h: (slab0 + h, b, 0)),
            pl.BlockSpec((1, S, HEAD_DIM), lambda b, h: (slab0 + H + h, b, 0)),
            pl.BlockSpec((1, S, HEAD_DIM), lambda b, h: (slab0 + 2 * H + h, b, 0)),
            pl.BlockSpec((1, S, LANES), lambda b, h: (b, 0, 0)),
        ],
        out_specs=pl.BlockSpec((S, HEAD_DIM), lambda b, h: (b, h)),
        out_shape=jax.ShapeDtypeStruct((T, H * HEAD_DIM), bf16),
        scratch_shapes=[pltpu.VMEM((S, 2 * HEAD_DIM), bf16)] * 3,
        compiler_params=_params("arbitrary", "arbitrary"),
        name="fox_attn",
    )(qkv, qkv, qkv, key_bias)


def _rel_bucket_table(n):
    max_exact = REL_BUCKETS // 2
    d = np.arange(n)
    ratio = np.log(np.maximum(d, 1).astype(np.float32) / np.float32(max_exact)) / np.float32(
        math.log(REL_MAX_DIST / max_exact))
    large = max_exact + (ratio * np.float32(REL_BUCKETS - max_exact)).astype(np.int32)
    large = np.minimum(large, REL_BUCKETS - 1)
    return np.where(d < max_exact, d, large).astype(np.int32)


def _moba_attn_kernel(rel_ref, q_ref, k_ref, v_ref, avg_ref, hot_ref, bk_ref, o_ref, bias_scr, kaug_scr,
                      qaug_scr, vo_scr):
    h = pl.program_id(1)
    blk = MOBA_BLOCK
    t = 2 * blk
    S = q_ref.shape[1]
    far_bias = rel_ref[h, REL_BUCKETS - 1]
    row = lax.broadcasted_iota(jnp.int32, (blk, blk), 0)
    col = lax.broadcasted_iota(jnp.int32, (blk, blk), 1)
    tiles = []
    for which in range(2):
        bk = bk_ref[which]
        tile = jnp.zeros((blk, blk), f32)
        for b in range(REL_BUCKETS - 1):
            tile = jnp.where(bk == b, (rel_ref[h, b] - far_bias) * LOG2E, tile)
        tiles.append(jnp.where(col <= row, tile, NEG) if which == 0 else tile)
    own_tile, prev_tile = tiles
    zero_tile = jnp.zeros((blk, blk), f32)
    for which, quads in enumerate((((zero_tile, prev_tile), (zero_tile, zero_tile)),
                                   ((own_tile, zero_tile), (prev_tile, own_tile)))):
        for r in range(2):
            for c in range(2):
                bias_scr[which, r * blk:(r + 1) * blk, c * blk:(c + 1) * blk] = quads[r][c]
    kaug_scr[:, :HEAD_DIM] = k_ref[0]
    kaug_scr[:, HEAD_DIM:] = hot_ref[...]
    _with_ones(vo_scr, v_ref)
    kmean = jnp.dot(avg_ref[...], k_ref[0], preferred_element_type=f32).astype(bf16)

    lane = lax.broadcasted_iota(jnp.int32, (S, LANES), 1)
    own = lax.broadcasted_iota(jnp.int32, (S, LANES), 0) // blk
    valid = lane < own
    gate = jnp.where(valid, _qk(q_ref[0], kmean), NEG)
    sel = lane == own
    for _ in range(MOBA_TOPK):
        mx = jnp.max(gate, axis=-1, keepdims=True)
        first = jnp.min(jnp.where(gate == mx, lane, LANES), axis=-1, keepdims=True)
        pick = lane == first
        sel = jnp.logical_or(sel, jnp.logical_and(pick, valid))
        gate = jnp.where(pick, -jnp.inf, gate)
    qaug_scr[:, :HEAD_DIM] = q_ref[0]
    qaug_scr[:, HEAD_DIM:] = jnp.where(sel, 0.0, NEG).astype(bf16)

    for j in range(S // t):
        rows = slice(j * t, (j + 1) * t)
        qa = qaug_scr[rows, :]
        carry = _softmax_step2(_softmax_init(t), _qk(qa, kaug_scr[rows, :]) + bias_scr[1], vo_scr[rows, :])
        if j > 0:
            keys = slice((j - 1) * t, j * t)
            carry = _softmax_step2(carry, _qk(qa, kaug_scr[keys, :]) + bias_scr[0], vo_scr[keys, :])
        for c in range(j - 1):
            keys = slice(c * t, (c + 1) * t)
            carry = _softmax_step2(carry, _qk(qa, kaug_scr[keys, :]), vo_scr[keys, :])
        o_ref[rows, :] = _softmax_result(carry).astype(o_ref.dtype)


def _moba_attn(qkv, rel_bias, B, S, slab0):
    T = B * S
    H = N_MOBA_HEADS
    blk = MOBA_BLOCK
    assert S % (2 * blk) == 0 and S // blk <= LANES
    nb = S // blk
    avg = np.zeros((LANES, S), np.float32)
    for n in range(nb):
        avg[n, n * blk:(n + 1) * blk] = 1.0 / blk
    hot = (avg.T > 0).astype(np.float32)
    table = _rel_bucket_table(2 * blk)
    dist = np.arange(blk)[:, None] - np.arange(blk)[None, :]
    buckets = np.stack([table[np.maximum(dist, 0)], table[dist + blk]]).astype(np.int32)
    grid_spec = pltpu.PrefetchScalarGridSpec(
        num_scalar_prefetch=1,
        grid=(B, H),
        in_specs=[
            pl.BlockSpec((1, S, HEAD_DIM), lambda b, h, r: (slab0 + h, b, 0)),
            pl.BlockSpec((1, S, HEAD_DIM), lambda b, h, r: (slab0 + H + h, b, 0)),
            pl.BlockSpec((1, S, HEAD_DIM), lambda b, h, r: (slab0 + 2 * H + h, b, 0)),
            pl.BlockSpec((LANES, S), lambda b, h, r: (0, 0)),
            pl.BlockSpec((S, LANES), lambda b, h, r: (0, 0)),
            pl.BlockSpec((2, blk, blk), lambda b, h, r: (0, 0, 0)),
        ],
        out_specs=pl.BlockSpec((S, HEAD_DIM), lambda b, h, r: (b, h)),
        scratch_shapes=[pltpu.VMEM((2, 2 * blk, 2 * blk), f32)] + [pltpu.VMEM((S, 2 * HEAD_DIM), bf16)] * 3,
    )
    return pl.pallas_call(
        _moba_attn_kernel,
        grid_spec=grid_spec,
        out_shape=jax.ShapeDtypeStruct((T, H * HEAD_DIM), bf16),
        compiler_params=_params("arbitrary", "arbitrary"),
        name="moba_attn",
    )(rel_bias, qkv, qkv, qkv, jnp.asarray(avg, bf16), jnp.asarray(hot, bf16), jnp.asarray(buckets))


def _rms(x, g):
    return x * lax.rsqrt(jnp.mean(x * x, axis=-1, keepdims=True) + EPS) * g


def _first_max(x, lane):
    mx = jnp.max(x, axis=-1, keepdims=True)
    return mx, jnp.min(jnp.where(x == mx, lane, LANES), axis=-1, keepdims=True)


def _route(logits, lane, grp=None):
    gl = jnp.where(lane < N_GROUPS, logits, -jnp.inf)
    gmax, top = _first_max(gl, lane)
    grp = top if grp is None else grp
    pg = jnp.exp(gl - gmax)
    p_top_group = jnp.sum(jnp.where(lane == grp, pg, 0.0), axis=-1, keepdims=True) / jnp.sum(
        pg, axis=-1, keepdims=True)
    lo = N_GROUPS + grp * EXPERTS_PER_GROUP
    in_grp = jnp.logical_and(lane >= lo, lane < lo + EXPERTS_PER_GROUP)
    el = jnp.where(in_grp, logits, -jnp.inf)
    pe = jnp.exp(el - jnp.max(el, axis=-1, keepdims=True))
    p_exp = jnp.where(in_grp, pe / jnp.sum(pe, axis=-1, keepdims=True), -1.0)
    p1, i1 = _first_max(p_exp, lane)
    p2, i2 = _first_max(jnp.where(lane == i1, -1.0, p_exp), lane)
    tot = p1 + p2
    combine = jnp.where(lane == i1, p1 / tot, jnp.where(lane == i2, p2 / tot, 0.0)) * p_top_group
    return grp, jnp.minimum(i1, i2) - lo, jnp.maximum(i1, i2) - lo, combine


def _out_proj_kernel(fo_ref, mo_ref, x_ref, gf_ref, gm_ref, w_ref, gffn_ref, wr_ref, br_ref,
                     x1_ref, route_ref, count_ref, count_scr):
    @pl.when(pl.program_id(0) == 0)
    def _():
        count_scr[...] = jnp.zeros_like(count_scr)

    fo = _rms(fo_ref[...].astype(f32), gf_ref[...]).astype(bf16)
    mo = _rms(mo_ref[...].astype(f32), gm_ref[...]).astype(bf16)
    mix = jnp.concatenate([fo, mo], axis=-1)
    x1 = x_ref[...] + jnp.dot(mix, w_ref[...], preferred_element_type=f32)
    h2 = _rms(x1, gffn_ref[...]).astype(bf16)
    logits = jnp.dot(h2, wr_ref[...], preferred_element_type=f32) + br_ref[...]
    tm, D = x1.shape
    lane = lax.broadcasted_iota(jnp.int32, logits.shape, 1)
    grp, ea, eb, combine = _route(logits, lane)
    x1_ref[:, :D] = x1
    x1_ref[:, D:] = combine
    pair = (ea * (2 * EXPERTS_PER_GROUP - 1 - ea)) // 2 + eb - ea - 1
    cls = grp * EXPERT_PAIRS + pair
    onehot = jnp.where(lane == cls, 1.0, 0.0)
    earlier = lax.broadcasted_iota(jnp.int32, (tm, tm), 1) < lax.broadcasted_iota(jnp.int32, (tm, tm), 0)
    before = jnp.dot(jnp.where(earlier, 1.0, 0.0).astype(bf16), onehot.astype(bf16),
                     preferred_element_type=f32) + count_scr[...]
    rank = jnp.sum(onehot * before, axis=-1, keepdims=True).astype(jnp.int32)
    route_ref[...] = jnp.where(lane == 0, cls, jnp.where(lane == 1, rank, 0))
    count_scr[...] += jnp.sum(onehot, axis=0, keepdims=True)
    count_ref[...] = count_scr[...].astype(jnp.int32)


def _out_proj(fo, mo, x2, gf, gm, w_out, g_ffn, w_r, b_r):
    T, D = x2.shape
    Wf = fo.shape[1]
    Wm = mo.shape[1]
    tm = _pick(T, 512)
    const = lambda i: (0, 0)
    rows = lambda i: (i, 0)
    return pl.pallas_call(
        _out_proj_kernel,
        grid=(T // tm,),
        in_specs=[
            pl.BlockSpec((tm, Wf), rows),
            pl.BlockSpec((tm, Wm), rows),
            pl.BlockSpec((tm, D), rows),
            pl.BlockSpec((1, Wf), const),
            pl.BlockSpec((1, Wm), const),
            pl.BlockSpec((Wf + Wm, D), const),
            pl.BlockSpec((1, D), const),
            pl.BlockSpec((D, ROUTER_COLS), const),
            pl.BlockSpec((1, ROUTER_COLS), const),
        ],
        out_specs=[
            pl.BlockSpec((tm, D + ROUTER_COLS), rows),
            pl.BlockSpec((tm, ROUTER_COLS), rows),
            pl.BlockSpec((1, ROUTER_COLS), const),
        ],
        out_shape=[
            jax.ShapeDtypeStruct((T, D + ROUTER_COLS), f32),
            jax.ShapeDtypeStruct((T, ROUTER_COLS), jnp.int32),
            jax.ShapeDtypeStruct((1, ROUTER_COLS), jnp.int32),
        ],
        scratch_shapes=[pltpu.VMEM((1, ROUTER_COLS), f32)],
        compiler_params=_params("arbitrary"),
        name="out_proj",
    )(fo, mo, x2, gf, gm, w_out, g_ffn, w_r, b_r)


def _slot(grp_ref, rank_ref, base_ref, t):
    return base_ref[grp_ref[t]] + rank_ref[t]


def _dispatch_kernel(grp_ref, rank_ref, base_ref, fill_ref, x_ref, xs_ref, zero_scr, sem, zsem):
    i = pl.program_id(0)
    tm = x_ref.shape[0]

    zrows = zero_scr.shape[0]

    def row_copy(r):
        d = _slot(grp_ref, rank_ref, base_ref, i * tm + r)
        return pltpu.make_async_copy(x_ref.at[pl.ds(r, 1)], xs_ref.at[pl.ds(d, 1)], sem)

    def zero_row(r):
        return pltpu.make_async_copy(zero_scr.at[pl.ds(0, 1)], xs_ref.at[pl.ds(r, 1)], zsem)

    def zero_sublanes(c):
        dst = xs_ref.at[pl.ds(pl.multiple_of(c * SUBLANES, SUBLANES), SUBLANES)]
        return pltpu.make_async_copy(zero_scr.at[pl.ds(0, SUBLANES)], dst, zsem)

    def zero_chunk(c):
        return pltpu.make_async_copy(zero_scr, xs_ref.at[pl.ds(pl.multiple_of(c * zrows, zrows), zrows)], zsem)

    def start(copy):
        def body(r, carry):
            copy(r).start()
            return carry
        return body

    def wait(copy):
        def body(r, carry):
            copy(r).wait()
            return carry
        return body

    lax.fori_loop(0, tm, start(row_copy), 0, unroll=DMA_UNROLL)

    @pl.when(i == 0)
    def _():
        zero_scr[...] = jnp.zeros_like(zero_scr)

        def fill_class(c, carry):
            lo, hi = fill_ref[2 * c], fill_ref[2 * c + 1]
            mid = jnp.minimum((lo + SUBLANES - 1) // SUBLANES * SUBLANES, hi)
            lax.fori_loop(lo, mid, start(zero_row), 0)
            lax.fori_loop(lo, mid, wait(zero_row), 0)
            lax.fori_loop(mid // SUBLANES, hi // SUBLANES, start(zero_sublanes), 0)
            lax.fori_loop(mid // SUBLANES, hi // SUBLANES, wait(zero_sublanes), 0)
            return carry

        lax.fori_loop(0, N_CLASSES, fill_class, 0)
        lo, hi = fill_ref[2 * N_CLASSES] // zrows, fill_ref[2 * N_CLASSES + 1] // zrows
        lax.fori_loop(lo, hi, start(zero_chunk), 0)
        lax.fori_loop(lo, hi, wait(zero_chunk), 0)

    pltpu.make_async_copy(x_ref, xs_ref.at[pl.ds(0, tm)], sem).wait()


def _dispatch(x1, grp, rank, base, fill, rows_out):
    T, D = x1.shape
    tm = _pick(T, 512)
    grid_spec = pltpu.PrefetchScalarGridSpec(
        num_scalar_prefetch=4,
        grid=(T // tm,),
        in_specs=[pl.BlockSpec((tm, D), lambda i, *_: (i, 0))],
        out_specs=pl.BlockSpec(memory_space=pl.ANY),
        scratch_shapes=[pltpu.VMEM((ZERO_ROWS, D), f32), pltpu.SemaphoreType.DMA, pltpu.SemaphoreType.DMA],
    )
    return pl.pallas_call(
        _dispatch_kernel,
        grid_spec=grid_spec,
        out_shape=jax.ShapeDtypeStruct((rows_out, D), f32),
        compiler_params=_params("arbitrary"),
        name="moe_dispatch",
    )(grp, rank, base, fill, x1)


def _collect_kernel(grp_ref, rank_ref, base_ref, ys_ref, o_ref, sem):
    i = pl.program_id(0)
    tm = o_ref.shape[0]

    def row_copy(r):
        d = _slot(grp_ref, rank_ref, base_ref, i * tm + r)
        return pltpu.make_async_copy(ys_ref.at[pl.ds(d, 1)], o_ref.at[pl.ds(r, 1)], sem)

    def start(r, carry):
        row_copy(r).start()
        return carry

    lax.fori_loop(0, tm, start, 0, unroll=DMA_UNROLL)
    pltpu.make_async_copy(ys_ref.at[pl.ds(0, tm)], o_ref, sem).wait()


def _collect(ys, grp, rank, base, T):
    D = ys.shape[1]
    tm = _pick(T, 512)
    grid_spec = pltpu.PrefetchScalarGridSpec(
        num_scalar_prefetch=3,
        grid=(T // tm,),
        in_specs=[pl.BlockSpec(memory_space=pl.ANY)],
        out_specs=pl.BlockSpec((tm, D), lambda i, *_: (i, 0)),
        scratch_shapes=[pltpu.SemaphoreType.DMA],
    )
    return pl.pallas_call(
        _collect_kernel,
        grid_spec=grid_spec,
        out_shape=jax.ShapeDtypeStruct((T, D), f32),
        compiler_params=_params("arbitrary"),
        name="moe_collect",
    )(grp, rank, base, ys)


def _moe_kernel(ea_ref, eb_ref, nu_ref, xs_ref, gffn_ref, wga_ref, wua_ref, wda_ref,
                wgb_ref, wub_ref, wdb_ref, gfin_ref, o_ref):
    i = pl.program_id(0)
    D = o_ref.shape[1]

    @pl.when(i < nu_ref[0])
    def _():
        x = xs_ref[:, :D]
        combine = xs_ref[:, D:]
        h = _rms(x, gffn_ref[...]).astype(bf16)
        lane = lax.broadcasted_iota(jnp.int32, combine.shape, 1)
        y = jnp.zeros(x.shape, f32)
        for e_ref, wg_ref, wu_ref, wd_ref in ((ea_ref, wga_ref, wua_ref, wda_ref),
                                               (eb_ref, wgb_ref, wub_ref, wdb_ref)):
            c = jnp.sum(jnp.where(lane == N_GROUPS + e_ref[i], combine, 0.0), axis=-1, keepdims=True)
            gate = jnp.dot(h, wg_ref[0], preferred_element_type=f32)
            up = jnp.dot(h, wu_ref[0], preferred_element_type=f32)
            hid = (jax.nn.silu(gate) * up * c).astype(bf16)
            y = y + jnp.dot(hid, wd_ref[0], preferred_element_type=f32)
        o_ref[...] = _rms(x + y, gfin_ref[...])

    @pl.when(i >= nu_ref[0])
    def _():
        o_ref[...] = jnp.zeros_like(o_ref)


def _moe(xs, tile_ea, tile_eb, n_used, g_ffn, w_gate, w_up, w_down, g_final, ts):
    R = xs.shape[0]
    E, D, F = w_gate.shape
    const = lambda i, ea, eb, nu: (0, 0)
    rows = lambda i, ea, eb, nu: (jnp.minimum(i, nu[0] - 1), 0)
    first = lambda i, ea, eb, nu: (ea[i], 0, 0)
    second = lambda i, ea, eb, nu: (eb[i], 0, 0)
    grid_spec = pltpu.PrefetchScalarGridSpec(
        num_scalar_prefetch=3,
        grid=(R // ts,),
        in_specs=[
            pl.BlockSpec((ts, D + ROUTER_COLS), rows),
            pl.BlockSpec((1, D), const),
            pl.BlockSpec((1, D, F), first),
            pl.BlockSpec((1, D, F), first),
            pl.BlockSpec((1, F, D), first),
            pl.BlockSpec((1, D, F), second),
            pl.BlockSpec((1, D, F), second),
            pl.BlockSpec((1, F, D), second),
            pl.BlockSpec((1, D), const),
        ],
        out_specs=pl.BlockSpec((ts, D), lambda i, ea, eb, nu: (i, 0)),
    )
    return pl.pallas_call(
        _moe_kernel,
        grid_spec=grid_spec,
        out_shape=jax.ShapeDtypeStruct((R, D), f32),
        compiler_params=_params("arbitrary"),
        name="moe",
    )(tile_ea, tile_eb, n_used, xs, g_ffn, w_gate, w_up, w_down, w_gate, w_up, w_down, g_final)


def kernel(x, attn_norm_g, w_in, b_forget, fox_out_norm_g, moba_out_norm_g, rel_bias, w_out, ffn_norm_g,
           w_group_router, b_group_router, w_expert_router, b_expert_router, w_gate, w_up, w_down,
           final_norm_g):
    B, S, D = x.shape
    T = B * S
    depth = w_in.shape[0]
    fox_w = N_FOX_HEADS * HEAD_DIM
    moba_w = N_MOBA_HEADS * HEAD_DIM
    qkv_w = 3 * (fox_w + moba_w)
    assert w_in.shape[2] == qkv_w + N_FOX_HEADS
    assert N_GROUPS + N_EXPERTS <= ROUTER_COLS

    x2 = x.reshape(T, D)
    out = None
    for l in range(depth):
        w_qkv = w_in[l, :, :qkv_w].astype(bf16)
        w_fg_t = w_in[l, :, qkv_w:].T.astype(bf16)
        assert fox_w == moba_w
        col_scale = np.ones((qkv_w // fox_w,), np.float32)
        col_scale[[0, 3]] = HEAD_DIM ** -0.5 * LOG2E
        qkv, f_t = _in_proj(x2, attn_norm_g[l].reshape(1, D), w_qkv, w_fg_t, jnp.asarray(col_scale), fox_w)
        key_bias = _fox_decay(f_t, b_forget[l].reshape(N_FOX_HEADS, 1), B, S)
        fo = _fox_attn(qkv, key_bias, B, S, 0)
        mo = _moba_attn(qkv, rel_bias, B, S, 3 * N_FOX_HEADS)

        w_r = jnp.concatenate([w_group_router[l], w_expert_router[l].reshape(D, N_EXPERTS)], axis=1)
        w_r = jnp.pad(w_r, ((0, 0), (0, ROUTER_COLS - w_r.shape[1]))).astype(bf16)
        b_r = jnp.concatenate([b_group_router[l], b_expert_router[l].reshape(N_EXPERTS)])
        b_r = jnp.pad(b_r, (0, ROUTER_COLS - b_r.shape[0])).reshape(1, ROUTER_COLS)
        g_ffn = ffn_norm_g[l].reshape(1, D)
        x1, route, count = _out_proj(fo, mo, x2, fox_out_norm_g[l].reshape(1, fox_w),
                                     moba_out_norm_g[l].reshape(1, moba_w), w_out[l].astype(bf16),
                                     g_ffn, w_r, b_r)
        ts = MOE_TILE
        cls, rank = route[:, 0], route[:, 1]
        counts = count[0, :N_CLASSES]
        padded = (counts + ts - 1) // ts * ts
        ends = jnp.cumsum(padded)
        base = ends - padded
        rows_out = (T // ts + N_CLASSES) * ts
        fill = jnp.concatenate([jnp.stack([base + counts, ends], axis=1).reshape(-1),
                                jnp.stack([ends[-1], jnp.asarray(rows_out, ends.dtype)])])
        tile_start = jnp.arange(rows_out // ts, dtype=jnp.int32) * ts
        tile_cls = jnp.minimum(jnp.sum(tile_start[:, None] >= ends[None, :], axis=1), N_CLASSES - 1)
        pairs = [(a, b) for a in range(EXPERTS_PER_GROUP) for b in range(a + 1, EXPERTS_PER_GROUP)]
        first_of_cls = np.array([g * EXPERTS_PER_GROUP + a for g in range(N_GROUPS) for a, _ in pairs], np.int32)
        second_of_cls = np.array([g * EXPERTS_PER_GROUP + b for g in range(N_GROUPS) for _, b in pairs], np.int32)
        tile_ea = jnp.asarray(first_of_cls)[tile_cls]
        tile_eb = jnp.asarray(second_of_cls)[tile_cls]
        n_used = (ends[-1:] // ts).astype(jnp.int32)

        assert l == depth - 1, "the fused residual + final norm epilogue expects a single layer"
        xs = _dispatch(x1, cls, rank, base, fill, rows_out)
        ys = _moe(xs, tile_ea, tile_eb, n_used, g_ffn, w_gate[l].astype(bf16),
                  w_up[l].astype(bf16), w_down[l].astype(bf16), final_norm_g.reshape(1, D), ts)
        out = _collect(ys, cls, rank, base, T)
        x2 = out
    return out.reshape(B, S, D)
```

```python
import functools
import math

import numpy as np
import jax
import jax.numpy as jnp
from jax import lax
from jax.experimental import pallas as pl
from jax.experimental.pallas import tpu as pltpu

HEAD_DIM = 128
N_FOX_HEADS = 8
N_MOBA_HEADS = 8
MOBA_BLOCK = 256
MOBA_TOPK = 3
REL_BUCKETS = 32
REL_MAX_DIST = 128
N_GROUPS = 4
EXPERTS_PER_GROUP = 4
N_EXPERTS = N_GROUPS * EXPERTS_PER_GROUP
EXPERT_PAIRS = EXPERTS_PER_GROUP * (EXPERTS_PER_GROUP - 1) // 2
N_CLASSES = N_GROUPS * EXPERT_PAIRS
EPS = 1e-6
NEG = -1e30
LOG2E = math.log2(math.e)
LANES = 128
SUBLANES = 8
ROUTER_COLS = LANES
MOE_TILE = 256
KEY_BIAS_PARTS = 3
ZERO_ROWS = 64
DMA_UNROLL = 8
VMEM_LIMIT = 56 * 1024 * 1024

f32 = jnp.float32
bf16 = jnp.bfloat16


def _params(*sem):
    return pltpu.CompilerParams(dimension_semantics=sem, vmem_limit_bytes=VMEM_LIMIT)


def _pick(n, pref):
    t = min(n, pref)
    assert n % t == 0, (n, pref)
    return t


def _in_proj_kernel(cs_ref, x_ref, g_ref, w_ref, wfg_ref, qkv_ref, f_ref, h_scr):
    @pl.when(pl.program_id(1) == 0)
    def _():
        x = x_ref[...]
        y = x * lax.rsqrt(jnp.mean(x * x, axis=-1, keepdims=True) + EPS)
        hb = (y * g_ref[...]).astype(bf16)
        h_scr[...] = hb
        f_ref[...] = lax.dot_general(wfg_ref[...], hb, (((1,), (1,)), ((), ())),
                                     preferred_element_type=f32)

    acc = jnp.dot(h_scr[...], w_ref[...], preferred_element_type=f32) * cs_ref[pl.program_id(1)]
    for c in range(acc.shape[1] // HEAD_DIM):
        qkv_ref[c] = acc[:, c * HEAD_DIM:(c + 1) * HEAD_DIM].astype(bf16)


def _in_proj(x2, g, w_qkv, w_fg_t, col_scale, tn):
    T, D = x2.shape
    N = w_qkv.shape[1]
    tm = _pick(T, 1024)
    assert N % tn == 0 and col_scale.shape == (N // tn,)
    nh = w_fg_t.shape[0]
    return pl.pallas_call(
        _in_proj_kernel,
        grid=(T // tm, N // tn),
        in_specs=[
            pl.BlockSpec(memory_space=pltpu.SMEM),
            pl.BlockSpec((tm, D), lambda i, j: (i, 0)),
            pl.BlockSpec((1, D), lambda i, j: (0, 0)),
            pl.BlockSpec((D, tn), lambda i, j: (0, j)),
            pl.BlockSpec((nh, D), lambda i, j: (0, 0)),
        ],
        out_specs=[
            pl.BlockSpec((tn // HEAD_DIM, tm, HEAD_DIM), lambda i, j: (j, i, 0)),
            pl.BlockSpec((nh, tm), lambda i, j: (0, i)),
        ],
        out_shape=[
            jax.ShapeDtypeStruct((N // HEAD_DIM, T, HEAD_DIM), bf16),
            jax.ShapeDtypeStruct((nh, T), f32),
        ],
        scratch_shapes=[pltpu.VMEM((tm, D), bf16)],
        compiler_params=_params("arbitrary", "arbitrary"),
        name="in_proj",
    )(col_scale, x2, g, w_qkv, w_fg_t)


def _fox_decay_kernel(f_ref, b_ref, o_ref):
    lf = jax.nn.log_sigmoid(f_ref[...] + b_ref[...])
    S = lf.shape[1]
    lane = lax.broadcasted_iota(jnp.int32, lf.shape, 1)
    c = lf
    sh = 1
    while sh < S:
        c = c + jnp.where(lane >= sh, pltpu.roll(c, sh, axis=1), 0.0)
        sh *= 2
    rest = -c * LOG2E
    parts = []
    for _ in range(KEY_BIAS_PARTS):
        part = rest.astype(bf16).astype(f32)
        parts.append(part)
        rest = rest - part
    nh = lf.shape[0]
    cols = jnp.concatenate(parts + [jnp.zeros((LANES - KEY_BIAS_PARTS * nh, S), f32)], axis=0)
    o_ref[0] = jnp.transpose(cols).astype(bf16)


def _fox_decay(f_t, b_forget, B, S):
    nh = f_t.shape[0]
    assert KEY_BIAS_PARTS * nh <= LANES
    return pl.pallas_call(
        _fox_decay_kernel,
        grid=(B,),
        in_specs=[
            pl.BlockSpec((nh, S), lambda b: (0, b)),
            pl.BlockSpec((nh, 1), lambda b: (0, 0)),
        ],
        out_specs=pl.BlockSpec((1, S, LANES), lambda b: (b, 0, 0)),
        out_shape=jax.ShapeDtypeStruct((B, S, LANES), bf16),
        compiler_params=_params("arbitrary"),
        name="fox_decay",
    )(f_t, b_forget)


def _qk(q, k):
    return lax.dot_general(q, k, (((1,), (1,)), ((), ())), preferred_element_type=f32)


def _softmax_init(t):
    return jnp.full((t, 1), NEG, f32), jnp.zeros((t, 2 * HEAD_DIM), f32)


def _softmax_step2(carry, s2, v_ones):
    m, acc = carry
    m_new = jnp.maximum(m, jnp.max(s2, axis=-1, keepdims=True))
    p = jnp.exp2(s2 - m_new).astype(bf16)
    acc = jnp.exp2(m - m_new) * acc + jnp.dot(p, v_ones, preferred_element_type=f32)
    return m_new, acc


def _softmax_result(carry):
    _, acc = carry
    return acc[:, :HEAD_DIM] / acc[:, HEAD_DIM:]


def _with_ones(vo_scr, v_ref):
    vo_scr[:, :HEAD_DIM] = v_ref[0]
    vo_scr[:, HEAD_DIM:] = jnp.ones((vo_scr.shape[0], HEAD_DIM), vo_scr.dtype)


def _fox_attn_kernel(q_ref, k_ref, v_ref, kb_ref, o_ref, qaug_scr, kaug_scr, vo_scr, *, t):
    h = pl.program_id(1)
    S = q_ref.shape[1]
    lane = lax.broadcasted_iota(jnp.int32, (S, LANES), 1)
    mine = jnp.logical_and(lane % N_FOX_HEADS == h, lane < KEY_BIAS_PARTS * N_FOX_HEADS)
    qaug_scr[:, :HEAD_DIM] = q_ref[0]
    qaug_scr[:, HEAD_DIM:] = jnp.where(mine, 1.0, 0.0).astype(bf16)
    kaug_scr[:, :HEAD_DIM] = k_ref[0]
    kaug_scr[:, HEAD_DIM:] = kb_ref[0]
    _with_ones(vo_scr, v_ref)
    row = lax.broadcasted_iota(jnp.int32, (t, t), 0)
    col = lax.broadcasted_iota(jnp.int32, (t, t), 1)
    for qi in range(S // t):
        qa = qaug_scr[qi * t:(qi + 1) * t, :]
        carry = _softmax_init(t)
        for kb in range(qi + 1):
            keys = slice(kb * t, (kb + 1) * t)
            s2 = _qk(qa, kaug_scr[keys, :])
            if kb == qi:
                s2 = jnp.where(col <= row, s2, NEG)
            carry = _softmax_step2(carry, s2, vo_scr[keys, :])
        o_ref[qi * t:(qi + 1) * t, :] = _softmax_result(carry).astype(o_ref.dtype)


def _fox_attn(qkv, key_bias, B, S, slab0):
    T = B * S
    H = N_FOX_HEADS
    t = _pick(S, 512)
    kern = functools.partial(_fox_attn_kernel, t=t)
    return pl.pallas_call(
        kern,
        grid=(B, H),
        in_specs=[
            pl.BlockSpec((1, S, HEAD_DIM), lambda b, h: (slab0 + h, b, 0)),
            pl.BlockSpec((1, S, HEAD_DIM), lambda b, h: (slab0 + H + h, b, 0)),
            pl.BlockSpec((1, S, HEAD_DIM), lambda b, h: (slab0 + 2 * H + h, b, 0)),
            pl.BlockSpec((1, S, LANES), lambda b, h: (b, 0, 0)),
        ],
        out_specs=pl.BlockSpec((S, HEAD_DIM), lambda b, h: (b, h)),
        out_shape=jax.ShapeDtypeStruct((T, H * HEAD_DIM), bf16),
        scratch_shapes=[pltpu.VMEM((S, 2 * HEAD_DIM), bf16)] * 3,
        compiler_params=_params("arbitrary", "arbitrary"),
        name="fox_attn",
    )(qkv, qkv, qkv, key_bias)


def _rel_bucket_table(n):
    max_exact = REL_BUCKETS // 2
    d = np.arange(n)
    ratio = np.log(np.maximum(d, 1).astype(np.float32) / np.float32(max_exact)) / np.float32(
        math.log(REL_MAX_DIST / max_exact))
    large = max_exact + (ratio * np.float32(REL_BUCKETS - max_exact)).astype(np.int32)
    large = np.minimum(large, REL_BUCKETS - 1)
    return np.where(d < max_exact, d, large).astype(np.int32)


def _moba_attn_kernel(rel_ref, q_ref, k_ref, v_ref, avg_ref, hot_ref, bk_ref, o_ref, bias_scr, kaug_scr,
                      qaug_scr, vo_scr):
    h = pl.program_id(1)
    blk = MOBA_BLOCK
    t = 2 * blk
    S = q_ref.shape[1]
    far_bias = rel_ref[h, REL_BUCKETS - 1]
    row = lax.broadcasted_iota(jnp.int32, (blk, blk), 0)
    col = lax.broadcasted_iota(jnp.int32, (blk, blk), 1)
    tiles = []
    for which in range(2):
        bk = bk_ref[which]
        tile = jnp.zeros((blk, blk), f32)
        for b in range(REL_BUCKETS - 1):
            tile = jnp.where(bk == b, (rel_ref[h, b] - far_bias) * LOG2E, tile)
        tiles.append(jnp.where(col <= row, tile, NEG) if which == 0 else tile)
    own_tile, prev_tile = tiles
    zero_tile = jnp.zeros((blk, blk), f32)
    for which, quads in enumerate((((zero_tile, prev_tile), (zero_tile, zero_tile)),
                                   ((own_tile, zero_tile), (prev_tile, own_tile)))):
        for r in range(2):
            for c in range(2):
                bias_scr[which, r * blk:(r + 1) * blk, c * blk:(c + 1) * blk] = quads[r][c]
    kaug_scr[:, :HEAD_DIM] = k_ref[0]
    kaug_scr[:, HEAD_DIM:] = hot_ref[...]
    _with_ones(vo_scr, v_ref)
    kmean = jnp.dot(avg_ref[...], k_ref[0], preferred_element_type=f32).astype(bf16)

    lane = lax.broadcasted_iota(jnp.int32, (S, LANES), 1)
    own = lax.broadcasted_iota(jnp.int32, (S, LANES), 0) // blk
    valid = lane < own
    gate = jnp.where(valid, _qk(q_ref[0], kmean), NEG)
    sel = lane == own
    for _ in range(MOBA_TOPK):
        mx = jnp.max(gate, axis=-1, keepdims=True)
        first = jnp.min(jnp.where(gate == mx, lane, LANES), axis=-1, keepdims=True)
        pick = lane == first
        sel = jnp.logical_or(sel, jnp.logical_and(pick, valid))
        gate = jnp.where(pick, -jnp.inf, gate)
    qaug_scr[:, :HEAD_DIM] = q_ref[0]
    qaug_scr[:, HEAD_DIM:] = jnp.where(sel, 0.0, NEG).astype(bf16)

    for j in range(S // t):
        rows = slice(j * t, (j + 1) * t)
        qa = qaug_scr[rows, :]
        carry = _softmax_step2(_softmax_init(t), _qk(qa, kaug_scr[rows, :]) + bias_scr[1], vo_scr[rows, :])
        if j > 0:
            keys = slice((j - 1) * t, j * t)
            carry = _softmax_step2(carry, _qk(qa, kaug_scr[keys, :]) + bias_scr[0], vo_scr[keys, :])
        for c in range(j - 1):
            keys = slice(c * t, (c + 1) * t)
            carry = _softmax_step2(carry, _qk(qa, kaug_scr[keys, :]), vo_scr[keys, :])
        o_ref[rows, :] = _softmax_result(carry).astype(o_ref.dtype)


def _moba_attn(qkv, rel_bias, B, S, slab0):
    T = B * S
    H = N_MOBA_HEADS
    blk = MOBA_BLOCK
    assert S % (2 * blk) == 0 and S // blk <= LANES
    nb = S // blk
    avg = np.zeros((LANES, S), np.float32)
    for n in range(nb):
        avg[n, n * blk:(n + 1) * blk] = 1.0 / blk
    hot = (avg.T > 0).astype(np.float32)
    table = _rel_bucket_table(2 * blk)
    dist = np.arange(blk)[:, None] - np.arange(blk)[None, :]
    buckets = np.stack([table[np.maximum(dist, 0)], table[dist + blk]]).astype(np.int32)
    grid_spec = pltpu.PrefetchScalarGridSpec(
        num_scalar_prefetch=1,
        grid=(B, H),
        in_specs=[
            pl.BlockSpec((1, S, HEAD_DIM), lambda b, h, r: (slab0 + h, b, 0)),
            pl.BlockSpec((1, S, HEAD_DIM), lambda b, h, r: (slab0 + H + h, b, 0)),
            pl.BlockSpec((1, S, HEAD_DIM), lambda b, h, r: (slab0 + 2 * H + h, b, 0)),
            pl.BlockSpec((LANES, S), lambda b, h, r: (0, 0)),
            pl.BlockSpec((S, LANES), lambda b, h, r: (0, 0)),
            pl.BlockSpec((2, blk, blk), lambda b, h, r: (0, 0, 0)),
        ],
        out_specs=pl.BlockSpec((S, HEAD_DIM), lambda b, h, r: (b, h)),
        scratch_shapes=[pltpu.VMEM((2, 2 * blk, 2 * blk), f32)] + [pltpu.VMEM((S, 2 * HEAD_DIM), bf16)] * 3,
    )
    return pl.pallas_call(
        _moba_attn_kernel,
        grid_spec=grid_spec,
        out_shape=jax.ShapeDtypeStruct((T, H * HEAD_DIM), bf16),
        compiler_params=_params("arbitrary", "arbitrary"),
        name="moba_attn",
    )(rel_bias, qkv, qkv, qkv, jnp.asarray(avg, bf16), jnp.asarray(hot, bf16), jnp.asarray(buckets))


def _rms(x, g):
    return x * lax.rsqrt(jnp.mean(x * x, axis=-1, keepdims=True) + EPS) * g


def _first_max(x, lane):
    mx = jnp.max(x, axis=-1, keepdims=True)
    return mx, jnp.min(jnp.where(x == mx, lane, LANES), axis=-1, keepdims=True)


def _route(logits, lane, grp=None):
    gl = jnp.where(lane < N_GROUPS, logits, -jnp.inf)
    gmax, top = _first_max(gl, lane)
    grp = top if grp is None else grp
    pg = jnp.exp(gl - gmax)
    p_top_group = jnp.sum(jnp.where(lane == grp, pg, 0.0), axis=-1, keepdims=True) / jnp.sum(
        pg, axis=-1, keepdims=True)
    lo = N_GROUPS + grp * EXPERTS_PER_GROUP
    in_grp = jnp.logical_and(lane >= lo, lane < lo + EXPERTS_PER_GROUP)
    el = jnp.where(in_grp, logits, -jnp.inf)
    pe = jnp.exp(el - jnp.max(el, axis=-1, keepdims=True))
    p_exp = jnp.where(in_grp, pe / jnp.sum(pe, axis=-1, keepdims=True), -1.0)
    p1, i1 = _first_max(p_exp, lane)
    p2, i2 = _first_max(jnp.where(lane == i1, -1.0, p_exp), lane)
    tot = p1 + p2
    combine = jnp.where(lane == i1, p1 / tot, jnp.where(lane == i2, p2 / tot, 0.0)) * p_top_group
    return grp, jnp.minimum(i1, i2) - lo, jnp.maximum(i1, i2) - lo, combine


def _out_proj_kernel(fo_ref, mo_ref, x_ref, gf_ref, gm_ref, w_ref, gffn_ref, wr_ref, br_ref,
                     x1_ref, route_ref, count_ref, count_scr):
    @pl.when(pl.program_id(0) == 0)
    def _():
        count_scr[...] = jnp.zeros_like(count_scr)

    fo = _rms(fo_ref[...].astype(f32), gf_ref[...]).astype(bf16)
    mo = _rms(mo_ref[...].astype(f32), gm_ref[...]).astype(bf16)
    mix = jnp.concatenate([fo, mo], axis=-1)
    x1 = x_ref[...] + jnp.dot(mix, w_ref[...], preferred_element_type=f32)
    h2 = _rms(x1, gffn_ref[...]).astype(bf16)
    logits = jnp.dot(h2, wr_ref[...], preferred_element_type=f32) + br_ref[...]
    tm, D = x1.shape
    lane = lax.broadcasted_iota(jnp.int32, logits.shape, 1)
    grp, ea, eb, combine = _route(logits, lane)
    x1_ref[:, :D] = x1
    x1_ref[:, D:] = combine
    pair = (ea * (2 * EXPERTS_PER_GROUP - 1 - ea)) // 2 + eb - ea - 1
    cls = grp * EXPERT_PAIRS + pair
    onehot = jnp.where(lane == cls, 1.0, 0.0)
    earlier = lax.broadcasted_iota(jnp.int32, (tm, tm), 1) < lax.broadcasted_iota(jnp.int32, (tm, tm), 0)
    before = jnp.dot(jnp.where(earlier, 1.0, 0.0).astype(bf16), onehot.astype(bf16),
                     preferred_element_type=f32) + count_scr[...]
    rank = jnp.sum(onehot * before, axis=-1, keepdims=True).astype(jnp.int32)
    route_ref[...] = jnp.where(lane == 0, cls, jnp.where(lane == 1, rank, 0))
    count_scr[...] += jnp.sum(onehot, axis=0, keepdims=True)
    count_ref[...] = count_scr[...].astype(jnp.int32)


def _out_proj(fo, mo, x2, gf, gm, w_out, g_ffn, w_r, b_r):
    T, D = x2.shape
    Wf = fo.shape[1]
    Wm = mo.shape[1]
    tm = _pick(T, 512)
    const = lambda i: (0, 0)
    rows = lambda i: (i, 0)
    return pl.pallas_call(
        _out_proj_kernel,
        grid=(T // tm,),
        in_specs=[
            pl.BlockSpec((tm, Wf), rows),
            pl.BlockSpec((tm, Wm), rows),
            pl.BlockSpec((tm, D), rows),
            pl.BlockSpec((1, Wf), const),
            pl.BlockSpec((1, Wm), const),
            pl.BlockSpec((Wf + Wm, D), const),
            pl.BlockSpec((1, D), const),
            pl.BlockSpec((D, ROUTER_COLS), const),
            pl.BlockSpec((1, ROUTER_COLS), const),
        ],
        out_specs=[
            pl.BlockSpec((tm, D + ROUTER_COLS), rows),
            pl.BlockSpec((tm, ROUTER_COLS), rows),
            pl.BlockSpec((1, ROUTER_COLS), const),
        ],
        out_shape=[
            jax.ShapeDtypeStruct((T, D + ROUTER_COLS), f32),
            jax.ShapeDtypeStruct((T, ROUTER_COLS), jnp.int32),
            jax.ShapeDtypeStruct((1, ROUTER_COLS), jnp.int32),
        ],
        scratch_shapes=[pltpu.VMEM((1, ROUTER_COLS), f32)],
        compiler_params=_params("arbitrary"),
        name="out_proj",
    )(fo, mo, x2, gf, gm, w_out, g_ffn, w_r, b_r)


def _dispatch_kernel(dest_ref, fill_ref, x_ref, xs_ref, zero_scr, sem, zsem):
    i = pl.program_id(0)
    tm = x_ref.shape[0]

    zrows = zero_scr.shape[0]

    def row_copy(r):
        d = dest_ref[i * tm + r]
        return pltpu.make_async_copy(x_ref.at[pl.ds(r, 1)], xs_ref.at[pl.ds(d, 1)], sem)

    def zero_row(r):
        return pltpu.make_async_copy(zero_scr.at[pl.ds(0, 1)], xs_ref.at[pl.ds(r, 1)], zsem)

    def zero_sublanes(c):
        dst = xs_ref.at[pl.ds(pl.multiple_of(c * SUBLANES, SUBLANES), SUBLANES)]
        return pltpu.make_async_copy(zero_scr.at[pl.ds(0, SUBLANES)], dst, zsem)

    def zero_chunk(c):
        return pltpu.make_async_copy(zero_scr, xs_ref.at[pl.ds(pl.multiple_of(c * zrows, zrows), zrows)], zsem)

    def start(copy):
        def body(r, carry):
            copy(r).start()
            return carry
        return body

    def wait(copy):
        def body(r, carry):
            copy(r).wait()
            return carry
        return body

    lax.fori_loop(0, tm, start(row_copy), 0, unroll=DMA_UNROLL)

    @pl.when(i == 0)
    def _():
        zero_scr[...] = jnp.zeros_like(zero_scr)

        def fill_class(c, carry):
            lo, hi = fill_ref[2 * c], fill_ref[2 * c + 1]
            mid = jnp.minimum((lo + SUBLANES - 1) // SUBLANES * SUBLANES, hi)
            lax.fori_loop(lo, mid, start(zero_row), 0)
            lax.fori_loop(lo, mid, wait(zero_row), 0)
            lax.fori_loop(mid // SUBLANES, hi // SUBLANES, start(zero_sublanes), 0)
            lax.fori_loop(mid // SUBLANES, hi // SUBLANES, wait(zero_sublanes), 0)
            return carry

        lax.fori_loop(0, N_CLASSES, fill_class, 0)
        lo, hi = fill_ref[2 * N_CLASSES] // zrows, fill_ref[2 * N_CLASSES + 1] // zrows
        lax.fori_loop(lo, hi, start(zero_chunk), 0)
        lax.fori_loop(lo, hi, wait(zero_chunk), 0)

    pltpu.make_async_copy(x_ref, xs_ref.at[pl.ds(0, tm)], sem).wait()


def _dispatch(x1, dest, fill, rows_out):
    T, D = x1.shape
    tm = _pick(T, 512)
    grid_spec = pltpu.PrefetchScalarGridSpec(
        num_scalar_prefetch=2,
        grid=(T // tm,),
        in_specs=[pl.BlockSpec((tm, D), lambda i, *_: (i, 0))],
        out_specs=pl.BlockSpec(memory_space=pl.ANY),
        scratch_shapes=[pltpu.VMEM((ZERO_ROWS, D), f32), pltpu.SemaphoreType.DMA, pltpu.SemaphoreType.DMA],
    )
    return pl.pallas_call(
        _dispatch_kernel,
        grid_spec=grid_spec,
        out_shape=jax.ShapeDtypeStruct((rows_out, D), f32),
        compiler_params=_params("arbitrary"),
        name="moe_dispatch",
    )(dest, fill, x1)


def _collect_kernel(dest_ref, ys_ref, o_ref, sem):
    i = pl.program_id(0)
    tm = o_ref.shape[0]

    def row_copy(r):
        d = dest_ref[i * tm + r]
        return pltpu.make_async_copy(ys_ref.at[pl.ds(d, 1)], o_ref.at[pl.ds(r, 1)], sem)

    def start(r, carry):
        row_copy(r).start()
        return carry

    lax.fori_loop(0, tm, start, 0, unroll=DMA_UNROLL)
    pltpu.make_async_copy(ys_ref.at[pl.ds(0, tm)], o_ref, sem).wait()


def _collect(ys, dest, T):
    D = ys.shape[1]
    tm = _pick(T, 512)
    grid_spec = pltpu.PrefetchScalarGridSpec(
        num_scalar_prefetch=1,
        grid=(T // tm,),
        in_specs=[pl.BlockSpec(memory_space=pl.ANY)],
        out_specs=pl.BlockSpec((tm, D), lambda i, *_: (i, 0)),
        scratch_shapes=[pltpu.SemaphoreType.DMA],
    )
    return pl.pallas_call(
        _collect_kernel,
        grid_spec=grid_spec,
        out_shape=jax.ShapeDtypeStruct((T, D), f32),
        compiler_params=_params("arbitrary"),
        name="moe_collect",
    )(dest, ys)


def _moe_kernel(ea_ref, eb_ref, nu_ref, xs_ref, gffn_ref, wga_ref, wua_ref, wda_ref,
                wgb_ref, wub_ref, wdb_ref, gfin_ref, o_ref):
    i = pl.program_id(0)
    D = o_ref.shape[1]

    @pl.when(i < nu_ref[0])
    def _():
        x = xs_ref[:, :D]
        combine = xs_ref[:, D:]
        h = _rms(x, gffn_ref[...]).astype(bf16)
        lane = lax.broadcasted_iota(jnp.int32, combine.shape, 1)
        y = jnp.zeros(x.shape, f32)
        for e_ref, wg_ref, wu_ref, wd_ref in ((ea_ref, wga_ref, wua_ref, wda_ref),
                                               (eb_ref, wgb_ref, wub_ref, wdb_ref)):
            c = jnp.sum(jnp.where(lane == N_GROUPS + e_ref[i], combine, 0.0), axis=-1, keepdims=True)
            gate = jnp.dot(h, wg_ref[0], preferred_element_type=f32)
            up = jnp.dot(h, wu_ref[0], preferred_element_type=f32)
            hid = (jax.nn.silu(gate) * up * c).astype(bf16)
            y = y + jnp.dot(hid, wd_ref[0], preferred_element_type=f32)
        o_ref[...] = _rms(x + y, gfin_ref[...])

    @pl.when(i >= nu_ref[0])
    def _():
        o_ref[...] = jnp.zeros_like(o_ref)


def _moe(xs, tile_ea, tile_eb, n_used, g_ffn, w_gate, w_up, w_down, g_final, ts):
    R = xs.shape[0]
    E, D, F = w_gate.shape
    const = lambda i, ea, eb, nu: (0, 0)
    rows = lambda i, ea, eb, nu: (jnp.minimum(i, nu[0] - 1), 0)
    first = lambda i, ea, eb, nu: (ea[i], 0, 0)
    second = lambda i, ea, eb, nu: (eb[i], 0, 0)
    grid_spec = pltpu.PrefetchScalarGridSpec(
        num_scalar_prefetch=3,
        grid=(R // ts,),
        in_specs=[
            pl.BlockSpec((ts, D + ROUTER_COLS), rows),
            pl.BlockSpec((1, D), const),
            pl.BlockSpec((1, D, F), first),
            pl.BlockSpec((1, D, F), first),
            pl.BlockSpec((1, F, D), first),
            pl.BlockSpec((1, D, F), second),
            pl.BlockSpec((1, D, F), second),
            pl.BlockSpec((1, F, D), second),
            pl.BlockSpec((1, D), const),
        ],
        out_specs=pl.BlockSpec((ts, D), lambda i, ea, eb, nu: (i, 0)),
    )
    return pl.pallas_call(
        _moe_kernel,
        grid_spec=grid_spec,
        out_shape=jax.ShapeDtypeStruct((R, D), f32),
        compiler_params=_params("arbitrary"),
        name="moe",
    )(tile_ea, tile_eb, n_used, xs, g_ffn, w_gate, w_up, w_down, w_gate, w_up, w_down, g_final)


def kernel(x, attn_norm_g, w_in, b_forget, fox_out_norm_g, moba_out_norm_g, rel_bias, w_out, ffn_norm_g,
           w_group_router, b_group_router, w_expert_router, b_expert_router, w_gate, w_up, w_down,
           final_norm_g):
    B, S, D = x.shape
    T = B * S
    depth = w_in.shape[0]
    fox_w = N_FOX_HEADS * HEAD_DIM
    moba_w = N_MOBA_HEADS * HEAD_DIM
    qkv_w = 3 * (fox_w + moba_w)
    assert w_in.shape[2] == qkv_w + N_FOX_HEADS
    assert N_GROUPS + N_EXPERTS <= ROUTER_COLS

    x2 = x.reshape(T, D)
    out = None
    for l in range(depth):
        w_qkv = w_in[l, :, :qkv_w].astype(bf16)
        w_fg_t = w_in[l, :, qkv_w:].T.astype(bf16)
        assert fox_w == moba_w
        col_scale = np.ones((qkv_w // fox_w,), np.float32)
        col_scale[[0, 3]] = HEAD_DIM ** -0.5 * LOG2E
        qkv, f_t = _in_proj(x2, attn_norm_g[l].reshape(1, D), w_qkv, w_fg_t, jnp.asarray(col_scale), fox_w)
        key_bias = _fox_decay(f_t, b_forget[l].reshape(N_FOX_HEADS, 1), B, S)
        fo = _fox_attn(qkv, key_bias, B, S, 0)
        mo = _moba_attn(qkv, rel_bias, B, S, 3 * N_FOX_HEADS)

        w_r = jnp.concatenate([w_group_router[l], w_expert_router[l].reshape(D, N_EXPERTS)], axis=1)
        w_r = jnp.pad(w_r, ((0, 0), (0, ROUTER_COLS - w_r.shape[1]))).astype(bf16)
        b_r = jnp.concatenate([b_group_router[l], b_expert_router[l].reshape(N_EXPERTS)])
        b_r = jnp.pad(b_r, (0, ROUTER_COLS - b_r.shape[0])).reshape(1, ROUTER_COLS)
        g_ffn = ffn_norm_g[l].reshape(1, D)
        x1, route, count = _out_proj(fo, mo, x2, fox_out_norm_g[l].reshape(1, fox_w),
                                     moba_out_norm_g[l].reshape(1, moba_w), w_out[l].astype(bf16),
                                     g_ffn, w_r, b_r)
        ts = MOE_TILE
        cls, rank = route[:, 0], route[:, 1]
        counts = count[0, :N_CLASSES]
        padded = (counts + ts - 1) // ts * ts
        ends = jnp.cumsum(padded)
        base = ends - padded
        rows_out = (T // ts + N_CLASSES) * ts
        fill = jnp.concatenate([jnp.stack([base + counts, ends], axis=1).reshape(-1),
                                jnp.stack([ends[-1], jnp.asarray(rows_out, ends.dtype)])])
        tile_start = jnp.arange(rows_out // ts, dtype=jnp.int32) * ts
        tile_cls = jnp.minimum(jnp.sum(tile_start[:, None] >= ends[None, :], axis=1), N_CLASSES - 1)
        pairs = [(a, b) for a in range(EXPERTS_PER_GROUP) for b in range(a + 1, EXPERTS_PER_GROUP)]
        first_of_cls = np.array([g * EXPERTS_PER_GROUP + a for g in range(N_GROUPS) for a, _ in pairs], np.int32)
        second_of_cls = np.array([g * EXPERTS_PER_GROUP + b for g in range(N_GROUPS) for _, b in pairs], np.int32)
        tile_ea = jnp.asarray(first_of_cls)[tile_cls]
        tile_eb = jnp.asarray(second_of_cls)[tile_cls]
        n_used = (ends[-1:] // ts).astype(jnp.int32)

        assert l == depth - 1, "the fused residual + final norm epilogue expects a single layer"
        dest = base[cls] + rank
        xs = _dispatch(x1, dest, fill, rows_out)
        ys = _moe(xs, tile_ea, tile_eb, n_used, g_ffn, w_gate[l].astype(bf16),
                  w_up[l].astype(bf16), w_down[l].astype(bf16), final_norm_g.reshape(1, D), ts)
        out = _collect(ys, dest, T)
        x2 = out
    return out.reshape(B, S, D)
```

```python
import functools
import math

import numpy as np
import jax
import jax.numpy as jnp
from jax import lax
from jax.experimental import pallas as pl
from jax.experimental.pallas import tpu as pltpu

HEAD_DIM = 128
N_FOX_HEADS = 8
N_MOBA_HEADS = 8
MOBA_BLOCK = 256
MOBA_TOPK = 3
REL_BUCKETS = 32
REL_MAX_DIST = 128
N_GROUPS = 4
EXPERTS_PER_GROUP = 4
N_EXPERTS = N_GROUPS * EXPERTS_PER_GROUP
EXPERT_PAIRS = EXPERTS_PER_GROUP * (EXPERTS_PER_GROUP - 1) // 2
N_CLASSES = N_GROUPS * EXPERT_PAIRS
EPS = 1e-6
NEG = -1e30
LOG2E = math.log2(math.e)
LANES = 128
SUBLANES = 8
ROUTER_COLS = LANES
OUT_PROJ_PARTS = 2
MOE_TILE = 256
KEY_BIAS_PARTS = 3
ZERO_ROWS = 64
DMA_UNROLL = 8
VMEM_LIMIT = 56 * 1024 * 1024

f32 = jnp.float32
bf16 = jnp.bfloat16


def _params(*sem):
    return pltpu.CompilerParams(dimension_semantics=sem, vmem_limit_bytes=VMEM_LIMIT)


def _pick(n, pref):
    t = min(n, pref)
    assert n % t == 0, (n, pref)
    return t


def _in_proj_kernel(cs_ref, x_ref, g_ref, w_ref, wfg_ref, qkv_ref, f_ref, h_scr):
    @pl.when(pl.program_id(1) == 0)
    def _():
        x = x_ref[...]
        y = x * lax.rsqrt(jnp.mean(x * x, axis=-1, keepdims=True) + EPS)
        hb = (y * g_ref[...]).astype(bf16)
        h_scr[...] = hb
        f_ref[...] = lax.dot_general(wfg_ref[...], hb, (((1,), (1,)), ((), ())),
                                     preferred_element_type=f32)

    acc = jnp.dot(h_scr[...], w_ref[...], preferred_element_type=f32) * cs_ref[pl.program_id(1)]
    for c in range(acc.shape[1] // HEAD_DIM):
        qkv_ref[c] = acc[:, c * HEAD_DIM:(c + 1) * HEAD_DIM].astype(bf16)


def _in_proj(x2, g, w_qkv, w_fg_t, col_scale, tn):
    T, D = x2.shape
    N = w_qkv.shape[1]
    tm = _pick(T, 1024)
    assert N % tn == 0 and col_scale.shape == (N // tn,)
    nh = w_fg_t.shape[0]
    return pl.pallas_call(
        _in_proj_kernel,
        grid=(T // tm, N // tn),
        in_specs=[
            pl.BlockSpec(memory_space=pltpu.SMEM),
            pl.BlockSpec((tm, D), lambda i, j: (i, 0)),
            pl.BlockSpec((1, D), lambda i, j: (0, 0)),
            pl.BlockSpec((D, tn), lambda i, j: (0, j)),
            pl.BlockSpec((nh, D), lambda i, j: (0, 0)),
        ],
        out_specs=[
            pl.BlockSpec((tn // HEAD_DIM, tm, HEAD_DIM), lambda i, j: (j, i, 0)),
            pl.BlockSpec((nh, tm), lambda i, j: (0, i)),
        ],
        out_shape=[
            jax.ShapeDtypeStruct((N // HEAD_DIM, T, HEAD_DIM), bf16),
            jax.ShapeDtypeStruct((nh, T), f32),
        ],
        scratch_shapes=[pltpu.VMEM((tm, D), bf16)],
        compiler_params=_params("arbitrary", "arbitrary"),
        name="in_proj",
    )(col_scale, x2, g, w_qkv, w_fg_t)


def _fox_decay_kernel(f_ref, b_ref, o_ref):
    lf = jax.nn.log_sigmoid(f_ref[...] + b_ref[...])
    S = lf.shape[1]
    lane = lax.broadcasted_iota(jnp.int32, lf.shape, 1)
    c = lf
    sh = 1
    while sh < S:
        c = c + jnp.where(lane >= sh, pltpu.roll(c, sh, axis=1), 0.0)
        sh *= 2
    rest = -c * LOG2E
    parts = []
    for _ in range(KEY_BIAS_PARTS):
        part = rest.astype(bf16).astype(f32)
        parts.append(part)
        rest = rest - part
    nh = lf.shape[0]
    cols = jnp.concatenate(parts + [jnp.zeros((LANES - KEY_BIAS_PARTS * nh, S), f32)], axis=0)
    o_ref[0] = jnp.transpose(cols).astype(bf16)


def _fox_decay(f_t, b_forget, B, S):
    nh = f_t.shape[0]
    assert KEY_BIAS_PARTS * nh <= LANES
    return pl.pallas_call(
        _fox_decay_kernel,
        grid=(B,),
        in_specs=[
            pl.BlockSpec((nh, S), lambda b: (0, b)),
            pl.BlockSpec((nh, 1), lambda b: (0, 0)),
        ],
        out_specs=pl.BlockSpec((1, S, LANES), lambda b: (b, 0, 0)),
        out_shape=jax.ShapeDtypeStruct((B, S, LANES), bf16),
        compiler_params=_params("arbitrary"),
        name="fox_decay",
    )(f_t, b_forget)


def _qk(q, k):
    return lax.dot_general(q, k, (((1,), (1,)), ((), ())), preferred_element_type=f32)


def _softmax_init(t):
    return jnp.full((t, 1), NEG, f32), jnp.zeros((t, 2 * HEAD_DIM), f32)


def _softmax_step2(carry, s2, v_ones):
    m, acc = carry
    m_new = jnp.maximum(m, jnp.max(s2, axis=-1, keepdims=True))
    p = jnp.exp2(s2 - m_new).astype(bf16)
    acc = jnp.exp2(m - m_new) * acc + jnp.dot(p, v_ones, preferred_element_type=f32)
    return m_new, acc


def _softmax_result(carry):
    _, acc = carry
    return acc[:, :HEAD_DIM] / acc[:, HEAD_DIM:]


def _with_ones(vo_scr, v_ref):
    vo_scr[:, :HEAD_DIM] = v_ref[0]
    vo_scr[:, HEAD_DIM:] = jnp.ones((vo_scr.shape[0], HEAD_DIM), vo_scr.dtype)


def _fox_attn_kernel(q_ref, k_ref, v_ref, kb_ref, o_ref, qaug_scr, kaug_scr, vo_scr, *, t):
    h = pl.program_id(1)
    S = q_ref.shape[1]
    lane = lax.broadcasted_iota(jnp.int32, (S, LANES), 1)
    mine = jnp.logical_and(lane % N_FOX_HEADS == h, lane < KEY_BIAS_PARTS * N_FOX_HEADS)
    qaug_scr[:, :HEAD_DIM] = q_ref[0]
    qaug_scr[:, HEAD_DIM:] = jnp.where(mine, 1.0, 0.0).astype(bf16)
    kaug_scr[:, :HEAD_DIM] = k_ref[0]
    kaug_scr[:, HEAD_DIM:] = kb_ref[0]
    _with_ones(vo_scr, v_ref)
    row = lax.broadcasted_iota(jnp.int32, (t, t), 0)
    col = lax.broadcasted_iota(jnp.int32, (t, t), 1)
    for qi in range(S // t):
        qa = qaug_scr[qi * t:(qi + 1) * t, :]
        carry = _softmax_init(t)
        for kb in range(qi + 1):
            keys = slice(kb * t, (kb + 1) * t)
            s2 = _qk(qa, kaug_scr[keys, :])
            if kb == qi:
                s2 = jnp.where(col <= row, s2, NEG)
            carry = _softmax_step2(carry, s2, vo_scr[keys, :])
        o_ref[qi * t:(qi + 1) * t, :] = _softmax_result(carry).astype(o_ref.dtype)


def _fox_attn(qkv, key_bias, B, S, slab0):
    T = B * S
    H = N_FOX_HEADS
    t = _pick(S, 512)
    kern = functools.partial(_fox_attn_kernel, t=t)
    return pl.pallas_call(
        kern,
        grid=(B, H),
        in_specs=[
            pl.BlockSpec((1, S, HEAD_DIM), lambda b, h: (slab0 + h, b, 0)),
            pl.BlockSpec((1, S, HEAD_DIM), lambda b, h: (slab0 + H + h, b, 0)),
            pl.BlockSpec((1, S, HEAD_DIM), lambda b, h: (slab0 + 2 * H + h, b, 0)),
            pl.BlockSpec((1, S, LANES), lambda b, h: (b, 0, 0)),
        ],
        out_specs=pl.BlockSpec((S, HEAD_DIM), lambda b, h: (b, h)),
        out_shape=jax.ShapeDtypeStruct((T, H * HEAD_DIM), bf16),
        scratch_shapes=[pltpu.VMEM((S, 2 * HEAD_DIM), bf16)] * 3,
        compiler_params=_params("arbitrary", "arbitrary"),
        name="fox_attn",
    )(qkv, qkv, qkv, key_bias)


def _rel_bucket_table(n):
    max_exact = REL_BUCKETS // 2
    d = np.arange(n)
    ratio = np.log(np.maximum(d, 1).astype(np.float32) / np.float32(max_exact)) / np.float32(
        math.log(REL_MAX_DIST / max_exact))
    large = max_exact + (ratio * np.float32(REL_BUCKETS - max_exact)).astype(np.int32)
    large = np.minimum(large, REL_BUCKETS - 1)
    return np.where(d < max_exact, d, large).astype(np.int32)


def _moba_attn_kernel(rel_ref, q_ref, k_ref, v_ref, avg_ref, hot_ref, bk_ref, o_ref, bias_scr, kaug_scr,
                      qaug_scr, vo_scr):
    h = pl.program_id(1)
    blk = MOBA_BLOCK
    t = 2 * blk
    S = q_ref.shape[1]
    far_bias = rel_ref[h, REL_BUCKETS - 1]
    row = lax.broadcasted_iota(jnp.int32, (blk, blk), 0)
    col = lax.broadcasted_iota(jnp.int32, (blk, blk), 1)
    tiles = []
    for which in range(2):
        bk = bk_ref[which]
        tile = jnp.zeros((blk, blk), f32)
        for b in range(REL_BUCKETS - 1):
            tile = jnp.where(bk == b, (rel_ref[h, b] - far_bias) * LOG2E, tile)
        tiles.append(jnp.where(col <= row, tile, NEG) if which == 0 else tile)
    own_tile, prev_tile = tiles
    zero_tile = jnp.zeros((blk, blk), f32)
    for which, quads in enumerate((((zero_tile, prev_tile), (zero_tile, zero_tile)),
                                   ((own_tile, zero_tile), (prev_tile, own_tile)))):
        for r in range(2):
            for c in range(2):
                bias_scr[which, r * blk:(r + 1) * blk, c * blk:(c + 1) * blk] = quads[r][c]
    kaug_scr[:, :HEAD_DIM] = k_ref[0]
    kaug_scr[:, HEAD_DIM:] = hot_ref[...]
    _with_ones(vo_scr, v_ref)
    kmean = jnp.dot(avg_ref[...], k_ref[0], preferred_element_type=f32).astype(bf16)

    nbp = -(-(S // blk) // SUBLANES) * SUBLANES
    block = lax.broadcasted_iota(jnp.int32, (nbp, S), 0)
    own = lax.broadcasted_iota(jnp.int32, (nbp, S), 1) // blk
    valid = block < own
    gate = jnp.where(valid, _qk(kmean[:nbp], q_ref[0]), NEG)
    sel = block == own
    for _ in range(MOBA_TOPK):
        mx = jnp.max(gate, axis=0, keepdims=True)
        first = jnp.min(jnp.where(gate == mx, block, nbp), axis=0, keepdims=True)
        pick = block == first
        sel = jnp.logical_or(sel, jnp.logical_and(pick, valid))
        gate = jnp.where(pick, -jnp.inf, gate)
    chosen = jnp.concatenate([jnp.where(sel, 0.0, NEG), jnp.full((LANES - nbp, S), NEG, f32)], axis=0)
    qaug_scr[:, :HEAD_DIM] = q_ref[0]
    qaug_scr[:, HEAD_DIM:] = jnp.transpose(chosen).astype(bf16)

    for j in range(S // t):
        rows = slice(j * t, (j + 1) * t)
        qa = qaug_scr[rows, :]
        carry = _softmax_step2(_softmax_init(t), _qk(qa, kaug_scr[rows, :]) + bias_scr[1], vo_scr[rows, :])
        if j > 0:
            keys = slice((j - 1) * t, j * t)
            carry = _softmax_step2(carry, _qk(qa, kaug_scr[keys, :]) + bias_scr[0], vo_scr[keys, :])
        for c in range(j - 1):
            keys = slice(c * t, (c + 1) * t)
            carry = _softmax_step2(carry, _qk(qa, kaug_scr[keys, :]), vo_scr[keys, :])
        o_ref[rows, :] = _softmax_result(carry).astype(o_ref.dtype)


def _moba_attn(qkv, rel_bias, B, S, slab0):
    T = B * S
    H = N_MOBA_HEADS
    blk = MOBA_BLOCK
    assert S % (2 * blk) == 0 and S // blk <= LANES
    nb = S // blk
    avg = np.zeros((LANES, S), np.float32)
    for n in range(nb):
        avg[n, n * blk:(n + 1) * blk] = 1.0 / blk
    hot = (avg.T > 0).astype(np.float32)
    table = _rel_bucket_table(2 * blk)
    dist = np.arange(blk)[:, None] - np.arange(blk)[None, :]
    buckets = np.stack([table[np.maximum(dist, 0)], table[dist + blk]]).astype(np.int32)
    grid_spec = pltpu.PrefetchScalarGridSpec(
        num_scalar_prefetch=1,
        grid=(B, H),
        in_specs=[
            pl.BlockSpec((1, S, HEAD_DIM), lambda b, h, r: (slab0 + h, b, 0)),
            pl.BlockSpec((1, S, HEAD_DIM), lambda b, h, r: (slab0 + H + h, b, 0)),
            pl.BlockSpec((1, S, HEAD_DIM), lambda b, h, r: (slab0 + 2 * H + h, b, 0)),
            pl.BlockSpec((LANES, S), lambda b, h, r: (0, 0)),
            pl.BlockSpec((S, LANES), lambda b, h, r: (0, 0)),
            pl.BlockSpec((2, blk, blk), lambda b, h, r: (0, 0, 0)),
        ],
        out_specs=pl.BlockSpec((S, HEAD_DIM), lambda b, h, r: (b, h)),
        scratch_shapes=[pltpu.VMEM((2, 2 * blk, 2 * blk), f32)] + [pltpu.VMEM((S, 2 * HEAD_DIM), bf16)] * 3,
    )
    return pl.pallas_call(
        _moba_attn_kernel,
        grid_spec=grid_spec,
        out_shape=jax.ShapeDtypeStruct((T, H * HEAD_DIM), bf16),
        compiler_params=_params("arbitrary", "arbitrary"),
        name="moba_attn",
    )(rel_bias, qkv, qkv, qkv, jnp.asarray(avg, bf16), jnp.asarray(hot, bf16), jnp.asarray(buckets))


def _rms(x, g):
    return x * lax.rsqrt(jnp.mean(x * x, axis=-1, keepdims=True) + EPS) * g


def _first_max(x, lane):
    mx = jnp.max(x, axis=-1, keepdims=True)
    return mx, jnp.min(jnp.where(x == mx, lane, LANES), axis=-1, keepdims=True)


def _route(logits, lane, grp=None):
    gl = jnp.where(lane < N_GROUPS, logits, -jnp.inf)
    gmax, top = _first_max(gl, lane)
    grp = top if grp is None else grp
    pg = jnp.exp(gl - gmax)
    p_top_group = jnp.sum(jnp.where(lane == grp, pg, 0.0), axis=-1, keepdims=True) / jnp.sum(
        pg, axis=-1, keepdims=True)
    lo = N_GROUPS + grp * EXPERTS_PER_GROUP
    in_grp = jnp.logical_and(lane >= lo, lane < lo + EXPERTS_PER_GROUP)
    el = jnp.where(in_grp, logits, -jnp.inf)
    pe = jnp.exp(el - jnp.max(el, axis=-1, keepdims=True))
    p_exp = jnp.where(in_grp, pe / jnp.sum(pe, axis=-1, keepdims=True), -1.0)
    p1, i1 = _first_max(p_exp, lane)
    p2, i2 = _first_max(jnp.where(lane == i1, -1.0, p_exp), lane)
    tot = p1 + p2
    combine = jnp.where(lane == i1, p1 / tot, jnp.where(lane == i2, p2 / tot, 0.0)) * p_top_group
    return grp, jnp.minimum(i1, i2) - lo, jnp.maximum(i1, i2) - lo, combine


def _out_proj_kernel(fo_ref, mo_ref, x_ref, gf_ref, gm_ref, w_ref, gffn_ref, wr_ref, br_ref,
                     x1_ref, route_ref, count_ref, count_scr):
    @pl.when(pl.program_id(0) == 0)
    def _():
        count_scr[...] = jnp.zeros_like(count_scr)

    D = x_ref.shape[1]
    part = x_ref.shape[0] // OUT_PROJ_PARTS
    lane = lax.broadcasted_iota(jnp.int32, (part, ROUTER_COLS), 1)
    earlier = (lax.broadcasted_iota(jnp.int32, (part, part), 1)
               < lax.broadcasted_iota(jnp.int32, (part, part), 0))
    earlier = jnp.where(earlier, 1.0, 0.0).astype(bf16)
    counts = count_scr[...]
    for p in range(OUT_PROJ_PARTS):
        rows = slice(p * part, (p + 1) * part)
        fo = _rms(fo_ref[rows, :].astype(f32), gf_ref[...]).astype(bf16)
        mo = _rms(mo_ref[rows, :].astype(f32), gm_ref[...]).astype(bf16)
        mix = jnp.concatenate([fo, mo], axis=-1)
        x1 = x_ref[rows, :] + jnp.dot(mix, w_ref[...], preferred_element_type=f32)
        h2 = _rms(x1, gffn_ref[...]).astype(bf16)
        logits = jnp.dot(h2, wr_ref[...], preferred_element_type=f32) + br_ref[...]
        grp, ea, eb, combine = _route(logits, lane)
        x1_ref[rows, :D] = x1
        x1_ref[rows, D:] = combine
        pair = (ea * (2 * EXPERTS_PER_GROUP - 1 - ea)) // 2 + eb - ea - 1
        cls = grp * EXPERT_PAIRS + pair
        onehot = jnp.where(lane == cls, 1.0, 0.0)
        before = jnp.dot(earlier, onehot.astype(bf16), preferred_element_type=f32) + counts
        rank = jnp.sum(onehot * before, axis=-1, keepdims=True).astype(jnp.int32)
        route_ref[rows, :] = jnp.where(lane == 0, cls, jnp.where(lane == 1, rank, 0))
        counts = counts + jnp.sum(onehot, axis=0, keepdims=True)
    count_scr[...] = counts
    count_ref[...] = counts.astype(jnp.int32)


def _out_proj(fo, mo, x2, gf, gm, w_out, g_ffn, w_r, b_r):
    T, D = x2.shape
    Wf = fo.shape[1]
    Wm = mo.shape[1]
    tm = _pick(T, 512)
    const = lambda i: (0, 0)
    rows = lambda i: (i, 0)
    return pl.pallas_call(
        _out_proj_kernel,
        grid=(T // tm,),
        in_specs=[
            pl.BlockSpec((tm, Wf), rows),
            pl.BlockSpec((tm, Wm), rows),
            pl.BlockSpec((tm, D), rows),
            pl.BlockSpec((1, Wf), const),
            pl.BlockSpec((1, Wm), const),
            pl.BlockSpec((Wf + Wm, D), const),
            pl.BlockSpec((1, D), const),
            pl.BlockSpec((D, ROUTER_COLS), const),
            pl.BlockSpec((1, ROUTER_COLS), const),
        ],
        out_specs=[
            pl.BlockSpec((tm, D + ROUTER_COLS), rows),
            pl.BlockSpec((tm, ROUTER_COLS), rows),
            pl.BlockSpec((1, ROUTER_COLS), const),
        ],
        out_shape=[
            jax.ShapeDtypeStruct((T, D + ROUTER_COLS), f32),
            jax.ShapeDtypeStruct((T, ROUTER_COLS), jnp.int32),
            jax.ShapeDtypeStruct((1, ROUTER_COLS), jnp.int32),
        ],
        scratch_shapes=[pltpu.VMEM((1, ROUTER_COLS), f32)],
        compiler_params=_params("arbitrary"),
        name="out_proj",
    )(fo, mo, x2, gf, gm, w_out, g_ffn, w_r, b_r)


def _slot(cls_ref, rank_ref, base_ref, t):
    return base_ref[cls_ref[t]] + rank_ref[t]


def _dispatch_kernel(cls_ref, rank_ref, base_ref, fill_ref, x_ref, xs_ref, zero_scr, sem, zsem):
    i = pl.program_id(0)
    tm = x_ref.shape[0]

    zrows = zero_scr.shape[0]

    def row_copy(r):
        d = _slot(cls_ref, rank_ref, base_ref, i * tm + r)
        return pltpu.make_async_copy(x_ref.at[pl.ds(r, 1)], xs_ref.at[pl.ds(d, 1)], sem)

    def zero_row(r):
        return pltpu.make_async_copy(zero_scr.at[pl.ds(0, 1)], xs_ref.at[pl.ds(r, 1)], zsem)

    def zero_sublanes(c):
        dst = xs_ref.at[pl.ds(pl.multiple_of(c * SUBLANES, SUBLANES), SUBLANES)]
        return pltpu.make_async_copy(zero_scr.at[pl.ds(0, SUBLANES)], dst, zsem)

    def zero_chunk(c):
        return pltpu.make_async_copy(zero_scr, xs_ref.at[pl.ds(pl.multiple_of(c * zrows, zrows), zrows)], zsem)

    def start(copy):
        def body(r, carry):
            copy(r).start()
            return carry
        return body

    def wait(copy):
        def body(r, carry):
            copy(r).wait()
            return carry
        return body

    lax.fori_loop(0, tm, start(row_copy), 0, unroll=DMA_UNROLL)

    @pl.when(i == 0)
    def _():
        zero_scr[...] = jnp.zeros_like(zero_scr)

        def fill_class(c, carry):
            lo, hi = fill_ref[2 * c], fill_ref[2 * c + 1]
            mid = jnp.minimum((lo + SUBLANES - 1) // SUBLANES * SUBLANES, hi)
            lax.fori_loop(lo, mid, start(zero_row), 0)
            lax.fori_loop(lo, mid, wait(zero_row), 0)
            lax.fori_loop(mid // SUBLANES, hi // SUBLANES, start(zero_sublanes), 0)
            lax.fori_loop(mid // SUBLANES, hi // SUBLANES, wait(zero_sublanes), 0)
            return carry

        lax.fori_loop(0, N_CLASSES, fill_class, 0)
        lo, hi = fill_ref[2 * N_CLASSES] // zrows, fill_ref[2 * N_CLASSES + 1] // zrows
        lax.fori_loop(lo, hi, start(zero_chunk), 0)
        lax.fori_loop(lo, hi, wait(zero_chunk), 0)

    pltpu.make_async_copy(x_ref, xs_ref.at[pl.ds(0, tm)], sem).wait()


def _dispatch(x1, cls, rank, base, fill, rows_out):
    T, D = x1.shape
    tm = _pick(T, 512)
    grid_spec = pltpu.PrefetchScalarGridSpec(
        num_scalar_prefetch=4,
        grid=(T // tm,),
        in_specs=[pl.BlockSpec((tm, D), lambda i, *_: (i, 0))],
        out_specs=pl.BlockSpec(memory_space=pl.ANY),
        scratch_shapes=[pltpu.VMEM((ZERO_ROWS, D), f32), pltpu.SemaphoreType.DMA, pltpu.SemaphoreType.DMA],
    )
    return pl.pallas_call(
        _dispatch_kernel,
        grid_spec=grid_spec,
        out_shape=jax.ShapeDtypeStruct((rows_out, D), f32),
        compiler_params=_params("arbitrary"),
        name="moe_dispatch",
    )(cls, rank, base, fill, x1)


def _collect_kernel(cls_ref, rank_ref, base_ref, ys_ref, o_ref, sem):
    i = pl.program_id(0)
    tm = o_ref.shape[0]

    def row_copy(r):
        d = _slot(cls_ref, rank_ref, base_ref, i * tm + r)
        return pltpu.make_async_copy(ys_ref.at[pl.ds(d, 1)], o_ref.at[pl.ds(r, 1)], sem)

    def start(r, carry):
        row_copy(r).start()
        return carry

    lax.fori_loop(0, tm, start, 0, unroll=DMA_UNROLL)
    pltpu.make_async_copy(ys_ref.at[pl.ds(0, tm)], o_ref, sem).wait()


def _collect(ys, cls, rank, base, T):
    D = ys.shape[1]
    tm = _pick(T, 512)
    grid_spec = pltpu.PrefetchScalarGridSpec(
        num_scalar_prefetch=3,
        grid=(T // tm,),
        in_specs=[pl.BlockSpec(memory_space=pl.ANY)],
        out_specs=pl.BlockSpec((tm, D), lambda i, *_: (i, 0)),
        scratch_shapes=[pltpu.SemaphoreType.DMA],
    )
    return pl.pallas_call(
        _collect_kernel,
        grid_spec=grid_spec,
        out_shape=jax.ShapeDtypeStruct((T, D), f32),
        compiler_params=_params("arbitrary"),
        name="moe_collect",
    )(cls, rank, base, ys)


def _moe_kernel(ea_ref, eb_ref, nu_ref, xs_ref, gffn_ref, wga_ref, wua_ref, wda_ref,
                wgb_ref, wub_ref, wdb_ref, gfin_ref, o_ref):
    i = pl.program_id(0)
    D = o_ref.shape[1]

    @pl.when(i < nu_ref[0])
    def _():
        x = xs_ref[:, :D]
        combine = xs_ref[:, D:]
        h = _rms(x, gffn_ref[...]).astype(bf16)
        lane = lax.broadcasted_iota(jnp.int32, combine.shape, 1)
        y = jnp.zeros(x.shape, f32)
        for e_ref, wg_ref, wu_ref, wd_ref in ((ea_ref, wga_ref, wua_ref, wda_ref),
                                               (eb_ref, wgb_ref, wub_ref, wdb_ref)):
            c = jnp.sum(jnp.where(lane == N_GROUPS + e_ref[i], combine, 0.0), axis=-1, keepdims=True)
            gate = jnp.dot(h, wg_ref[0], preferred_element_type=f32)
            up = jnp.dot(h, wu_ref[0], preferred_element_type=f32)
            hid = (jax.nn.silu(gate) * up * c).astype(bf16)
            y = y + jnp.dot(hid, wd_ref[0], preferred_element_type=f32)
        o_ref[...] = _rms(x + y, gfin_ref[...])

    @pl.when(i >= nu_ref[0])
    def _():
        o_ref[...] = jnp.zeros_like(o_ref)


def _moe(xs, tile_ea, tile_eb, n_used, g_ffn, w_gate, w_up, w_down, g_final, ts):
    R = xs.shape[0]
    E, D, F = w_gate.shape
    const = lambda i, ea, eb, nu: (0, 0)
    rows = lambda i, ea, eb, nu: (jnp.minimum(i, nu[0] - 1), 0)
    first = lambda i, ea, eb, nu: (ea[i], 0, 0)
    second = lambda i, ea, eb, nu: (eb[i], 0, 0)
    grid_spec = pltpu.PrefetchScalarGridSpec(
        num_scalar_prefetch=3,
        grid=(R // ts,),
        in_specs=[
            pl.BlockSpec((ts, D + ROUTER_COLS), rows),
            pl.BlockSpec((1, D), const),
            pl.BlockSpec((1, D, F), first),
            pl.BlockSpec((1, D, F), first),
            pl.BlockSpec((1, F, D), first),
            pl.BlockSpec((1, D, F), second),
            pl.BlockSpec((1, D, F), second),
            pl.BlockSpec((1, F, D), second),
            pl.BlockSpec((1, D), const),
        ],
        out_specs=pl.BlockSpec((ts, D), lambda i, ea, eb, nu: (i, 0)),
    )
    return pl.pallas_call(
        _moe_kernel,
        grid_spec=grid_spec,
        out_shape=jax.ShapeDtypeStruct((R, D), f32),
        compiler_params=_params("arbitrary"),
        name="moe",
    )(tile_ea, tile_eb, n_used, xs, g_ffn, w_gate, w_up, w_down, w_gate, w_up, w_down, g_final)


def kernel(x, attn_norm_g, w_in, b_forget, fox_out_norm_g, moba_out_norm_g, rel_bias, w_out, ffn_norm_g,
           w_group_router, b_group_router, w_expert_router, b_expert_router, w_gate, w_up, w_down,
           final_norm_g):
    B, S, D = x.shape
    T = B * S
    depth = w_in.shape[0]
    fox_w = N_FOX_HEADS * HEAD_DIM
    moba_w = N_MOBA_HEADS * HEAD_DIM
    qkv_w = 3 * (fox_w + moba_w)
    assert w_in.shape[2] == qkv_w + N_FOX_HEADS
    assert N_GROUPS + N_EXPERTS <= ROUTER_COLS

    x2 = x.reshape(T, D)
    out = None
    for l in range(depth):
        w_qkv = w_in[l, :, :qkv_w].astype(bf16)
        w_fg_t = w_in[l, :, qkv_w:].T.astype(bf16)
        assert fox_w == moba_w
        col_scale = np.ones((qkv_w // fox_w,), np.float32)
        col_scale[[0, 3]] = HEAD_DIM ** -0.5 * LOG2E
        qkv, f_t = _in_proj(x2, attn_norm_g[l].reshape(1, D), w_qkv, w_fg_t, jnp.asarray(col_scale), fox_w)
        key_bias = _fox_decay(f_t, b_forget[l].reshape(N_FOX_HEADS, 1), B, S)
        fo = _fox_attn(qkv, key_bias, B, S, 0)
        mo = _moba_attn(qkv, rel_bias, B, S, 3 * N_FOX_HEADS)

        w_r = jnp.concatenate([w_group_router[l], w_expert_router[l].reshape(D, N_EXPERTS)], axis=1)
        w_r = jnp.pad(w_r, ((0, 0), (0, ROUTER_COLS - w_r.shape[1]))).astype(bf16)
        b_r = jnp.concatenate([b_group_router[l], b_expert_router[l].reshape(N_EXPERTS)])
        b_r = jnp.pad(b_r, (0, ROUTER_COLS - b_r.shape[0])).reshape(1, ROUTER_COLS)
        g_ffn = ffn_norm_g[l].reshape(1, D)
        x1, route, count = _out_proj(fo, mo, x2, fox_out_norm_g[l].reshape(1, fox_w),
                                     moba_out_norm_g[l].reshape(1, moba_w), w_out[l].astype(bf16),
                                     g_ffn, w_r, b_r)
        ts = MOE_TILE
        cls, rank = route[:, 0], route[:, 1]
        counts = count[0, :N_CLASSES]
        padded = (counts + ts - 1) // ts * ts
        ends = jnp.cumsum(padded)
        base = ends - padded
        rows_out = (T // ts + N_CLASSES) * ts
        fill = jnp.concatenate([jnp.stack([base + counts, ends], axis=1).reshape(-1),
                                jnp.stack([ends[-1], jnp.asarray(rows_out, ends.dtype)])])
        tile_start = jnp.arange(rows_out // ts, dtype=jnp.int32) * ts
        tile_cls = jnp.minimum(jnp.sum(tile_start[:, None] >= ends[None, :], axis=1), N_CLASSES - 1)
        pairs = [(a, b) for a in range(EXPERTS_PER_GROUP) for b in range(a + 1, EXPERTS_PER_GROUP)]
        first_of_cls = np.array([g * EXPERTS_PER_GROUP + a for g in range(N_GROUPS) for a, _ in pairs], np.int32)
        second_of_cls = np.array([g * EXPERTS_PER_GROUP + b for g in range(N_GROUPS) for _, b in pairs], np.int32)
        tile_ea = jnp.asarray(first_of_cls)[tile_cls]
        tile_eb = jnp.asarray(second_of_cls)[tile_cls]
        n_used = (ends[-1:] // ts).astype(jnp.int32)

        assert l == depth - 1, "the fused residual + final norm epilogue expects a single layer"
        xs = _dispatch(x1, cls, rank, base, fill, rows_out)
        ys = _moe(xs, tile_ea, tile_eb, n_used, g_ffn, w_gate[l].astype(bf16),
                  w_up[l].astype(bf16), w_down[l].astype(bf16), final_norm_g.reshape(1, D), ts)
        out = _collect(ys, cls, rank, base, T)
        x2 = out
    return out.reshape(B, S, D)
```

```python
import functools
import math

import numpy as np
import jax
import jax.numpy as jnp
from jax import lax
from jax.experimental import pallas as pl
from jax.experimental.pallas import tpu as pltpu

HEAD_DIM = 128
N_FOX_HEADS = 8
N_MOBA_HEADS = 8
MOBA_BLOCK = 256
MOBA_TOPK = 3
REL_BUCKETS = 32
REL_MAX_DIST = 128
N_GROUPS = 4
EXPERTS_PER_GROUP = 4
N_EXPERTS = N_GROUPS * EXPERTS_PER_GROUP
EXPERT_PAIRS = EXPERTS_PER_GROUP * (EXPERTS_PER_GROUP - 1) // 2
N_CLASSES = N_GROUPS * EXPERT_PAIRS
EPS = 1e-6
NEG = -1e30
LOG2E = math.log2(math.e)
LANES = 128
SUBLANES = 8
ROUTER_COLS = LANES
OUT_PROJ_PARTS = 2
MOE_TILE = 256
KEY_BIAS_PARTS = 3
ZERO_ROWS = 64
DMA_UNROLL = 8
VMEM_LIMIT = 56 * 1024 * 1024

f32 = jnp.float32
bf16 = jnp.bfloat16


def _params(*sem):
    return pltpu.CompilerParams(dimension_semantics=sem, vmem_limit_bytes=VMEM_LIMIT)


def _pick(n, pref):
    t = min(n, pref)
    assert n % t == 0, (n, pref)
    return t


def _in_proj_kernel(cs_ref, x_ref, g_ref, w_ref, wfg_ref, qkv_ref, f_ref, h_scr):
    @pl.when(pl.program_id(1) == 0)
    def _():
        x = x_ref[...]
        y = x * lax.rsqrt(jnp.mean(x * x, axis=-1, keepdims=True) + EPS)
        hb = (y * g_ref[...]).astype(bf16)
        h_scr[...] = hb
        f_ref[...] = lax.dot_general(wfg_ref[...], hb, (((1,), (1,)), ((), ())),
                                     preferred_element_type=f32)

    w = w_ref[...].astype(bf16)
    acc = jnp.dot(h_scr[...], w, preferred_element_type=f32) * cs_ref[pl.program_id(1)]
    for c in range(acc.shape[1] // HEAD_DIM):
        qkv_ref[c] = acc[:, c * HEAD_DIM:(c + 1) * HEAD_DIM].astype(bf16)


def _in_proj(x2, g, w_in, w_fg_t, col_scale, tn):
    T, D = x2.shape
    N = col_scale.shape[0] * tn
    tm = _pick(T, 1024)
    assert N <= w_in.shape[1]
    nh = w_fg_t.shape[0]
    return pl.pallas_call(
        _in_proj_kernel,
        grid=(T // tm, N // tn),
        in_specs=[
            pl.BlockSpec(memory_space=pltpu.SMEM),
            pl.BlockSpec((tm, D), lambda i, j: (i, 0)),
            pl.BlockSpec((1, D), lambda i, j: (0, 0)),
            pl.BlockSpec((D, tn), lambda i, j: (0, j)),
            pl.BlockSpec((nh, D), lambda i, j: (0, 0)),
        ],
        out_specs=[
            pl.BlockSpec((tn // HEAD_DIM, tm, HEAD_DIM), lambda i, j: (j, i, 0)),
            pl.BlockSpec((nh, tm), lambda i, j: (0, i)),
        ],
        out_shape=[
            jax.ShapeDtypeStruct((N // HEAD_DIM, T, HEAD_DIM), bf16),
            jax.ShapeDtypeStruct((nh, T), f32),
        ],
        scratch_shapes=[pltpu.VMEM((tm, D), bf16)],
        compiler_params=_params("arbitrary", "arbitrary"),
        name="in_proj",
    )(col_scale, x2, g, w_in, w_fg_t)


def _fox_decay_kernel(f_ref, b_ref, o_ref):
    lf = jax.nn.log_sigmoid(f_ref[...] + b_ref[...])
    S = lf.shape[1]
    lane = lax.broadcasted_iota(jnp.int32, lf.shape, 1)
    c = lf
    sh = 1
    while sh < S:
        c = c + jnp.where(lane >= sh, pltpu.roll(c, sh, axis=1), 0.0)
        sh *= 2
    rest = -c * LOG2E
    parts = []
    for _ in range(KEY_BIAS_PARTS):
        part = rest.astype(bf16).astype(f32)
        parts.append(part)
        rest = rest - part
    nh = lf.shape[0]
    cols = jnp.concatenate(parts + [jnp.zeros((LANES - KEY_BIAS_PARTS * nh, S), f32)], axis=0)
    o_ref[0] = jnp.transpose(cols).astype(bf16)


def _fox_decay(f_t, b_forget, B, S):
    nh = f_t.shape[0]
    assert KEY_BIAS_PARTS * nh <= LANES
    return pl.pallas_call(
        _fox_decay_kernel,
        grid=(B,),
        in_specs=[
            pl.BlockSpec((nh, S), lambda b: (0, b)),
            pl.BlockSpec((nh, 1), lambda b: (0, 0)),
        ],
        out_specs=pl.BlockSpec((1, S, LANES), lambda b: (b, 0, 0)),
        out_shape=jax.ShapeDtypeStruct((B, S, LANES), bf16),
        compiler_params=_params("arbitrary"),
        name="fox_decay",
    )(f_t, b_forget)


def _qk(q, k):
    return lax.dot_general(q, k, (((1,), (1,)), ((), ())), preferred_element_type=f32)


def _softmax_init(t):
    return jnp.full((t, 1), NEG, f32), jnp.zeros((t, 2 * HEAD_DIM), f32)


def _softmax_step2(carry, s2, v_ones):
    m, acc = carry
    m_new = jnp.maximum(m, jnp.max(s2, axis=-1, keepdims=True))
    p = jnp.exp2(s2 - m_new).astype(bf16)
    acc = jnp.exp2(m - m_new) * acc + jnp.dot(p, v_ones, preferred_element_type=f32)
    return m_new, acc


def _softmax_result(carry):
    _, acc = carry
    return acc[:, :HEAD_DIM] / acc[:, HEAD_DIM:]


def _with_ones(vo_scr, v_ref):
    vo_scr[:, :HEAD_DIM] = v_ref[0]
    vo_scr[:, HEAD_DIM:] = jnp.ones((vo_scr.shape[0], HEAD_DIM), vo_scr.dtype)


def _fox_attn_kernel(q_ref, k_ref, v_ref, kb_ref, o_ref, qaug_scr, kaug_scr, vo_scr, *, t):
    h = pl.program_id(1)
    S = q_ref.shape[1]
    lane = lax.broadcasted_iota(jnp.int32, (S, LANES), 1)
    mine = jnp.logical_and(lane % N_FOX_HEADS == h, lane < KEY_BIAS_PARTS * N_FOX_HEADS)
    qaug_scr[:, :HEAD_DIM] = q_ref[0]
    qaug_scr[:, HEAD_DIM:] = jnp.where(mine, 1.0, 0.0).astype(bf16)
    kaug_scr[:, :HEAD_DIM] = k_ref[0]
    kaug_scr[:, HEAD_DIM:] = kb_ref[0]
    _with_ones(vo_scr, v_ref)
    row = lax.broadcasted_iota(jnp.int32, (t, t), 0)
    col = lax.broadcasted_iota(jnp.int32, (t, t), 1)
    for qi in range(S // t):
        qa = qaug_scr[qi * t:(qi + 1) * t, :]
        carry = _softmax_init(t)
        for kb in range(qi + 1):
            keys = slice(kb * t, (kb + 1) * t)
            s2 = _qk(qa, kaug_scr[keys, :])
            if kb == qi:
                s2 = jnp.where(col <= row, s2, NEG)
            carry = _softmax_step2(carry, s2, vo_scr[keys, :])
        o_ref[qi * t:(qi + 1) * t, :] = _softmax_result(carry).astype(o_ref.dtype)


def _fox_attn(qkv, key_bias, B, S, slab0):
    T = B * S
    H = N_FOX_HEADS
    t = _pick(S, 512)
    kern = functools.partial(_fox_attn_kernel, t=t)
    return pl.pallas_call(
        kern,
        grid=(B, H),
        in_specs=[
            pl.BlockSpec((1, S, HEAD_DIM), lambda b, h: (slab0 + h, b, 0)),
            pl.BlockSpec((1, S, HEAD_DIM), lambda b, h: (slab0 + H + h, b, 0)),
            pl.BlockSpec((1, S, HEAD_DIM), lambda b, h: (slab0 + 2 * H + h, b, 0)),
            pl.BlockSpec((1, S, LANES), lambda b, h: (b, 0, 0)),
        ],
        out_specs=pl.BlockSpec((S, HEAD_DIM), lambda b, h: (b, h)),
        out_shape=jax.ShapeDtypeStruct((T, H * HEAD_DIM), bf16),
        scratch_shapes=[pltpu.VMEM((S, 2 * HEAD_DIM), bf16)] * 3,
        compiler_params=_params("arbitrary", "arbitrary"),
        name="fox_attn",
    )(qkv, qkv, qkv, key_bias)


def _rel_bucket_table(n):
    max_exact = REL_BUCKETS // 2
    d = np.arange(n)
    ratio = np.log(np.maximum(d, 1).astype(np.float32) / np.float32(max_exact)) / np.float32(
        math.log(REL_MAX_DIST / max_exact))
    large = max_exact + (ratio * np.float32(REL_BUCKETS - max_exact)).astype(np.int32)
    large = np.minimum(large, REL_BUCKETS - 1)
    return np.where(d < max_exact, d, large).astype(np.int32)


def _moba_attn_kernel(rel_ref, q_ref, k_ref, v_ref, avg_ref, hot_ref, bk_ref, o_ref, bias_scr, kaug_scr,
                      qaug_scr, vo_scr):
    h = pl.program_id(1)
    blk = MOBA_BLOCK
    t = 2 * blk
    S = q_ref.shape[1]
    far_bias = rel_ref[h, REL_BUCKETS - 1]
    row = lax.broadcasted_iota(jnp.int32, (blk, blk), 0)
    col = lax.broadcasted_iota(jnp.int32, (blk, blk), 1)
    tiles = []
    for which in range(2):
        bk = bk_ref[which]
        tile = jnp.zeros((blk, blk), f32)
        for b in range(REL_BUCKETS - 1):
            tile = jnp.where(bk == b, (rel_ref[h, b] - far_bias) * LOG2E, tile)
        tiles.append(jnp.where(col <= row, tile, NEG) if which == 0 else tile)
    own_tile, prev_tile = tiles
    zero_tile = jnp.zeros((blk, blk), f32)
    for which, quads in enumerate((((zero_tile, prev_tile), (zero_tile, zero_tile)),
                                   ((own_tile, zero_tile), (prev_tile, own_tile)))):
        for r in range(2):
            for c in range(2):
                bias_scr[which, r * blk:(r + 1) * blk, c * blk:(c + 1) * blk] = quads[r][c]
    kaug_scr[:, :HEAD_DIM] = k_ref[0]
    kaug_scr[:, HEAD_DIM:] = hot_ref[...]
    _with_ones(vo_scr, v_ref)
    kmean = jnp.dot(avg_ref[...], k_ref[0], preferred_element_type=f32).astype(bf16)

    nbp = -(-(S // blk) // SUBLANES) * SUBLANES
    block = lax.broadcasted_iota(jnp.int32, (nbp, S), 0)
    own = lax.broadcasted_iota(jnp.int32, (nbp, S), 1) // blk
    valid = block < own
    gate = jnp.where(valid, _qk(kmean[:nbp], q_ref[0]), NEG)
    sel = block == own
    for _ in range(MOBA_TOPK):
        mx = jnp.max(gate, axis=0, keepdims=True)
        first = jnp.min(jnp.where(gate == mx, block, nbp), axis=0, keepdims=True)
        pick = block == first
        sel = jnp.logical_or(sel, jnp.logical_and(pick, valid))
        gate = jnp.where(pick, -jnp.inf, gate)
    chosen = jnp.concatenate([jnp.where(sel, 0.0, NEG), jnp.full((LANES - nbp, S), NEG, f32)], axis=0)
    qaug_scr[:, :HEAD_DIM] = q_ref[0]
    qaug_scr[:, HEAD_DIM:] = jnp.transpose(chosen).astype(bf16)

    for j in range(S // t):
        rows = slice(j * t, (j + 1) * t)
        qa = qaug_scr[rows, :]
        carry = _softmax_step2(_softmax_init(t), _qk(qa, kaug_scr[rows, :]) + bias_scr[1], vo_scr[rows, :])
        if j > 0:
            keys = slice((j - 1) * t, j * t)
            carry = _softmax_step2(carry, _qk(qa, kaug_scr[keys, :]) + bias_scr[0], vo_scr[keys, :])
        for c in range(j - 1):
            keys = slice(c * t, (c + 1) * t)
            carry = _softmax_step2(carry, _qk(qa, kaug_scr[keys, :]), vo_scr[keys, :])
        o_ref[rows, :] = _softmax_result(carry).astype(o_ref.dtype)


def _moba_attn(qkv, rel_bias, B, S, slab0):
    T = B * S
    H = N_MOBA_HEADS
    blk = MOBA_BLOCK
    assert S % (2 * blk) == 0 and S // blk <= LANES
    nb = S // blk
    avg = np.zeros((LANES, S), np.float32)
    for n in range(nb):
        avg[n, n * blk:(n + 1) * blk] = 1.0 / blk
    hot = (avg.T > 0).astype(np.float32)
    table = _rel_bucket_table(2 * blk)
    dist = np.arange(blk)[:, None] - np.arange(blk)[None, :]
    buckets = np.stack([table[np.maximum(dist, 0)], table[dist + blk]]).astype(np.int32)
    grid_spec = pltpu.PrefetchScalarGridSpec(
        num_scalar_prefetch=1,
        grid=(B, H),
        in_specs=[
            pl.BlockSpec((1, S, HEAD_DIM), lambda b, h, r: (slab0 + h, b, 0)),
            pl.BlockSpec((1, S, HEAD_DIM), lambda b, h, r: (slab0 + H + h, b, 0)),
            pl.BlockSpec((1, S, HEAD_DIM), lambda b, h, r: (slab0 + 2 * H + h, b, 0)),
            pl.BlockSpec((LANES, S), lambda b, h, r: (0, 0)),
            pl.BlockSpec((S, LANES), lambda b, h, r: (0, 0)),
            pl.BlockSpec((2, blk, blk), lambda b, h, r: (0, 0, 0)),
        ],
        out_specs=pl.BlockSpec((S, HEAD_DIM), lambda b, h, r: (b, h)),
        scratch_shapes=[pltpu.VMEM((2, 2 * blk, 2 * blk), f32)] + [pltpu.VMEM((S, 2 * HEAD_DIM), bf16)] * 3,
    )
    return pl.pallas_call(
        _moba_attn_kernel,
        grid_spec=grid_spec,
        out_shape=jax.ShapeDtypeStruct((T, H * HEAD_DIM), bf16),
        compiler_params=_params("arbitrary", "arbitrary"),
        name="moba_attn",
    )(rel_bias, qkv, qkv, qkv, jnp.asarray(avg, bf16), jnp.asarray(hot, bf16), jnp.asarray(buckets))


def _rms(x, g):
    return x * lax.rsqrt(jnp.mean(x * x, axis=-1, keepdims=True) + EPS) * g


def _first_max(x, lane):
    mx = jnp.max(x, axis=-1, keepdims=True)
    return mx, jnp.min(jnp.where(x == mx, lane, LANES), axis=-1, keepdims=True)


def _route(logits, lane, grp=None):
    gl = jnp.where(lane < N_GROUPS, logits, -jnp.inf)
    gmax, top = _first_max(gl, lane)
    grp = top if grp is None else grp
    pg = jnp.exp(gl - gmax)
    p_top_group = jnp.sum(jnp.where(lane == grp, pg, 0.0), axis=-1, keepdims=True) / jnp.sum(
        pg, axis=-1, keepdims=True)
    lo = N_GROUPS + grp * EXPERTS_PER_GROUP
    in_grp = jnp.logical_and(lane >= lo, lane < lo + EXPERTS_PER_GROUP)
    el = jnp.where(in_grp, logits, -jnp.inf)
    pe = jnp.exp(el - jnp.max(el, axis=-1, keepdims=True))
    p_exp = jnp.where(in_grp, pe / jnp.sum(pe, axis=-1, keepdims=True), -1.0)
    p1, i1 = _first_max(p_exp, lane)
    p2, i2 = _first_max(jnp.where(lane == i1, -1.0, p_exp), lane)
    tot = p1 + p2
    combine = jnp.where(lane == i1, p1 / tot, jnp.where(lane == i2, p2 / tot, 0.0)) * p_top_group
    return grp, jnp.minimum(i1, i2) - lo, jnp.maximum(i1, i2) - lo, combine


def _out_proj_kernel(fo_ref, mo_ref, x_ref, gf_ref, gm_ref, w_ref, gffn_ref, wr_ref, br_ref,
                     x1_ref, route_ref, count_ref, count_scr):
    @pl.when(pl.program_id(0) == 0)
    def _():
        count_scr[...] = jnp.zeros_like(count_scr)

    D = x_ref.shape[1]
    part = x_ref.shape[0] // OUT_PROJ_PARTS
    lane = lax.broadcasted_iota(jnp.int32, (part, ROUTER_COLS), 1)
    earlier = (lax.broadcasted_iota(jnp.int32, (part, part), 1)
               < lax.broadcasted_iota(jnp.int32, (part, part), 0))
    earlier = jnp.where(earlier, 1.0, 0.0).astype(bf16)
    counts = count_scr[...]
    for p in range(OUT_PROJ_PARTS):
        rows = slice(p * part, (p + 1) * part)
        fo = _rms(fo_ref[rows, :].astype(f32), gf_ref[...]).astype(bf16)
        mo = _rms(mo_ref[rows, :].astype(f32), gm_ref[...]).astype(bf16)
        mix = jnp.concatenate([fo, mo], axis=-1)
        x1 = x_ref[rows, :] + jnp.dot(mix, w_ref[...], preferred_element_type=f32)
        h2 = _rms(x1, gffn_ref[...]).astype(bf16)
        logits = jnp.dot(h2, wr_ref[...], preferred_element_type=f32) + br_ref[...]
        grp, ea, eb, combine = _route(logits, lane)
        x1_ref[rows, :D] = x1
        x1_ref[rows, D:] = combine
        pair = (ea * (2 * EXPERTS_PER_GROUP - 1 - ea)) // 2 + eb - ea - 1
        cls = grp * EXPERT_PAIRS + pair
        onehot = jnp.where(lane == cls, 1.0, 0.0)
        before = jnp.dot(earlier, onehot.astype(bf16), preferred_element_type=f32) + counts
        rank = jnp.sum(onehot * before, axis=-1, keepdims=True).astype(jnp.int32)
        route_ref[rows, :] = jnp.where(lane == 0, cls, jnp.where(lane == 1, rank, 0))
        counts = counts + jnp.sum(onehot, axis=0, keepdims=True)
    count_scr[...] = counts
    count_ref[...] = counts.astype(jnp.int32)


def _out_proj(fo, mo, x2, gf, gm, w_out, g_ffn, w_r, b_r):
    T, D = x2.shape
    Wf = fo.shape[1]
    Wm = mo.shape[1]
    tm = _pick(T, 512)
    const = lambda i: (0, 0)
    rows = lambda i: (i, 0)
    return pl.pallas_call(
        _out_proj_kernel,
        grid=(T // tm,),
        in_specs=[
            pl.BlockSpec((tm, Wf), rows),
            pl.BlockSpec((tm, Wm), rows),
            pl.BlockSpec((tm, D), rows),
            pl.BlockSpec((1, Wf), const),
            pl.BlockSpec((1, Wm), const),
            pl.BlockSpec((Wf + Wm, D), const),
            pl.BlockSpec((1, D), const),
            pl.BlockSpec((D, ROUTER_COLS), const),
            pl.BlockSpec((1, ROUTER_COLS), const),
        ],
        out_specs=[
            pl.BlockSpec((tm, D + ROUTER_COLS), rows),
            pl.BlockSpec((tm, ROUTER_COLS), rows),
            pl.BlockSpec((1, ROUTER_COLS), const),
        ],
        out_shape=[
            jax.ShapeDtypeStruct((T, D + ROUTER_COLS), f32),
            jax.ShapeDtypeStruct((T, ROUTER_COLS), jnp.int32),
            jax.ShapeDtypeStruct((1, ROUTER_COLS), jnp.int32),
        ],
        scratch_shapes=[pltpu.VMEM((1, ROUTER_COLS), f32)],
        compiler_params=_params("arbitrary"),
        name="out_proj",
    )(fo, mo, x2, gf, gm, w_out, g_ffn, w_r, b_r)


def _start_rows(row_copy, n):
    assert n % DMA_UNROLL == 0

    def body(g, carry):
        for k in range(DMA_UNROLL):
            row_copy(g * DMA_UNROLL + k).start(priority=k % 2)
        return carry

    lax.fori_loop(0, n // DMA_UNROLL, body, 0)


def _slot(cls_ref, rank_ref, base_ref, t):
    return base_ref[cls_ref[t]] + rank_ref[t]


def _dispatch_kernel(cls_ref, rank_ref, base_ref, fill_ref, x_ref, xs_ref, zero_scr, sem, zsem):
    i = pl.program_id(0)
    tm = x_ref.shape[0]

    zrows = zero_scr.shape[0]

    def row_copy(r):
        d = _slot(cls_ref, rank_ref, base_ref, i * tm + r)
        return pltpu.make_async_copy(x_ref.at[pl.ds(r, 1)], xs_ref.at[pl.ds(d, 1)], sem)

    def zero_row(r):
        return pltpu.make_async_copy(zero_scr.at[pl.ds(0, 1)], xs_ref.at[pl.ds(r, 1)], zsem)

    def zero_sublanes(c):
        dst = xs_ref.at[pl.ds(pl.multiple_of(c * SUBLANES, SUBLANES), SUBLANES)]
        return pltpu.make_async_copy(zero_scr.at[pl.ds(0, SUBLANES)], dst, zsem)

    def zero_chunk(c):
        return pltpu.make_async_copy(zero_scr, xs_ref.at[pl.ds(pl.multiple_of(c * zrows, zrows), zrows)], zsem)

    def start(copy):
        def body(r, carry):
            copy(r).start()
            return carry
        return body

    def wait(copy):
        def body(r, carry):
            copy(r).wait()
            return carry
        return body

    _start_rows(row_copy, tm)

    @pl.when(i == 0)
    def _():
        zero_scr[...] = jnp.zeros_like(zero_scr)

        def fill_class(c, carry):
            lo, hi = fill_ref[2 * c], fill_ref[2 * c + 1]
            mid = jnp.minimum((lo + SUBLANES - 1) // SUBLANES * SUBLANES, hi)
            lax.fori_loop(lo, mid, start(zero_row), 0)
            lax.fori_loop(lo, mid, wait(zero_row), 0)
            lax.fori_loop(mid // SUBLANES, hi // SUBLANES, start(zero_sublanes), 0)
            lax.fori_loop(mid // SUBLANES, hi // SUBLANES, wait(zero_sublanes), 0)
            return carry

        lax.fori_loop(0, N_CLASSES, fill_class, 0)
        lo, hi = fill_ref[2 * N_CLASSES] // zrows, fill_ref[2 * N_CLASSES + 1] // zrows
        lax.fori_loop(lo, hi, start(zero_chunk), 0)
        lax.fori_loop(lo, hi, wait(zero_chunk), 0)

    pltpu.make_async_copy(x_ref, xs_ref.at[pl.ds(0, tm)], sem).wait()


def _dispatch(x1, cls, rank, base, fill, rows_out):
    T, D = x1.shape
    tm = _pick(T, 512)
    grid_spec = pltpu.PrefetchScalarGridSpec(
        num_scalar_prefetch=4,
        grid=(T // tm,),
        in_specs=[pl.BlockSpec((tm, D), lambda i, *_: (i, 0))],
        out_specs=pl.BlockSpec(memory_space=pl.ANY),
        scratch_shapes=[pltpu.VMEM((ZERO_ROWS, D), f32), pltpu.SemaphoreType.DMA, pltpu.SemaphoreType.DMA],
    )
    return pl.pallas_call(
        _dispatch_kernel,
        grid_spec=grid_spec,
        out_shape=jax.ShapeDtypeStruct((rows_out, D), f32),
        compiler_params=_params("arbitrary"),
        name="moe_dispatch",
    )(cls, rank, base, fill, x1)


def _collect_kernel(cls_ref, rank_ref, base_ref, ys_ref, o_ref, sem):
    i = pl.program_id(0)
    tm = o_ref.shape[0]

    def row_copy(r):
        d = _slot(cls_ref, rank_ref, base_ref, i * tm + r)
        return pltpu.make_async_copy(ys_ref.at[pl.ds(d, 1)], o_ref.at[pl.ds(r, 1)], sem)

    _start_rows(row_copy, tm)
    pltpu.make_async_copy(ys_ref.at[pl.ds(0, tm)], o_ref, sem).wait()


def _collect(ys, cls, rank, base, T):
    D = ys.shape[1]
    tm = _pick(T, 512)
    grid_spec = pltpu.PrefetchScalarGridSpec(
        num_scalar_prefetch=3,
        grid=(T // tm,),
        in_specs=[pl.BlockSpec(memory_space=pl.ANY)],
        out_specs=pl.BlockSpec((tm, D), lambda i, *_: (i, 0)),
        scratch_shapes=[pltpu.SemaphoreType.DMA],
    )
    return pl.pallas_call(
        _collect_kernel,
        grid_spec=grid_spec,
        out_shape=jax.ShapeDtypeStruct((T, D), f32),
        compiler_params=_params("arbitrary"),
        name="moe_collect",
    )(cls, rank, base, ys)


def _moe_kernel(ea_ref, eb_ref, nu_ref, xs_ref, gffn_ref, wga_ref, wua_ref, wda_ref,
                wgb_ref, wub_ref, wdb_ref, gfin_ref, o_ref):
    i = pl.program_id(0)
    D = o_ref.shape[1]

    @pl.when(i < nu_ref[0])
    def _():
        x = xs_ref[:, :D]
        combine = xs_ref[:, D:]
        h = _rms(x, gffn_ref[...]).astype(bf16)
        lane = lax.broadcasted_iota(jnp.int32, combine.shape, 1)
        y = jnp.zeros(x.shape, f32)
        for e_ref, wg_ref, wu_ref, wd_ref in ((ea_ref, wga_ref, wua_ref, wda_ref),
                                               (eb_ref, wgb_ref, wub_ref, wdb_ref)):
            c = jnp.sum(jnp.where(lane == N_GROUPS + e_ref[i], combine, 0.0), axis=-1, keepdims=True)
            gate = jnp.dot(h, wg_ref[0], preferred_element_type=f32)
            up = jnp.dot(h, wu_ref[0], preferred_element_type=f32)
            hid = (jax.nn.silu(gate) * up * c).astype(bf16)
            y = y + jnp.dot(hid, wd_ref[0], preferred_element_type=f32)
        o_ref[...] = _rms(x + y, gfin_ref[...])

    @pl.when(i >= nu_ref[0])
    def _():
        o_ref[...] = jnp.zeros_like(o_ref)


def _moe(xs, tile_ea, tile_eb, n_used, g_ffn, w_gate, w_up, w_down, g_final, ts):
    R = xs.shape[0]
    E, D, F = w_gate.shape
    const = lambda i, ea, eb, nu: (0, 0)
    rows = lambda i, ea, eb, nu: (jnp.minimum(i, nu[0] - 1), 0)
    first = lambda i, ea, eb, nu: (ea[i], 0, 0)
    second = lambda i, ea, eb, nu: (eb[i], 0, 0)
    grid_spec = pltpu.PrefetchScalarGridSpec(
        num_scalar_prefetch=3,
        grid=(R // ts,),
        in_specs=[
            pl.BlockSpec((ts, D + ROUTER_COLS), rows),
            pl.BlockSpec((1, D), const),
            pl.BlockSpec((1, D, F), first),
            pl.BlockSpec((1, D, F), first),
            pl.BlockSpec((1, F, D), first),
            pl.BlockSpec((1, D, F), second),
            pl.BlockSpec((1, D, F), second),
            pl.BlockSpec((1, F, D), second),
            pl.BlockSpec((1, D), const),
        ],
        out_specs=pl.BlockSpec((ts, D), lambda i, ea, eb, nu: (i, 0)),
    )
    return pl.pallas_call(
        _moe_kernel,
        grid_spec=grid_spec,
        out_shape=jax.ShapeDtypeStruct((R, D), f32),
        compiler_params=_params("arbitrary"),
        name="moe",
    )(tile_ea, tile_eb, n_used, xs, g_ffn, w_gate, w_up, w_down, w_gate, w_up, w_down, g_final)


def kernel(x, attn_norm_g, w_in, b_forget, fox_out_norm_g, moba_out_norm_g, rel_bias, w_out, ffn_norm_g,
           w_group_router, b_group_router, w_expert_router, b_expert_router, w_gate, w_up, w_down,
           final_norm_g):
    B, S, D = x.shape
    T = B * S
    depth = w_in.shape[0]
    fox_w = N_FOX_HEADS * HEAD_DIM
    moba_w = N_MOBA_HEADS * HEAD_DIM
    qkv_w = 3 * (fox_w + moba_w)
    assert w_in.shape[2] == qkv_w + N_FOX_HEADS
    assert N_GROUPS + N_EXPERTS <= ROUTER_COLS

    x2 = x.reshape(T, D)
    out = None
    for l in range(depth):
        w_fg_t = w_in[l, :, qkv_w:].T.astype(bf16)
        assert fox_w == moba_w
        col_scale = np.ones((qkv_w // fox_w,), np.float32)
        col_scale[[0, 3]] = HEAD_DIM ** -0.5 * LOG2E
        qkv, f_t = _in_proj(x2, attn_norm_g[l].reshape(1, D), w_in[l], w_fg_t, jnp.asarray(col_scale), fox_w)
        key_bias = _fox_decay(f_t, b_forget[l].reshape(N_FOX_HEADS, 1), B, S)
        fo = _fox_attn(qkv, key_bias, B, S, 0)
        mo = _moba_attn(qkv, rel_bias, B, S, 3 * N_FOX_HEADS)

        w_r = jnp.concatenate([w_group_router[l], w_expert_router[l].reshape(D, N_EXPERTS)], axis=1)
        w_r = jnp.pad(w_r, ((0, 0), (0, ROUTER_COLS - w_r.shape[1]))).astype(bf16)
        b_r = jnp.concatenate([b_group_router[l], b_expert_router[l].reshape(N_EXPERTS)])
        b_r = jnp.pad(b_r, (0, ROUTER_COLS - b_r.shape[0])).reshape(1, ROUTER_COLS)
        g_ffn = ffn_norm_g[l].reshape(1, D)
        x1, route, count = _out_proj(fo, mo, x2, fox_out_norm_g[l].reshape(1, fox_w),
                                     moba_out_norm_g[l].reshape(1, moba_w), w_out[l].astype(bf16),
                                     g_ffn, w_r, b_r)
        ts = MOE_TILE
        cls, rank = route[:, 0], route[:, 1]
        counts = count[0, :N_CLASSES]
        padded = (counts + ts - 1) // ts * ts
        ends = jnp.cumsum(padded)
        base = ends - padded
        rows_out = (T // ts + N_CLASSES) * ts
        fill = jnp.concatenate([jnp.stack([base + counts, ends], axis=1).reshape(-1),
                                jnp.stack([ends[-1], jnp.asarray(rows_out, ends.dtype)])])
        tile_start = jnp.arange(rows_out // ts, dtype=jnp.int32) * ts
        tile_cls = jnp.minimum(jnp.sum(tile_start[:, None] >= ends[None, :], axis=1), N_CLASSES - 1)
        pairs = [(a, b) for a in range(EXPERTS_PER_GROUP) for b in range(a + 1, EXPERTS_PER_GROUP)]
        first_of_cls = np.array([g * EXPERTS_PER_GROUP + a for g in range(N_GROUPS) for a, _ in pairs], np.int32)
        second_of_cls = np.array([g * EXPERTS_PER_GROUP + b for g in range(N_GROUPS) for _, b in pairs], np.int32)
        tile_ea = jnp.asarray(first_of_cls)[tile_cls]
        tile_eb = jnp.asarray(second_of_cls)[tile_cls]
        n_used = (ends[-1:] // ts).astype(jnp.int32)

        assert l == depth - 1, "the fused residual + final norm epilogue expects a single layer"
        xs = _dispatch(x1, cls, rank, base, fill, rows_out)
        ys = _moe(xs, tile_ea, tile_eb, n_used, g_ffn, w_gate[l].astype(bf16),
                  w_up[l].astype(bf16), w_down[l].astype(bf16), final_norm_g.reshape(1, D), ts)
        out = _collect(ys, cls, rank, base, T)
        x2 = out
    return out.reshape(B, S, D)
```

```python
import functools
import math

import numpy as np
import jax
import jax.numpy as jnp
from jax import lax
from jax.experimental import pallas as pl
from jax.experimental.pallas import tpu as pltpu

HEAD_DIM = 128
N_FOX_HEADS = 8
N_MOBA_HEADS = 8
MOBA_BLOCK = 256
MOBA_TOPK = 3
REL_BUCKETS = 32
REL_MAX_DIST = 128
N_GROUPS = 4
EXPERTS_PER_GROUP = 4
N_EXPERTS = N_GROUPS * EXPERTS_PER_GROUP
EXPERT_PAIRS = EXPERTS_PER_GROUP * (EXPERTS_PER_GROUP - 1) // 2
N_CLASSES = N_GROUPS * EXPERT_PAIRS
EPS = 1e-6
NEG = -1e30
LOG2E = math.log2(math.e)
LANES = 128
SUBLANES = 8
ROUTER_COLS = LANES
OUT_PROJ_PARTS = 2
MOE_TILE = 256
KEY_BIAS_PARTS = 3
ZERO_ROWS = 64
DMA_UNROLL = 8
VMEM_LIMIT = 56 * 1024 * 1024

f32 = jnp.float32
bf16 = jnp.bfloat16


def _params(*sem):
    return pltpu.CompilerParams(dimension_semantics=sem, vmem_limit_bytes=VMEM_LIMIT)


def _pick(n, pref):
    t = min(n, pref)
    assert n % t == 0, (n, pref)
    return t


def _in_proj_kernel(cs_ref, x_ref, g_ref, w_ref, wfg_ref, qkv_ref, f_ref, h_scr):
    @pl.when(pl.program_id(1) == 0)
    def _():
        x = x_ref[...]
        y = x * lax.rsqrt(jnp.mean(x * x, axis=-1, keepdims=True) + EPS)
        hb = (y * g_ref[...]).astype(bf16)
        h_scr[...] = hb
        f_ref[...] = lax.dot_general(wfg_ref[...], hb, (((1,), (1,)), ((), ())),
                                     preferred_element_type=f32)

    acc = jnp.dot(h_scr[...], w_ref[...], preferred_element_type=f32) * cs_ref[pl.program_id(1)]
    for c in range(acc.shape[1] // HEAD_DIM):
        qkv_ref[c] = acc[:, c * HEAD_DIM:(c + 1) * HEAD_DIM].astype(bf16)


def _in_proj(x2, g, w_in, layer, w_fg_t, col_scale, tn):
    T, D = x2.shape
    N = col_scale.shape[0] * tn
    tm = _pick(T, 1024)
    assert N <= w_in.shape[2]
    nh = w_fg_t.shape[0]
    return pl.pallas_call(
        _in_proj_kernel,
        grid=(T // tm, N // tn),
        in_specs=[
            pl.BlockSpec(memory_space=pltpu.SMEM),
            pl.BlockSpec((tm, D), lambda i, j: (i, 0)),
            pl.BlockSpec((1, D), lambda i, j: (0, 0)),
            pl.BlockSpec((None, D, tn), lambda i, j: (layer, 0, j)),
            pl.BlockSpec((nh, D), lambda i, j: (0, 0)),
        ],
        out_specs=[
            pl.BlockSpec((tn // HEAD_DIM, tm, HEAD_DIM), lambda i, j: (j, i, 0)),
            pl.BlockSpec((nh, tm), lambda i, j: (0, i)),
        ],
        out_shape=[
            jax.ShapeDtypeStruct((N // HEAD_DIM, T, HEAD_DIM), bf16),
            jax.ShapeDtypeStruct((nh, T), f32),
        ],
        scratch_shapes=[pltpu.VMEM((tm, D), bf16)],
        compiler_params=_params("arbitrary", "arbitrary"),
        name="in_proj",
    )(col_scale, x2, g, w_in, w_fg_t)


def _fox_decay_kernel(f_ref, b_ref, o_ref):
    lf = jax.nn.log_sigmoid(f_ref[...] + b_ref[...])
    S = lf.shape[1]
    lane = lax.broadcasted_iota(jnp.int32, lf.shape, 1)
    c = lf
    sh = 1
    while sh < S:
        c = c + jnp.where(lane >= sh, pltpu.roll(c, sh, axis=1), 0.0)
        sh *= 2
    rest = -c * LOG2E
    parts = []
    for _ in range(KEY_BIAS_PARTS):
        part = rest.astype(bf16).astype(f32)
        parts.append(part)
        rest = rest - part
    nh = lf.shape[0]
    cols = jnp.concatenate(parts + [jnp.zeros((LANES - KEY_BIAS_PARTS * nh, S), f32)], axis=0)
    o_ref[0] = jnp.transpose(cols).astype(bf16)


def _fox_decay(f_t, b_forget, B, S):
    nh = f_t.shape[0]
    assert KEY_BIAS_PARTS * nh <= LANES
    return pl.pallas_call(
        _fox_decay_kernel,
        grid=(B,),
        in_specs=[
            pl.BlockSpec((nh, S), lambda b: (0, b)),
            pl.BlockSpec((nh, 1), lambda b: (0, 0)),
        ],
        out_specs=pl.BlockSpec((1, S, LANES), lambda b: (b, 0, 0)),
        out_shape=jax.ShapeDtypeStruct((B, S, LANES), bf16),
        compiler_params=_params("arbitrary"),
        name="fox_decay",
    )(f_t, b_forget)


def _qk(q, k):
    return lax.dot_general(q, k, (((1,), (1,)), ((), ())), preferred_element_type=f32)


def _softmax_init(t):
    return jnp.full((t, 1), NEG, f32), jnp.zeros((t, 2 * HEAD_DIM), f32)


def _softmax_step2(carry, s2, v_ones):
    m, acc = carry
    m_new = jnp.maximum(m, jnp.max(s2, axis=-1, keepdims=True))
    p = jnp.exp2(s2 - m_new).astype(bf16)
    acc = jnp.exp2(m - m_new) * acc + jnp.dot(p, v_ones, preferred_element_type=f32)
    return m_new, acc


def _carry_scratch(t):
    return [pltpu.VMEM((t, 1), f32), pltpu.VMEM((t, 2 * HEAD_DIM), f32)]


def _softmax_result(carry):
    _, acc = carry
    return acc[:, :HEAD_DIM] / acc[:, HEAD_DIM:]


def _with_ones(vo_scr, v_ref):
    vo_scr[:, :HEAD_DIM] = v_ref[0]
    vo_scr[:, HEAD_DIM:] = jnp.ones((vo_scr.shape[0], HEAD_DIM), vo_scr.dtype)


def _fox_attn_kernel(q_ref, k_ref, v_ref, kb_ref, o_ref, qaug_scr, kaug_scr, vo_scr, m_scr, acc_scr, *, t):
    h = pl.program_id(1)
    S = q_ref.shape[1]
    lane = lax.broadcasted_iota(jnp.int32, (S, LANES), 1)
    mine = jnp.logical_and(lane % N_FOX_HEADS == h, lane < KEY_BIAS_PARTS * N_FOX_HEADS)
    qaug_scr[:, :HEAD_DIM] = q_ref[0]
    qaug_scr[:, HEAD_DIM:] = jnp.where(mine, 1.0, 0.0).astype(bf16)
    kaug_scr[:, :HEAD_DIM] = k_ref[0]
    kaug_scr[:, HEAD_DIM:] = kb_ref[0]
    _with_ones(vo_scr, v_ref)
    half = t // 2
    for qi in range(S // t):
        qa = qaug_scr[qi * t:(qi + 1) * t, :]
        carry = _softmax_init(t)
        for kb in range(qi):
            keys = slice(kb * t, (kb + 1) * t)
            carry = _softmax_step2(carry, _qk(qa, kaug_scr[keys, :]), vo_scr[keys, :])
        m_scr[...], acc_scr[...] = carry
        for lo, hi in ((0, half), (half, t)):
            keys = slice(qi * t, qi * t + hi)
            s2 = _qk(qaug_scr[qi * t + lo:qi * t + hi, :], kaug_scr[keys, :])
            row = lax.broadcasted_iota(jnp.int32, (half, hi), 0)
            col = lax.broadcasted_iota(jnp.int32, (half, hi), 1)
            s2 = jnp.where(col <= row + lo, s2, NEG)
            part = _softmax_step2((m_scr[lo:hi, :], acc_scr[lo:hi, :]), s2, vo_scr[keys, :])
            o_ref[qi * t + lo:qi * t + hi, :] = _softmax_result(part).astype(o_ref.dtype)


def _fox_attn(qkv, key_bias, B, S, slab0):
    T = B * S
    H = N_FOX_HEADS
    t = _pick(S, 512)
    kern = functools.partial(_fox_attn_kernel, t=t)
    return pl.pallas_call(
        kern,
        grid=(B, H),
        in_specs=[
            pl.BlockSpec((1, S, HEAD_DIM), lambda b, h: (slab0 + h, b, 0)),
            pl.BlockSpec((1, S, HEAD_DIM), lambda b, h: (slab0 + H + h, b, 0)),
            pl.BlockSpec((1, S, HEAD_DIM), lambda b, h: (slab0 + 2 * H + h, b, 0)),
            pl.BlockSpec((1, S, LANES), lambda b, h: (b, 0, 0)),
        ],
        out_specs=pl.BlockSpec((S, HEAD_DIM), lambda b, h: (b, h)),
        out_shape=jax.ShapeDtypeStruct((T, H * HEAD_DIM), bf16),
        scratch_shapes=[pltpu.VMEM((S, 2 * HEAD_DIM), bf16)] * 3 + _carry_scratch(t),
        compiler_params=_params("arbitrary", "arbitrary"),
        name="fox_attn",
    )(qkv, qkv, qkv, key_bias)


def _rel_bucket_table(n):
    max_exact = REL_BUCKETS // 2
    d = np.arange(n)
    ratio = np.log(np.maximum(d, 1).astype(np.float32) / np.float32(max_exact)) / np.float32(
        math.log(REL_MAX_DIST / max_exact))
    large = max_exact + (ratio * np.float32(REL_BUCKETS - max_exact)).astype(np.int32)
    large = np.minimum(large, REL_BUCKETS - 1)
    return np.where(d < max_exact, d, large).astype(np.int32)


def _moba_attn_kernel(rel_ref, q_ref, k_ref, v_ref, avg_ref, hot_ref, bk_ref, o_ref, bias_scr, kaug_scr,
                      qaug_scr, vo_scr, m_scr, acc_scr):
    h = pl.program_id(1)
    blk = MOBA_BLOCK
    t = 2 * blk
    S = q_ref.shape[1]
    far_bias = rel_ref[h, REL_BUCKETS - 1]
    row = lax.broadcasted_iota(jnp.int32, (blk, blk), 0)
    col = lax.broadcasted_iota(jnp.int32, (blk, blk), 1)
    tiles = []
    for which in range(2):
        bk = bk_ref[which]
        tile = jnp.zeros((blk, blk), f32)
        for b in range(REL_BUCKETS - 1):
            tile = jnp.where(bk == b, (rel_ref[h, b] - far_bias) * LOG2E, tile)
        tiles.append(jnp.where(col <= row, tile, NEG) if which == 0 else tile)
    own_tile, prev_tile = tiles
    zero_tile = jnp.zeros((blk, blk), f32)
    for which, quads in enumerate((((zero_tile, prev_tile), (zero_tile, zero_tile)),
                                   ((zero_tile, own_tile), (prev_tile, own_tile)))):
        for r in range(2):
            for c in range(2):
                bias_scr[which, r * blk:(r + 1) * blk, c * blk:(c + 1) * blk] = quads[r][c]
    kaug_scr[:, :HEAD_DIM] = k_ref[0]
    kaug_scr[:, HEAD_DIM:] = hot_ref[...]
    _with_ones(vo_scr, v_ref)
    kmean = jnp.dot(avg_ref[...], k_ref[0], preferred_element_type=f32).astype(bf16)

    nbp = -(-(S // blk) // SUBLANES) * SUBLANES
    block = lax.broadcasted_iota(jnp.int32, (nbp, S), 0)
    own = lax.broadcasted_iota(jnp.int32, (nbp, S), 1) // blk
    valid = block < own
    gate = jnp.where(valid, _qk(kmean[:nbp], q_ref[0]), NEG)
    sel = block == own
    for _ in range(MOBA_TOPK):
        mx = jnp.max(gate, axis=0, keepdims=True)
        first = jnp.min(jnp.where(gate == mx, block, nbp), axis=0, keepdims=True)
        pick = block == first
        sel = jnp.logical_or(sel, jnp.logical_and(pick, valid))
        gate = jnp.where(pick, -jnp.inf, gate)
    chosen = jnp.concatenate([jnp.where(sel, 0.0, NEG), jnp.full((LANES - nbp, S), NEG, f32)], axis=0)
    qaug_scr[:, :HEAD_DIM] = q_ref[0]
    qaug_scr[:, HEAD_DIM:] = jnp.transpose(chosen).astype(bf16)

    for j in range(S // t):
        rows = slice(j * t, (j + 1) * t)
        qa = qaug_scr[rows, :]
        carry = _softmax_init(t)
        for c in range(j - 1):
            keys = slice(c * t, (c + 1) * t)
            carry = _softmax_step2(carry, _qk(qa, kaug_scr[keys, :]), vo_scr[keys, :])
        if j > 0:
            keys = slice((j - 1) * t, j * t)
            carry = _softmax_step2(carry, _qk(qa, kaug_scr[keys, :]) + bias_scr[0], vo_scr[keys, :])
        m_scr[...], acc_scr[...] = carry
        for lo, hi in ((0, blk), (blk, t)):
            keys = slice(j * t, j * t + hi)
            s2 = _qk(qaug_scr[j * t + lo:j * t + hi, :], kaug_scr[keys, :]) + bias_scr[1, lo:hi, t - hi:]
            part = _softmax_step2((m_scr[lo:hi, :], acc_scr[lo:hi, :]), s2, vo_scr[keys, :])
            o_ref[j * t + lo:j * t + hi, :] = _softmax_result(part).astype(o_ref.dtype)


def _moba_attn(qkv, rel_bias, B, S, slab0):
    T = B * S
    H = N_MOBA_HEADS
    blk = MOBA_BLOCK
    assert S % (2 * blk) == 0 and S // blk <= LANES
    nb = S // blk
    avg = np.zeros((LANES, S), np.float32)
    for n in range(nb):
        avg[n, n * blk:(n + 1) * blk] = 1.0 / blk
    hot = (avg.T > 0).astype(np.float32)
    table = _rel_bucket_table(2 * blk)
    dist = np.arange(blk)[:, None] - np.arange(blk)[None, :]
    buckets = np.stack([table[np.maximum(dist, 0)], table[dist + blk]]).astype(np.int32)
    grid_spec = pltpu.PrefetchScalarGridSpec(
        num_scalar_prefetch=1,
        grid=(B, H),
        in_specs=[
            pl.BlockSpec((1, S, HEAD_DIM), lambda b, h, r: (slab0 + h, b, 0)),
            pl.BlockSpec((1, S, HEAD_DIM), lambda b, h, r: (slab0 + H + h, b, 0)),
            pl.BlockSpec((1, S, HEAD_DIM), lambda b, h, r: (slab0 + 2 * H + h, b, 0)),
            pl.BlockSpec((LANES, S), lambda b, h, r: (0, 0)),
            pl.BlockSpec((S, LANES), lambda b, h, r: (0, 0)),
            pl.BlockSpec((2, blk, blk), lambda b, h, r: (0, 0, 0)),
        ],
        out_specs=pl.BlockSpec((S, HEAD_DIM), lambda b, h, r: (b, h)),
        scratch_shapes=([pltpu.VMEM((2, 2 * blk, 2 * blk), f32)] + [pltpu.VMEM((S, 2 * HEAD_DIM), bf16)] * 3
                        + _carry_scratch(2 * blk)),
    )
    return pl.pallas_call(
        _moba_attn_kernel,
        grid_spec=grid_spec,
        out_shape=jax.ShapeDtypeStruct((T, H * HEAD_DIM), bf16),
        compiler_params=_params("arbitrary", "arbitrary"),
        name="moba_attn",
    )(rel_bias, qkv, qkv, qkv, jnp.asarray(avg, bf16), jnp.asarray(hot, bf16), jnp.asarray(buckets))


def _rms(x, g):
    return x * lax.rsqrt(jnp.mean(x * x, axis=-1, keepdims=True) + EPS) * g


def _first_max(x, lane):
    mx = jnp.max(x, axis=-1, keepdims=True)
    return mx, jnp.min(jnp.where(x == mx, lane, LANES), axis=-1, keepdims=True)


def _route(logits, lane, grp=None):
    gl = jnp.where(lane < N_GROUPS, logits, -jnp.inf)
    gmax, top = _first_max(gl, lane)
    grp = top if grp is None else grp
    pg = jnp.exp(gl - gmax)
    p_top_group = jnp.sum(jnp.where(lane == grp, pg, 0.0), axis=-1, keepdims=True) / jnp.sum(
        pg, axis=-1, keepdims=True)
    lo = N_GROUPS + grp * EXPERTS_PER_GROUP
    in_grp = jnp.logical_and(lane >= lo, lane < lo + EXPERTS_PER_GROUP)
    el = jnp.where(in_grp, logits, -jnp.inf)
    pe = jnp.exp(el - jnp.max(el, axis=-1, keepdims=True))
    p_exp = jnp.where(in_grp, pe / jnp.sum(pe, axis=-1, keepdims=True), -1.0)
    p1, i1 = _first_max(p_exp, lane)
    p2, i2 = _first_max(jnp.where(lane == i1, -1.0, p_exp), lane)
    tot = p1 + p2
    combine = jnp.where(lane == i1, p1 / tot, jnp.where(lane == i2, p2 / tot, 0.0)) * p_top_group
    return grp, jnp.minimum(i1, i2) - lo, jnp.maximum(i1, i2) - lo, combine


def _out_proj_kernel(fo_ref, mo_ref, x_ref, gf_ref, gm_ref, w_ref, gffn_ref, wr_ref, br_ref,
                     x1_ref, route_ref, count_ref, count_scr):
    @pl.when(pl.program_id(0) == 0)
    def _():
        count_scr[...] = jnp.zeros_like(count_scr)

    D = x_ref.shape[1]
    part = x_ref.shape[0] // OUT_PROJ_PARTS
    lane = lax.broadcasted_iota(jnp.int32, (part, ROUTER_COLS), 1)
    earlier = (lax.broadcasted_iota(jnp.int32, (part, part), 1)
               < lax.broadcasted_iota(jnp.int32, (part, part), 0))
    earlier = jnp.where(earlier, 1.0, 0.0).astype(bf16)
    counts = count_scr[...]
    for p in range(OUT_PROJ_PARTS):
        rows = slice(p * part, (p + 1) * part)
        fo = _rms(fo_ref[rows, :].astype(f32), gf_ref[...]).astype(bf16)
        mo = _rms(mo_ref[rows, :].astype(f32), gm_ref[...]).astype(bf16)
        mix = jnp.concatenate([fo, mo], axis=-1)
        x1 = x_ref[rows, :] + jnp.dot(mix, w_ref[...], preferred_element_type=f32)
        h2 = _rms(x1, gffn_ref[...]).astype(bf16)
        logits = jnp.dot(h2, wr_ref[...], preferred_element_type=f32) + br_ref[...]
        grp, ea, eb, combine = _route(logits, lane)
        x1_ref[rows, :D] = x1
        x1_ref[rows, D:] = combine
        pair = (ea * (2 * EXPERTS_PER_GROUP - 1 - ea)) // 2 + eb - ea - 1
        cls = grp * EXPERT_PAIRS + pair
        onehot = jnp.where(lane == cls, 1.0, 0.0)
        before = jnp.dot(earlier, onehot.astype(bf16), preferred_element_type=f32) + counts
        rank = jnp.sum(onehot * before, axis=-1, keepdims=True).astype(jnp.int32)
        route_ref[rows, :] = jnp.where(lane == 0, cls, jnp.where(lane == 1, rank, 0))
        counts = counts + jnp.sum(onehot, axis=0, keepdims=True)
    count_scr[...] = counts
    count_ref[...] = counts.astype(jnp.int32)


def _out_proj(fo, mo, x2, gf, gm, w_out, g_ffn, w_r, b_r):
    T, D = x2.shape
    Wf = fo.shape[1]
    Wm = mo.shape[1]
    tm = _pick(T, 512)
    const = lambda i: (0, 0)
    rows = lambda i: (i, 0)
    return pl.pallas_call(
        _out_proj_kernel,
        grid=(T // tm,),
        in_specs=[
            pl.BlockSpec((tm, Wf), rows),
            pl.BlockSpec((tm, Wm), rows),
            pl.BlockSpec((tm, D), rows),
            pl.BlockSpec((1, Wf), const),
            pl.BlockSpec((1, Wm), const),
            pl.BlockSpec((Wf + Wm, D), const),
            pl.BlockSpec((1, D), const),
            pl.BlockSpec((D, ROUTER_COLS), const),
            pl.BlockSpec((1, ROUTER_COLS), const),
        ],
        out_specs=[
            pl.BlockSpec((tm, D + ROUTER_COLS), rows),
            pl.BlockSpec((tm, ROUTER_COLS), rows),
            pl.BlockSpec((1, ROUTER_COLS), const),
        ],
        out_shape=[
            jax.ShapeDtypeStruct((T, D + ROUTER_COLS), f32),
            jax.ShapeDtypeStruct((T, ROUTER_COLS), jnp.int32),
            jax.ShapeDtypeStruct((1, ROUTER_COLS), jnp.int32),
        ],
        scratch_shapes=[pltpu.VMEM((1, ROUTER_COLS), f32)],
        compiler_params=_params("arbitrary"),
        name="out_proj",
    )(fo, mo, x2, gf, gm, w_out, g_ffn, w_r, b_r)


def _start_rows(row_copy, n):
    assert n % DMA_UNROLL == 0

    def body(g, carry):
        for k in range(DMA_UNROLL):
            row_copy(g * DMA_UNROLL + k).start()
        return carry

    lax.fori_loop(0, n // DMA_UNROLL, body, 0)


def _slot(cls_ref, rank_ref, base_ref, t):
    return base_ref[cls_ref[t]] + rank_ref[t]


def _dispatch_kernel(cls_ref, rank_ref, base_ref, fill_ref, x_ref, xs_ref, zero_scr, sem, zsem):
    i = pl.program_id(0)
    tm = x_ref.shape[0]

    zrows = zero_scr.shape[0]

    def row_copy(r):
        d = _slot(cls_ref, rank_ref, base_ref, i * tm + r)
        return pltpu.make_async_copy(x_ref.at[pl.ds(r, 1)], xs_ref.at[pl.ds(d, 1)], sem)

    def zero_row(r):
        return pltpu.make_async_copy(zero_scr.at[pl.ds(0, 1)], xs_ref.at[pl.ds(r, 1)], zsem)

    def zero_sublanes(c):
        dst = xs_ref.at[pl.ds(pl.multiple_of(c * SUBLANES, SUBLANES), SUBLANES)]
        return pltpu.make_async_copy(zero_scr.at[pl.ds(0, SUBLANES)], dst, zsem)

    def zero_chunk(c):
        return pltpu.make_async_copy(zero_scr, xs_ref.at[pl.ds(pl.multiple_of(c * zrows, zrows), zrows)], zsem)

    def start(copy):
        def body(r, carry):
            copy(r).start()
            return carry
        return body

    def wait(copy):
        def body(r, carry):
            copy(r).wait()
            return carry
        return body

    _start_rows(row_copy, tm)

    @pl.when(i == 0)
    def _():
        zero_scr[...] = jnp.zeros_like(zero_scr)

        def fill_class(c, carry):
            lo, hi = fill_ref[2 * c], fill_ref[2 * c + 1]
            mid = jnp.minimum((lo + SUBLANES - 1) // SUBLANES * SUBLANES, hi)
            lax.fori_loop(lo, mid, start(zero_row), 0)
            lax.fori_loop(lo, mid, wait(zero_row), 0)
            lax.fori_loop(mid // SUBLANES, hi // SUBLANES, start(zero_sublanes), 0)
            lax.fori_loop(mid // SUBLANES, hi // SUBLANES, wait(zero_sublanes), 0)
            return carry

        lax.fori_loop(0, N_CLASSES, fill_class, 0)
        lo, hi = fill_ref[2 * N_CLASSES] // zrows, fill_ref[2 * N_CLASSES + 1] // zrows
        lax.fori_loop(lo, hi, start(zero_chunk), 0)
        lax.fori_loop(lo, hi, wait(zero_chunk), 0)

    pltpu.make_async_copy(x_ref, xs_ref.at[pl.ds(0, tm)], sem).wait()


def _dispatch(x1, cls, rank, base, fill, rows_out):
    T, D = x1.shape
    tm = _pick(T, 512)
    grid_spec = pltpu.PrefetchScalarGridSpec(
        num_scalar_prefetch=4,
        grid=(T // tm,),
        in_specs=[pl.BlockSpec((tm, D), lambda i, *_: (i, 0))],
        out_specs=pl.BlockSpec(memory_space=pl.ANY),
        scratch_shapes=[pltpu.VMEM((ZERO_ROWS, D), f32), pltpu.SemaphoreType.DMA, pltpu.SemaphoreType.DMA],
    )
    return pl.pallas_call(
        _dispatch_kernel,
        grid_spec=grid_spec,
        out_shape=jax.ShapeDtypeStruct((rows_out, D), f32),
        compiler_params=_params("arbitrary"),
        name="moe_dispatch",
    )(cls, rank, base, fill, x1)


def _collect_kernel(cls_ref, rank_ref, base_ref, ys_ref, o_ref, sem):
    i = pl.program_id(0)
    tm = o_ref.shape[0]

    def row_copy(r):
        d = _slot(cls_ref, rank_ref, base_ref, i * tm + r)
        return pltpu.make_async_copy(ys_ref.at[pl.ds(d, 1)], o_ref.at[pl.ds(r, 1)], sem)

    _start_rows(row_copy, tm)
    pltpu.make_async_copy(ys_ref.at[pl.ds(0, tm)], o_ref, sem).wait()


def _collect(ys, cls, rank, base, T):
    D = ys.shape[1]
    tm = _pick(T, 512)
    grid_spec = pltpu.PrefetchScalarGridSpec(
        num_scalar_prefetch=3,
        grid=(T // tm,),
        in_specs=[pl.BlockSpec(memory_space=pl.ANY)],
        out_specs=pl.BlockSpec((tm, D), lambda i, *_: (i, 0)),
        scratch_shapes=[pltpu.SemaphoreType.DMA],
    )
    return pl.pallas_call(
        _collect_kernel,
        grid_spec=grid_spec,
        out_shape=jax.ShapeDtypeStruct((T, D), f32),
        compiler_params=_params("arbitrary"),
        name="moe_collect",
    )(cls, rank, base, ys)


def _moe_kernel(ea_ref, eb_ref, nu_ref, xs_ref, gffn_ref, wga_ref, wua_ref, wda_ref,
                wgb_ref, wub_ref, wdb_ref, gfin_ref, o_ref):
    i = pl.program_id(0)
    D = o_ref.shape[1]

    @pl.when(i < nu_ref[0])
    def _():
        x = xs_ref[:, :D]
        combine = xs_ref[:, D:]
        h = _rms(x, gffn_ref[...]).astype(bf16)
        lane = lax.broadcasted_iota(jnp.int32, combine.shape, 1)
        y = jnp.zeros(x.shape, f32)
        for e_ref, wg_ref, wu_ref, wd_ref in ((ea_ref, wga_ref, wua_ref, wda_ref),
                                               (eb_ref, wgb_ref, wub_ref, wdb_ref)):
            c = jnp.sum(jnp.where(lane == N_GROUPS + e_ref[i], combine, 0.0), axis=-1, keepdims=True)
            gate = jnp.dot(h, wg_ref[0], preferred_element_type=f32)
            up = jnp.dot(h, wu_ref[0], preferred_element_type=f32)
            hid = (jax.nn.silu(gate) * up * c).astype(bf16)
            y = y + jnp.dot(hid, wd_ref[0], preferred_element_type=f32)
        o_ref[...] = _rms(x + y, gfin_ref[...])

    @pl.when(i >= nu_ref[0])
    def _():
        o_ref[...] = jnp.zeros_like(o_ref)


def _moe(xs, tile_ea, tile_eb, n_used, g_ffn, w_gate, w_up, w_down, g_final, ts):
    R = xs.shape[0]
    E, D, F = w_gate.shape
    const = lambda i, ea, eb, nu: (0, 0)
    rows = lambda i, ea, eb, nu: (jnp.minimum(i, nu[0] - 1), 0)
    first = lambda i, ea, eb, nu: (ea[i], 0, 0)
    second = lambda i, ea, eb, nu: (eb[i], 0, 0)
    grid_spec = pltpu.PrefetchScalarGridSpec(
        num_scalar_prefetch=3,
        grid=(R // ts,),
        in_specs=[
            pl.BlockSpec((ts, D + ROUTER_COLS), rows),
            pl.BlockSpec((1, D), const),
            pl.BlockSpec((1, D, F), first),
            pl.BlockSpec((1, D, F), first),
            pl.BlockSpec((1, F, D), first),
            pl.BlockSpec((1, D, F), second),
            pl.BlockSpec((1, D, F), second),
            pl.BlockSpec((1, F, D), second),
            pl.BlockSpec((1, D), const),
        ],
        out_specs=pl.BlockSpec((ts, D), lambda i, ea, eb, nu: (i, 0)),
    )
    return pl.pallas_call(
        _moe_kernel,
        grid_spec=grid_spec,
        out_shape=jax.ShapeDtypeStruct((R, D), f32),
        compiler_params=_params("arbitrary"),
        name="moe",
    )(tile_ea, tile_eb, n_used, xs, g_ffn, w_gate, w_up, w_down, w_gate, w_up, w_down, g_final)


def kernel(x, attn_norm_g, w_in, b_forget, fox_out_norm_g, moba_out_norm_g, rel_bias, w_out, ffn_norm_g,
           w_group_router, b_group_router, w_expert_router, b_expert_router, w_gate, w_up, w_down,
           final_norm_g):
    B, S, D = x.shape
    T = B * S
    depth = w_in.shape[0]
    fox_w = N_FOX_HEADS * HEAD_DIM
    moba_w = N_MOBA_HEADS * HEAD_DIM
    qkv_w = 3 * (fox_w + moba_w)
    assert w_in.shape[2] == qkv_w + N_FOX_HEADS
    assert N_GROUPS + N_EXPERTS <= ROUTER_COLS

    x2 = x.reshape(T, D)
    out = None
    for l in range(depth):
        w_in_bf16 = w_in.astype(bf16)
        w_fg_t = w_in_bf16[l, :, qkv_w:].T
        assert fox_w == moba_w
        col_scale = np.ones((qkv_w // fox_w,), np.float32)
        col_scale[[0, 3]] = HEAD_DIM ** -0.5 * LOG2E
        qkv, f_t = _in_proj(x2, attn_norm_g[l].reshape(1, D), w_in_bf16, l, w_fg_t, jnp.asarray(col_scale),
                            fox_w)
        key_bias = _fox_decay(f_t, b_forget[l].reshape(N_FOX_HEADS, 1), B, S)
        fo = _fox_attn(qkv, key_bias, B, S, 0)
        mo = _moba_attn(qkv, rel_bias, B, S, 3 * N_FOX_HEADS)

        w_r = jnp.concatenate([w_group_router[l], w_expert_router[l].reshape(D, N_EXPERTS)], axis=1)
        w_r = jnp.pad(w_r, ((0, 0), (0, ROUTER_COLS - w_r.shape[1]))).astype(bf16)
        b_r = jnp.concatenate([b_group_router[l], b_expert_router[l].reshape(N_EXPERTS)])
        b_r = jnp.pad(b_r, (0, ROUTER_COLS - b_r.shape[0])).reshape(1, ROUTER_COLS)
        g_ffn = ffn_norm_g[l].reshape(1, D)
        x1, route, count = _out_proj(fo, mo, x2, fox_out_norm_g[l].reshape(1, fox_w),
                                     moba_out_norm_g[l].reshape(1, moba_w), w_out[l].astype(bf16),
                                     g_ffn, w_r, b_r)
        ts = MOE_TILE
        cls, rank = route[:, 0], route[:, 1]
        counts = count[0, :N_CLASSES]
        padded = (counts + ts - 1) // ts * ts
        ends = jnp.cumsum(padded)
        base = ends - padded
        rows_out = (T // ts + N_CLASSES) * ts
        fill = jnp.concatenate([jnp.stack([base + counts, ends], axis=1).reshape(-1),
                                jnp.stack([ends[-1], jnp.asarray(rows_out, ends.dtype)])])
        tile_start = jnp.arange(rows_out // ts, dtype=jnp.int32) * ts
        tile_cls = jnp.minimum(jnp.sum(tile_start[:, None] >= ends[None, :], axis=1), N_CLASSES - 1)
        pairs = [(a, b) for a in range(EXPERTS_PER_GROUP) for b in range(a + 1, EXPERTS_PER_GROUP)]
        first_of_cls = np.array([g * EXPERTS_PER_GROUP + a for g in range(N_GROUPS) for a, _ in pairs], np.int32)
        second_of_cls = np.array([g * EXPERTS_PER_GROUP + b for g in range(N_GROUPS) for _, b in pairs], np.int32)
        tile_ea = jnp.asarray(first_of_cls)[tile_cls]
        tile_eb = jnp.asarray(second_of_cls)[tile_cls]
        n_used = (ends[-1:] // ts).astype(jnp.int32)

        assert l == depth - 1, "the fused residual + final norm epilogue expects a single layer"
        xs = _dispatch(x1, cls, rank, base, fill, rows_out)
        ys = _moe(xs, tile_ea, tile_eb, n_used, g_ffn, w_gate[l].astype(bf16),
                  w_up[l].astype(bf16), w_down[l].astype(bf16), final_norm_g.reshape(1, D), ts)
        out = _collect(ys, cls, rank, base, T)
        x2 = out
    return out.reshape(B, S, D)
```

```python
import functools
import math

import numpy as np
import jax
import jax.numpy as jnp
from jax import lax
from jax.experimental import pallas as pl
from jax.experimental.pallas import tpu as pltpu

HEAD_DIM = 128
N_FOX_HEADS = 8
N_MOBA_HEADS = 8
MOBA_BLOCK = 256
MOBA_TOPK = 3
REL_BUCKETS = 32
REL_MAX_DIST = 128
N_GROUPS = 4
EXPERTS_PER_GROUP = 4
N_EXPERTS = N_GROUPS * EXPERTS_PER_GROUP
EXPERT_PAIRS = EXPERTS_PER_GROUP * (EXPERTS_PER_GROUP - 1) // 2
N_CLASSES = N_GROUPS * EXPERT_PAIRS
EPS = 1e-6
NEG = -1e30
LOG2E = math.log2(math.e)
LANES = 128
SUBLANES = 8
ROUTER_COLS = LANES
OUT_PROJ_PARTS = 2
MOE_TILE = 256
KEY_BIAS_PARTS = 3
ZERO_ROWS = 64
DMA_UNROLL = 8
VMEM_LIMIT = 56 * 1024 * 1024

f32 = jnp.float32
bf16 = jnp.bfloat16


def _params(*sem):
    return pltpu.CompilerParams(dimension_semantics=sem, vmem_limit_bytes=VMEM_LIMIT)


def _pick(n, pref):
    t = min(n, pref)
    assert n % t == 0, (n, pref)
    return t


def _in_proj_kernel(cs_ref, x_ref, g_ref, w_ref, wfg_ref, qkv_ref, f_ref, h_scr):
    @pl.when(pl.program_id(1) == 0)
    def _():
        x = x_ref[...]
        y = x * lax.rsqrt(jnp.mean(x * x, axis=-1, keepdims=True) + EPS)
        hb = (y * g_ref[...]).astype(bf16)
        h_scr[...] = hb
        f_ref[...] = lax.dot_general(wfg_ref[...], hb, (((1,), (1,)), ((), ())),
                                     preferred_element_type=f32)

    acc = jnp.dot(h_scr[...], w_ref[...], preferred_element_type=f32) * cs_ref[pl.program_id(1)]
    for c in range(acc.shape[1] // HEAD_DIM):
        qkv_ref[c] = acc[:, c * HEAD_DIM:(c + 1) * HEAD_DIM].astype(bf16)


def _in_proj(x2, g, w_in, layer, w_fg_t, col_scale, tn):
    T, D = x2.shape
    N = col_scale.shape[0] * tn
    tm = _pick(T, 1024)
    assert N <= w_in.shape[2]
    nh = w_fg_t.shape[0]
    return pl.pallas_call(
        _in_proj_kernel,
        grid=(T // tm, N // tn),
        in_specs=[
            pl.BlockSpec(memory_space=pltpu.SMEM),
            pl.BlockSpec((tm, D), lambda i, j: (i, 0)),
            pl.BlockSpec((1, D), lambda i, j: (0, 0)),
            pl.BlockSpec((None, D, tn), lambda i, j: (layer, 0, j)),
            pl.BlockSpec((nh, D), lambda i, j: (0, 0)),
        ],
        out_specs=[
            pl.BlockSpec((tn // HEAD_DIM, tm, HEAD_DIM), lambda i, j: (j, i, 0)),
            pl.BlockSpec((nh, tm), lambda i, j: (0, i)),
        ],
        out_shape=[
            jax.ShapeDtypeStruct((N // HEAD_DIM, T, HEAD_DIM), bf16),
            jax.ShapeDtypeStruct((nh, T), f32),
        ],
        scratch_shapes=[pltpu.VMEM((tm, D), bf16)],
        compiler_params=_params("arbitrary", "arbitrary"),
        name="in_proj",
    )(col_scale, x2, g, w_in, w_fg_t)


def _fox_decay_kernel(f_ref, b_ref, o_ref):
    lf = jax.nn.log_sigmoid(f_ref[...] + b_ref[...])
    S = lf.shape[1]
    lane = lax.broadcasted_iota(jnp.int32, lf.shape, 1)
    c = lf
    sh = 1
    while sh < S:
        c = c + jnp.where(lane >= sh, pltpu.roll(c, sh, axis=1), 0.0)
        sh *= 2
    rest = -c * LOG2E
    parts = []
    for _ in range(KEY_BIAS_PARTS):
        part = rest.astype(bf16).astype(f32)
        parts.append(part)
        rest = rest - part
    nh = lf.shape[0]
    cols = jnp.concatenate(parts + [jnp.zeros((LANES - KEY_BIAS_PARTS * nh, S), f32)], axis=0)
    o_ref[0] = jnp.transpose(cols).astype(bf16)


def _fox_decay(f_t, b_forget, B, S):
    nh = f_t.shape[0]
    assert KEY_BIAS_PARTS * nh <= LANES
    return pl.pallas_call(
        _fox_decay_kernel,
        grid=(B,),
        in_specs=[
            pl.BlockSpec((nh, S), lambda b: (0, b)),
            pl.BlockSpec((nh, 1), lambda b: (0, 0)),
        ],
        out_specs=pl.BlockSpec((1, S, LANES), lambda b: (b, 0, 0)),
        out_shape=jax.ShapeDtypeStruct((B, S, LANES), bf16),
        compiler_params=_params("arbitrary"),
        name="fox_decay",
    )(f_t, b_forget)


def _qk(q, k):
    return lax.dot_general(q, k, (((1,), (1,)), ((), ())), preferred_element_type=f32)


def _softmax_init(t):
    return jnp.full((t, 1), NEG, f32), jnp.zeros((t, 2 * HEAD_DIM), f32)


def _softmax_step2(carry, s2, v_ones):
    m, acc = carry
    m_new = jnp.maximum(m, jnp.max(s2, axis=-1, keepdims=True))
    p = jnp.exp2(s2 - m_new).astype(bf16)
    acc = jnp.exp2(m - m_new) * acc + jnp.dot(p, v_ones, preferred_element_type=f32)
    return m_new, acc


def _softmax_result(carry):
    _, acc = carry
    return acc[:, :HEAD_DIM] / acc[:, HEAD_DIM:]


def _with_ones(vo_scr, v_ref):
    vo_scr[:, :HEAD_DIM] = v_ref[0]
    vo_scr[:, HEAD_DIM:] = jnp.ones((vo_scr.shape[0], HEAD_DIM), vo_scr.dtype)


def _fox_attn_kernel(q_ref, k_ref, v_ref, kb_ref, o_ref, qaug_scr, kaug_scr, vo_scr, *, t):
    h = pl.program_id(1)
    S = q_ref.shape[1]
    lane = lax.broadcasted_iota(jnp.int32, (S, LANES), 1)
    mine = jnp.logical_and(lane % N_FOX_HEADS == h, lane < KEY_BIAS_PARTS * N_FOX_HEADS)
    qaug_scr[:, :HEAD_DIM] = q_ref[0]
    qaug_scr[:, HEAD_DIM:] = jnp.where(mine, 1.0, 0.0).astype(bf16)
    kaug_scr[:, :HEAD_DIM] = k_ref[0]
    kaug_scr[:, HEAD_DIM:] = kb_ref[0]
    _with_ones(vo_scr, v_ref)
    row = lax.broadcasted_iota(jnp.int32, (t, t), 0)
    col = lax.broadcasted_iota(jnp.int32, (t, t), 1)
    for qi in range(S // t):
        qa = qaug_scr[qi * t:(qi + 1) * t, :]
        carry = _softmax_init(t)
        for kb in range(qi + 1):
            keys = slice(kb * t, (kb + 1) * t)
            s2 = _qk(qa, kaug_scr[keys, :])
            if kb == qi:
                s2 = jnp.where(col <= row, s2, NEG)
            carry = _softmax_step2(carry, s2, vo_scr[keys, :])
        o_ref[qi * t:(qi + 1) * t, :] = _softmax_result(carry).astype(o_ref.dtype)


def _fox_attn(qkv, key_bias, B, S, slab0):
    T = B * S
    H = N_FOX_HEADS
    t = _pick(S, 512)
    kern = functools.partial(_fox_attn_kernel, t=t)
    return pl.pallas_call(
        kern,
        grid=(B, H),
        in_specs=[
            pl.BlockSpec((1, S, HEAD_DIM), lambda b, h: (slab0 + h, b, 0)),
            pl.BlockSpec((1, S, HEAD_DIM), lambda b, h: (slab0 + H + h, b, 0)),
            pl.BlockSpec((1, S, HEAD_DIM), lambda b, h: (slab0 + 2 * H + h, b, 0)),
            pl.BlockSpec((1, S, LANES), lambda b, h: (b, 0, 0)),
        ],
        out_specs=pl.BlockSpec((S, HEAD_DIM), lambda b, h: (b, h)),
        out_shape=jax.ShapeDtypeStruct((T, H * HEAD_DIM), bf16),
        scratch_shapes=[pltpu.VMEM((S, 2 * HEAD_DIM), bf16)] * 3,
        compiler_params=_params("arbitrary", "arbitrary"),
        name="fox_attn",
    )(qkv, qkv, qkv, key_bias)


def _rel_bucket_table(n):
    max_exact = REL_BUCKETS // 2
    d = np.arange(n)
    ratio = np.log(np.maximum(d, 1).astype(np.float32) / np.float32(max_exact)) / np.float32(
        math.log(REL_MAX_DIST / max_exact))
    large = max_exact + (ratio * np.float32(REL_BUCKETS - max_exact)).astype(np.int32)
    large = np.minimum(large, REL_BUCKETS - 1)
    return np.where(d < max_exact, d, large).astype(np.int32)


def _moba_attn_kernel(rel_ref, q_ref, k_ref, v_ref, avg_ref, hot_ref, bk_ref, o_ref, bias_scr, kaug_scr,
                      qaug_scr, vo_scr):
    h = pl.program_id(1)
    blk = MOBA_BLOCK
    t = 2 * blk
    S = q_ref.shape[1]
    far_bias = rel_ref[h, REL_BUCKETS - 1]
    row = lax.broadcasted_iota(jnp.int32, (blk, blk), 0)
    col = lax.broadcasted_iota(jnp.int32, (blk, blk), 1)
    tiles = []
    for which in range(2):
        bk = bk_ref[which]
        tile = jnp.zeros((blk, blk), f32)
        for b in range(REL_BUCKETS - 1):
            tile = jnp.where(bk == b, (rel_ref[h, b] - far_bias) * LOG2E, tile)
        tiles.append(jnp.where(col <= row, tile, NEG) if which == 0 else tile)
    own_tile, prev_tile = tiles
    zero_tile = jnp.zeros((blk, blk), f32)
    for which, quads in enumerate((((zero_tile, prev_tile), (zero_tile, zero_tile)),
                                   ((own_tile, zero_tile), (prev_tile, own_tile)))):
        for r in range(2):
            for c in range(2):
                bias_scr[which, r * blk:(r + 1) * blk, c * blk:(c + 1) * blk] = quads[r][c]
    kaug_scr[:, :HEAD_DIM] = k_ref[0]
    kaug_scr[:, HEAD_DIM:] = hot_ref[...]
    _with_ones(vo_scr, v_ref)
    kmean = jnp.dot(avg_ref[...], k_ref[0], preferred_element_type=f32).astype(bf16)

    nbp = -(-(S // blk) // SUBLANES) * SUBLANES
    block = lax.broadcasted_iota(jnp.int32, (nbp, S), 0)
    own = lax.broadcasted_iota(jnp.int32, (nbp, S), 1) // blk
    valid = block < own
    gate = jnp.where(valid, _qk(kmean[:nbp], q_ref[0]), NEG)
    sel = block == own
    for _ in range(MOBA_TOPK):
        mx = jnp.max(gate, axis=0, keepdims=True)
        first = jnp.min(jnp.where(gate == mx, block, nbp), axis=0, keepdims=True)
        pick = block == first
        sel = jnp.logical_or(sel, jnp.logical_and(pick, valid))
        gate = jnp.where(pick, -jnp.inf, gate)
    chosen = jnp.concatenate([jnp.where(sel, 0.0, NEG), jnp.full((LANES - nbp, S), NEG, f32)], axis=0)
    qaug_scr[:, :HEAD_DIM] = q_ref[0]
    qaug_scr[:, HEAD_DIM:] = jnp.transpose(chosen).astype(bf16)

    for j in range(S // t):
        rows = slice(j * t, (j + 1) * t)
        qa = qaug_scr[rows, :]
        carry = _softmax_step2(_softmax_init(t), _qk(qa, kaug_scr[rows, :]) + bias_scr[1], vo_scr[rows, :])
        if j > 0:
            keys = slice((j - 1) * t, j * t)
            carry = _softmax_step2(carry, _qk(qa, kaug_scr[keys, :]) + bias_scr[0], vo_scr[keys, :])
        for c in range(j - 1):
            keys = slice(c * t, (c + 1) * t)
            carry = _softmax_step2(carry, _qk(qa, kaug_scr[keys, :]), vo_scr[keys, :])
        o_ref[rows, :] = _softmax_result(carry).astype(o_ref.dtype)


def _moba_attn(qkv, rel_bias, B, S, slab0):
    T = B * S
    H = N_MOBA_HEADS
    blk = MOBA_BLOCK
    assert S % (2 * blk) == 0 and S // blk <= LANES
    nb = S // blk
    avg = np.zeros((LANES, S), np.float32)
    for n in range(nb):
        avg[n, n * blk:(n + 1) * blk] = 1.0 / blk
    hot = (avg.T > 0).astype(np.float32)
    table = _rel_bucket_table(2 * blk)
    dist = np.arange(blk)[:, None] - np.arange(blk)[None, :]
    buckets = np.stack([table[np.maximum(dist, 0)], table[dist + blk]]).astype(np.int32)
    grid_spec = pltpu.PrefetchScalarGridSpec(
        num_scalar_prefetch=1,
        grid=(B, H),
        in_specs=[
            pl.BlockSpec((1, S, HEAD_DIM), lambda b, h, r: (slab0 + h, b, 0)),
            pl.BlockSpec((1, S, HEAD_DIM), lambda b, h, r: (slab0 + H + h, b, 0)),
            pl.BlockSpec((1, S, HEAD_DIM), lambda b, h, r: (slab0 + 2 * H + h, b, 0)),
            pl.BlockSpec((LANES, S), lambda b, h, r: (0, 0)),
            pl.BlockSpec((S, LANES), lambda b, h, r: (0, 0)),
            pl.BlockSpec((2, blk, blk), lambda b, h, r: (0, 0, 0)),
        ],
        out_specs=pl.BlockSpec((S, HEAD_DIM), lambda b, h, r: (b, h)),
        scratch_shapes=[pltpu.VMEM((2, 2 * blk, 2 * blk), f32)] + [pltpu.VMEM((S, 2 * HEAD_DIM), bf16)] * 3,
    )
    return pl.pallas_call(
        _moba_attn_kernel,
        grid_spec=grid_spec,
        out_shape=jax.ShapeDtypeStruct((T, H * HEAD_DIM), bf16),
        compiler_params=_params("arbitrary", "arbitrary"),
        name="moba_attn",
    )(rel_bias, qkv, qkv, qkv, jnp.asarray(avg, bf16), jnp.asarray(hot, bf16), jnp.asarray(buckets))


def _rms(x, g):
    return x * lax.rsqrt(jnp.mean(x * x, axis=-1, keepdims=True) + EPS) * g


def _first_max(x, lane):
    mx = jnp.max(x, axis=-1, keepdims=True)
    return mx, jnp.min(jnp.where(x == mx, lane, LANES), axis=-1, keepdims=True)


def _route(logits, lane, grp=None):
    gl = jnp.where(lane < N_GROUPS, logits, -jnp.inf)
    gmax, top = _first_max(gl, lane)
    grp = top if grp is None else grp
    pg = jnp.exp(gl - gmax)
    p_top_group = jnp.sum(jnp.where(lane == grp, pg, 0.0), axis=-1, keepdims=True) / jnp.sum(
        pg, axis=-1, keepdims=True)
    lo = N_GROUPS + grp * EXPERTS_PER_GROUP
    in_grp = jnp.logical_and(lane >= lo, lane < lo + EXPERTS_PER_GROUP)
    el = jnp.where(in_grp, logits, -jnp.inf)
    pe = jnp.exp(el - jnp.max(el, axis=-1, keepdims=True))
    p_exp = jnp.where(in_grp, pe / jnp.sum(pe, axis=-1, keepdims=True), -1.0)
    p1, i1 = _first_max(p_exp, lane)
    p2, i2 = _first_max(jnp.where(lane == i1, -1.0, p_exp), lane)
    tot = p1 + p2
    combine = jnp.where(lane == i1, p1 / tot, jnp.where(lane == i2, p2 / tot, 0.0)) * p_top_group
    return grp, jnp.minimum(i1, i2) - lo, jnp.maximum(i1, i2) - lo, combine


def _out_proj_kernel(fo_ref, mo_ref, x_ref, gf_ref, gm_ref, w_ref, gffn_ref, wr_ref, br_ref,
                     x1_ref, route_ref, count_ref, count_scr):
    @pl.when(pl.program_id(0) == 0)
    def _():
        count_scr[...] = jnp.zeros_like(count_scr)

    D = x_ref.shape[1]
    part = x_ref.shape[0] // OUT_PROJ_PARTS
    lane = lax.broadcasted_iota(jnp.int32, (part, ROUTER_COLS), 1)
    earlier = (lax.broadcasted_iota(jnp.int32, (part, part), 1)
               < lax.broadcasted_iota(jnp.int32, (part, part), 0))
    earlier = jnp.where(earlier, 1.0, 0.0).astype(bf16)
    counts = count_scr[...]
    for p in range(OUT_PROJ_PARTS):
        rows = slice(p * part, (p + 1) * part)
        fo = _rms(fo_ref[rows, :].astype(f32), gf_ref[...]).astype(bf16)
        mo = _rms(mo_ref[rows, :].astype(f32), gm_ref[...]).astype(bf16)
        mix = jnp.concatenate([fo, mo], axis=-1)
        x1 = x_ref[rows, :] + jnp.dot(mix, w_ref[...], preferred_element_type=f32)
        h2 = _rms(x1, gffn_ref[...]).astype(bf16)
        logits = jnp.dot(h2, wr_ref[...], preferred_element_type=f32) + br_ref[...]
        grp, ea, eb, combine = _route(logits, lane)
        x1_ref[rows, :D] = x1
        x1_ref[rows, D:] = combine
        pair = (ea * (2 * EXPERTS_PER_GROUP - 1 - ea)) // 2 + eb - ea - 1
        cls = grp * EXPERT_PAIRS + pair
        onehot = jnp.where(lane == cls, 1.0, 0.0)
        before = jnp.dot(earlier, onehot.astype(bf16), preferred_element_type=f32) + counts
        rank = jnp.sum(onehot * before, axis=-1, keepdims=True).astype(jnp.int32)
        route_ref[rows, :] = jnp.where(lane == 0, cls, jnp.where(lane == 1, rank, 0))
        counts = counts + jnp.sum(onehot, axis=0, keepdims=True)
    count_scr[...] = counts
    count_ref[...] = counts.astype(jnp.int32)


def _out_proj(fo, mo, x2, gf, gm, w_out, g_ffn, w_r, b_r):
    T, D = x2.shape
    Wf = fo.shape[1]
    Wm = mo.shape[1]
    tm = _pick(T, 512)
    const = lambda i: (0, 0)
    rows = lambda i: (i, 0)
    return pl.pallas_call(
        _out_proj_kernel,
        grid=(T // tm,),
        in_specs=[
            pl.BlockSpec((tm, Wf), rows),
            pl.BlockSpec((tm, Wm), rows),
            pl.BlockSpec((tm, D), rows),
            pl.BlockSpec((1, Wf), const),
            pl.BlockSpec((1, Wm), const),
            pl.BlockSpec((Wf + Wm, D), const),
            pl.BlockSpec((1, D), const),
            pl.BlockSpec((D, ROUTER_COLS), const),
            pl.BlockSpec((1, ROUTER_COLS), const),
        ],
        out_specs=[
            pl.BlockSpec((tm, D + ROUTER_COLS), rows),
            pl.BlockSpec((tm, ROUTER_COLS), rows),
            pl.BlockSpec((1, ROUTER_COLS), const),
        ],
        out_shape=[
            jax.ShapeDtypeStruct((T, D + ROUTER_COLS), f32),
            jax.ShapeDtypeStruct((T, ROUTER_COLS), jnp.int32),
            jax.ShapeDtypeStruct((1, ROUTER_COLS), jnp.int32),
        ],
        scratch_shapes=[pltpu.VMEM((1, ROUTER_COLS), f32)],
        compiler_params=_params("arbitrary"),
        name="out_proj",
    )(fo, mo, x2, gf, gm, w_out, g_ffn, w_r, b_r)


def _start_rows(row_copy, n):
    assert n % DMA_UNROLL == 0

    def body(g, carry):
        for k in range(DMA_UNROLL):
            row_copy(g * DMA_UNROLL + k).start()
        return carry

    lax.fori_loop(0, n // DMA_UNROLL, body, 0)


def _slot(cls_ref, rank_ref, base_ref, t):
    return base_ref[cls_ref[t]] + rank_ref[t]


def _dispatch_kernel(cls_ref, rank_ref, base_ref, fill_ref, x_ref, xs_ref, zero_scr, sem, zsem):
    i = pl.program_id(0)
    tm = x_ref.shape[0]

    zrows = zero_scr.shape[0]

    def row_copy(r):
        d = _slot(cls_ref, rank_ref, base_ref, i * tm + r)
        return pltpu.make_async_copy(x_ref.at[pl.ds(r, 1)], xs_ref.at[pl.ds(d, 1)], sem)

    def zero_row(r):
        return pltpu.make_async_copy(zero_scr.at[pl.ds(0, 1)], xs_ref.at[pl.ds(r, 1)], zsem)

    def zero_sublanes(c):
        dst = xs_ref.at[pl.ds(pl.multiple_of(c * SUBLANES, SUBLANES), SUBLANES)]
        return pltpu.make_async_copy(zero_scr.at[pl.ds(0, SUBLANES)], dst, zsem)

    def zero_chunk(c):
        return pltpu.make_async_copy(zero_scr, xs_ref.at[pl.ds(pl.multiple_of(c * zrows, zrows), zrows)], zsem)

    def start(copy):
        def body(r, carry):
            copy(r).start()
            return carry
        return body

    def wait(copy):
        def body(r, carry):
            copy(r).wait()
            return carry
        return body

    _start_rows(row_copy, tm)

    @pl.when(i == 0)
    def _():
        zero_scr[...] = jnp.zeros_like(zero_scr)

        def fill_class(c, carry):
            lo, hi = fill_ref[2 * c], fill_ref[2 * c + 1]
            mid = jnp.minimum((lo + SUBLANES - 1) // SUBLANES * SUBLANES, hi)
            lax.fori_loop(lo, mid, start(zero_row), 0)
            lax.fori_loop(lo, mid, wait(zero_row), 0)
            lax.fori_loop(mid // SUBLANES, hi // SUBLANES, start(zero_sublanes), 0)
            lax.fori_loop(mid // SUBLANES, hi // SUBLANES, wait(zero_sublanes), 0)
            return carry

        lax.fori_loop(0, N_CLASSES, fill_class, 0)
        lo, hi = fill_ref[2 * N_CLASSES] // zrows, fill_ref[2 * N_CLASSES + 1] // zrows
        lax.fori_loop(lo, hi, start(zero_chunk), 0)
        lax.fori_loop(lo, hi, wait(zero_chunk), 0)

    pltpu.make_async_copy(x_ref, xs_ref.at[pl.ds(0, tm)], sem).wait()


def _dispatch(x1, cls, rank, base, fill, rows_out):
    T, D = x1.shape
    tm = _pick(T, 512)
    grid_spec = pltpu.PrefetchScalarGridSpec(
        num_scalar_prefetch=4,
        grid=(T // tm,),
        in_specs=[pl.BlockSpec((tm, D), lambda i, *_: (i, 0))],
        out_specs=pl.BlockSpec(memory_space=pl.ANY),
        scratch_shapes=[pltpu.VMEM((ZERO_ROWS, D), f32), pltpu.SemaphoreType.DMA, pltpu.SemaphoreType.DMA],
    )
    return pl.pallas_call(
        _dispatch_kernel,
        grid_spec=grid_spec,
        out_shape=jax.ShapeDtypeStruct((rows_out, D), f32),
        compiler_params=_params("arbitrary"),
        name="moe_dispatch",
    )(cls, rank, base, fill, x1)


def _collect_kernel(cls_ref, rank_ref, base_ref, ys_ref, o_ref, sem):
    i = pl.program_id(0)
    tm = o_ref.shape[0]

    def row_copy(r):
        d = _slot(cls_ref, rank_ref, base_ref, i * tm + r)
        return pltpu.make_async_copy(ys_ref.at[pl.ds(d, 1)], o_ref.at[pl.ds(r, 1)], sem)

    _start_rows(row_copy, tm)
    pltpu.make_async_copy(ys_ref.at[pl.ds(0, tm)], o_ref, sem).wait()


def _collect(ys, cls, rank, base, T):
    D = ys.shape[1]
    tm = _pick(T, 512)
    grid_spec = pltpu.PrefetchScalarGridSpec(
        num_scalar_prefetch=3,
        grid=(T // tm,),
        in_specs=[pl.BlockSpec(memory_space=pl.ANY)],
        out_specs=pl.BlockSpec((tm, D), lambda i, *_: (i, 0)),
        scratch_shapes=[pltpu.SemaphoreType.DMA],
    )
    return pl.pallas_call(
        _collect_kernel,
        grid_spec=grid_spec,
        out_shape=jax.ShapeDtypeStruct((T, D), f32),
        compiler_params=_params("arbitrary"),
        name="moe_collect",
    )(cls, rank, base, ys)


def _moe_kernel(ea_ref, eb_ref, nu_ref, xs_ref, gffn_ref, wga_ref, wua_ref, wda_ref,
                wgb_ref, wub_ref, wdb_ref, gfin_ref, o_ref):
    i = pl.program_id(0)
    D = o_ref.shape[1]

    @pl.when(i < nu_ref[0])
    def _():
        x = xs_ref[:, :D]
        combine = xs_ref[:, D:]
        h = _rms(x, gffn_ref[...]).astype(bf16)
        lane = lax.broadcasted_iota(jnp.int32, combine.shape, 1)
        y = jnp.zeros(x.shape, f32)
        for e_ref, wg_ref, wu_ref, wd_ref in ((ea_ref, wga_ref, wua_ref, wda_ref),
                                               (eb_ref, wgb_ref, wub_ref, wdb_ref)):
            c = jnp.sum(jnp.where(lane == N_GROUPS + e_ref[i], combine, 0.0), axis=-1, keepdims=True)
            gate = jnp.dot(h, wg_ref[0], preferred_element_type=f32)
            up = jnp.dot(h, wu_ref[0], preferred_element_type=f32)
            hid = (jax.nn.silu(gate) * up * c).astype(bf16)
            y = y + jnp.dot(hid, wd_ref[0], preferred_element_type=f32)
        o_ref[...] = _rms(x + y, gfin_ref[...])

    @pl.when(i >= nu_ref[0])
    def _():
        o_ref[...] = jnp.zeros_like(o_ref)


def _moe(xs, tile_ea, tile_eb, n_used, g_ffn, w_gate, w_up, w_down, g_final, ts):
    R = xs.shape[0]
    E, D, F = w_gate.shape
    const = lambda i, ea, eb, nu: (0, 0)
    rows = lambda i, ea, eb, nu: (jnp.minimum(i, nu[0] - 1), 0)
    first = lambda i, ea, eb, nu: (ea[i], 0, 0)
    second = lambda i, ea, eb, nu: (eb[i], 0, 0)
    grid_spec = pltpu.PrefetchScalarGridSpec(
        num_scalar_prefetch=3,
        grid=(R // ts,),
        in_specs=[
            pl.BlockSpec((ts, D + ROUTER_COLS), rows),
            pl.BlockSpec((1, D), const),
            pl.BlockSpec((1, D, F), first),
            pl.BlockSpec((1, D, F), first),
            pl.BlockSpec((1, F, D), first),
            pl.BlockSpec((1, D, F), second),
            pl.BlockSpec((1, D, F), second),
            pl.BlockSpec((1, F, D), second),
            pl.BlockSpec((1, D), const),
        ],
        out_specs=pl.BlockSpec((ts, D), lambda i, ea, eb, nu: (i, 0)),
    )
    return pl.pallas_call(
        _moe_kernel,
        grid_spec=grid_spec,
        out_shape=jax.ShapeDtypeStruct((R, D), f32),
        compiler_params=_params("arbitrary"),
        name="moe",
    )(tile_ea, tile_eb, n_used, xs, g_ffn, w_gate, w_up, w_down, w_gate, w_up, w_down, g_final)


def kernel(x, attn_norm_g, w_in, b_forget, fox_out_norm_g, moba_out_norm_g, rel_bias, w_out, ffn_norm_g,
           w_group_router, b_group_router, w_expert_router, b_expert_router, w_gate, w_up, w_down,
           final_norm_g):
    B, S, D = x.shape
    T = B * S
    depth = w_in.shape[0]
    fox_w = N_FOX_HEADS * HEAD_DIM
    moba_w = N_MOBA_HEADS * HEAD_DIM
    qkv_w = 3 * (fox_w + moba_w)
    assert w_in.shape[2] == qkv_w + N_FOX_HEADS
    assert N_GROUPS + N_EXPERTS <= ROUTER_COLS

    x2 = x.reshape(T, D)
    out = None
    for l in range(depth):
        w_in_bf16 = w_in.astype(bf16)
        w_fg_t = w_in_bf16[l, :, qkv_w:].T
        assert fox_w == moba_w
        col_scale = np.ones((qkv_w // fox_w,), np.float32)
        col_scale[[0, 3]] = HEAD_DIM ** -0.5 * LOG2E
        qkv, f_t = _in_proj(x2, attn_norm_g[l].reshape(1, D), w_in_bf16, l, w_fg_t, jnp.asarray(col_scale),
                            fox_w)
        key_bias = _fox_decay(f_t, b_forget[l].reshape(N_FOX_HEADS, 1), B, S)
        fo = _fox_attn(qkv, key_bias, B, S, 0)
        mo = _moba_attn(qkv, rel_bias, B, S, 3 * N_FOX_HEADS)

        w_r = jnp.concatenate([w_group_router[l], w_expert_router[l].reshape(D, N_EXPERTS)], axis=1)
        w_r = jnp.pad(w_r, ((0, 0), (0, ROUTER_COLS - w_r.shape[1]))).astype(bf16)
        b_r = jnp.concatenate([b_group_router[l], b_expert_router[l].reshape(N_EXPERTS)])
        b_r = jnp.pad(b_r, (0, ROUTER_COLS - b_r.shape[0])).reshape(1, ROUTER_COLS)
        g_ffn = ffn_norm_g[l].reshape(1, D)
        x1, route, count = _out_proj(fo, mo, x2, fox_out_norm_g[l].reshape(1, fox_w),
                                     moba_out_norm_g[l].reshape(1, moba_w), w_out[l].astype(bf16),
                                     g_ffn, w_r, b_r)
        ts = MOE_TILE
        cls, rank = route[:, 0], route[:, 1]
        counts = count[0, :N_CLASSES]
        padded = (counts + ts - 1) // ts * ts
        ends = jnp.cumsum(padded)
        base = ends - padded
        rows_out = (T // ts + N_CLASSES) * ts
        fill = jnp.concatenate([jnp.stack([base + counts, ends], axis=1).reshape(-1),
                                jnp.stack([ends[-1], jnp.asarray(rows_out, ends.dtype)])])
        tile_start = jnp.arange(rows_out // ts, dtype=jnp.int32) * ts
        tile_cls = jnp.minimum(jnp.sum(tile_start[:, None] >= ends[None, :], axis=1), N_CLASSES - 1)
        pairs = [(a, b) for a in range(EXPERTS_PER_GROUP) for b in range(a + 1, EXPERTS_PER_GROUP)]
        first_of_cls = np.array([g * EXPERTS_PER_GROUP + a for g in range(N_GROUPS) for a, _ in pairs], np.int32)
        second_of_cls = np.array([g * EXPERTS_PER_GROUP + b for g in range(N_GROUPS) for _, b in pairs], np.int32)
        tile_ea = jnp.asarray(first_of_cls)[tile_cls]
        tile_eb = jnp.asarray(second_of_cls)[tile_cls]
        n_used = (ends[-1:] // ts).astype(jnp.int32)

        assert l == depth - 1, "the fused residual + final norm epilogue expects a single layer"
        xs = _dispatch(x1, cls, rank, base, fill, rows_out)
        ys = _moe(xs, tile_ea, tile_eb, n_used, g_ffn, w_gate[l].astype(bf16),
                  w_up[l].astype(bf16), w_down[l].astype(bf16), final_norm_g.reshape(1, D), ts)
        out = _collect(ys, cls, rank, base, T)
        x2 = out
    return out.reshape(B, S, D)
```

```python
import functools
import math

import numpy as np
import jax
import jax.numpy as jnp
from jax import lax
from jax.experimental import pallas as pl
from jax.experimental.pallas import tpu as pltpu

HEAD_DIM = 128
N_FOX_HEADS = 8
N_MOBA_HEADS = 8
MOBA_BLOCK = 256
MOBA_TOPK = 3
REL_BUCKETS = 32
REL_MAX_DIST = 128
N_GROUPS = 4
EXPERTS_PER_GROUP = 4
N_EXPERTS = N_GROUPS * EXPERTS_PER_GROUP
EXPERT_PAIRS = EXPERTS_PER_GROUP * (EXPERTS_PER_GROUP - 1) // 2
N_CLASSES = N_GROUPS * EXPERT_PAIRS
EPS = 1e-6
NEG = -1e30
LOG2E = math.log2(math.e)
LANES = 128
SUBLANES = 8
ROUTER_COLS = LANES
OUT_PROJ_CHUNKS = 4
MOE_TILE = 256
KEY_BIAS_PARTS = 3
ZERO_ROWS = 64
DMA_UNROLL = 8
VMEM_LIMIT = 56 * 1024 * 1024

f32 = jnp.float32
bf16 = jnp.bfloat16


def _params(*sem):
    return pltpu.CompilerParams(dimension_semantics=sem, vmem_limit_bytes=VMEM_LIMIT)


def _pick(n, pref):
    t = min(n, pref)
    assert n % t == 0, (n, pref)
    return t


def _in_proj_kernel(cs_ref, x_ref, g_ref, w_ref, wfg_ref, qkv_ref, f_ref, h_scr):
    @pl.when(pl.program_id(1) == 0)
    def _():
        x = x_ref[...]
        y = x * lax.rsqrt(jnp.mean(x * x, axis=-1, keepdims=True) + EPS)
        hb = (y * g_ref[...]).astype(bf16)
        h_scr[...] = hb
        f_ref[...] = lax.dot_general(wfg_ref[...], hb, (((1,), (1,)), ((), ())),
                                     preferred_element_type=f32)

    acc = jnp.dot(h_scr[...], w_ref[...], preferred_element_type=f32) * cs_ref[pl.program_id(1)]
    for c in range(acc.shape[1] // HEAD_DIM):
        qkv_ref[c] = acc[:, c * HEAD_DIM:(c + 1) * HEAD_DIM].astype(bf16)


def _in_proj(x2, g, w_in, layer, w_fg_t, col_scale, tn):
    T, D = x2.shape
    N = col_scale.shape[0] * tn
    tm = _pick(T, 1024)
    assert N <= w_in.shape[2]
    nh = w_fg_t.shape[0]
    return pl.pallas_call(
        _in_proj_kernel,
        grid=(T // tm, N // tn),
        in_specs=[
            pl.BlockSpec(memory_space=pltpu.SMEM),
            pl.BlockSpec((tm, D), lambda i, j: (i, 0)),
            pl.BlockSpec((1, D), lambda i, j: (0, 0)),
            pl.BlockSpec((None, D, tn), lambda i, j: (layer, 0, j)),
            pl.BlockSpec((nh, D), lambda i, j: (0, 0)),
        ],
        out_specs=[
            pl.BlockSpec((tn // HEAD_DIM, tm, HEAD_DIM), lambda i, j: (j, i, 0)),
            pl.BlockSpec((nh, tm), lambda i, j: (0, i)),
        ],
        out_shape=[
            jax.ShapeDtypeStruct((N // HEAD_DIM, T, HEAD_DIM), bf16),
            jax.ShapeDtypeStruct((nh, T), f32),
        ],
        scratch_shapes=[pltpu.VMEM((tm, D), bf16)],
        compiler_params=_params("arbitrary", "arbitrary"),
        name="in_proj",
    )(col_scale, x2, g, w_in, w_fg_t)


def _fox_decay_kernel(f_ref, b_ref, o_ref):
    lf = jax.nn.log_sigmoid(f_ref[...] + b_ref[...])
    S = lf.shape[1]
    lane = lax.broadcasted_iota(jnp.int32, lf.shape, 1)
    c = lf
    sh = 1
    while sh < S:
        c = c + jnp.where(lane >= sh, pltpu.roll(c, sh, axis=1), 0.0)
        sh *= 2
    rest = -c * LOG2E
    parts = []
    for _ in range(KEY_BIAS_PARTS):
        part = rest.astype(bf16).astype(f32)
        parts.append(part)
        rest = rest - part
    nh = lf.shape[0]
    cols = jnp.concatenate(parts + [jnp.zeros((LANES - KEY_BIAS_PARTS * nh, S), f32)], axis=0)
    o_ref[0] = jnp.transpose(cols).astype(bf16)


def _fox_decay(f_t, b_forget, B, S):
    nh = f_t.shape[0]
    assert KEY_BIAS_PARTS * nh <= LANES
    return pl.pallas_call(
        _fox_decay_kernel,
        grid=(B,),
        in_specs=[
            pl.BlockSpec((nh, S), lambda b: (0, b)),
            pl.BlockSpec((nh, 1), lambda b: (0, 0)),
        ],
        out_specs=pl.BlockSpec((1, S, LANES), lambda b: (b, 0, 0)),
        out_shape=jax.ShapeDtypeStruct((B, S, LANES), bf16),
        compiler_params=_params("arbitrary"),
        name="fox_decay",
    )(f_t, b_forget)


def _qk(q, k):
    return lax.dot_general(q, k, (((1,), (1,)), ((), ())), preferred_element_type=f32)


def _softmax_init(t):
    return jnp.full((t, 1), NEG, f32), jnp.zeros((t, 2 * HEAD_DIM), f32)


def _softmax_step2(carry, s2, v_ones):
    m, acc = carry
    m_new = jnp.maximum(m, jnp.max(s2, axis=-1, keepdims=True))
    p = jnp.exp2(s2 - m_new).astype(bf16)
    acc = jnp.exp2(m - m_new) * acc + jnp.dot(p, v_ones, preferred_element_type=f32)
    return m_new, acc


def _softmax_result(carry):
    _, acc = carry
    return acc[:, :HEAD_DIM] / acc[:, HEAD_DIM:]


def _with_ones(vo_scr, v_ref):
    vo_scr[:, :HEAD_DIM] = v_ref[0]
    vo_scr[:, HEAD_DIM:] = jnp.ones((vo_scr.shape[0], HEAD_DIM), vo_scr.dtype)


def _fox_attn_kernel(q_ref, k_ref, v_ref, kb_ref, o_ref, qaug_scr, kaug_scr, vo_scr, *, t):
    h = pl.program_id(1)
    S = q_ref.shape[1]
    lane = lax.broadcasted_iota(jnp.int32, (S, LANES), 1)
    mine = jnp.logical_and(lane % N_FOX_HEADS == h, lane < KEY_BIAS_PARTS * N_FOX_HEADS)
    qaug_scr[:, :HEAD_DIM] = q_ref[0]
    qaug_scr[:, HEAD_DIM:] = jnp.where(mine, 1.0, 0.0).astype(bf16)
    kaug_scr[:, :HEAD_DIM] = k_ref[0]
    kaug_scr[:, HEAD_DIM:] = kb_ref[0]
    _with_ones(vo_scr, v_ref)
    row = lax.broadcasted_iota(jnp.int32, (t, t), 0)
    col = lax.broadcasted_iota(jnp.int32, (t, t), 1)
    for qi in range(S // t):
        qa = qaug_scr[qi * t:(qi + 1) * t, :]
        carry = _softmax_init(t)
        for kb in range(qi + 1):
            keys = slice(kb * t, (kb + 1) * t)
            s2 = _qk(qa, kaug_scr[keys, :])
            if kb == qi:
                s2 = jnp.where(col <= row, s2, NEG)
            carry = _softmax_step2(carry, s2, vo_scr[keys, :])
        o_ref[qi * t:(qi + 1) * t, :] = _softmax_result(carry).astype(o_ref.dtype)


def _fox_attn(qkv, key_bias, B, S, slab0):
    T = B * S
    H = N_FOX_HEADS
    t = _pick(S, 512)
    kern = functools.partial(_fox_attn_kernel, t=t)
    return pl.pallas_call(
        kern,
        grid=(B, H),
        in_specs=[
            pl.BlockSpec((1, S, HEAD_DIM), lambda b, h: (slab0 + h, b, 0)),
            pl.BlockSpec((1, S, HEAD_DIM), lambda b, h: (slab0 + H + h, b, 0)),
            pl.BlockSpec((1, S, HEAD_DIM), lambda b, h: (slab0 + 2 * H + h, b, 0)),
            pl.BlockSpec((1, S, LANES), lambda b, h: (b, 0, 0)),
        ],
        out_specs=pl.BlockSpec((S, HEAD_DIM), lambda b, h: (b, h)),
        out_shape=jax.ShapeDtypeStruct((T, H * HEAD_DIM), bf16),
        scratch_shapes=[pltpu.VMEM((S, 2 * HEAD_DIM), bf16)] * 3,
        compiler_params=_params("arbitrary", "arbitrary"),
        name="fox_attn",
    )(qkv, qkv, qkv, key_bias)


def _rel_bucket_table(n):
    max_exact = REL_BUCKETS // 2
    d = np.arange(n)
    ratio = np.log(np.maximum(d, 1).astype(np.float32) / np.float32(max_exact)) / np.float32(
        math.log(REL_MAX_DIST / max_exact))
    large = max_exact + (ratio * np.float32(REL_BUCKETS - max_exact)).astype(np.int32)
    large = np.minimum(large, REL_BUCKETS - 1)
    return np.where(d < max_exact, d, large).astype(np.int32)


def _moba_attn_kernel(rel_ref, q_ref, k_ref, v_ref, avg_ref, hot_ref, bk_ref, o_ref, bias_scr, kaug_scr,
                      qaug_scr, vo_scr):
    h = pl.program_id(1)
    blk = MOBA_BLOCK
    t = 2 * blk
    S = q_ref.shape[1]
    far_bias = rel_ref[h, REL_BUCKETS - 1]
    row = lax.broadcasted_iota(jnp.int32, (blk, blk), 0)
    col = lax.broadcasted_iota(jnp.int32, (blk, blk), 1)
    tiles = []
    for which in range(2):
        bk = bk_ref[which]
        tile = jnp.zeros((blk, blk), f32)
        for b in range(REL_BUCKETS - 1):
            tile = jnp.where(bk == b, (rel_ref[h, b] - far_bias) * LOG2E, tile)
        tiles.append(jnp.where(col <= row, tile, NEG) if which == 0 else tile)
    own_tile, prev_tile = tiles
    zero_tile = jnp.zeros((blk, blk), f32)
    for which, quads in enumerate((((zero_tile, prev_tile), (zero_tile, zero_tile)),
                                   ((own_tile, zero_tile), (prev_tile, own_tile)))):
        for r in range(2):
            for c in range(2):
                bias_scr[which, r * blk:(r + 1) * blk, c * blk:(c + 1) * blk] = quads[r][c]
    kaug_scr[:, :HEAD_DIM] = k_ref[0]
    kaug_scr[:, HEAD_DIM:] = hot_ref[...]
    _with_ones(vo_scr, v_ref)
    kmean = jnp.dot(avg_ref[...], k_ref[0], preferred_element_type=f32).astype(bf16)

    nbp = -(-(S // blk) // SUBLANES) * SUBLANES
    block = lax.broadcasted_iota(jnp.int32, (nbp, S), 0)
    own = lax.broadcasted_iota(jnp.int32, (nbp, S), 1) // blk
    valid = block < own
    gate = jnp.where(valid, _qk(kmean[:nbp], q_ref[0]), NEG)
    sel = block == own
    for _ in range(MOBA_TOPK):
        mx = jnp.max(gate, axis=0, keepdims=True)
        first = jnp.min(jnp.where(gate == mx, block, nbp), axis=0, keepdims=True)
        pick = block == first
        sel = jnp.logical_or(sel, jnp.logical_and(pick, valid))
        gate = jnp.where(pick, -jnp.inf, gate)
    chosen = jnp.concatenate([jnp.where(sel, 0.0, NEG), jnp.full((LANES - nbp, S), NEG, f32)], axis=0)
    qaug_scr[:, :HEAD_DIM] = q_ref[0]
    qaug_scr[:, HEAD_DIM:] = jnp.transpose(chosen).astype(bf16)

    for j in range(S // t):
        rows = slice(j * t, (j + 1) * t)
        qa = qaug_scr[rows, :]
        carry = _softmax_step2(_softmax_init(t), _qk(qa, kaug_scr[rows, :]) + bias_scr[1], vo_scr[rows, :])
        if j > 0:
            keys = slice((j - 1) * t, j * t)
            carry = _softmax_step2(carry, _qk(qa, kaug_scr[keys, :]) + bias_scr[0], vo_scr[keys, :])
        for c in range(j - 1):
            keys = slice(c * t, (c + 1) * t)
            carry = _softmax_step2(carry, _qk(qa, kaug_scr[keys, :]), vo_scr[keys, :])
        o_ref[rows, :] = _softmax_result(carry).astype(o_ref.dtype)


def _moba_attn(qkv, rel_bias, B, S, slab0):
    T = B * S
    H = N_MOBA_HEADS
    blk = MOBA_BLOCK
    assert S % (2 * blk) == 0 and S // blk <= LANES
    nb = S // blk
    avg = np.zeros((LANES, S), np.float32)
    for n in range(nb):
        avg[n, n * blk:(n + 1) * blk] = 1.0 / blk
    hot = (avg.T > 0).astype(np.float32)
    table = _rel_bucket_table(2 * blk)
    dist = np.arange(blk)[:, None] - np.arange(blk)[None, :]
    buckets = np.stack([table[np.maximum(dist, 0)], table[dist + blk]]).astype(np.int32)
    grid_spec = pltpu.PrefetchScalarGridSpec(
        num_scalar_prefetch=1,
        grid=(B, H),
        in_specs=[
            pl.BlockSpec((1, S, HEAD_DIM), lambda b, h, r: (slab0 + h, b, 0)),
            pl.BlockSpec((1, S, HEAD_DIM), lambda b, h, r: (slab0 + H + h, b, 0)),
            pl.BlockSpec((1, S, HEAD_DIM), lambda b, h, r: (slab0 + 2 * H + h, b, 0)),
            pl.BlockSpec((LANES, S), lambda b, h, r: (0, 0)),
            pl.BlockSpec((S, LANES), lambda b, h, r: (0, 0)),
            pl.BlockSpec((2, blk, blk), lambda b, h, r: (0, 0, 0)),
        ],
        out_specs=pl.BlockSpec((S, HEAD_DIM), lambda b, h, r: (b, h)),
        scratch_shapes=[pltpu.VMEM((2, 2 * blk, 2 * blk), f32)] + [pltpu.VMEM((S, 2 * HEAD_DIM), bf16)] * 3,
    )
    return pl.pallas_call(
        _moba_attn_kernel,
        grid_spec=grid_spec,
        out_shape=jax.ShapeDtypeStruct((T, H * HEAD_DIM), bf16),
        compiler_params=_params("arbitrary", "arbitrary"),
        name="moba_attn",
    )(rel_bias, qkv, qkv, qkv, jnp.asarray(avg, bf16), jnp.asarray(hot, bf16), jnp.asarray(buckets))


def _rms(x, g):
    return x * lax.rsqrt(jnp.mean(x * x, axis=-1, keepdims=True) + EPS) * g


def _first_max(x, lane):
    mx = jnp.max(x, axis=-1, keepdims=True)
    return mx, jnp.min(jnp.where(x == mx, lane, LANES), axis=-1, keepdims=True)


def _route(logits, lane, grp=None):
    gl = jnp.where(lane < N_GROUPS, logits, -jnp.inf)
    gmax, top = _first_max(gl, lane)
    grp = top if grp is None else grp
    pg = jnp.exp(gl - gmax)
    p_top_group = jnp.sum(jnp.where(lane == grp, pg, 0.0), axis=-1, keepdims=True) / jnp.sum(
        pg, axis=-1, keepdims=True)
    lo = N_GROUPS + grp * EXPERTS_PER_GROUP
    in_grp = jnp.logical_and(lane >= lo, lane < lo + EXPERTS_PER_GROUP)
    el = jnp.where(in_grp, logits, -jnp.inf)
    pe = jnp.exp(el - jnp.max(el, axis=-1, keepdims=True))
    p_exp = jnp.where(in_grp, pe / jnp.sum(pe, axis=-1, keepdims=True), -1.0)
    p1, i1 = _first_max(p_exp, lane)
    p2, i2 = _first_max(jnp.where(lane == i1, -1.0, p_exp), lane)
    tot = p1 + p2
    combine = jnp.where(lane == i1, p1 / tot, jnp.where(lane == i2, p2 / tot, 0.0)) * p_top_group
    return grp, jnp.minimum(i1, i2) - lo, jnp.maximum(i1, i2) - lo, combine


def _out_proj_kernel(fo_ref, mo_ref, x_ref, gf_ref, gm_ref, w_ref, gffn_ref, wr_ref, br_ref,
                     x1_ref, route_ref, count_ref, count_scr, x1_scr):
    i = pl.program_id(0)

    @pl.when(i == 0)
    def _():
        count_scr[...] = jnp.zeros_like(count_scr)
        x1_scr[1] = jnp.zeros(x1_scr.shape[1:], x1_scr.dtype)

    tm, D = x_ref.shape
    chunk = D // OUT_PROJ_CHUNKS

    fo = _rms(fo_ref[...].astype(f32), gf_ref[...]).astype(bf16)
    mo = _rms(mo_ref[...].astype(f32), gm_ref[...]).astype(bf16)
    mix = jnp.concatenate([fo, mo], axis=-1)

    def project(c):
        cols = slice(c * chunk, (c + 1) * chunk)
        x1_scr[i % 2, :, cols] = x_ref[:, cols] + jnp.dot(mix, w_ref[:, cols], preferred_element_type=f32)

    x1 = x1_scr[(i + 1) % 2]
    h2 = _rms(x1, gffn_ref[...]).astype(bf16)
    logits = jnp.dot(h2, wr_ref[...], preferred_element_type=f32) + br_ref[...]
    project(0)
    lane = lax.broadcasted_iota(jnp.int32, logits.shape, 1)
    grp, ea, eb, combine = _route(logits, lane)
    x1_ref[:, :D] = x1
    x1_ref[:, D:] = combine
    project(1)
    pair = (ea * (2 * EXPERTS_PER_GROUP - 1 - ea)) // 2 + eb - ea - 1
    cls = grp * EXPERT_PAIRS + pair
    onehot = jnp.where(lane == cls, 1.0, 0.0)
    earlier = lax.broadcasted_iota(jnp.int32, (tm, tm), 1) < lax.broadcasted_iota(jnp.int32, (tm, tm), 0)
    counts = count_scr[...]
    before = jnp.dot(jnp.where(earlier, 1.0, 0.0).astype(bf16), onehot.astype(bf16),
                     preferred_element_type=f32) + counts
    rank = jnp.sum(onehot * before, axis=-1, keepdims=True).astype(jnp.int32)
    route_ref[...] = jnp.where(lane == 0, cls, jnp.where(lane == 1, rank, 0))
    counts = counts + jnp.where(i > 0, jnp.sum(onehot, axis=0, keepdims=True), 0.0)
    count_scr[...] = counts
    count_ref[...] = counts.astype(jnp.int32)
    for c in range(2, OUT_PROJ_CHUNKS):
        project(c)


def _out_proj(fo, mo, x2, gf, gm, w_out, g_ffn, w_r, b_r):
    T, D = x2.shape
    Wf = fo.shape[1]
    Wm = mo.shape[1]
    tm = _pick(T, 512)
    n = T // tm
    const = lambda i: (0, 0)
    rows_in = lambda i: (jnp.minimum(i, n - 1), 0)
    rows = lambda i: (jnp.maximum(i - 1, 0), 0)
    return pl.pallas_call(
        _out_proj_kernel,
        grid=(n + 1,),
        in_specs=[
            pl.BlockSpec((tm, Wf), rows_in),
            pl.BlockSpec((tm, Wm), rows_in),
            pl.BlockSpec((tm, D), rows_in),
            pl.BlockSpec((1, Wf), const),
            pl.BlockSpec((1, Wm), const),
            pl.BlockSpec((Wf + Wm, D), const),
            pl.BlockSpec((1, D), const),
            pl.BlockSpec((D, ROUTER_COLS), const),
            pl.BlockSpec((1, ROUTER_COLS), const),
        ],
        out_specs=[
            pl.BlockSpec((tm, D + ROUTER_COLS), rows),
            pl.BlockSpec((tm, ROUTER_COLS), rows),
            pl.BlockSpec((1, ROUTER_COLS), const),
        ],
        out_shape=[
            jax.ShapeDtypeStruct((T, D + ROUTER_COLS), f32),
            jax.ShapeDtypeStruct((T, ROUTER_COLS), jnp.int32),
            jax.ShapeDtypeStruct((1, ROUTER_COLS), jnp.int32),
        ],
        scratch_shapes=[pltpu.VMEM((1, ROUTER_COLS), f32), pltpu.VMEM((2, tm, D), f32)],
        compiler_params=_params("arbitrary"),
        name="out_proj",
    )(fo, mo, x2, gf, gm, w_out, g_ffn, w_r, b_r)


def _start_rows(row_copy, n):
    assert n % DMA_UNROLL == 0

    def body(g, carry):
        for k in range(DMA_UNROLL):
            row_copy(g * DMA_UNROLL + k).start()
        return carry

    lax.fori_loop(0, n // DMA_UNROLL, body, 0)


def _slot(cls_ref, rank_ref, base_ref, t):
    return base_ref[cls_ref[t]] + rank_ref[t]


def _dispatch_kernel(cls_ref, rank_ref, base_ref, fill_ref, x_ref, xs_ref, zero_scr, sem, zsem):
    i = pl.program_id(0)
    tm = x_ref.shape[0]

    zrows = zero_scr.shape[0]

    def row_copy(r):
        d = _slot(cls_ref, rank_ref, base_ref, i * tm + r)
        return pltpu.make_async_copy(x_ref.at[pl.ds(r, 1)], xs_ref.at[pl.ds(d, 1)], sem)

    def zero_row(r):
        return pltpu.make_async_copy(zero_scr.at[pl.ds(0, 1)], xs_ref.at[pl.ds(r, 1)], zsem)

    def zero_sublanes(c):
        dst = xs_ref.at[pl.ds(pl.multiple_of(c * SUBLANES, SUBLANES), SUBLANES)]
        return pltpu.make_async_copy(zero_scr.at[pl.ds(0, SUBLANES)], dst, zsem)

    def zero_chunk(c):
        return pltpu.make_async_copy(zero_scr, xs_ref.at[pl.ds(pl.multiple_of(c * zrows, zrows), zrows)], zsem)

    def start(copy):
        def body(r, carry):
            copy(r).start()
            return carry
        return body

    def wait(copy):
        def body(r, carry):
            copy(r).wait()
            return carry
        return body

    _start_rows(row_copy, tm)

    @pl.when(i == 0)
    def _():
        zero_scr[...] = jnp.zeros_like(zero_scr)

        def fill_class(c, carry):
            lo, hi = fill_ref[2 * c], fill_ref[2 * c + 1]
            mid = jnp.minimum((lo + SUBLANES - 1) // SUBLANES * SUBLANES, hi)
            lax.fori_loop(lo, mid, start(zero_row), 0)
            lax.fori_loop(lo, mid, wait(zero_row), 0)
            lax.fori_loop(mid // SUBLANES, hi // SUBLANES, start(zero_sublanes), 0)
            lax.fori_loop(mid // SUBLANES, hi // SUBLANES, wait(zero_sublanes), 0)
            return carry

        lax.fori_loop(0, N_CLASSES, fill_class, 0)
        lo, hi = fill_ref[2 * N_CLASSES] // zrows, fill_ref[2 * N_CLASSES + 1] // zrows
        lax.fori_loop(lo, hi, start(zero_chunk), 0)
        lax.fori_loop(lo, hi, wait(zero_chunk), 0)

    pltpu.make_async_copy(x_ref, xs_ref.at[pl.ds(0, tm)], sem).wait()


def _dispatch(x1, cls, rank, base, fill, rows_out):
    T, D = x1.shape
    tm = _pick(T, 512)
    grid_spec = pltpu.PrefetchScalarGridSpec(
        num_scalar_prefetch=4,
        grid=(T // tm,),
        in_specs=[pl.BlockSpec((tm, D), lambda i, *_: (i, 0))],
        out_specs=pl.BlockSpec(memory_space=pl.ANY),
        scratch_shapes=[pltpu.VMEM((ZERO_ROWS, D), f32), pltpu.SemaphoreType.DMA, pltpu.SemaphoreType.DMA],
    )
    return pl.pallas_call(
        _dispatch_kernel,
        grid_spec=grid_spec,
        out_shape=jax.ShapeDtypeStruct((rows_out, D), f32),
        compiler_params=_params("arbitrary"),
        name="moe_dispatch",
    )(cls, rank, base, fill, x1)


def _collect_kernel(cls_ref, rank_ref, base_ref, ys_ref, o_ref, sem):
    i = pl.program_id(0)
    tm = o_ref.shape[0]

    def row_copy(r):
        d = _slot(cls_ref, rank_ref, base_ref, i * tm + r)
        return pltpu.make_async_copy(ys_ref.at[pl.ds(d, 1)], o_ref.at[pl.ds(r, 1)], sem)

    _start_rows(row_copy, tm)
    pltpu.make_async_copy(ys_ref.at[pl.ds(0, tm)], o_ref, sem).wait()


def _collect(ys, cls, rank, base, T):
    D = ys.shape[1]
    tm = _pick(T, 512)
    grid_spec = pltpu.PrefetchScalarGridSpec(
        num_scalar_prefetch=3,
        grid=(T // tm,),
        in_specs=[pl.BlockSpec(memory_space=pl.ANY)],
        out_specs=pl.BlockSpec((tm, D), lambda i, *_: (i, 0)),
        scratch_shapes=[pltpu.SemaphoreType.DMA],
    )
    return pl.pallas_call(
        _collect_kernel,
        grid_spec=grid_spec,
        out_shape=jax.ShapeDtypeStruct((T, D), f32),
        compiler_params=_params("arbitrary"),
        name="moe_collect",
    )(cls, rank, base, ys)


def _moe_kernel(ea_ref, eb_ref, nu_ref, xs_ref, gffn_ref, wga_ref, wua_ref, wda_ref,
                wgb_ref, wub_ref, wdb_ref, gfin_ref, o_ref):
    i = pl.program_id(0)
    D = o_ref.shape[1]

    @pl.when(i < nu_ref[0])
    def _():
        x = xs_ref[:, :D]
        combine = xs_ref[:, D:]
        h = _rms(x, gffn_ref[...]).astype(bf16)
        lane = lax.broadcasted_iota(jnp.int32, combine.shape, 1)
        y = jnp.zeros(x.shape, f32)
        for e_ref, wg_ref, wu_ref, wd_ref in ((ea_ref, wga_ref, wua_ref, wda_ref),
                                               (eb_ref, wgb_ref, wub_ref, wdb_ref)):
            c = jnp.sum(jnp.where(lane == N_GROUPS + e_ref[i], combine, 0.0), axis=-1, keepdims=True)
            gate = jnp.dot(h, wg_ref[0], preferred_element_type=f32)
            up = jnp.dot(h, wu_ref[0], preferred_element_type=f32)
            hid = (jax.nn.silu(gate) * up * c).astype(bf16)
            y = y + jnp.dot(hid, wd_ref[0], preferred_element_type=f32)
        o_ref[...] = _rms(x + y, gfin_ref[...])

    @pl.when(i >= nu_ref[0])
    def _():
        o_ref[...] = jnp.zeros_like(o_ref)


def _moe(xs, tile_ea, tile_eb, n_used, g_ffn, w_gate, w_up, w_down, g_final, ts):
    R = xs.shape[0]
    E, D, F = w_gate.shape
    const = lambda i, ea, eb, nu: (0, 0)
    rows = lambda i, ea, eb, nu: (jnp.minimum(i, nu[0] - 1), 0)
    first = lambda i, ea, eb, nu: (ea[i], 0, 0)
    second = lambda i, ea, eb, nu: (eb[i], 0, 0)
    grid_spec = pltpu.PrefetchScalarGridSpec(
        num_scalar_prefetch=3,
        grid=(R // ts,),
        in_specs=[
            pl.BlockSpec((ts, D + ROUTER_COLS), rows),
            pl.BlockSpec((1, D), const),
            pl.BlockSpec((1, D, F), first),
            pl.BlockSpec((1, D, F), first),
            pl.BlockSpec((1, F, D), first),
            pl.BlockSpec((1, D, F), second),
            pl.BlockSpec((1, D, F), second),
            pl.BlockSpec((1, F, D), second),
            pl.BlockSpec((1, D), const),
        ],
        out_specs=pl.BlockSpec((ts, D), lambda i, ea, eb, nu: (i, 0)),
    )
    return pl.pallas_call(
        _moe_kernel,
        grid_spec=grid_spec,
        out_shape=jax.ShapeDtypeStruct((R, D), f32),
        compiler_params=_params("arbitrary"),
        name="moe",
    )(tile_ea, tile_eb, n_used, xs, g_ffn, w_gate, w_up, w_down, w_gate, w_up, w_down, g_final)


def kernel(x, attn_norm_g, w_in, b_forget, fox_out_norm_g, moba_out_norm_g, rel_bias, w_out, ffn_norm_g,
           w_group_router, b_group_router, w_expert_router, b_expert_router, w_gate, w_up, w_down,
           final_norm_g):
    B, S, D = x.shape
    T = B * S
    depth = w_in.shape[0]
    fox_w = N_FOX_HEADS * HEAD_DIM
    moba_w = N_MOBA_HEADS * HEAD_DIM
    qkv_w = 3 * (fox_w + moba_w)
    assert w_in.shape[2] == qkv_w + N_FOX_HEADS
    assert N_GROUPS + N_EXPERTS <= ROUTER_COLS

    x2 = x.reshape(T, D)
    out = None
    for l in range(depth):
        w_in_bf16 = w_in.astype(bf16)
        w_fg_t = w_in_bf16[l, :, qkv_w:].T
        assert fox_w == moba_w
        col_scale = np.ones((qkv_w // fox_w,), np.float32)
        col_scale[[0, 3]] = HEAD_DIM ** -0.5 * LOG2E
        qkv, f_t = _in_proj(x2, attn_norm_g[l].reshape(1, D), w_in_bf16, l, w_fg_t, jnp.asarray(col_scale),
                            fox_w)
        key_bias = _fox_decay(f_t, b_forget[l].reshape(N_FOX_HEADS, 1), B, S)
        fo = _fox_attn(qkv, key_bias, B, S, 0)
        mo = _moba_attn(qkv, rel_bias, B, S, 3 * N_FOX_HEADS)

        w_r = jnp.concatenate([w_group_router[l], w_expert_router[l].reshape(D, N_EXPERTS)], axis=1)
        w_r = jnp.pad(w_r, ((0, 0), (0, ROUTER_COLS - w_r.shape[1]))).astype(bf16)
        b_r = jnp.concatenate([b_group_router[l], b_expert_router[l].reshape(N_EXPERTS)])
        b_r = jnp.pad(b_r, (0, ROUTER_COLS - b_r.shape[0])).reshape(1, ROUTER_COLS)
        g_ffn = ffn_norm_g[l].reshape(1, D)
        x1, route, count = _out_proj(fo, mo, x2, fox_out_norm_g[l].reshape(1, fox_w),
                                     moba_out_norm_g[l].reshape(1, moba_w), w_out[l].astype(bf16),
                                     g_ffn, w_r, b_r)
        ts = MOE_TILE
        cls, rank = route[:, 0], route[:, 1]
        counts = count[0, :N_CLASSES]
        padded = (counts + ts - 1) // ts * ts
        ends = jnp.cumsum(padded)
        base = ends - padded
        rows_out = (T // ts + N_CLASSES) * ts
        fill = jnp.concatenate([jnp.stack([base + counts, ends], axis=1).reshape(-1),
                                jnp.stack([ends[-1], jnp.asarray(rows_out, ends.dtype)])])
        tile_start = jnp.arange(rows_out // ts, dtype=jnp.int32) * ts
        tile_cls = jnp.minimum(jnp.sum(tile_start[:, None] >= ends[None, :], axis=1), N_CLASSES - 1)
        pairs = [(a, b) for a in range(EXPERTS_PER_GROUP) for b in range(a + 1, EXPERTS_PER_GROUP)]
        first_of_cls = np.array([g * EXPERTS_PER_GROUP + a for g in range(N_GROUPS) for a, _ in pairs], np.int32)
        second_of_cls = np.array([g * EXPERTS_PER_GROUP + b for g in range(N_GROUPS) for _, b in pairs], np.int32)
        tile_ea = jnp.asarray(first_of_cls)[tile_cls]
        tile_eb = jnp.asarray(second_of_cls)[tile_cls]
        n_used = (ends[-1:] // ts).astype(jnp.int32)

        assert l == depth - 1, "the fused residual + final norm epilogue expects a single layer"
        xs = _dispatch(x1, cls, rank, base, fill, rows_out)
        ys = _moe(xs, tile_ea, tile_eb, n_used, g_ffn, w_gate[l].astype(bf16),
                  w_up[l].astype(bf16), w_down[l].astype(bf16), final_norm_g.reshape(1, D), ts)
        out = _collect(ys, cls, rank, base, T)
        x2 = out
    return out.reshape(B, S, D)
```

```python
import functools
import math

import numpy as np
import jax
import jax.numpy as jnp
from jax import lax
from jax.experimental import pallas as pl
from jax.experimental.pallas import tpu as pltpu

HEAD_DIM = 128
N_FOX_HEADS = 8
N_MOBA_HEADS = 8
MOBA_BLOCK = 256
MOBA_TOPK = 3
REL_BUCKETS = 32
REL_MAX_DIST = 128
N_GROUPS = 4
EXPERTS_PER_GROUP = 4
N_EXPERTS = N_GROUPS * EXPERTS_PER_GROUP
EXPERT_PAIRS = EXPERTS_PER_GROUP * (EXPERTS_PER_GROUP - 1) // 2
N_CLASSES = N_GROUPS * EXPERT_PAIRS
EPS = 1e-6
NEG = -1e30
LOG2E = math.log2(math.e)
LANES = 128
SUBLANES = 8
ROUTER_COLS = LANES
OUT_PROJ_CHUNKS = 4
MOE_TILE = 256
KEY_BIAS_PARTS = 3
ZERO_ROWS = 64
DMA_UNROLL = 8
VMEM_LIMIT = 56 * 1024 * 1024

f32 = jnp.float32
bf16 = jnp.bfloat16


def _params(*sem):
    return pltpu.CompilerParams(dimension_semantics=sem, vmem_limit_bytes=VMEM_LIMIT)


def _pick(n, pref):
    t = min(n, pref)
    assert n % t == 0, (n, pref)
    return t


def _in_proj_kernel(cs_ref, x_ref, g_ref, w_ref, wfg_ref, qkv_ref, f_ref, h_scr):
    @pl.when(pl.program_id(1) == 0)
    def _():
        x = x_ref[...]
        y = x * lax.rsqrt(jnp.mean(x * x, axis=-1, keepdims=True) + EPS)
        hb = (y * g_ref[...]).astype(bf16)
        h_scr[...] = hb
        f_ref[...] = lax.dot_general(wfg_ref[...], hb, (((1,), (1,)), ((), ())),
                                     preferred_element_type=f32)

    acc = jnp.dot(h_scr[...], w_ref[...], preferred_element_type=f32) * cs_ref[pl.program_id(1)]
    for c in range(acc.shape[1] // HEAD_DIM):
        qkv_ref[c] = acc[:, c * HEAD_DIM:(c + 1) * HEAD_DIM].astype(bf16)


def _in_proj(x2, g, w_in, layer, w_fg_t, col_scale, tn):
    T, D = x2.shape
    N = col_scale.shape[0] * tn
    tm = _pick(T, 1024)
    assert N <= w_in.shape[2]
    nh = w_fg_t.shape[0]
    return pl.pallas_call(
        _in_proj_kernel,
        grid=(T // tm, N // tn),
        in_specs=[
            pl.BlockSpec(memory_space=pltpu.SMEM),
            pl.BlockSpec((tm, D), lambda i, j: (i, 0)),
            pl.BlockSpec((1, D), lambda i, j: (0, 0)),
            pl.BlockSpec((None, D, tn), lambda i, j: (layer, 0, j)),
            pl.BlockSpec((nh, D), lambda i, j: (0, 0)),
        ],
        out_specs=[
            pl.BlockSpec((tn // HEAD_DIM, tm, HEAD_DIM), lambda i, j: (j, i, 0)),
            pl.BlockSpec((nh, tm), lambda i, j: (0, i)),
        ],
        out_shape=[
            jax.ShapeDtypeStruct((N // HEAD_DIM, T, HEAD_DIM), bf16),
            jax.ShapeDtypeStruct((nh, T), f32),
        ],
        scratch_shapes=[pltpu.VMEM((tm, D), bf16)],
        compiler_params=_params("arbitrary", "arbitrary"),
        name="in_proj",
    )(col_scale, x2, g, w_in, w_fg_t)


def _fox_decay_kernel(f_ref, b_ref, o_ref):
    lf = jax.nn.log_sigmoid(f_ref[...] + b_ref[...])
    S = lf.shape[1]
    lane = lax.broadcasted_iota(jnp.int32, lf.shape, 1)
    c = lf
    sh = 1
    while sh < S:
        c = c + jnp.where(lane >= sh, pltpu.roll(c, sh, axis=1), 0.0)
        sh *= 2
    rest = -c * LOG2E
    parts = []
    for _ in range(KEY_BIAS_PARTS):
        part = rest.astype(bf16).astype(f32)
        parts.append(part)
        rest = rest - part
    nh = lf.shape[0]
    cols = jnp.concatenate(parts + [jnp.zeros((LANES - KEY_BIAS_PARTS * nh, S), f32)], axis=0)
    o_ref[0] = jnp.transpose(cols).astype(bf16)


def _fox_decay(f_t, b_forget, B, S):
    nh = f_t.shape[0]
    assert KEY_BIAS_PARTS * nh <= LANES
    return pl.pallas_call(
        _fox_decay_kernel,
        grid=(B,),
        in_specs=[
            pl.BlockSpec((nh, S), lambda b: (0, b)),
            pl.BlockSpec((nh, 1), lambda b: (0, 0)),
        ],
        out_specs=pl.BlockSpec((1, S, LANES), lambda b: (b, 0, 0)),
        out_shape=jax.ShapeDtypeStruct((B, S, LANES), bf16),
        compiler_params=_params("arbitrary"),
        name="fox_decay",
    )(f_t, b_forget)


def _qk(q, k):
    return lax.dot_general(q, k, (((1,), (1,)), ((), ())), preferred_element_type=f32)


def _softmax_init(t):
    return jnp.full((t, 1), NEG, f32), jnp.zeros((t, 2 * HEAD_DIM), f32)


def _softmax_step2(carry, s2, v_ones):
    m, acc = carry
    m_new = jnp.maximum(m, jnp.max(s2, axis=-1, keepdims=True))
    p = jnp.exp2(s2 - m_new).astype(bf16)
    acc = jnp.exp2(m - m_new) * acc + jnp.dot(p, v_ones, preferred_element_type=f32)
    return m_new, acc


def _softmax_result(carry):
    _, acc = carry
    return acc[:, :HEAD_DIM] / acc[:, HEAD_DIM:]


def _with_ones(vo_scr, v_ref):
    vo_scr[:, :HEAD_DIM] = v_ref[0]
    vo_scr[:, HEAD_DIM:] = jnp.ones((vo_scr.shape[0], HEAD_DIM), vo_scr.dtype)


def _fox_attn_kernel(q_ref, k_ref, v_ref, kb_ref, o_ref, qaug_scr, kaug_scr, vo_scr, *, t):
    h = pl.program_id(1)
    S = q_ref.shape[1]
    lane = lax.broadcasted_iota(jnp.int32, (S, LANES), 1)
    mine = jnp.logical_and(lane % N_FOX_HEADS == h, lane < KEY_BIAS_PARTS * N_FOX_HEADS)
    qaug_scr[:, :HEAD_DIM] = q_ref[0]
    qaug_scr[:, HEAD_DIM:] = jnp.where(mine, 1.0, 0.0).astype(bf16)
    kaug_scr[:, :HEAD_DIM] = k_ref[0]
    kaug_scr[:, HEAD_DIM:] = kb_ref[0]
    _with_ones(vo_scr, v_ref)
    row = lax.broadcasted_iota(jnp.int32, (t, t), 0)
    col = lax.broadcasted_iota(jnp.int32, (t, t), 1)
    for qi in range(S // t):
        qa = qaug_scr[qi * t:(qi + 1) * t, :]
        carry = _softmax_init(t)
        for kb in range(qi + 1):
            keys = slice(kb * t, (kb + 1) * t)
            s2 = _qk(qa, kaug_scr[keys, :])
            if kb == qi:
                s2 = jnp.where(col <= row, s2, NEG)
            carry = _softmax_step2(carry, s2, vo_scr[keys, :])
        o_ref[qi * t:(qi + 1) * t, :] = _softmax_result(carry).astype(o_ref.dtype)


def _fox_attn(qkv, key_bias, B, S, slab0):
    T = B * S
    H = N_FOX_HEADS
    t = _pick(S, 512)
    kern = functools.partial(_fox_attn_kernel, t=t)
    return pl.pallas_call(
        kern,
        grid=(B, H),
        in_specs=[
            pl.BlockSpec((1, S, HEAD_DIM), lambda b, h: (slab0 + h, b, 0)),
            pl.BlockSpec((1, S, HEAD_DIM), lambda b, h: (slab0 + H + h, b, 0)),
            pl.BlockSpec((1, S, HEAD_DIM), lambda b, h: (slab0 + 2 * H + h, b, 0)),
            pl.BlockSpec((1, S, LANES), lambda b, h: (b, 0, 0)),
        ],
        out_specs=pl.BlockSpec((S, HEAD_DIM), lambda b, h: (b, h)),
        out_shape=jax.ShapeDtypeStruct((T, H * HEAD_DIM), bf16),
        scratch_shapes=[pltpu.VMEM((S, 2 * HEAD_DIM), bf16)] * 3,
        compiler_params=_params("arbitrary", "arbitrary"),
        name="fox_attn",
    )(qkv, qkv, qkv, key_bias)


def _rel_bucket_table(n):
    max_exact = REL_BUCKETS // 2
    d = np.arange(n)
    ratio = np.log(np.maximum(d, 1).astype(np.float32) / np.float32(max_exact)) / np.float32(
        math.log(REL_MAX_DIST / max_exact))
    large = max_exact + (ratio * np.float32(REL_BUCKETS - max_exact)).astype(np.int32)
    large = np.minimum(large, REL_BUCKETS - 1)
    return np.where(d < max_exact, d, large).astype(np.int32)


def _moba_attn_kernel(rel_ref, q_ref, k_ref, v_ref, avg_ref, hot_ref, bk_ref, o_ref, bias_scr, kaug_scr,
                      qaug_scr, vo_scr):
    h = pl.program_id(1)
    blk = MOBA_BLOCK
    t = 2 * blk
    S = q_ref.shape[1]
    far_bias = rel_ref[h, REL_BUCKETS - 1]
    row = lax.broadcasted_iota(jnp.int32, (blk, blk), 0)
    col = lax.broadcasted_iota(jnp.int32, (blk, blk), 1)
    tiles = []
    for which in range(2):
        bk = bk_ref[which]
        tile = jnp.zeros((blk, blk), f32)
        for b in range(REL_BUCKETS - 1):
            tile = jnp.where(bk == b, (rel_ref[h, b] - far_bias) * LOG2E, tile)
        tiles.append(jnp.where(col <= row, tile, NEG) if which == 0 else tile)
    own_tile, prev_tile = tiles
    zero_tile = jnp.zeros((blk, blk), f32)
    for which, quads in enumerate((((zero_tile, prev_tile), (zero_tile, zero_tile)),
                                   ((own_tile, zero_tile), (prev_tile, own_tile)))):
        for r in range(2):
            for c in range(2):
                bias_scr[which, r * blk:(r + 1) * blk, c * blk:(c + 1) * blk] = quads[r][c]
    kaug_scr[:, :HEAD_DIM] = k_ref[0]
    kaug_scr[:, HEAD_DIM:] = hot_ref[...]
    _with_ones(vo_scr, v_ref)
    kmean = jnp.dot(avg_ref[...], k_ref[0], preferred_element_type=f32).astype(bf16)

    nbp = -(-(S // blk) // SUBLANES) * SUBLANES
    block = lax.broadcasted_iota(jnp.int32, (nbp, S), 0)
    own = lax.broadcasted_iota(jnp.int32, (nbp, S), 1) // blk
    valid = block < own
    gate = jnp.where(valid, _qk(kmean[:nbp], q_ref[0]), NEG)
    sel = block == own
    for _ in range(MOBA_TOPK):
        mx = jnp.max(gate, axis=0, keepdims=True)
        first = jnp.min(jnp.where(gate == mx, block, nbp), axis=0, keepdims=True)
        pick = block == first
        sel = jnp.logical_or(sel, jnp.logical_and(pick, valid))
        gate = jnp.where(pick, -jnp.inf, gate)
    chosen = jnp.concatenate([jnp.where(sel, 0.0, NEG), jnp.full((LANES - nbp, S), NEG, f32)], axis=0)
    qaug_scr[:, :HEAD_DIM] = q_ref[0]
    qaug_scr[:, HEAD_DIM:] = jnp.transpose(chosen).astype(bf16)

    for j in range(S // t):
        rows = slice(j * t, (j + 1) * t)
        qa = qaug_scr[rows, :]
        carry = _softmax_step2(_softmax_init(t), _qk(qa, kaug_scr[rows, :]) + bias_scr[1], vo_scr[rows, :])
        if j > 0:
            keys = slice((j - 1) * t, j * t)
            carry = _softmax_step2(carry, _qk(qa, kaug_scr[keys, :]) + bias_scr[0], vo_scr[keys, :])
        for c in range(j - 1):
            keys = slice(c * t, (c + 1) * t)
            carry = _softmax_step2(carry, _qk(qa, kaug_scr[keys, :]), vo_scr[keys, :])
        o_ref[rows, :] = _softmax_result(carry).astype(o_ref.dtype)


def _moba_attn(qkv, rel_bias, B, S, slab0):
    T = B * S
    H = N_MOBA_HEADS
    blk = MOBA_BLOCK
    assert S % (2 * blk) == 0 and S // blk <= LANES
    nb = S // blk
    avg = np.zeros((LANES, S), np.float32)
    for n in range(nb):
        avg[n, n * blk:(n + 1) * blk] = 1.0 / blk
    hot = (avg.T > 0).astype(np.float32)
    table = _rel_bucket_table(2 * blk)
    dist = np.arange(blk)[:, None] - np.arange(blk)[None, :]
    buckets = np.stack([table[np.maximum(dist, 0)], table[dist + blk]]).astype(np.int32)
    grid_spec = pltpu.PrefetchScalarGridSpec(
        num_scalar_prefetch=1,
        grid=(B, H),
        in_specs=[
            pl.BlockSpec((1, S, HEAD_DIM), lambda b, h, r: (slab0 + h, b, 0)),
            pl.BlockSpec((1, S, HEAD_DIM), lambda b, h, r: (slab0 + H + h, b, 0)),
            pl.BlockSpec((1, S, HEAD_DIM), lambda b, h, r: (slab0 + 2 * H + h, b, 0)),
            pl.BlockSpec((LANES, S), lambda b, h, r: (0, 0)),
            pl.BlockSpec((S, LANES), lambda b, h, r: (0, 0)),
            pl.BlockSpec((2, blk, blk), lambda b, h, r: (0, 0, 0)),
        ],
        out_specs=pl.BlockSpec((S, HEAD_DIM), lambda b, h, r: (b, h)),
        scratch_shapes=[pltpu.VMEM((2, 2 * blk, 2 * blk), f32)] + [pltpu.VMEM((S, 2 * HEAD_DIM), bf16)] * 3,
    )
    return pl.pallas_call(
        _moba_attn_kernel,
        grid_spec=grid_spec,
        out_shape=jax.ShapeDtypeStruct((T, H * HEAD_DIM), bf16),
        compiler_params=_params("arbitrary", "arbitrary"),
        name="moba_attn",
    )(rel_bias, qkv, qkv, qkv, jnp.asarray(avg, bf16), jnp.asarray(hot, bf16), jnp.asarray(buckets))


def _rms(x, g):
    return x * lax.rsqrt(jnp.mean(x * x, axis=-1, keepdims=True) + EPS) * g


def _first_max(x, lane):
    mx = jnp.max(x, axis=-1, keepdims=True)
    return mx, jnp.min(jnp.where(x == mx, lane, LANES), axis=-1, keepdims=True)


def _route(logits, lane, grp=None):
    gl = jnp.where(lane < N_GROUPS, logits, -jnp.inf)
    gmax, top = _first_max(gl, lane)
    grp = top if grp is None else grp
    pg = jnp.exp(gl - gmax)
    p_top_group = jnp.sum(jnp.where(lane == grp, pg, 0.0), axis=-1, keepdims=True) / jnp.sum(
        pg, axis=-1, keepdims=True)
    lo = N_GROUPS + grp * EXPERTS_PER_GROUP
    in_grp = jnp.logical_and(lane >= lo, lane < lo + EXPERTS_PER_GROUP)
    el = jnp.where(in_grp, logits, -jnp.inf)
    pe = jnp.exp(el - jnp.max(el, axis=-1, keepdims=True))
    p_exp = jnp.where(in_grp, pe / jnp.sum(pe, axis=-1, keepdims=True), -1.0)
    p1, i1 = _first_max(p_exp, lane)
    p2, i2 = _first_max(jnp.where(lane == i1, -1.0, p_exp), lane)
    tot = p1 + p2
    combine = jnp.where(lane == i1, p1 / tot, jnp.where(lane == i2, p2 / tot, 0.0)) * p_top_group
    return grp, jnp.minimum(i1, i2) - lo, jnp.maximum(i1, i2) - lo, combine


def _out_proj_kernel(fo_ref, mo_ref, x_ref, gf_ref, gm_ref, w_ref, gffn_ref, wr_ref, br_ref,
                     x1_ref, route_ref, count_ref, count_scr, x1_scr):
    i = pl.program_id(0)

    @pl.when(i == 0)
    def _():
        count_scr[...] = jnp.zeros_like(count_scr)
        x1_scr[1] = jnp.zeros(x1_scr.shape[1:], x1_scr.dtype)

    tm, D = x_ref.shape
    chunk = D // OUT_PROJ_CHUNKS

    fo = _rms(fo_ref[...].astype(f32), gf_ref[...]).astype(bf16)
    mo = _rms(mo_ref[...].astype(f32), gm_ref[...]).astype(bf16)
    mix = jnp.concatenate([fo, mo], axis=-1)

    def project(c):
        cols = slice(c * chunk, (c + 1) * chunk)
        x1_scr[i % 2, :, cols] = x_ref[:, cols] + jnp.dot(mix, w_ref[:, cols], preferred_element_type=f32)

    x1 = x1_scr[(i + 1) % 2]
    h2 = _rms(x1, gffn_ref[...]).astype(bf16)
    logits = jnp.dot(h2, wr_ref[...], preferred_element_type=f32) + br_ref[...]
    project(0)
    lane = lax.broadcasted_iota(jnp.int32, logits.shape, 1)
    grp, ea, eb, combine = _route(logits, lane)
    x1_ref[:, :D] = x1
    x1_ref[:, D:] = combine
    project(1)
    pair = (ea * (2 * EXPERTS_PER_GROUP - 1 - ea)) // 2 + eb - ea - 1
    cls = grp * EXPERT_PAIRS + pair
    onehot = jnp.where(lane == cls, 1.0, 0.0)
    earlier = lax.broadcasted_iota(jnp.int32, (tm, tm), 1) < lax.broadcasted_iota(jnp.int32, (tm, tm), 0)
    counts = count_scr[...]
    before = jnp.dot(jnp.where(earlier, 1.0, 0.0).astype(bf16), onehot.astype(bf16),
                     preferred_element_type=f32) + counts
    rank = jnp.sum(onehot * before, axis=-1, keepdims=True).astype(jnp.int32)
    route_ref[...] = jnp.where(lane == 0, cls, jnp.where(lane == 1, rank, 0))
    counts = counts + jnp.where(i > 0, jnp.sum(onehot, axis=0, keepdims=True), 0.0)
    count_scr[...] = counts
    count_ref[...] = counts.astype(jnp.int32)
    for c in range(2, OUT_PROJ_CHUNKS):
        project(c)


def _out_proj(fo, mo, x2, gf, gm, w_out, g_ffn, w_r, b_r):
    T, D = x2.shape
    Wf = fo.shape[1]
    Wm = mo.shape[1]
    tm = _pick(T, 512)
    n = T // tm
    const = lambda i: (0, 0)
    rows_in = lambda i: (jnp.minimum(i, n - 1), 0)
    rows = lambda i: (jnp.maximum(i - 1, 0), 0)
    return pl.pallas_call(
        _out_proj_kernel,
        grid=(n + 1,),
        in_specs=[
            pl.BlockSpec((tm, Wf), rows_in),
            pl.BlockSpec((tm, Wm), rows_in),
            pl.BlockSpec((tm, D), rows_in),
            pl.BlockSpec((1, Wf), const),
            pl.BlockSpec((1, Wm), const),
            pl.BlockSpec((Wf + Wm, D), const),
            pl.BlockSpec((1, D), const),
            pl.BlockSpec((D, ROUTER_COLS), const),
            pl.BlockSpec((1, ROUTER_COLS), const),
        ],
        out_specs=[
            pl.BlockSpec((tm, D + ROUTER_COLS), rows),
            pl.BlockSpec((tm, ROUTER_COLS), rows),
            pl.BlockSpec((1, ROUTER_COLS), const),
        ],
        out_shape=[
            jax.ShapeDtypeStruct((T, D + ROUTER_COLS), f32),
            jax.ShapeDtypeStruct((T, ROUTER_COLS), jnp.int32),
            jax.ShapeDtypeStruct((1, ROUTER_COLS), jnp.int32),
        ],
        scratch_shapes=[pltpu.VMEM((1, ROUTER_COLS), f32), pltpu.VMEM((2, tm, D), f32)],
        compiler_params=_params("arbitrary"),
        name="out_proj",
    )(fo, mo, x2, gf, gm, w_out, g_ffn, w_r, b_r)


def _start_rows(row_copy, n):
    assert n % DMA_UNROLL == 0

    def body(g, carry):
        for k in range(DMA_UNROLL):
            row_copy(g * DMA_UNROLL + k).start()
        return carry

    lax.fori_loop(0, n // DMA_UNROLL, body, 0)


def _slot(cls_ref, rank_ref, base_ref, t):
    return base_ref[cls_ref[t]] + rank_ref[t]


def _dispatch_kernel(cls_ref, rank_ref, base_ref, fill_ref, x_ref, xs_ref, zero_scr, sem, zsem):
    i = pl.program_id(0)
    tm = x_ref.shape[0]

    zrows = zero_scr.shape[0]

    def row_copy(r):
        d = _slot(cls_ref, rank_ref, base_ref, i * tm + r)
        return pltpu.make_async_copy(x_ref.at[pl.ds(r, 1)], xs_ref.at[pl.ds(d, 1)], sem)

    def zero_row(r):
        return pltpu.make_async_copy(zero_scr.at[pl.ds(0, 1)], xs_ref.at[pl.ds(r, 1)], zsem)

    def zero_sublanes(c):
        dst = xs_ref.at[pl.ds(pl.multiple_of(c * SUBLANES, SUBLANES), SUBLANES)]
        return pltpu.make_async_copy(zero_scr.at[pl.ds(0, SUBLANES)], dst, zsem)

    def zero_chunk(c):
        return pltpu.make_async_copy(zero_scr, xs_ref.at[pl.ds(pl.multiple_of(c * zrows, zrows), zrows)], zsem)

    def start(copy):
        def body(r, carry):
            copy(r).start()
            return carry
        return body

    def wait(copy):
        def body(r, carry):
            copy(r).wait()
            return carry
        return body

    _start_rows(row_copy, tm)

    def zero_fill(act):
        def fill_class(c, carry):
            lo, hi = fill_ref[2 * c], fill_ref[2 * c + 1]
            mid = jnp.minimum((lo + SUBLANES - 1) // SUBLANES * SUBLANES, hi)
            lax.fori_loop(lo, mid, act(zero_row), 0)
            lax.fori_loop(mid // SUBLANES, hi // SUBLANES, act(zero_sublanes), 0)
            return carry

        lax.fori_loop(0, N_CLASSES, fill_class, 0)
        lo, hi = fill_ref[2 * N_CLASSES] // zrows, fill_ref[2 * N_CLASSES + 1] // zrows
        lax.fori_loop(lo, hi, act(zero_chunk), 0)

    @pl.when(i == 0)
    def _():
        zero_scr[...] = jnp.zeros_like(zero_scr)
        zero_fill(start)

    pltpu.make_async_copy(x_ref, xs_ref.at[pl.ds(0, tm)], sem).wait()

    @pl.when(i == 0)
    def _():
        zero_fill(wait)


def _dispatch(x1, cls, rank, base, fill, rows_out):
    T, D = x1.shape
    tm = _pick(T, 512)
    grid_spec = pltpu.PrefetchScalarGridSpec(
        num_scalar_prefetch=4,
        grid=(T // tm,),
        in_specs=[pl.BlockSpec((tm, D), lambda i, *_: (i, 0))],
        out_specs=pl.BlockSpec(memory_space=pl.ANY),
        scratch_shapes=[pltpu.VMEM((ZERO_ROWS, D), f32), pltpu.SemaphoreType.DMA, pltpu.SemaphoreType.DMA],
    )
    return pl.pallas_call(
        _dispatch_kernel,
        grid_spec=grid_spec,
        out_shape=jax.ShapeDtypeStruct((rows_out, D), f32),
        compiler_params=_params("arbitrary"),
        name="moe_dispatch",
    )(cls, rank, base, fill, x1)


def _collect_kernel(cls_ref, rank_ref, base_ref, ys_ref, o_ref, sem):
    i = pl.program_id(0)
    tm = o_ref.shape[0]

    def row_copy(r):
        d = _slot(cls_ref, rank_ref, base_ref, i * tm + r)
        return pltpu.make_async_copy(ys_ref.at[pl.ds(d, 1)], o_ref.at[pl.ds(r, 1)], sem)

    _start_rows(row_copy, tm)
    pltpu.make_async_copy(ys_ref.at[pl.ds(0, tm)], o_ref, sem).wait()


def _collect(ys, cls, rank, base, T):
    D = ys.shape[1]
    tm = _pick(T, 512)
    grid_spec = pltpu.PrefetchScalarGridSpec(
        num_scalar_prefetch=3,
        grid=(T // tm,),
        in_specs=[pl.BlockSpec(memory_space=pl.ANY)],
        out_specs=pl.BlockSpec((tm, D), lambda i, *_: (i, 0)),
        scratch_shapes=[pltpu.SemaphoreType.DMA],
    )
    return pl.pallas_call(
        _collect_kernel,
        grid_spec=grid_spec,
        out_shape=jax.ShapeDtypeStruct((T, D), f32),
        compiler_params=_params("arbitrary"),
        name="moe_collect",
    )(cls, rank, base, ys)


def _moe_kernel(ea_ref, eb_ref, nu_ref, xs_ref, gffn_ref, wga_ref, wua_ref, wda_ref,
                wgb_ref, wub_ref, wdb_ref, gfin_ref, o_ref):
    i = pl.program_id(0)
    D = o_ref.shape[1]

    @pl.when(i < nu_ref[0])
    def _():
        x = xs_ref[:, :D]
        combine = xs_ref[:, D:]
        h = _rms(x, gffn_ref[...]).astype(bf16)
        lane = lax.broadcasted_iota(jnp.int32, combine.shape, 1)
        y = jnp.zeros(x.shape, f32)
        for e_ref, wg_ref, wu_ref, wd_ref in ((ea_ref, wga_ref, wua_ref, wda_ref),
                                               (eb_ref, wgb_ref, wub_ref, wdb_ref)):
            c = jnp.sum(jnp.where(lane == N_GROUPS + e_ref[i], combine, 0.0), axis=-1, keepdims=True)
            gate = jnp.dot(h, wg_ref[0], preferred_element_type=f32)
            up = jnp.dot(h, wu_ref[0], preferred_element_type=f32)
            hid = (jax.nn.silu(gate) * up * c).astype(bf16)
            y = y + jnp.dot(hid, wd_ref[0], preferred_element_type=f32)
        o_ref[...] = _rms(x + y, gfin_ref[...])

    @pl.when(i >= nu_ref[0])
    def _():
        o_ref[...] = jnp.zeros_like(o_ref)


def _moe(xs, tile_ea, tile_eb, n_used, g_ffn, w_gate, w_up, w_down, g_final, ts):
    R = xs.shape[0]
    E, D, F = w_gate.shape
    const = lambda i, ea, eb, nu: (0, 0)
    rows = lambda i, ea, eb, nu: (jnp.minimum(i, nu[0] - 1), 0)
    first = lambda i, ea, eb, nu: (ea[i], 0, 0)
    second = lambda i, ea, eb, nu: (eb[i], 0, 0)
    grid_spec = pltpu.PrefetchScalarGridSpec(
        num_scalar_prefetch=3,
        grid=(R // ts,),
        in_specs=[
            pl.BlockSpec((ts, D + ROUTER_COLS), rows),
            pl.BlockSpec((1, D), const),
            pl.BlockSpec((1, D, F), first),
            pl.BlockSpec((1, D, F), first),
            pl.BlockSpec((1, F, D), first),
            pl.BlockSpec((1, D, F), second),
            pl.BlockSpec((1, D, F), second),
            pl.BlockSpec((1, F, D), second),
            pl.BlockSpec((1, D), const),
        ],
        out_specs=pl.BlockSpec((ts, D), lambda i, ea, eb, nu: (i, 0)),
    )
    return pl.pallas_call(
        _moe_kernel,
        grid_spec=grid_spec,
        out_shape=jax.ShapeDtypeStruct((R, D), f32),
        compiler_params=_params("arbitrary"),
        name="moe",
    )(tile_ea, tile_eb, n_used, xs, g_ffn, w_gate, w_up, w_down, w_gate, w_up, w_down, g_final)


def kernel(x, attn_norm_g, w_in, b_forget, fox_out_norm_g, moba_out_norm_g, rel_bias, w_out, ffn_norm_g,
           w_group_router, b_group_router, w_expert_router, b_expert_router, w_gate, w_up, w_down,
           final_norm_g):
    B, S, D = x.shape
    T = B * S
    depth = w_in.shape[0]
    fox_w = N_FOX_HEADS * HEAD_DIM
    moba_w = N_MOBA_HEADS * HEAD_DIM
    qkv_w = 3 * (fox_w + moba_w)
    assert w_in.shape[2] == qkv_w + N_FOX_HEADS
    assert N_GROUPS + N_EXPERTS <= ROUTER_COLS

    x2 = x.reshape(T, D)
    out = None
    for l in range(depth):
        w_in_bf16 = w_in.astype(bf16)
        w_fg_t = w_in_bf16[l, :, qkv_w:].T
        assert fox_w == moba_w
        col_scale = np.ones((qkv_w // fox_w,), np.float32)
        col_scale[[0, 3]] = HEAD_DIM ** -0.5 * LOG2E
        qkv, f_t = _in_proj(x2, attn_norm_g[l].reshape(1, D), w_in_bf16, l, w_fg_t, jnp.asarray(col_scale),
                            fox_w)
        key_bias = _fox_decay(f_t, b_forget[l].reshape(N_FOX_HEADS, 1), B, S)
        fo = _fox_attn(qkv, key_bias, B, S, 0)
        mo = _moba_attn(qkv, rel_bias, B, S, 3 * N_FOX_HEADS)

        w_r = jnp.concatenate([w_group_router[l], w_expert_router[l].reshape(D, N_EXPERTS)], axis=1)
        w_r = jnp.pad(w_r, ((0, 0), (0, ROUTER_COLS - w_r.shape[1]))).astype(bf16)
        b_r = jnp.concatenate([b_group_router[l], b_expert_router[l].reshape(N_EXPERTS)])
        b_r = jnp.pad(b_r, (0, ROUTER_COLS - b_r.shape[0])).reshape(1, ROUTER_COLS)
        g_ffn = ffn_norm_g[l].reshape(1, D)
        x1, route, count = _out_proj(fo, mo, x2, fox_out_norm_g[l].reshape(1, fox_w),
                                     moba_out_norm_g[l].reshape(1, moba_w), w_out[l].astype(bf16),
                                     g_ffn, w_r, b_r)
        ts = MOE_TILE
        cls, rank = route[:, 0], route[:, 1]
        counts = count[0, :N_CLASSES]
        padded = (counts + ts - 1) // ts * ts
        ends = jnp.cumsum(padded)
        base = ends - padded
        rows_out = (T // ts + N_CLASSES) * ts
        fill = jnp.concatenate([jnp.stack([base + counts, ends], axis=1).reshape(-1),
                                jnp.stack([ends[-1], jnp.asarray(rows_out, ends.dtype)])])
        tile_start = jnp.arange(rows_out // ts, dtype=jnp.int32) * ts
        tile_cls = jnp.minimum(jnp.sum(tile_start[:, None] >= ends[None, :], axis=1), N_CLASSES - 1)
        pairs = [(a, b) for a in range(EXPERTS_PER_GROUP) for b in range(a + 1, EXPERTS_PER_GROUP)]
        first_of_cls = np.array([g * EXPERTS_PER_GROUP + a for g in range(N_GROUPS) for a, _ in pairs], np.int32)
        second_of_cls = np.array([g * EXPERTS_PER_GROUP + b for g in range(N_GROUPS) for _, b in pairs], np.int32)
        tile_ea = jnp.asarray(first_of_cls)[tile_cls]
        tile_eb = jnp.asarray(second_of_cls)[tile_cls]
        n_used = (ends[-1:] // ts).astype(jnp.int32)

        assert l == depth - 1, "the fused residual + final norm epilogue expects a single layer"
        xs = _dispatch(x1, cls, rank, base, fill, rows_out)
        ys = _moe(xs, tile_ea, tile_eb, n_used, g_ffn, w_gate[l].astype(bf16),
                  w_up[l].astype(bf16), w_down[l].astype(bf16), final_norm_g.reshape(1, D), ts)
        out = _collect(ys, cls, rank, base, T)
        x2 = out
    return out.reshape(B, S, D)
```

```python
import functools
import math

import numpy as np
import jax
import jax.numpy as jnp
from jax import lax
from jax.experimental import pallas as pl
from jax.experimental.pallas import tpu as pltpu

HEAD_DIM = 128
N_FOX_HEADS = 8
N_MOBA_HEADS = 8
MOBA_BLOCK = 256
MOBA_TOPK = 3
REL_BUCKETS = 32
REL_MAX_DIST = 128
N_GROUPS = 4
EXPERTS_PER_GROUP = 4
N_EXPERTS = N_GROUPS * EXPERTS_PER_GROUP
EXPERT_PAIRS = EXPERTS_PER_GROUP * (EXPERTS_PER_GROUP - 1) // 2
N_CLASSES = N_GROUPS * EXPERT_PAIRS
EPS = 1e-6
NEG = -1e30
LOG2E = math.log2(math.e)
LANES = 128
SUBLANES = 8
ROUTER_COLS = LANES
OUT_PROJ_CHUNKS = 4
MOE_TILE = 256
KEY_BIAS_PARTS = 3
ZERO_ROWS = 64
DMA_UNROLL = 8
VMEM_LIMIT = 56 * 1024 * 1024

f32 = jnp.float32
bf16 = jnp.bfloat16


def _params(*sem):
    return pltpu.CompilerParams(dimension_semantics=sem, vmem_limit_bytes=VMEM_LIMIT)


def _pick(n, pref):
    t = min(n, pref)
    assert n % t == 0, (n, pref)
    return t


def _in_proj_kernel(cs_ref, x_ref, g_ref, w_ref, wfg_ref, qkv_ref, f_ref, h_scr):
    @pl.when(pl.program_id(1) == 0)
    def _():
        x = x_ref[...]
        y = x * lax.rsqrt(jnp.mean(x * x, axis=-1, keepdims=True) + EPS)
        hb = (y * g_ref[...]).astype(bf16)
        h_scr[...] = hb
        f_ref[...] = lax.dot_general(wfg_ref[...], hb, (((1,), (1,)), ((), ())),
                                     preferred_element_type=f32)

    acc = lax.dot_general(h_scr[...], w_ref[...], (((1,), (1,)), ((), ())),
                          preferred_element_type=f32) * cs_ref[pl.program_id(1)]
    for c in range(acc.shape[1] // HEAD_DIM):
        qkv_ref[c] = acc[:, c * HEAD_DIM:(c + 1) * HEAD_DIM].astype(bf16)


def _in_proj(x2, g, w_in_t, layer, w_fg_t, col_scale, tn):
    T, D = x2.shape
    N = col_scale.shape[0] * tn
    tm = _pick(T, 1024)
    assert N <= w_in_t.shape[1]
    nh = w_fg_t.shape[0]
    return pl.pallas_call(
        _in_proj_kernel,
        grid=(T // tm, N // tn),
        in_specs=[
            pl.BlockSpec(memory_space=pltpu.SMEM),
            pl.BlockSpec((tm, D), lambda i, j: (i, 0)),
            pl.BlockSpec((1, D), lambda i, j: (0, 0)),
            pl.BlockSpec((None, tn, D), lambda i, j: (layer, j, 0)),
            pl.BlockSpec((nh, D), lambda i, j: (0, 0)),
        ],
        out_specs=[
            pl.BlockSpec((tn // HEAD_DIM, tm, HEAD_DIM), lambda i, j: (j, i, 0)),
            pl.BlockSpec((nh, tm), lambda i, j: (0, i)),
        ],
        out_shape=[
            jax.ShapeDtypeStruct((N // HEAD_DIM, T, HEAD_DIM), bf16),
            jax.ShapeDtypeStruct((nh, T), f32),
        ],
        scratch_shapes=[pltpu.VMEM((tm, D), bf16)],
        compiler_params=_params("arbitrary", "arbitrary"),
        name="in_proj",
    )(col_scale, x2, g, w_in_t, w_fg_t)


def _fox_decay_kernel(f_ref, b_ref, o_ref):
    lf = jax.nn.log_sigmoid(f_ref[...] + b_ref[...])
    S = lf.shape[1]
    lane = lax.broadcasted_iota(jnp.int32, lf.shape, 1)
    c = lf
    sh = 1
    while sh < S:
        c = c + jnp.where(lane >= sh, pltpu.roll(c, sh, axis=1), 0.0)
        sh *= 2
    rest = -c * LOG2E
    parts = []
    for _ in range(KEY_BIAS_PARTS):
        part = rest.astype(bf16).astype(f32)
        parts.append(part)
        rest = rest - part
    nh = lf.shape[0]
    cols = jnp.concatenate(parts + [jnp.zeros((LANES - KEY_BIAS_PARTS * nh, S), f32)], axis=0)
    o_ref[0] = jnp.transpose(cols).astype(bf16)


def _fox_decay(f_t, b_forget, B, S):
    nh = f_t.shape[0]
    assert KEY_BIAS_PARTS * nh <= LANES
    return pl.pallas_call(
        _fox_decay_kernel,
        grid=(B,),
        in_specs=[
            pl.BlockSpec((nh, S), lambda b: (0, b)),
            pl.BlockSpec((nh, 1), lambda b: (0, 0)),
        ],
        out_specs=pl.BlockSpec((1, S, LANES), lambda b: (b, 0, 0)),
        out_shape=jax.ShapeDtypeStruct((B, S, LANES), bf16),
        compiler_params=_params("arbitrary"),
        name="fox_decay",
    )(f_t, b_forget)


def _qk(q, k):
    return lax.dot_general(q, k, (((1,), (1,)), ((), ())), preferred_element_type=f32)


def _softmax_init(t):
    return jnp.full((t, 1), NEG, f32), jnp.zeros((t, 2 * HEAD_DIM), f32)


def _softmax_step2(carry, s2, v_ones):
    m, acc = carry
    m_new = jnp.maximum(m, jnp.max(s2, axis=-1, keepdims=True))
    p = jnp.exp2(s2 - m_new).astype(bf16)
    acc = jnp.exp2(m - m_new) * acc + jnp.dot(p, v_ones, preferred_element_type=f32)
    return m_new, acc


def _softmax_result(carry):
    _, acc = carry
    return acc[:, :HEAD_DIM] / acc[:, HEAD_DIM:]


def _with_ones(vo_scr, v_ref):
    vo_scr[:, :HEAD_DIM] = v_ref[0]
    vo_scr[:, HEAD_DIM:] = jnp.ones((vo_scr.shape[0], HEAD_DIM), vo_scr.dtype)


def _fox_attn_kernel(q_ref, k_ref, v_ref, kb_ref, o_ref, qaug_scr, kaug_scr, vo_scr, *, t):
    h = pl.program_id(1)
    S = q_ref.shape[1]
    lane = lax.broadcasted_iota(jnp.int32, (S, LANES), 1)
    mine = jnp.logical_and(lane % N_FOX_HEADS == h, lane < KEY_BIAS_PARTS * N_FOX_HEADS)
    qaug_scr[:, :HEAD_DIM] = q_ref[0]
    qaug_scr[:, HEAD_DIM:] = jnp.where(mine, 1.0, 0.0).astype(bf16)
    kaug_scr[:, :HEAD_DIM] = k_ref[0]
    kaug_scr[:, HEAD_DIM:] = kb_ref[0]
    _with_ones(vo_scr, v_ref)
    row = lax.broadcasted_iota(jnp.int32, (t, t), 0)
    col = lax.broadcasted_iota(jnp.int32, (t, t), 1)
    for qi in range(S // t):
        qa = qaug_scr[qi * t:(qi + 1) * t, :]
        carry = _softmax_init(t)
        for kb in range(qi + 1):
            keys = slice(kb * t, (kb + 1) * t)
            s2 = _qk(qa, kaug_scr[keys, :])
            if kb == qi:
                s2 = jnp.where(col <= row, s2, NEG)
            carry = _softmax_step2(carry, s2, vo_scr[keys, :])
        o_ref[qi * t:(qi + 1) * t, :] = _softmax_result(carry).astype(o_ref.dtype)


def _fox_attn(qkv, key_bias, B, S, slab0):
    T = B * S
    H = N_FOX_HEADS
    t = _pick(S, 512)
    kern = functools.partial(_fox_attn_kernel, t=t)
    return pl.pallas_call(
        kern,
        grid=(B, H),
        in_specs=[
            pl.BlockSpec((1, S, HEAD_DIM), lambda b, h: (slab0 + h, b, 0)),
            pl.BlockSpec((1, S, HEAD_DIM), lambda b, h: (slab0 + H + h, b, 0)),
            pl.BlockSpec((1, S, HEAD_DIM), lambda b, h: (slab0 + 2 * H + h, b, 0)),
            pl.BlockSpec((1, S, LANES), lambda b, h: (b, 0, 0)),
        ],
        out_specs=pl.BlockSpec((S, HEAD_DIM), lambda b, h: (b, h)),
        out_shape=jax.ShapeDtypeStruct((T, H * HEAD_DIM), bf16),
        scratch_shapes=[pltpu.VMEM((S, 2 * HEAD_DIM), bf16)] * 3,
        compiler_params=_params("arbitrary", "arbitrary"),
        name="fox_attn",
    )(qkv, qkv, qkv, key_bias)


def _rel_bucket_table(n):
    max_exact = REL_BUCKETS // 2
    d = np.arange(n)
    ratio = np.log(np.maximum(d, 1).astype(np.float32) / np.float32(max_exact)) / np.float32(
        math.log(REL_MAX_DIST / max_exact))
    large = max_exact + (ratio * np.float32(REL_BUCKETS - max_exact)).astype(np.int32)
    large = np.minimum(large, REL_BUCKETS - 1)
    return np.where(d < max_exact, d, large).astype(np.int32)


def _moba_attn_kernel(rel_ref, q_ref, k_ref, v_ref, avg_ref, hot_ref, bk_ref, o_ref, bias_scr, kaug_scr,
                      qaug_scr, vo_scr):
    h = pl.program_id(1)
    blk = MOBA_BLOCK
    t = 2 * blk
    S = q_ref.shape[1]
    far_bias = rel_ref[h, REL_BUCKETS - 1]
    row = lax.broadcasted_iota(jnp.int32, (blk, blk), 0)
    col = lax.broadcasted_iota(jnp.int32, (blk, blk), 1)
    tiles = []
    for which in range(2):
        bk = bk_ref[which]
        tile = jnp.zeros((blk, blk), f32)
        for b in range(REL_BUCKETS - 1):
            tile = jnp.where(bk == b, (rel_ref[h, b] - far_bias) * LOG2E, tile)
        tiles.append(jnp.where(col <= row, tile, NEG) if which == 0 else tile)
    own_tile, prev_tile = tiles
    zero_tile = jnp.zeros((blk, blk), f32)
    for which, quads in enumerate((((zero_tile, prev_tile), (zero_tile, zero_tile)),
                                   ((own_tile, zero_tile), (prev_tile, own_tile)))):
        for r in range(2):
            for c in range(2):
                bias_scr[which, r * blk:(r + 1) * blk, c * blk:(c + 1) * blk] = quads[r][c]
    kaug_scr[:, :HEAD_DIM] = k_ref[0]
    kaug_scr[:, HEAD_DIM:] = hot_ref[...]
    _with_ones(vo_scr, v_ref)
    kmean = jnp.dot(avg_ref[...], k_ref[0], preferred_element_type=f32).astype(bf16)

    nbp = -(-(S // blk) // SUBLANES) * SUBLANES
    block = lax.broadcasted_iota(jnp.int32, (nbp, S), 0)
    own = lax.broadcasted_iota(jnp.int32, (nbp, S), 1) // blk
    valid = block < own
    gate = jnp.where(valid, _qk(kmean[:nbp], q_ref[0]), NEG)
    sel = block == own
    for _ in range(MOBA_TOPK):
        mx = jnp.max(gate, axis=0, keepdims=True)
        first = jnp.min(jnp.where(gate == mx, block, nbp), axis=0, keepdims=True)
        pick = block == first
        sel = jnp.logical_or(sel, jnp.logical_and(pick, valid))
        gate = jnp.where(pick, -jnp.inf, gate)
    chosen = jnp.concatenate([jnp.where(sel, 0.0, NEG), jnp.full((LANES - nbp, S), NEG, f32)], axis=0)
    qaug_scr[:, :HEAD_DIM] = q_ref[0]
    qaug_scr[:, HEAD_DIM:] = jnp.transpose(chosen).astype(bf16)

    for j in range(S // t):
        rows = slice(j * t, (j + 1) * t)
        qa = qaug_scr[rows, :]
        carry = _softmax_step2(_softmax_init(t), _qk(qa, kaug_scr[rows, :]) + bias_scr[1], vo_scr[rows, :])
        if j > 0:
            keys = slice((j - 1) * t, j * t)
            carry = _softmax_step2(carry, _qk(qa, kaug_scr[keys, :]) + bias_scr[0], vo_scr[keys, :])
        for c in range(j - 1):
            keys = slice(c * t, (c + 1) * t)
            carry = _softmax_step2(carry, _qk(qa, kaug_scr[keys, :]), vo_scr[keys, :])
        o_ref[rows, :] = _softmax_result(carry).astype(o_ref.dtype)


def _moba_attn(qkv, rel_bias, B, S, slab0):
    T = B * S
    H = N_MOBA_HEADS
    blk = MOBA_BLOCK
    assert S % (2 * blk) == 0 and S // blk <= LANES
    nb = S // blk
    avg = np.zeros((LANES, S), np.float32)
    for n in range(nb):
        avg[n, n * blk:(n + 1) * blk] = 1.0 / blk
    hot = (avg.T > 0).astype(np.float32)
    table = _rel_bucket_table(2 * blk)
    dist = np.arange(blk)[:, None] - np.arange(blk)[None, :]
    buckets = np.stack([table[np.maximum(dist, 0)], table[dist + blk]]).astype(np.int32)
    grid_spec = pltpu.PrefetchScalarGridSpec(
        num_scalar_prefetch=1,
        grid=(B, H),
        in_specs=[
            pl.BlockSpec((1, S, HEAD_DIM), lambda b, h, r: (slab0 + h, b, 0)),
            pl.BlockSpec((1, S, HEAD_DIM), lambda b, h, r: (slab0 + H + h, b, 0)),
            pl.BlockSpec((1, S, HEAD_DIM), lambda b, h, r: (slab0 + 2 * H + h, b, 0)),
            pl.BlockSpec((LANES, S), lambda b, h, r: (0, 0)),
            pl.BlockSpec((S, LANES), lambda b, h, r: (0, 0)),
            pl.BlockSpec((2, blk, blk), lambda b, h, r: (0, 0, 0)),
        ],
        out_specs=pl.BlockSpec((S, HEAD_DIM), lambda b, h, r: (b, h)),
        scratch_shapes=[pltpu.VMEM((2, 2 * blk, 2 * blk), f32)] + [pltpu.VMEM((S, 2 * HEAD_DIM), bf16)] * 3,
    )
    return pl.pallas_call(
        _moba_attn_kernel,
        grid_spec=grid_spec,
        out_shape=jax.ShapeDtypeStruct((T, H * HEAD_DIM), bf16),
        compiler_params=_params("arbitrary", "arbitrary"),
        name="moba_attn",
    )(rel_bias, qkv, qkv, qkv, jnp.asarray(avg, bf16), jnp.asarray(hot, bf16), jnp.asarray(buckets))


def _rms(x, g):
    return x * lax.rsqrt(jnp.mean(x * x, axis=-1, keepdims=True) + EPS) * g


def _first_max(x, lane):
    mx = jnp.max(x, axis=-1, keepdims=True)
    return mx, jnp.min(jnp.where(x == mx, lane, LANES), axis=-1, keepdims=True)


def _route(logits, lane, grp=None):
    gl = jnp.where(lane < N_GROUPS, logits, -jnp.inf)
    gmax, top = _first_max(gl, lane)
    grp = top if grp is None else grp
    pg = jnp.exp(gl - gmax)
    p_top_group = jnp.sum(jnp.where(lane == grp, pg, 0.0), axis=-1, keepdims=True) / jnp.sum(
        pg, axis=-1, keepdims=True)
    lo = N_GROUPS + grp * EXPERTS_PER_GROUP
    in_grp = jnp.logical_and(lane >= lo, lane < lo + EXPERTS_PER_GROUP)
    el = jnp.where(in_grp, logits, -jnp.inf)
    pe = jnp.exp(el - jnp.max(el, axis=-1, keepdims=True))
    p_exp = jnp.where(in_grp, pe / jnp.sum(pe, axis=-1, keepdims=True), -1.0)
    p1, i1 = _first_max(p_exp, lane)
    p2, i2 = _first_max(jnp.where(lane == i1, -1.0, p_exp), lane)
    tot = p1 + p2
    combine = jnp.where(lane == i1, p1 / tot, jnp.where(lane == i2, p2 / tot, 0.0)) * p_top_group
    return grp, jnp.minimum(i1, i2) - lo, jnp.maximum(i1, i2) - lo, combine


def _out_proj_kernel(fo_ref, mo_ref, x_ref, gf_ref, gm_ref, w_ref, gffn_ref, wr_ref, br_ref,
                     x1_ref, route_ref, count_ref, count_scr, x1_scr):
    i = pl.program_id(0)

    @pl.when(i == 0)
    def _():
        count_scr[...] = jnp.zeros_like(count_scr)
        x1_scr[1] = jnp.zeros(x1_scr.shape[1:], x1_scr.dtype)

    tm, D = x_ref.shape
    chunk = D // OUT_PROJ_CHUNKS

    fo = _rms(fo_ref[...].astype(f32), gf_ref[...]).astype(bf16)
    mo = _rms(mo_ref[...].astype(f32), gm_ref[...]).astype(bf16)
    mix = jnp.concatenate([fo, mo], axis=-1)

    def project(c):
        cols = slice(c * chunk, (c + 1) * chunk)
        x1_scr[i % 2, :, cols] = x_ref[:, cols] + jnp.dot(mix, w_ref[:, cols], preferred_element_type=f32)

    x1 = x1_scr[(i + 1) % 2]
    h2 = _rms(x1, gffn_ref[...]).astype(bf16)
    logits = jnp.dot(h2, wr_ref[...], preferred_element_type=f32) + br_ref[...]
    project(0)
    lane = lax.broadcasted_iota(jnp.int32, logits.shape, 1)
    grp, ea, eb, combine = _route(logits, lane)
    x1_ref[:, :D] = x1
    x1_ref[:, D:] = combine
    project(1)
    pair = (ea * (2 * EXPERTS_PER_GROUP - 1 - ea)) // 2 + eb - ea - 1
    cls = grp * EXPERT_PAIRS + pair
    onehot = jnp.where(lane == cls, 1.0, 0.0)
    earlier = lax.broadcasted_iota(jnp.int32, (tm, tm), 1) < lax.broadcasted_iota(jnp.int32, (tm, tm), 0)
    counts = count_scr[...]
    before = jnp.dot(jnp.where(earlier, 1.0, 0.0).astype(bf16), onehot.astype(bf16),
                     preferred_element_type=f32) + counts
    rank = jnp.sum(onehot * before, axis=-1, keepdims=True).astype(jnp.int32)
    route_ref[...] = jnp.where(lane == 0, cls, jnp.where(lane == 1, rank, 0))
    counts = counts + jnp.where(i > 0, jnp.sum(onehot, axis=0, keepdims=True), 0.0)
    count_scr[...] = counts
    count_ref[...] = counts.astype(jnp.int32)
    for c in range(2, OUT_PROJ_CHUNKS):
        project(c)


def _out_proj(fo, mo, x2, gf, gm, w_out, g_ffn, w_r, b_r):
    T, D = x2.shape
    Wf = fo.shape[1]
    Wm = mo.shape[1]
    tm = _pick(T, 512)
    n = T // tm
    const = lambda i: (0, 0)
    rows_in = lambda i: (jnp.minimum(i, n - 1), 0)
    rows = lambda i: (jnp.maximum(i - 1, 0), 0)
    return pl.pallas_call(
        _out_proj_kernel,
        grid=(n + 1,),
        in_specs=[
            pl.BlockSpec((tm, Wf), rows_in),
            pl.BlockSpec((tm, Wm), rows_in),
            pl.BlockSpec((tm, D), rows_in),
            pl.BlockSpec((1, Wf), const),
            pl.BlockSpec((1, Wm), const),
            pl.BlockSpec((Wf + Wm, D), const),
            pl.BlockSpec((1, D), const),
            pl.BlockSpec((D, ROUTER_COLS), const),
            pl.BlockSpec((1, ROUTER_COLS), const),
        ],
        out_specs=[
            pl.BlockSpec((tm, D + ROUTER_COLS), rows),
            pl.BlockSpec((tm, ROUTER_COLS), rows),
            pl.BlockSpec((1, ROUTER_COLS), const),
        ],
        out_shape=[
            jax.ShapeDtypeStruct((T, D + ROUTER_COLS), f32),
            jax.ShapeDtypeStruct((T, ROUTER_COLS), jnp.int32),
            jax.ShapeDtypeStruct((1, ROUTER_COLS), jnp.int32),
        ],
        scratch_shapes=[pltpu.VMEM((1, ROUTER_COLS), f32), pltpu.VMEM((2, tm, D), f32)],
        compiler_params=_params("arbitrary"),
        name="out_proj",
    )(fo, mo, x2, gf, gm, w_out, g_ffn, w_r, b_r)


def _start_rows(row_copy, n):
    assert n % DMA_UNROLL == 0

    def body(g, carry):
        for k in range(DMA_UNROLL):
            row_copy(g * DMA_UNROLL + k).start()
        return carry

    lax.fori_loop(0, n // DMA_UNROLL, body, 0)


def _slot(cls_ref, rank_ref, base_ref, t):
    return base_ref[cls_ref[t]] + rank_ref[t]


def _dispatch_kernel(cls_ref, rank_ref, base_ref, fill_ref, x_ref, xs_ref, zero_scr, sem, zsem):
    i = pl.program_id(0)
    tm = x_ref.shape[0]

    zrows = zero_scr.shape[0]

    def row_copy(r):
        d = _slot(cls_ref, rank_ref, base_ref, i * tm + r)
        return pltpu.make_async_copy(x_ref.at[pl.ds(r, 1)], xs_ref.at[pl.ds(d, 1)], sem)

    def zero_row(r):
        return pltpu.make_async_copy(zero_scr.at[pl.ds(0, 1)], xs_ref.at[pl.ds(r, 1)], zsem)

    def zero_sublanes(c):
        dst = xs_ref.at[pl.ds(pl.multiple_of(c * SUBLANES, SUBLANES), SUBLANES)]
        return pltpu.make_async_copy(zero_scr.at[pl.ds(0, SUBLANES)], dst, zsem)

    def zero_chunk(c):
        return pltpu.make_async_copy(zero_scr, xs_ref.at[pl.ds(pl.multiple_of(c * zrows, zrows), zrows)], zsem)

    def start(copy):
        def body(r, carry):
            copy(r).start()
            return carry
        return body

    def wait(copy):
        def body(r, carry):
            copy(r).wait()
            return carry
        return body

    _start_rows(row_copy, tm)

    def zero_fill(act):
        def fill_class(c, carry):
            lo, hi = fill_ref[2 * c], fill_ref[2 * c + 1]
            mid = jnp.minimum((lo + SUBLANES - 1) // SUBLANES * SUBLANES, hi)
            lax.fori_loop(lo, mid, act(zero_row), 0)
            lax.fori_loop(mid // SUBLANES, hi // SUBLANES, act(zero_sublanes), 0)
            return carry

        lax.fori_loop(0, N_CLASSES, fill_class, 0)
        lo, hi = fill_ref[2 * N_CLASSES] // zrows, fill_ref[2 * N_CLASSES + 1] // zrows
        lax.fori_loop(lo, hi, act(zero_chunk), 0)

    @pl.when(i == 0)
    def _():
        zero_scr[...] = jnp.zeros_like(zero_scr)
        zero_fill(start)

    pltpu.make_async_copy(x_ref, xs_ref.at[pl.ds(0, tm)], sem).wait()

    @pl.when(i == 0)
    def _():
        zero_fill(wait)


def _dispatch(x1, cls, rank, base, fill, rows_out):
    T, D = x1.shape
    tm = _pick(T, 512)
    grid_spec = pltpu.PrefetchScalarGridSpec(
        num_scalar_prefetch=4,
        grid=(T // tm,),
        in_specs=[pl.BlockSpec((tm, D), lambda i, *_: (i, 0))],
        out_specs=pl.BlockSpec(memory_space=pl.ANY),
        scratch_shapes=[pltpu.VMEM((ZERO_ROWS, D), f32), pltpu.SemaphoreType.DMA, pltpu.SemaphoreType.DMA],
    )
    return pl.pallas_call(
        _dispatch_kernel,
        grid_spec=grid_spec,
        out_shape=jax.ShapeDtypeStruct((rows_out, D), f32),
        compiler_params=_params("arbitrary"),
        name="moe_dispatch",
    )(cls, rank, base, fill, x1)


def _collect_kernel(cls_ref, rank_ref, base_ref, ys_ref, o_ref, sem):
    i = pl.program_id(0)
    tm = o_ref.shape[0]

    def row_copy(r):
        d = _slot(cls_ref, rank_ref, base_ref, i * tm + r)
        return pltpu.make_async_copy(ys_ref.at[pl.ds(d, 1)], o_ref.at[pl.ds(r, 1)], sem)

    _start_rows(row_copy, tm)
    pltpu.make_async_copy(ys_ref.at[pl.ds(0, tm)], o_ref, sem).wait()


def _collect(ys, cls, rank, base, T):
    D = ys.shape[1]
    tm = _pick(T, 512)
    grid_spec = pltpu.PrefetchScalarGridSpec(
        num_scalar_prefetch=3,
        grid=(T // tm,),
        in_specs=[pl.BlockSpec(memory_space=pl.ANY)],
        out_specs=pl.BlockSpec((tm, D), lambda i, *_: (i, 0)),
        scratch_shapes=[pltpu.SemaphoreType.DMA],
    )
    return pl.pallas_call(
        _collect_kernel,
        grid_spec=grid_spec,
        out_shape=jax.ShapeDtypeStruct((T, D), f32),
        compiler_params=_params("arbitrary"),
        name="moe_collect",
    )(cls, rank, base, ys)


def _moe_kernel(ea_ref, eb_ref, nu_ref, xs_ref, gffn_ref, wga_ref, wua_ref, wda_ref,
                wgb_ref, wub_ref, wdb_ref, gfin_ref, o_ref):
    i = pl.program_id(0)
    D = o_ref.shape[1]

    @pl.when(i < nu_ref[0])
    def _():
        x = xs_ref[:, :D]
        combine = xs_ref[:, D:]
        h = _rms(x, gffn_ref[...]).astype(bf16)
        lane = lax.broadcasted_iota(jnp.int32, combine.shape, 1)
        y = jnp.zeros(x.shape, f32)
        for e_ref, wg_ref, wu_ref, wd_ref in ((ea_ref, wga_ref, wua_ref, wda_ref),
                                               (eb_ref, wgb_ref, wub_ref, wdb_ref)):
            c = jnp.sum(jnp.where(lane == N_GROUPS + e_ref[i], combine, 0.0), axis=-1, keepdims=True)
            gate = jnp.dot(h, wg_ref[0], preferred_element_type=f32)
            up = jnp.dot(h, wu_ref[0], preferred_element_type=f32)
            hid = (jax.nn.silu(gate) * up * c).astype(bf16)
            y = y + jnp.dot(hid, wd_ref[0], preferred_element_type=f32)
        o_ref[...] = _rms(x + y, gfin_ref[...])

    @pl.when(i >= nu_ref[0])
    def _():
        o_ref[...] = jnp.zeros_like(o_ref)


def _moe(xs, tile_ea, tile_eb, n_used, g_ffn, w_gate, w_up, w_down, g_final, ts):
    R = xs.shape[0]
    E, D, F = w_gate.shape
    const = lambda i, ea, eb, nu: (0, 0)
    rows = lambda i, ea, eb, nu: (jnp.minimum(i, nu[0] - 1), 0)
    first = lambda i, ea, eb, nu: (ea[i], 0, 0)
    second = lambda i, ea, eb, nu: (eb[i], 0, 0)
    grid_spec = pltpu.PrefetchScalarGridSpec(
        num_scalar_prefetch=3,
        grid=(R // ts,),
        in_specs=[
            pl.BlockSpec((ts, D + ROUTER_COLS), rows),
            pl.BlockSpec((1, D), const),
            pl.BlockSpec((1, D, F), first),
            pl.BlockSpec((1, D, F), first),
            pl.BlockSpec((1, F, D), first),
            pl.BlockSpec((1, D, F), second),
            pl.BlockSpec((1, D, F), second),
            pl.BlockSpec((1, F, D), second),
            pl.BlockSpec((1, D), const),
        ],
        out_specs=pl.BlockSpec((ts, D), lambda i, ea, eb, nu: (i, 0)),
    )
    return pl.pallas_call(
        _moe_kernel,
        grid_spec=grid_spec,
        out_shape=jax.ShapeDtypeStruct((R, D), f32),
        compiler_params=_params("arbitrary"),
        name="moe",
    )(tile_ea, tile_eb, n_used, xs, g_ffn, w_gate, w_up, w_down, w_gate, w_up, w_down, g_final)


def kernel(x, attn_norm_g, w_in, b_forget, fox_out_norm_g, moba_out_norm_g, rel_bias, w_out, ffn_norm_g,
           w_group_router, b_group_router, w_expert_router, b_expert_router, w_gate, w_up, w_down,
           final_norm_g):
    B, S, D = x.shape
    T = B * S
    depth = w_in.shape[0]
    fox_w = N_FOX_HEADS * HEAD_DIM
    moba_w = N_MOBA_HEADS * HEAD_DIM
    qkv_w = 3 * (fox_w + moba_w)
    assert w_in.shape[2] == qkv_w + N_FOX_HEADS
    assert N_GROUPS + N_EXPERTS <= ROUTER_COLS

    x2 = x.reshape(T, D)
    out = None
    for l in range(depth):
        w_in_t = jnp.swapaxes(w_in, 1, 2).astype(bf16)
        w_fg_t = w_in_t[l, qkv_w:, :]
        assert fox_w == moba_w
        col_scale = np.ones((qkv_w // fox_w,), np.float32)
        col_scale[[0, 3]] = HEAD_DIM ** -0.5 * LOG2E
        qkv, f_t = _in_proj(x2, attn_norm_g[l].reshape(1, D), w_in_t, l, w_fg_t, jnp.asarray(col_scale),
                            fox_w)
        key_bias = _fox_decay(f_t, b_forget[l].reshape(N_FOX_HEADS, 1), B, S)
        fo = _fox_attn(qkv, key_bias, B, S, 0)
        mo = _moba_attn(qkv, rel_bias, B, S, 3 * N_FOX_HEADS)

        w_r = jnp.concatenate([w_group_router[l], w_expert_router[l].reshape(D, N_EXPERTS)], axis=1)
        w_r = jnp.pad(w_r, ((0, 0), (0, ROUTER_COLS - w_r.shape[1]))).astype(bf16)
        b_r = jnp.concatenate([b_group_router[l], b_expert_router[l].reshape(N_EXPERTS)])
        b_r = jnp.pad(b_r, (0, ROUTER_COLS - b_r.shape[0])).reshape(1, ROUTER_COLS)
        g_ffn = ffn_norm_g[l].reshape(1, D)
        x1, route, count = _out_proj(fo, mo, x2, fox_out_norm_g[l].reshape(1, fox_w),
                                     moba_out_norm_g[l].reshape(1, moba_w), w_out[l].astype(bf16),
                                     g_ffn, w_r, b_r)
        ts = MOE_TILE
        cls, rank = route[:, 0], route[:, 1]
        counts = count[0, :N_CLASSES]
        padded = (counts + ts - 1) // ts * ts
        ends = jnp.cumsum(padded)
        base = ends - padded
        rows_out = (T // ts + N_CLASSES) * ts
        fill = jnp.concatenate([jnp.stack([base + counts, ends], axis=1).reshape(-1),
                                jnp.stack([ends[-1], jnp.asarray(rows_out, ends.dtype)])])
        tile_start = jnp.arange(rows_out // ts, dtype=jnp.int32) * ts
        tile_cls = jnp.minimum(jnp.sum(tile_start[:, None] >= ends[None, :], axis=1), N_CLASSES - 1)
        pairs = [(a, b) for a in range(EXPERTS_PER_GROUP) for b in range(a + 1, EXPERTS_PER_GROUP)]
        first_of_cls = np.array([g * EXPERTS_PER_GROUP + a for g in range(N_GROUPS) for a, _ in pairs], np.int32)
        second_of_cls = np.array([g * EXPERTS_PER_GROUP + b for g in range(N_GROUPS) for _, b in pairs], np.int32)
        tile_ea = jnp.asarray(first_of_cls)[tile_cls]
        tile_eb = jnp.asarray(second_of_cls)[tile_cls]
        n_used = (ends[-1:] // ts).astype(jnp.int32)

        assert l == depth - 1, "the fused residual + final norm epilogue expects a single layer"
        xs = _dispatch(x1, cls, rank, base, fill, rows_out)
        ys = _moe(xs, tile_ea, tile_eb, n_used, g_ffn, w_gate[l].astype(bf16),
                  w_up[l].astype(bf16), w_down[l].astype(bf16), final_norm_g.reshape(1, D), ts)
        out = _collect(ys, cls, rank, base, T)
        x2 = out
    return out.reshape(B, S, D)
```

```python
import functools
import math

import numpy as np
import jax
import jax.numpy as jnp
from jax import lax
from jax.experimental import pallas as pl
from jax.experimental.pallas import tpu as pltpu

HEAD_DIM = 128
N_FOX_HEADS = 8
N_MOBA_HEADS = 8
MOBA_BLOCK = 256
MOBA_TOPK = 3
REL_BUCKETS = 32
REL_MAX_DIST = 128
N_GROUPS = 4
EXPERTS_PER_GROUP = 4
N_EXPERTS = N_GROUPS * EXPERTS_PER_GROUP
EXPERT_PAIRS = EXPERTS_PER_GROUP * (EXPERTS_PER_GROUP - 1) // 2
N_CLASSES = N_GROUPS * EXPERT_PAIRS
EPS = 1e-6
NEG = -1e30
LOG2E = math.log2(math.e)
LANES = 128
SUBLANES = 8
ROUTER_COLS = LANES
OUT_PROJ_CHUNKS = 4
MOE_TILE = 256
KEY_BIAS_PARTS = 3
ZERO_ROWS = 64
DMA_UNROLL = 8
VMEM_LIMIT = 56 * 1024 * 1024

f32 = jnp.float32
bf16 = jnp.bfloat16


def _params(*sem):
    return pltpu.CompilerParams(dimension_semantics=sem, vmem_limit_bytes=VMEM_LIMIT)


def _pick(n, pref):
    t = min(n, pref)
    assert n % t == 0, (n, pref)
    return t


def _in_proj_kernel(cs_ref, x_ref, g_ref, w_ref, wfg_ref, qkv_ref, f_ref, h_scr):
    @pl.when(pl.program_id(1) == 0)
    def _():
        x = x_ref[...]
        y = x * lax.rsqrt(jnp.mean(x * x, axis=-1, keepdims=True) + EPS)
        hb = (y * g_ref[...]).astype(bf16)
        h_scr[...] = hb
        f_ref[...] = lax.dot_general(wfg_ref[...], hb, (((1,), (1,)), ((), ())),
                                     preferred_element_type=f32)

    acc = lax.dot_general(h_scr[...], w_ref[...], (((1,), (1,)), ((), ())),
                          preferred_element_type=f32) * cs_ref[pl.program_id(1)]
    for c in range(acc.shape[1] // HEAD_DIM):
        qkv_ref[c] = acc[:, c * HEAD_DIM:(c + 1) * HEAD_DIM].astype(bf16)


def _in_proj(x2, g, w_in_t, layer, w_fg_t, col_scale, tn):
    T, D = x2.shape
    N = col_scale.shape[0] * tn
    tm = _pick(T, 1024)
    assert N <= w_in_t.shape[1]
    nh = w_fg_t.shape[0]
    return pl.pallas_call(
        _in_proj_kernel,
        grid=(T // tm, N // tn),
        in_specs=[
            pl.BlockSpec(memory_space=pltpu.SMEM),
            pl.BlockSpec((tm, D), lambda i, j: (i, 0)),
            pl.BlockSpec((1, D), lambda i, j: (0, 0)),
            pl.BlockSpec((None, tn, D), lambda i, j: (layer, j, 0)),
            pl.BlockSpec((nh, D), lambda i, j: (0, 0)),
        ],
        out_specs=[
            pl.BlockSpec((tn // HEAD_DIM, tm, HEAD_DIM), lambda i, j: (j, i, 0)),
            pl.BlockSpec((nh, tm), lambda i, j: (0, i)),
        ],
        out_shape=[
            jax.ShapeDtypeStruct((N // HEAD_DIM, T, HEAD_DIM), bf16),
            jax.ShapeDtypeStruct((nh, T), f32),
        ],
        scratch_shapes=[pltpu.VMEM((tm, D), bf16)],
        compiler_params=_params("arbitrary", "arbitrary"),
        name="in_proj",
    )(col_scale, x2, g, w_in_t, w_fg_t)


def _fox_decay_kernel(f_ref, b_ref, o_ref):
    lf = jax.nn.log_sigmoid(f_ref[...] + b_ref[...])
    S = lf.shape[1]
    lane = lax.broadcasted_iota(jnp.int32, lf.shape, 1)
    c = lf
    sh = 1
    while sh < S:
        c = c + jnp.where(lane >= sh, pltpu.roll(c, sh, axis=1), 0.0)
        sh *= 2
    rest = -c * LOG2E
    parts = []
    for _ in range(KEY_BIAS_PARTS):
        part = rest.astype(bf16).astype(f32)
        parts.append(part)
        rest = rest - part
    nh = lf.shape[0]
    cols = jnp.concatenate(parts + [jnp.zeros((LANES - KEY_BIAS_PARTS * nh, S), f32)], axis=0)
    o_ref[0] = jnp.transpose(cols).astype(bf16)


def _fox_decay(f_t, b_forget, B, S):
    nh = f_t.shape[0]
    assert KEY_BIAS_PARTS * nh <= LANES
    return pl.pallas_call(
        _fox_decay_kernel,
        grid=(B,),
        in_specs=[
            pl.BlockSpec((nh, S), lambda b: (0, b)),
            pl.BlockSpec((nh, 1), lambda b: (0, 0)),
        ],
        out_specs=pl.BlockSpec((1, S, LANES), lambda b: (b, 0, 0)),
        out_shape=jax.ShapeDtypeStruct((B, S, LANES), bf16),
        compiler_params=_params("arbitrary"),
        name="fox_decay",
    )(f_t, b_forget)


def _qk(q, k):
    return lax.dot_general(q, k, (((1,), (1,)), ((), ())), preferred_element_type=f32)


def _softmax_init(t):
    return jnp.full((t, 1), NEG, f32), jnp.zeros((t, 2 * HEAD_DIM), f32)


def _softmax_step2(carry, s2, v_ones):
    m, acc = carry
    m_new = jnp.maximum(m, jnp.max(s2, axis=-1, keepdims=True))
    p = jnp.exp2(s2 - m_new).astype(bf16)
    acc = jnp.exp2(m - m_new) * acc + jnp.dot(p, v_ones, preferred_element_type=f32)
    return m_new, acc


def _softmax_result(carry):
    _, acc = carry
    return acc[:, :HEAD_DIM] / acc[:, HEAD_DIM:]


def _fox_attn_kernel(q_ref, k_ref, v_ref, kb_ref, o_ref, qaug_scr, kaug_scr, vo_scr, *, t):
    h = pl.program_id(1)
    S = q_ref.shape[1]

    @pl.when(h == 0)
    def _():
        kaug_scr[:, HEAD_DIM:] = kb_ref[0]
        vo_scr[:, HEAD_DIM:] = jnp.ones((S, HEAD_DIM), vo_scr.dtype)

    lane = lax.broadcasted_iota(jnp.int32, (1, LANES), 1)
    mine = jnp.logical_and(lane % N_FOX_HEADS == h, lane < KEY_BIAS_PARTS * N_FOX_HEADS)
    qaug_scr[:, :HEAD_DIM] = q_ref[0]
    qaug_scr[:, HEAD_DIM:] = jnp.broadcast_to(jnp.where(mine, 1.0, 0.0), (S, LANES)).astype(bf16)
    kaug_scr[:, :HEAD_DIM] = k_ref[0]
    vo_scr[:, :HEAD_DIM] = v_ref[0]
    row = lax.broadcasted_iota(jnp.int32, (t, t), 0)
    col = lax.broadcasted_iota(jnp.int32, (t, t), 1)
    for qi in range(S // t):
        qa = qaug_scr[qi * t:(qi + 1) * t, :]
        carry = _softmax_init(t)
        for kb in range(qi + 1):
            keys = slice(kb * t, (kb + 1) * t)
            s2 = _qk(qa, kaug_scr[keys, :])
            if kb == qi:
                s2 = jnp.where(col <= row, s2, NEG)
            carry = _softmax_step2(carry, s2, vo_scr[keys, :])
        o_ref[qi * t:(qi + 1) * t, :] = _softmax_result(carry).astype(o_ref.dtype)


def _fox_attn(qkv, key_bias, B, S, slab0):
    T = B * S
    H = N_FOX_HEADS
    t = _pick(S, 512)
    kern = functools.partial(_fox_attn_kernel, t=t)
    return pl.pallas_call(
        kern,
        grid=(B, H),
        in_specs=[
            pl.BlockSpec((1, S, HEAD_DIM), lambda b, h: (slab0 + h, b, 0)),
            pl.BlockSpec((1, S, HEAD_DIM), lambda b, h: (slab0 + H + h, b, 0)),
            pl.BlockSpec((1, S, HEAD_DIM), lambda b, h: (slab0 + 2 * H + h, b, 0)),
            pl.BlockSpec((1, S, LANES), lambda b, h: (b, 0, 0)),
        ],
        out_specs=pl.BlockSpec((S, HEAD_DIM), lambda b, h: (b, h)),
        out_shape=jax.ShapeDtypeStruct((T, H * HEAD_DIM), bf16),
        scratch_shapes=[pltpu.VMEM((S, 2 * HEAD_DIM), bf16)] * 3,
        compiler_params=_params("arbitrary", "arbitrary"),
        name="fox_attn",
    )(qkv, qkv, qkv, key_bias)


def _rel_bucket_table(n):
    max_exact = REL_BUCKETS // 2
    d = np.arange(n)
    ratio = np.log(np.maximum(d, 1).astype(np.float32) / np.float32(max_exact)) / np.float32(
        math.log(REL_MAX_DIST / max_exact))
    large = max_exact + (ratio * np.float32(REL_BUCKETS - max_exact)).astype(np.int32)
    large = np.minimum(large, REL_BUCKETS - 1)
    return np.where(d < max_exact, d, large).astype(np.int32)


def _moba_attn_kernel(rel_ref, q_ref, k_ref, v_ref, avg_ref, hot_ref, bk_ref, o_ref, bias_scr, kaug_scr,
                      qaug_scr, vo_scr):
    h = pl.program_id(0)
    blk = MOBA_BLOCK
    t = 2 * blk
    S = q_ref.shape[1]

    @pl.when(pl.program_id(1) == 0)
    def _():
        far_bias = rel_ref[h, REL_BUCKETS - 1]
        row = lax.broadcasted_iota(jnp.int32, (blk, blk), 0)
        col = lax.broadcasted_iota(jnp.int32, (blk, blk), 1)
        tiles = []
        for which in range(2):
            bk = bk_ref[which]
            tile = jnp.zeros((blk, blk), f32)
            for b in range(REL_BUCKETS - 1):
                tile = jnp.where(bk == b, (rel_ref[h, b] - far_bias) * LOG2E, tile)
            tiles.append(jnp.where(col <= row, tile, NEG) if which == 0 else tile)
        own_tile, prev_tile = tiles
        zero_tile = jnp.zeros((blk, blk), f32)
        for which, quads in enumerate((((zero_tile, prev_tile), (zero_tile, zero_tile)),
                                       ((own_tile, zero_tile), (prev_tile, own_tile)))):
            for r in range(2):
                for c in range(2):
                    bias_scr[which, r * blk:(r + 1) * blk, c * blk:(c + 1) * blk] = quads[r][c]
        kaug_scr[:, HEAD_DIM:] = hot_ref[...]
        vo_scr[:, HEAD_DIM:] = jnp.ones((S, HEAD_DIM), vo_scr.dtype)

    kaug_scr[:, :HEAD_DIM] = k_ref[0]
    vo_scr[:, :HEAD_DIM] = v_ref[0]
    kmean = jnp.dot(avg_ref[...], k_ref[0], preferred_element_type=f32).astype(bf16)

    nbp = -(-(S // blk) // SUBLANES) * SUBLANES
    block = lax.broadcasted_iota(jnp.int32, (nbp, S), 0)
    own = lax.broadcasted_iota(jnp.int32, (nbp, S), 1) // blk
    valid = block < own
    gate = jnp.where(valid, _qk(kmean[:nbp], q_ref[0]), NEG)
    sel = block == own
    for _ in range(MOBA_TOPK):
        mx = jnp.max(gate, axis=0, keepdims=True)
        first = jnp.min(jnp.where(gate == mx, block, nbp), axis=0, keepdims=True)
        pick = block == first
        sel = jnp.logical_or(sel, jnp.logical_and(pick, valid))
        gate = jnp.where(pick, -jnp.inf, gate)
    chosen = jnp.concatenate([jnp.where(sel, 0.0, NEG), jnp.full((LANES - nbp, S), NEG, f32)], axis=0)
    qaug_scr[:, :HEAD_DIM] = q_ref[0]
    qaug_scr[:, HEAD_DIM:] = jnp.transpose(chosen).astype(bf16)

    for j in range(S // t):
        rows = slice(j * t, (j + 1) * t)
        qa = qaug_scr[rows, :]
        carry = _softmax_step2(_softmax_init(t), _qk(qa, kaug_scr[rows, :]) + bias_scr[1], vo_scr[rows, :])
        if j > 0:
            keys = slice((j - 1) * t, j * t)
            carry = _softmax_step2(carry, _qk(qa, kaug_scr[keys, :]) + bias_scr[0], vo_scr[keys, :])
        for c in range(j - 1):
            keys = slice(c * t, (c + 1) * t)
            carry = _softmax_step2(carry, _qk(qa, kaug_scr[keys, :]), vo_scr[keys, :])
        o_ref[rows, :] = _softmax_result(carry).astype(o_ref.dtype)


def _moba_attn(qkv, rel_bias, B, S, slab0):
    T = B * S
    H = N_MOBA_HEADS
    blk = MOBA_BLOCK
    assert S % (2 * blk) == 0 and S // blk <= LANES
    nb = S // blk
    avg = np.zeros((LANES, S), np.float32)
    for n in range(nb):
        avg[n, n * blk:(n + 1) * blk] = 1.0 / blk
    hot = (avg.T > 0).astype(np.float32)
    table = _rel_bucket_table(2 * blk)
    dist = np.arange(blk)[:, None] - np.arange(blk)[None, :]
    buckets = np.stack([table[np.maximum(dist, 0)], table[dist + blk]]).astype(np.int32)
    grid_spec = pltpu.PrefetchScalarGridSpec(
        num_scalar_prefetch=1,
        grid=(H, B),
        in_specs=[
            pl.BlockSpec((1, S, HEAD_DIM), lambda h, b, r: (slab0 + h, b, 0)),
            pl.BlockSpec((1, S, HEAD_DIM), lambda h, b, r: (slab0 + H + h, b, 0)),
            pl.BlockSpec((1, S, HEAD_DIM), lambda h, b, r: (slab0 + 2 * H + h, b, 0)),
            pl.BlockSpec((LANES, S), lambda h, b, r: (0, 0)),
            pl.BlockSpec((S, LANES), lambda h, b, r: (0, 0)),
            pl.BlockSpec((2, blk, blk), lambda h, b, r: (0, 0, 0)),
        ],
        out_specs=pl.BlockSpec((S, HEAD_DIM), lambda h, b, r: (b, h)),
        scratch_shapes=[pltpu.VMEM((2, 2 * blk, 2 * blk), f32)] + [pltpu.VMEM((S, 2 * HEAD_DIM), bf16)] * 3,
    )
    return pl.pallas_call(
        _moba_attn_kernel,
        grid_spec=grid_spec,
        out_shape=jax.ShapeDtypeStruct((T, H * HEAD_DIM), bf16),
        compiler_params=_params("arbitrary", "arbitrary"),
        name="moba_attn",
    )(rel_bias, qkv, qkv, qkv, jnp.asarray(avg, bf16), jnp.asarray(hot, bf16), jnp.asarray(buckets))


def _rms(x, g):
    return x * lax.rsqrt(jnp.mean(x * x, axis=-1, keepdims=True) + EPS) * g


def _first_max(x, lane):
    mx = jnp.max(x, axis=-1, keepdims=True)
    return mx, jnp.min(jnp.where(x == mx, lane, LANES), axis=-1, keepdims=True)


def _route(logits, lane, grp=None):
    gl = jnp.where(lane < N_GROUPS, logits, -jnp.inf)
    gmax, top = _first_max(gl, lane)
    grp = top if grp is None else grp
    pg = jnp.exp(gl - gmax)
    p_top_group = jnp.sum(jnp.where(lane == grp, pg, 0.0), axis=-1, keepdims=True) / jnp.sum(
        pg, axis=-1, keepdims=True)
    lo = N_GROUPS + grp * EXPERTS_PER_GROUP
    in_grp = jnp.logical_and(lane >= lo, lane < lo + EXPERTS_PER_GROUP)
    el = jnp.where(in_grp, logits, -jnp.inf)
    pe = jnp.exp(el - jnp.max(el, axis=-1, keepdims=True))
    p_exp = jnp.where(in_grp, pe / jnp.sum(pe, axis=-1, keepdims=True), -1.0)
    p1, i1 = _first_max(p_exp, lane)
    p2, i2 = _first_max(jnp.where(lane == i1, -1.0, p_exp), lane)
    tot = p1 + p2
    combine = jnp.where(lane == i1, p1 / tot, jnp.where(lane == i2, p2 / tot, 0.0)) * p_top_group
    return grp, jnp.minimum(i1, i2) - lo, jnp.maximum(i1, i2) - lo, combine


def _out_proj_kernel(fo_ref, mo_ref, x_ref, gf_ref, gm_ref, w_ref, gffn_ref, wr_ref, br_ref,
                     x1_ref, route_ref, count_ref, count_scr, x1_scr):
    i = pl.program_id(0)

    @pl.when(i == 0)
    def _():
        count_scr[...] = jnp.zeros_like(count_scr)
        x1_scr[1] = jnp.zeros(x1_scr.shape[1:], x1_scr.dtype)

    tm, D = x_ref.shape
    chunk = D // OUT_PROJ_CHUNKS

    fo = _rms(fo_ref[...].astype(f32), gf_ref[...]).astype(bf16)
    mo = _rms(mo_ref[...].astype(f32), gm_ref[...]).astype(bf16)
    mix = jnp.concatenate([fo, mo], axis=-1)

    def project(c):
        cols = slice(c * chunk, (c + 1) * chunk)
        x1_scr[i % 2, :, cols] = x_ref[:, cols] + jnp.dot(mix, w_ref[:, cols], preferred_element_type=f32)

    x1 = x1_scr[(i + 1) % 2]
    h2 = _rms(x1, gffn_ref[...]).astype(bf16)
    logits = jnp.dot(h2, wr_ref[...], preferred_element_type=f32) + br_ref[...]
    project(0)
    lane = lax.broadcasted_iota(jnp.int32, logits.shape, 1)
    grp, ea, eb, combine = _route(logits, lane)
    x1_ref[:, :D] = x1
    x1_ref[:, D:] = combine
    project(1)
    pair = (ea * (2 * EXPERTS_PER_GROUP - 1 - ea)) // 2 + eb - ea - 1
    cls = grp * EXPERT_PAIRS + pair
    onehot = jnp.where(lane == cls, 1.0, 0.0)
    earlier = lax.broadcasted_iota(jnp.int32, (tm, tm), 1) < lax.broadcasted_iota(jnp.int32, (tm, tm), 0)
    counts = count_scr[...]
    before = jnp.dot(jnp.where(earlier, 1.0, 0.0).astype(bf16), onehot.astype(bf16),
                     preferred_element_type=f32) + counts
    rank = jnp.sum(onehot * before, axis=-1, keepdims=True).astype(jnp.int32)
    route_ref[...] = jnp.where(lane == 0, cls, jnp.where(lane == 1, rank, 0))
    counts = counts + jnp.where(i > 0, jnp.sum(onehot, axis=0, keepdims=True), 0.0)
    count_scr[...] = counts
    count_ref[...] = counts.astype(jnp.int32)
    for c in range(2, OUT_PROJ_CHUNKS):
        project(c)


def _out_proj(fo, mo, x2, gf, gm, w_out, g_ffn, w_r, b_r):
    T, D = x2.shape
    Wf = fo.shape[1]
    Wm = mo.shape[1]
    tm = _pick(T, 512)
    n = T // tm
    const = lambda i: (0, 0)
    rows_in = lambda i: (jnp.minimum(i, n - 1), 0)
    rows = lambda i: (jnp.maximum(i - 1, 0), 0)
    return pl.pallas_call(
        _out_proj_kernel,
        grid=(n + 1,),
        in_specs=[
            pl.BlockSpec((tm, Wf), rows_in),
            pl.BlockSpec((tm, Wm), rows_in),
            pl.BlockSpec((tm, D), rows_in),
            pl.BlockSpec((1, Wf), const),
            pl.BlockSpec((1, Wm), const),
            pl.BlockSpec((Wf + Wm, D), const),
            pl.BlockSpec((1, D), const),
            pl.BlockSpec((D, ROUTER_COLS), const),
            pl.BlockSpec((1, ROUTER_COLS), const),
        ],
        out_specs=[
            pl.BlockSpec((tm, D + ROUTER_COLS), rows),
            pl.BlockSpec((tm, ROUTER_COLS), rows),
            pl.BlockSpec((1, ROUTER_COLS), const),
        ],
        out_shape=[
            jax.ShapeDtypeStruct((T, D + ROUTER_COLS), f32),
            jax.ShapeDtypeStruct((T, ROUTER_COLS), jnp.int32),
            jax.ShapeDtypeStruct((1, ROUTER_COLS), jnp.int32),
        ],
        scratch_shapes=[pltpu.VMEM((1, ROUTER_COLS), f32), pltpu.VMEM((2, tm, D), f32)],
        compiler_params=_params("arbitrary"),
        name="out_proj",
    )(fo, mo, x2, gf, gm, w_out, g_ffn, w_r, b_r)


def _start_rows(row_copy, n):
    assert n % DMA_UNROLL == 0

    def body(g, carry):
        for k in range(DMA_UNROLL):
            row_copy(g * DMA_UNROLL + k).start()
        return carry

    lax.fori_loop(0, n // DMA_UNROLL, body, 0)


def _slot(cls_ref, rank_ref, base_ref, t):
    return base_ref[cls_ref[t]] + rank_ref[t]


def _dispatch_kernel(cls_ref, rank_ref, base_ref, fill_ref, x_ref, xs_ref, zero_scr, sem, zsem):
    i = pl.program_id(0)
    tm = x_ref.shape[0]

    zrows = zero_scr.shape[0]

    def row_copy(r):
        d = _slot(cls_ref, rank_ref, base_ref, i * tm + r)
        return pltpu.make_async_copy(x_ref.at[pl.ds(r, 1)], xs_ref.at[pl.ds(d, 1)], sem)

    def zero_row(r):
        return pltpu.make_async_copy(zero_scr.at[pl.ds(0, 1)], xs_ref.at[pl.ds(r, 1)], zsem)

    def zero_sublanes(c):
        dst = xs_ref.at[pl.ds(pl.multiple_of(c * SUBLANES, SUBLANES), SUBLANES)]
        return pltpu.make_async_copy(zero_scr.at[pl.ds(0, SUBLANES)], dst, zsem)

    def zero_chunk(c):
        return pltpu.make_async_copy(zero_scr, xs_ref.at[pl.ds(pl.multiple_of(c * zrows, zrows), zrows)], zsem)

    def start(copy):
        def body(r, carry):
            copy(r).start()
            return carry
        return body

    def wait(copy):
        def body(r, carry):
            copy(r).wait()
            return carry
        return body

    _start_rows(row_copy, tm)

    def zero_fill(act):
        def fill_class(c, carry):
            lo, hi = fill_ref[2 * c], fill_ref[2 * c + 1]
            mid = jnp.minimum((lo + SUBLANES - 1) // SUBLANES * SUBLANES, hi)
            lax.fori_loop(lo, mid, act(zero_row), 0)
            lax.fori_loop(mid // SUBLANES, hi // SUBLANES, act(zero_sublanes), 0)
            return carry

        lax.fori_loop(0, N_CLASSES, fill_class, 0)
        lo, hi = fill_ref[2 * N_CLASSES] // zrows, fill_ref[2 * N_CLASSES + 1] // zrows
        lax.fori_loop(lo, hi, act(zero_chunk), 0)

    @pl.when(i == 0)
    def _():
        zero_scr[...] = jnp.zeros_like(zero_scr)
        zero_fill(start)

    pltpu.make_async_copy(x_ref, xs_ref.at[pl.ds(0, tm)], sem).wait()

    @pl.when(i == 0)
    def _():
        zero_fill(wait)


def _dispatch(x1, cls, rank, base, fill, rows_out):
    T, D = x1.shape
    tm = _pick(T, 512)
    grid_spec = pltpu.PrefetchScalarGridSpec(
        num_scalar_prefetch=4,
        grid=(T // tm,),
        in_specs=[pl.BlockSpec((tm, D), lambda i, *_: (i, 0))],
        out_specs=pl.BlockSpec(memory_space=pl.ANY),
        scratch_shapes=[pltpu.VMEM((ZERO_ROWS, D), f32), pltpu.SemaphoreType.DMA, pltpu.SemaphoreType.DMA],
    )
    return pl.pallas_call(
        _dispatch_kernel,
        grid_spec=grid_spec,
        out_shape=jax.ShapeDtypeStruct((rows_out, D), f32),
        compiler_params=_params("arbitrary"),
        name="moe_dispatch",
    )(cls, rank, base, fill, x1)


def _collect_kernel(cls_ref, rank_ref, base_ref, ys_ref, o_ref, sem):
    i = pl.program_id(0)
    tm = o_ref.shape[0]

    def row_copy(r):
        d = _slot(cls_ref, rank_ref, base_ref, i * tm + r)
        return pltpu.make_async_copy(ys_ref.at[pl.ds(d, 1)], o_ref.at[pl.ds(r, 1)], sem)

    _start_rows(row_copy, tm)
    pltpu.make_async_copy(ys_ref.at[pl.ds(0, tm)], o_ref, sem).wait()


def _collect(ys, cls, rank, base, T):
    D = ys.shape[1]
    tm = _pick(T, 512)
    grid_spec = pltpu.PrefetchScalarGridSpec(
        num_scalar_prefetch=3,
        grid=(T // tm,),
        in_specs=[pl.BlockSpec(memory_space=pl.ANY)],
        out_specs=pl.BlockSpec((tm, D), lambda i, *_: (i, 0)),
        scratch_shapes=[pltpu.SemaphoreType.DMA],
    )
    return pl.pallas_call(
        _collect_kernel,
        grid_spec=grid_spec,
        out_shape=jax.ShapeDtypeStruct((T, D), f32),
        compiler_params=_params("arbitrary"),
        name="moe_collect",
    )(cls, rank, base, ys)


def _moe_kernel(ea_ref, eb_ref, nu_ref, xs_ref, gffn_ref, wga_ref, wua_ref, wda_ref,
                wgb_ref, wub_ref, wdb_ref, gfin_ref, o_ref):
    i = pl.program_id(0)
    D = o_ref.shape[1]

    @pl.when(i < nu_ref[0])
    def _():
        x = xs_ref[:, :D]
        combine = xs_ref[:, D:]
        h = _rms(x, gffn_ref[...]).astype(bf16)
        lane = lax.broadcasted_iota(jnp.int32, combine.shape, 1)
        y = jnp.zeros(x.shape, f32)
        for e_ref, wg_ref, wu_ref, wd_ref in ((ea_ref, wga_ref, wua_ref, wda_ref),
                                               (eb_ref, wgb_ref, wub_ref, wdb_ref)):
            c = jnp.sum(jnp.where(lane == N_GROUPS + e_ref[i], combine, 0.0), axis=-1, keepdims=True)
            gate = jnp.dot(h, wg_ref[0], preferred_element_type=f32)
            up = jnp.dot(h, wu_ref[0], preferred_element_type=f32)
            hid = (jax.nn.silu(gate) * up * c).astype(bf16)
            y = y + jnp.dot(hid, wd_ref[0], preferred_element_type=f32)
        o_ref[...] = _rms(x + y, gfin_ref[...])

    @pl.when(i >= nu_ref[0])
    def _():
        o_ref[...] = jnp.zeros_like(o_ref)


def _moe(xs, tile_ea, tile_eb, n_used, g_ffn, w_gate, w_up, w_down, g_final, ts):
    R = xs.shape[0]
    E, D, F = w_gate.shape
    const = lambda i, ea, eb, nu: (0, 0)
    rows = lambda i, ea, eb, nu: (jnp.minimum(i, nu[0] - 1), 0)
    first = lambda i, ea, eb, nu: (ea[i], 0, 0)
    second = lambda i, ea, eb, nu: (eb[i], 0, 0)
    grid_spec = pltpu.PrefetchScalarGridSpec(
        num_scalar_prefetch=3,
        grid=(R // ts,),
        in_specs=[
            pl.BlockSpec((ts, D + ROUTER_COLS), rows),
            pl.BlockSpec((1, D), const),
            pl.BlockSpec((1, D, F), first),
            pl.BlockSpec((1, D, F), first),
            pl.BlockSpec((1, F, D), first),
            pl.BlockSpec((1, D, F), second),
            pl.BlockSpec((1, D, F), second),
            pl.BlockSpec((1, F, D), second),
            pl.BlockSpec((1, D), const),
        ],
        out_specs=pl.BlockSpec((ts, D), lambda i, ea, eb, nu: (i, 0)),
    )
    return pl.pallas_call(
        _moe_kernel,
        grid_spec=grid_spec,
        out_shape=jax.ShapeDtypeStruct((R, D), f32),
        compiler_params=_params("arbitrary"),
        name="moe",
    )(tile_ea, tile_eb, n_used, xs, g_ffn, w_gate, w_up, w_down, w_gate, w_up, w_down, g_final)


def kernel(x, attn_norm_g, w_in, b_forget, fox_out_norm_g, moba_out_norm_g, rel_bias, w_out, ffn_norm_g,
           w_group_router, b_group_router, w_expert_router, b_expert_router, w_gate, w_up, w_down,
           final_norm_g):
    B, S, D = x.shape
    T = B * S
    depth = w_in.shape[0]
    fox_w = N_FOX_HEADS * HEAD_DIM
    moba_w = N_MOBA_HEADS * HEAD_DIM
    qkv_w = 3 * (fox_w + moba_w)
    assert w_in.shape[2] == qkv_w + N_FOX_HEADS
    assert N_GROUPS + N_EXPERTS <= ROUTER_COLS

    x2 = x.reshape(T, D)
    out = None
    for l in range(depth):
        w_in_t = jnp.swapaxes(w_in, 1, 2).astype(bf16)
        w_fg_t = w_in_t[l, qkv_w:, :]
        assert fox_w == moba_w
        col_scale = np.ones((qkv_w // fox_w,), np.float32)
        col_scale[[0, 3]] = HEAD_DIM ** -0.5 * LOG2E
        qkv, f_t = _in_proj(x2, attn_norm_g[l].reshape(1, D), w_in_t, l, w_fg_t, jnp.asarray(col_scale),
                            fox_w)
        key_bias = _fox_decay(f_t, b_forget[l].reshape(N_FOX_HEADS, 1), B, S)
        fo = _fox_attn(qkv, key_bias, B, S, 0)
        mo = _moba_attn(qkv, rel_bias, B, S, 3 * N_FOX_HEADS)

        w_r = jnp.concatenate([w_group_router[l], w_expert_router[l].reshape(D, N_EXPERTS)], axis=1)
        w_r = jnp.pad(w_r, ((0, 0), (0, ROUTER_COLS - w_r.shape[1]))).astype(bf16)
        b_r = jnp.concatenate([b_group_router[l], b_expert_router[l].reshape(N_EXPERTS)])
        b_r = jnp.pad(b_r, (0, ROUTER_COLS - b_r.shape[0])).reshape(1, ROUTER_COLS)
        g_ffn = ffn_norm_g[l].reshape(1, D)
        x1, route, count = _out_proj(fo, mo, x2, fox_out_norm_g[l].reshape(1, fox_w),
                                     moba_out_norm_g[l].reshape(1, moba_w), w_out[l].astype(bf16),
                                     g_ffn, w_r, b_r)
        ts = MOE_TILE
        cls, rank = route[:, 0], route[:, 1]
        counts = count[0, :N_CLASSES]
        padded = (counts + ts - 1) // ts * ts
        ends = jnp.cumsum(padded)
        base = ends - padded
        rows_out = (T // ts + N_CLASSES) * ts
        fill = jnp.concatenate([jnp.stack([base + counts, ends], axis=1).reshape(-1),
                                jnp.stack([ends[-1], jnp.asarray(rows_out, ends.dtype)])])
        tile_start = jnp.arange(rows_out // ts, dtype=jnp.int32) * ts
        tile_cls = jnp.minimum(jnp.sum(tile_start[:, None] >= ends[None, :], axis=1), N_CLASSES - 1)
        pairs = [(a, b) for a in range(EXPERTS_PER_GROUP) for b in range(a + 1, EXPERTS_PER_GROUP)]
        first_of_cls = np.array([g * EXPERTS_PER_GROUP + a for g in range(N_GROUPS) for a, _ in pairs], np.int32)
        second_of_cls = np.array([g * EXPERTS_PER_GROUP + b for g in range(N_GROUPS) for _, b in pairs], np.int32)
        tile_ea = jnp.asarray(first_of_cls)[tile_cls]
        tile_eb = jnp.asarray(second_of_cls)[tile_cls]
        n_used = (ends[-1:] // ts).astype(jnp.int32)

        assert l == depth - 1, "the fused residual + final norm epilogue expects a single layer"
        xs = _dispatch(x1, cls, rank, base, fill, rows_out)
        ys = _moe(xs, tile_ea, tile_eb, n_used, g_ffn, w_gate[l].astype(bf16),
                  w_up[l].astype(bf16), w_down[l].astype(bf16), final_norm_g.reshape(1, D), ts)
        out = _collect(ys, cls, rank, base, T)
        x2 = out
    return out.reshape(B, S, D)
```

```python
import functools
import math

import numpy as np
import jax
import jax.numpy as jnp
from jax import lax
from jax.experimental import pallas as pl
from jax.experimental.pallas import tpu as pltpu

HEAD_DIM = 128
N_FOX_HEADS = 8
N_MOBA_HEADS = 8
MOBA_BLOCK = 256
MOBA_TOPK = 3
REL_BUCKETS = 32
REL_MAX_DIST = 128
N_GROUPS = 4
EXPERTS_PER_GROUP = 4
N_EXPERTS = N_GROUPS * EXPERTS_PER_GROUP
EXPERT_PAIRS = EXPERTS_PER_GROUP * (EXPERTS_PER_GROUP - 1) // 2
N_CLASSES = N_GROUPS * EXPERT_PAIRS
EPS = 1e-6
NEG = -1e30
LOG2E = math.log2(math.e)
LANES = 128
SUBLANES = 8
ROUTER_COLS = LANES
OUT_PROJ_CHUNKS = 4
MOE_TILE = 256
KEY_BIAS_PARTS = 3
ZERO_ROWS = 64
DMA_UNROLL = 8
VMEM_LIMIT = 56 * 1024 * 1024

f32 = jnp.float32
bf16 = jnp.bfloat16


def _params(*sem):
    return pltpu.CompilerParams(dimension_semantics=sem, vmem_limit_bytes=VMEM_LIMIT)


def _pick(n, pref):
    t = min(n, pref)
    assert n % t == 0, (n, pref)
    return t


def _in_proj_kernel(cs_ref, x_ref, g_ref, w_ref, wfg_ref, qkv_ref, f_ref, h_scr):
    @pl.when(pl.program_id(1) == 0)
    def _():
        x = x_ref[...]
        y = x * lax.rsqrt(jnp.mean(x * x, axis=-1, keepdims=True) + EPS)
        hb = (y * g_ref[...]).astype(bf16)
        h_scr[...] = hb
        f_ref[...] = lax.dot_general(wfg_ref[...], hb, (((1,), (1,)), ((), ())),
                                     preferred_element_type=f32)

    acc = lax.dot_general(h_scr[...], w_ref[...], (((1,), (1,)), ((), ())),
                          preferred_element_type=f32) * cs_ref[pl.program_id(1)]
    for c in range(acc.shape[1] // HEAD_DIM):
        qkv_ref[c] = acc[:, c * HEAD_DIM:(c + 1) * HEAD_DIM].astype(bf16)


def _in_proj(x2, g, w_in_t, layer, w_fg_t, col_scale, tn):
    T, D = x2.shape
    N = col_scale.shape[0] * tn
    tm = _pick(T, 1024)
    assert N <= w_in_t.shape[1]
    nh = w_fg_t.shape[0]
    return pl.pallas_call(
        _in_proj_kernel,
        grid=(T // tm, N // tn),
        in_specs=[
            pl.BlockSpec(memory_space=pltpu.SMEM),
            pl.BlockSpec((tm, D), lambda i, j: (i, 0)),
            pl.BlockSpec((1, D), lambda i, j: (0, 0)),
            pl.BlockSpec((None, tn, D), lambda i, j: (layer, j, 0)),
            pl.BlockSpec((nh, D), lambda i, j: (0, 0)),
        ],
        out_specs=[
            pl.BlockSpec((tn // HEAD_DIM, tm, HEAD_DIM), lambda i, j: (j, i, 0)),
            pl.BlockSpec((nh, tm), lambda i, j: (0, i)),
        ],
        out_shape=[
            jax.ShapeDtypeStruct((N // HEAD_DIM, T, HEAD_DIM), bf16),
            jax.ShapeDtypeStruct((nh, T), f32),
        ],
        scratch_shapes=[pltpu.VMEM((tm, D), bf16)],
        compiler_params=_params("arbitrary", "arbitrary"),
        name="in_proj",
    )(col_scale, x2, g, w_in_t, w_fg_t)


def _fox_decay_kernel(f_ref, b_ref, o_ref):
    lf = jax.nn.log_sigmoid(f_ref[...] + b_ref[...])
    S = lf.shape[1]
    lane = lax.broadcasted_iota(jnp.int32, lf.shape, 1)
    c = lf
    sh = 1
    while sh < S:
        c = c + jnp.where(lane >= sh, pltpu.roll(c, sh, axis=1), 0.0)
        sh *= 2
    rest = -c * LOG2E
    parts = []
    for _ in range(KEY_BIAS_PARTS):
        part = rest.astype(bf16).astype(f32)
        parts.append(part)
        rest = rest - part
    nh = lf.shape[0]
    cols = jnp.concatenate(parts + [jnp.zeros((LANES - KEY_BIAS_PARTS * nh, S), f32)], axis=0)
    o_ref[0] = jnp.transpose(cols).astype(bf16)


def _fox_decay(f_t, b_forget, B, S):
    nh = f_t.shape[0]
    assert KEY_BIAS_PARTS * nh <= LANES
    return pl.pallas_call(
        _fox_decay_kernel,
        grid=(B,),
        in_specs=[
            pl.BlockSpec((nh, S), lambda b: (0, b)),
            pl.BlockSpec((nh, 1), lambda b: (0, 0)),
        ],
        out_specs=pl.BlockSpec((1, S, LANES), lambda b: (b, 0, 0)),
        out_shape=jax.ShapeDtypeStruct((B, S, LANES), bf16),
        compiler_params=_params("arbitrary"),
        name="fox_decay",
    )(f_t, b_forget)


def _qk(q, k):
    return lax.dot_general(q, k, (((1,), (1,)), ((), ())), preferred_element_type=f32)


def _softmax_init(t):
    return jnp.full((t, 1), NEG, f32), jnp.zeros((t, 2 * HEAD_DIM), f32)


def _softmax_step2(carry, s2, v_ones):
    m, acc = carry
    m_new = jnp.maximum(m, jnp.max(s2, axis=-1, keepdims=True))
    p = jnp.exp2(s2 - m_new).astype(bf16)
    acc = jnp.exp2(m - m_new) * acc + jnp.dot(p, v_ones, preferred_element_type=f32)
    return m_new, acc


def _softmax_result(carry):
    _, acc = carry
    return acc[:, :HEAD_DIM] / acc[:, HEAD_DIM:]


def _fox_attn_kernel(q_ref, k_ref, v_ref, kb_ref, o_ref, qaug_scr, kaug_scr, vo_scr, *, t):
    h = pl.program_id(1)
    S = q_ref.shape[1]

    @pl.when(h == 0)
    def _():
        kaug_scr[:, HEAD_DIM:] = kb_ref[0]
        vo_scr[:, HEAD_DIM:] = jnp.ones((S, HEAD_DIM), vo_scr.dtype)

    lane = lax.broadcasted_iota(jnp.int32, (1, LANES), 1)
    mine = jnp.logical_and(lane % N_FOX_HEADS == h, lane < KEY_BIAS_PARTS * N_FOX_HEADS)
    qaug_scr[:, :HEAD_DIM] = q_ref[0]
    qaug_scr[:, HEAD_DIM:] = jnp.broadcast_to(jnp.where(mine, 1.0, 0.0), (S, LANES)).astype(bf16)
    kaug_scr[:, :HEAD_DIM] = k_ref[0]
    vo_scr[:, :HEAD_DIM] = v_ref[0]
    row = lax.broadcasted_iota(jnp.int32, (t, t), 0)
    col = lax.broadcasted_iota(jnp.int32, (t, t), 1)
    for qi in range(S // t):
        qa = qaug_scr[qi * t:(qi + 1) * t, :]
        carry = _softmax_init(t)
        for kb in range(qi + 1):
            keys = slice(kb * t, (kb + 1) * t)
            s2 = _qk(qa, kaug_scr[keys, :])
            if kb == qi:
                s2 = jnp.where(col <= row, s2, NEG)
            carry = _softmax_step2(carry, s2, vo_scr[keys, :])
        o_ref[qi * t:(qi + 1) * t, :] = _softmax_result(carry).astype(o_ref.dtype)


def _fox_attn(qkv, key_bias, B, S, slab0):
    T = B * S
    H = N_FOX_HEADS
    t = _pick(S, 512)
    kern = functools.partial(_fox_attn_kernel, t=t)
    return pl.pallas_call(
        kern,
        grid=(B, H),
        in_specs=[
            pl.BlockSpec((1, S, HEAD_DIM), lambda b, h: (slab0 + h, b, 0)),
            pl.BlockSpec((1, S, HEAD_DIM), lambda b, h: (slab0 + H + h, b, 0)),
            pl.BlockSpec((1, S, HEAD_DIM), lambda b, h: (slab0 + 2 * H + h, b, 0)),
            pl.BlockSpec((1, S, LANES), lambda b, h: (b, 0, 0)),
        ],
        out_specs=pl.BlockSpec((S, HEAD_DIM), lambda b, h: (b, h)),
        out_shape=jax.ShapeDtypeStruct((T, H * HEAD_DIM), bf16),
        scratch_shapes=[pltpu.VMEM((S, 2 * HEAD_DIM), bf16)] * 3,
        compiler_params=_params("arbitrary", "arbitrary"),
        name="fox_attn",
    )(qkv, qkv, qkv, key_bias)


def _rel_bucket_table(n):
    max_exact = REL_BUCKETS // 2
    d = np.arange(n)
    ratio = np.log(np.maximum(d, 1).astype(np.float32) / np.float32(max_exact)) / np.float32(
        math.log(REL_MAX_DIST / max_exact))
    large = max_exact + (ratio * np.float32(REL_BUCKETS - max_exact)).astype(np.int32)
    large = np.minimum(large, REL_BUCKETS - 1)
    return np.where(d < max_exact, d, large).astype(np.int32)


def _moba_attn_kernel(rel_ref, q_ref, k_ref, v_ref, avg_ref, hot_ref, bk_ref, o_ref, bias_scr, kaug_scr,
                      qaug_scr, vo_scr):
    h = pl.program_id(0)
    blk = MOBA_BLOCK
    t = 2 * blk
    S = q_ref.shape[1]

    @pl.when(pl.program_id(1) == 0)
    def _():
        far_bias = rel_ref[h, REL_BUCKETS - 1]
        row = lax.broadcasted_iota(jnp.int32, (blk, blk), 0)
        col = lax.broadcasted_iota(jnp.int32, (blk, blk), 1)
        tiles = []
        for which in range(2):
            bk = bk_ref[which]
            tile = jnp.zeros((blk, blk), f32)
            for b in range(REL_BUCKETS - 1):
                tile = jnp.where(bk == b, (rel_ref[h, b] - far_bias) * LOG2E, tile)
            tiles.append(jnp.where(col <= row, tile, NEG) if which == 0 else tile)
        own_tile, prev_tile = tiles
        zero_tile = jnp.zeros((blk, blk), f32)
        for which, quads in enumerate((((zero_tile, prev_tile), (zero_tile, zero_tile)),
                                       ((own_tile, zero_tile), (prev_tile, own_tile)))):
            for r in range(2):
                for c in range(2):
                    bias_scr[which, r * blk:(r + 1) * blk, c * blk:(c + 1) * blk] = quads[r][c]
        kaug_scr[:, HEAD_DIM:] = hot_ref[...]
        vo_scr[:, HEAD_DIM:] = jnp.ones((S, HEAD_DIM), vo_scr.dtype)

    kaug_scr[:, :HEAD_DIM] = k_ref[0]
    vo_scr[:, :HEAD_DIM] = v_ref[0]
    kmean = jnp.dot(avg_ref[...], k_ref[0], preferred_element_type=f32).astype(bf16)

    nbp = avg_ref.shape[0]
    block = lax.broadcasted_iota(jnp.int32, (nbp, S), 0)
    own = lax.broadcasted_iota(jnp.int32, (nbp, S), 1) // blk
    valid = block < own
    gate = jnp.where(valid, _qk(kmean, q_ref[0]), NEG)
    sel = block == own
    for _ in range(MOBA_TOPK):
        mx = jnp.max(gate, axis=0, keepdims=True)
        first = jnp.min(jnp.where(gate == mx, block, nbp), axis=0, keepdims=True)
        pick = block == first
        sel = jnp.logical_or(sel, jnp.logical_and(pick, valid))
        gate = jnp.where(pick, -jnp.inf, gate)
    chosen = jnp.concatenate([jnp.where(sel, 0.0, NEG), jnp.full((LANES - nbp, S), NEG, f32)], axis=0)
    qaug_scr[:, :HEAD_DIM] = q_ref[0]
    qaug_scr[:, HEAD_DIM:] = jnp.transpose(chosen).astype(bf16)

    for j in range(S // t):
        rows = slice(j * t, (j + 1) * t)
        qa = qaug_scr[rows, :]
        carry = _softmax_step2(_softmax_init(t), _qk(qa, kaug_scr[rows, :]) + bias_scr[1], vo_scr[rows, :])
        if j > 0:
            keys = slice((j - 1) * t, j * t)
            carry = _softmax_step2(carry, _qk(qa, kaug_scr[keys, :]) + bias_scr[0], vo_scr[keys, :])
        for c in range(j - 1):
            keys = slice(c * t, (c + 1) * t)
            carry = _softmax_step2(carry, _qk(qa, kaug_scr[keys, :]), vo_scr[keys, :])
        o_ref[rows, :] = _softmax_result(carry).astype(o_ref.dtype)


def _moba_attn(qkv, rel_bias, B, S, slab0):
    T = B * S
    H = N_MOBA_HEADS
    blk = MOBA_BLOCK
    assert S % (2 * blk) == 0 and S // blk <= LANES
    nb = S // blk
    nbp = -(-nb // SUBLANES) * SUBLANES
    avg = np.zeros((nbp, S), np.float32)
    hot = np.zeros((S, LANES), np.float32)
    for n in range(nb):
        avg[n, n * blk:(n + 1) * blk] = 1.0 / blk
        hot[n * blk:(n + 1) * blk, n] = 1.0
    table = _rel_bucket_table(2 * blk)
    dist = np.arange(blk)[:, None] - np.arange(blk)[None, :]
    buckets = np.stack([table[np.maximum(dist, 0)], table[dist + blk]]).astype(np.int32)
    grid_spec = pltpu.PrefetchScalarGridSpec(
        num_scalar_prefetch=1,
        grid=(H, B),
        in_specs=[
            pl.BlockSpec((1, S, HEAD_DIM), lambda h, b, r: (slab0 + h, b, 0)),
            pl.BlockSpec((1, S, HEAD_DIM), lambda h, b, r: (slab0 + H + h, b, 0)),
            pl.BlockSpec((1, S, HEAD_DIM), lambda h, b, r: (slab0 + 2 * H + h, b, 0)),
            pl.BlockSpec((nbp, S), lambda h, b, r: (0, 0)),
            pl.BlockSpec((S, LANES), lambda h, b, r: (0, 0)),
            pl.BlockSpec((2, blk, blk), lambda h, b, r: (0, 0, 0)),
        ],
        out_specs=pl.BlockSpec((S, HEAD_DIM), lambda h, b, r: (b, h)),
        scratch_shapes=[pltpu.VMEM((2, 2 * blk, 2 * blk), f32)] + [pltpu.VMEM((S, 2 * HEAD_DIM), bf16)] * 3,
    )
    return pl.pallas_call(
        _moba_attn_kernel,
        grid_spec=grid_spec,
        out_shape=jax.ShapeDtypeStruct((T, H * HEAD_DIM), bf16),
        compiler_params=_params("arbitrary", "arbitrary"),
        name="moba_attn",
    )(rel_bias, qkv, qkv, qkv, jnp.asarray(avg, bf16), jnp.asarray(hot, bf16), jnp.asarray(buckets))


def _rms(x, g):
    return x * lax.rsqrt(jnp.mean(x * x, axis=-1, keepdims=True) + EPS) * g


def _first_max(x, lane):
    mx = jnp.max(x, axis=-1, keepdims=True)
    return mx, jnp.min(jnp.where(x == mx, lane, LANES), axis=-1, keepdims=True)


def _route(logits, lane, grp=None):
    gl = jnp.where(lane < N_GROUPS, logits, -jnp.inf)
    gmax, top = _first_max(gl, lane)
    grp = top if grp is None else grp
    pg = jnp.exp(gl - gmax)
    p_top_group = jnp.sum(jnp.where(lane == grp, pg, 0.0), axis=-1, keepdims=True) / jnp.sum(
        pg, axis=-1, keepdims=True)
    lo = N_GROUPS + grp * EXPERTS_PER_GROUP
    in_grp = jnp.logical_and(lane >= lo, lane < lo + EXPERTS_PER_GROUP)
    el = jnp.where(in_grp, logits, -jnp.inf)
    pe = jnp.exp(el - jnp.max(el, axis=-1, keepdims=True))
    p_exp = jnp.where(in_grp, pe / jnp.sum(pe, axis=-1, keepdims=True), -1.0)
    p1, i1 = _first_max(p_exp, lane)
    p2, i2 = _first_max(jnp.where(lane == i1, -1.0, p_exp), lane)
    tot = p1 + p2
    combine = jnp.where(lane == i1, p1 / tot, jnp.where(lane == i2, p2 / tot, 0.0)) * p_top_group
    return grp, jnp.minimum(i1, i2) - lo, jnp.maximum(i1, i2) - lo, combine


def _out_proj_kernel(fo_ref, mo_ref, x_ref, gf_ref, gm_ref, w_ref, gffn_ref, wr_ref, br_ref,
                     x1_ref, route_ref, count_ref, count_scr, x1_scr):
    i = pl.program_id(0)

    @pl.when(i == 0)
    def _():
        count_scr[...] = jnp.zeros_like(count_scr)
        x1_scr[1] = jnp.zeros(x1_scr.shape[1:], x1_scr.dtype)

    tm, D = x_ref.shape
    chunk = D // OUT_PROJ_CHUNKS

    fo = _rms(fo_ref[...].astype(f32), gf_ref[...]).astype(bf16)
    mo = _rms(mo_ref[...].astype(f32), gm_ref[...]).astype(bf16)
    mix = jnp.concatenate([fo, mo], axis=-1)

    def project(c):
        cols = slice(c * chunk, (c + 1) * chunk)
        x1_scr[i % 2, :, cols] = x_ref[:, cols] + jnp.dot(mix, w_ref[:, cols], preferred_element_type=f32)

    x1 = x1_scr[(i + 1) % 2]
    h2 = _rms(x1, gffn_ref[...]).astype(bf16)
    logits = jnp.dot(h2, wr_ref[...], preferred_element_type=f32) + br_ref[...]
    project(0)
    lane = lax.broadcasted_iota(jnp.int32, logits.shape, 1)
    grp, ea, eb, combine = _route(logits, lane)
    x1_ref[:, :D] = x1
    x1_ref[:, D:] = combine
    project(1)
    pair = (ea * (2 * EXPERTS_PER_GROUP - 1 - ea)) // 2 + eb - ea - 1
    cls = grp * EXPERT_PAIRS + pair
    onehot = jnp.where(lane == cls, 1.0, 0.0)
    earlier = lax.broadcasted_iota(jnp.int32, (tm, tm), 1) < lax.broadcasted_iota(jnp.int32, (tm, tm), 0)
    counts = count_scr[...]
    before = jnp.dot(jnp.where(earlier, 1.0, 0.0).astype(bf16), onehot.astype(bf16),
                     preferred_element_type=f32) + counts
    rank = jnp.sum(onehot * before, axis=-1, keepdims=True).astype(jnp.int32)
    route_ref[...] = jnp.where(lane == 0, cls, jnp.where(lane == 1, rank, 0))
    counts = counts + jnp.where(i > 0, jnp.sum(onehot, axis=0, keepdims=True), 0.0)
    count_scr[...] = counts
    count_ref[...] = counts.astype(jnp.int32)
    for c in range(2, OUT_PROJ_CHUNKS):
        project(c)


def _out_proj(fo, mo, x2, gf, gm, w_out, g_ffn, w_r, b_r):
    T, D = x2.shape
    Wf = fo.shape[1]
    Wm = mo.shape[1]
    tm = _pick(T, 512)
    n = T // tm
    const = lambda i: (0, 0)
    rows_in = lambda i: (jnp.minimum(i, n - 1), 0)
    rows = lambda i: (jnp.maximum(i - 1, 0), 0)
    return pl.pallas_call(
        _out_proj_kernel,
        grid=(n + 1,),
        in_specs=[
            pl.BlockSpec((tm, Wf), rows_in),
            pl.BlockSpec((tm, Wm), rows_in),
            pl.BlockSpec((tm, D), rows_in),
            pl.BlockSpec((1, Wf), const),
            pl.BlockSpec((1, Wm), const),
            pl.BlockSpec((Wf + Wm, D), const),
            pl.BlockSpec((1, D), const),
            pl.BlockSpec((D, ROUTER_COLS), const),
            pl.BlockSpec((1, ROUTER_COLS), const),
        ],
        out_specs=[
            pl.BlockSpec((tm, D + ROUTER_COLS), rows),
            pl.BlockSpec((tm, ROUTER_COLS), rows),
            pl.BlockSpec((1, ROUTER_COLS), const),
        ],
        out_shape=[
            jax.ShapeDtypeStruct((T, D + ROUTER_COLS), f32),
            jax.ShapeDtypeStruct((T, ROUTER_COLS), jnp.int32),
            jax.ShapeDtypeStruct((1, ROUTER_COLS), jnp.int32),
        ],
        scratch_shapes=[pltpu.VMEM((1, ROUTER_COLS), f32), pltpu.VMEM((2, tm, D), f32)],
        compiler_params=_params("arbitrary"),
        name="out_proj",
    )(fo, mo, x2, gf, gm, w_out, g_ffn, w_r, b_r)


def _start_rows(row_copy, n):
    assert n % DMA_UNROLL == 0

    def body(g, carry):
        for k in range(DMA_UNROLL):
            row_copy(g * DMA_UNROLL + k).start()
        return carry

    lax.fori_loop(0, n // DMA_UNROLL, body, 0)


def _slot(cls_ref, rank_ref, base_ref, t):
    return base_ref[cls_ref[t]] + rank_ref[t]


def _dispatch_kernel(cls_ref, rank_ref, base_ref, fill_ref, x_ref, xs_ref, zero_scr, sem, zsem):
    i = pl.program_id(0)
    tm = x_ref.shape[0]

    zrows = zero_scr.shape[0]

    def row_copy(r):
        d = _slot(cls_ref, rank_ref, base_ref, i * tm + r)
        return pltpu.make_async_copy(x_ref.at[pl.ds(r, 1)], xs_ref.at[pl.ds(d, 1)], sem)

    def zero_row(r):
        return pltpu.make_async_copy(zero_scr.at[pl.ds(0, 1)], xs_ref.at[pl.ds(r, 1)], zsem)

    def zero_sublanes(c):
        dst = xs_ref.at[pl.ds(pl.multiple_of(c * SUBLANES, SUBLANES), SUBLANES)]
        return pltpu.make_async_copy(zero_scr.at[pl.ds(0, SUBLANES)], dst, zsem)

    def zero_chunk(c):
        return pltpu.make_async_copy(zero_scr, xs_ref.at[pl.ds(pl.multiple_of(c * zrows, zrows), zrows)], zsem)

    def start(copy):
        def body(r, carry):
            copy(r).start()
            return carry
        return body

    def wait(copy):
        def body(r, carry):
            copy(r).wait()
            return carry
        return body

    _start_rows(row_copy, tm)

    def zero_fill(act):
        def fill_class(c, carry):
            lo, hi = fill_ref[2 * c], fill_ref[2 * c + 1]
            mid = jnp.minimum((lo + SUBLANES - 1) // SUBLANES * SUBLANES, hi)
            lax.fori_loop(lo, mid, act(zero_row), 0)
            lax.fori_loop(mid // SUBLANES, hi // SUBLANES, act(zero_sublanes), 0)
            return carry

        lax.fori_loop(0, N_CLASSES, fill_class, 0)
        lo, hi = fill_ref[2 * N_CLASSES] // zrows, fill_ref[2 * N_CLASSES + 1] // zrows
        lax.fori_loop(lo, hi, act(zero_chunk), 0)

    @pl.when(i == 0)
    def _():
        zero_scr[...] = jnp.zeros_like(zero_scr)
        zero_fill(start)

    pltpu.make_async_copy(x_ref, xs_ref.at[pl.ds(0, tm)], sem).wait()

    @pl.when(i == 0)
    def _():
        zero_fill(wait)


def _dispatch(x1, cls, rank, base, fill, rows_out):
    T, D = x1.shape
    tm = _pick(T, 1024)
    grid_spec = pltpu.PrefetchScalarGridSpec(
        num_scalar_prefetch=4,
        grid=(T // tm,),
        in_specs=[pl.BlockSpec((tm, D), lambda i, *_: (i, 0))],
        out_specs=pl.BlockSpec(memory_space=pl.ANY),
        scratch_shapes=[pltpu.VMEM((ZERO_ROWS, D), f32), pltpu.SemaphoreType.DMA, pltpu.SemaphoreType.DMA],
    )
    return pl.pallas_call(
        _dispatch_kernel,
        grid_spec=grid_spec,
        out_shape=jax.ShapeDtypeStruct((rows_out, D), f32),
        compiler_params=_params("arbitrary"),
        name="moe_dispatch",
    )(cls, rank, base, fill, x1)


def _collect_kernel(cls_ref, rank_ref, base_ref, ys_ref, o_ref, sem):
    i = pl.program_id(0)
    tm = o_ref.shape[0]

    def row_copy(r):
        d = _slot(cls_ref, rank_ref, base_ref, i * tm + r)
        return pltpu.make_async_copy(ys_ref.at[pl.ds(d, 1)], o_ref.at[pl.ds(r, 1)], sem)

    _start_rows(row_copy, tm)
    pltpu.make_async_copy(ys_ref.at[pl.ds(0, tm)], o_ref, sem).wait()


def _collect(ys, cls, rank, base, T):
    D = ys.shape[1]
    tm = _pick(T, 1024)
    grid_spec = pltpu.PrefetchScalarGridSpec(
        num_scalar_prefetch=3,
        grid=(T // tm,),
        in_specs=[pl.BlockSpec(memory_space=pl.ANY)],
        out_specs=pl.BlockSpec((tm, D), lambda i, *_: (i, 0)),
        scratch_shapes=[pltpu.SemaphoreType.DMA],
    )
    return pl.pallas_call(
        _collect_kernel,
        grid_spec=grid_spec,
        out_shape=jax.ShapeDtypeStruct((T, D), f32),
        compiler_params=_params("arbitrary"),
        name="moe_collect",
    )(cls, rank, base, ys)


def _moe_kernel(ea_ref, eb_ref, nu_ref, xs_ref, gffn_ref, wga_ref, wua_ref, wda_ref,
                wgb_ref, wub_ref, wdb_ref, gfin_ref, o_ref):
    i = pl.program_id(0)
    D = o_ref.shape[1]

    @pl.when(i < nu_ref[0])
    def _():
        x = xs_ref[:, :D]
        combine = xs_ref[:, D:]
        h = _rms(x, gffn_ref[...]).astype(bf16)
        lane = lax.broadcasted_iota(jnp.int32, combine.shape, 1)
        y = jnp.zeros(x.shape, f32)
        for e_ref, wg_ref, wu_ref, wd_ref in ((ea_ref, wga_ref, wua_ref, wda_ref),
                                               (eb_ref, wgb_ref, wub_ref, wdb_ref)):
            c = jnp.sum(jnp.where(lane == N_GROUPS + e_ref[i], combine, 0.0), axis=-1, keepdims=True)
            gate = jnp.dot(h, wg_ref[0], preferred_element_type=f32)
            up = jnp.dot(h, wu_ref[0], preferred_element_type=f32)
            hid = (jax.nn.silu(gate) * up * c).astype(bf16)
            y = y + jnp.dot(hid, wd_ref[0], preferred_element_type=f32)
        o_ref[...] = _rms(x + y, gfin_ref[...])

    @pl.when(i >= nu_ref[0])
    def _():
        o_ref[...] = jnp.zeros_like(o_ref)


def _moe(xs, tile_ea, tile_eb, n_used, g_ffn, w_gate, w_up, w_down, g_final, ts):
    R = xs.shape[0]
    E, D, F = w_gate.shape
    const = lambda i, ea, eb, nu: (0, 0)
    rows = lambda i, ea, eb, nu: (jnp.minimum(i, nu[0] - 1), 0)
    first = lambda i, ea, eb, nu: (ea[i], 0, 0)
    second = lambda i, ea, eb, nu: (eb[i], 0, 0)
    grid_spec = pltpu.PrefetchScalarGridSpec(
        num_scalar_prefetch=3,
        grid=(R // ts,),
        in_specs=[
            pl.BlockSpec((ts, D + ROUTER_COLS), rows),
            pl.BlockSpec((1, D), const),
            pl.BlockSpec((1, D, F), first),
            pl.BlockSpec((1, D, F), first),
            pl.BlockSpec((1, F, D), first),
            pl.BlockSpec((1, D, F), second),
            pl.BlockSpec((1, D, F), second),
            pl.BlockSpec((1, F, D), second),
            pl.BlockSpec((1, D), const),
        ],
        out_specs=pl.BlockSpec((ts, D), lambda i, ea, eb, nu: (i, 0)),
    )
    return pl.pallas_call(
        _moe_kernel,
        grid_spec=grid_spec,
        out_shape=jax.ShapeDtypeStruct((R, D), f32),
        compiler_params=_params("arbitrary"),
        name="moe",
    )(tile_ea, tile_eb, n_used, xs, g_ffn, w_gate, w_up, w_down, w_gate, w_up, w_down, g_final)


def kernel(x, attn_norm_g, w_in, b_forget, fox_out_norm_g, moba_out_norm_g, rel_bias, w_out, ffn_norm_g,
           w_group_router, b_group_router, w_expert_router, b_expert_router, w_gate, w_up, w_down,
           final_norm_g):
    B, S, D = x.shape
    T = B * S
    depth = w_in.shape[0]
    fox_w = N_FOX_HEADS * HEAD_DIM
    moba_w = N_MOBA_HEADS * HEAD_DIM
    qkv_w = 3 * (fox_w + moba_w)
    assert w_in.shape[2] == qkv_w + N_FOX_HEADS
    assert N_GROUPS + N_EXPERTS <= ROUTER_COLS

    x2 = x.reshape(T, D)
    out = None
    for l in range(depth):
        w_in_t = jnp.swapaxes(w_in, 1, 2).astype(bf16)
        w_fg_t = w_in_t[l, qkv_w:, :]
        assert fox_w == moba_w
        col_scale = np.ones((qkv_w // fox_w,), np.float32)
        col_scale[[0, 3]] = HEAD_DIM ** -0.5 * LOG2E
        qkv, f_t = _in_proj(x2, attn_norm_g[l].reshape(1, D), w_in_t, l, w_fg_t, jnp.asarray(col_scale),
                            fox_w)
        key_bias = _fox_decay(f_t, b_forget[l].reshape(N_FOX_HEADS, 1), B, S)
        fo = _fox_attn(qkv, key_bias, B, S, 0)
        mo = _moba_attn(qkv, rel_bias, B, S, 3 * N_FOX_HEADS)

        w_r = jnp.concatenate([w_group_router[l], w_expert_router[l].reshape(D, N_EXPERTS)], axis=1)
        w_r = jnp.pad(w_r, ((0, 0), (0, ROUTER_COLS - w_r.shape[1]))).astype(bf16)
        b_r = jnp.concatenate([b_group_router[l], b_expert_router[l].reshape(N_EXPERTS)])
        b_r = jnp.pad(b_r, (0, ROUTER_COLS - b_r.shape[0])).reshape(1, ROUTER_COLS)
        g_ffn = ffn_norm_g[l].reshape(1, D)
        x1, route, count = _out_proj(fo, mo, x2, fox_out_norm_g[l].reshape(1, fox_w),
                                     moba_out_norm_g[l].reshape(1, moba_w), w_out[l].astype(bf16),
                                     g_ffn, w_r, b_r)
        ts = MOE_TILE
        cls, rank = route[:, 0], route[:, 1]
        counts = count[0, :N_CLASSES]
        padded = (counts + ts - 1) // ts * ts
        ends = jnp.cumsum(padded)
        base = ends - padded
        rows_out = (T // ts + N_CLASSES) * ts
        fill = jnp.concatenate([jnp.stack([base + counts, ends], axis=1).reshape(-1),
                                jnp.stack([ends[-1], jnp.asarray(rows_out, ends.dtype)])])
        tile_start = jnp.arange(rows_out // ts, dtype=jnp.int32) * ts
        tile_cls = jnp.minimum(jnp.sum(tile_start[:, None] >= ends[None, :], axis=1), N_CLASSES - 1)
        pairs = [(a, b) for a in range(EXPERTS_PER_GROUP) for b in range(a + 1, EXPERTS_PER_GROUP)]
        first_of_cls = np.array([g * EXPERTS_PER_GROUP + a for g in range(N_GROUPS) for a, _ in pairs], np.int32)
        second_of_cls = np.array([g * EXPERTS_PER_GROUP + b for g in range(N_GROUPS) for _, b in pairs], np.int32)
        tile_ea = jnp.asarray(first_of_cls)[tile_cls]
        tile_eb = jnp.asarray(second_of_cls)[tile_cls]
        n_used = (ends[-1:] // ts).astype(jnp.int32)

        assert l == depth - 1, "the fused residual + final norm epilogue expects a single layer"
        xs = _dispatch(x1, cls, rank, base, fill, rows_out)
        ys = _moe(xs, tile_ea, tile_eb, n_used, g_ffn, w_gate[l].astype(bf16),
                  w_up[l].astype(bf16), w_down[l].astype(bf16), final_norm_g.reshape(1, D), ts)
        out = _collect(ys, cls, rank, base, T)
        x2 = out
    return out.reshape(B, S, D)
```

```python
import functools
import math

import numpy as np
import jax
import jax.numpy as jnp
from jax import lax
from jax.experimental import pallas as pl
from jax.experimental.pallas import tpu as pltpu

HEAD_DIM = 128
N_FOX_HEADS = 8
N_MOBA_HEADS = 8
MOBA_BLOCK = 256
MOBA_TOPK = 3
REL_BUCKETS = 32
REL_MAX_DIST = 128
N_GROUPS = 4
EXPERTS_PER_GROUP = 4
N_EXPERTS = N_GROUPS * EXPERTS_PER_GROUP
EXPERT_PAIRS = EXPERTS_PER_GROUP * (EXPERTS_PER_GROUP - 1) // 2
N_CLASSES = N_GROUPS * EXPERT_PAIRS
EPS = 1e-6
NEG = -1e30
LOG2E = math.log2(math.e)
LANES = 128
SUBLANES = 8
ROUTER_COLS = LANES
OUT_PROJ_CHUNKS = 4
MOE_TILE = 256
KEY_BIAS_PARTS = 3
ZERO_ROWS = 64
ROW_COPY_TILE = 2048
DMA_UNROLL = 8
VMEM_LIMIT = 56 * 1024 * 1024

f32 = jnp.float32
bf16 = jnp.bfloat16


def _params(*sem):
    return pltpu.CompilerParams(dimension_semantics=sem, vmem_limit_bytes=VMEM_LIMIT)


def _pick(n, pref):
    t = min(n, pref)
    assert n % t == 0, (n, pref)
    return t


def _in_proj_kernel(cs_ref, x_ref, g_ref, w_ref, wfg_ref, qkv_ref, f_ref, h_scr):
    @pl.when(pl.program_id(1) == 0)
    def _():
        x = x_ref[...]
        y = x * lax.rsqrt(jnp.mean(x * x, axis=-1, keepdims=True) + EPS)
        hb = (y * g_ref[...]).astype(bf16)
        h_scr[...] = hb
        f_ref[...] = lax.dot_general(wfg_ref[...], hb, (((1,), (1,)), ((), ())),
                                     preferred_element_type=f32)

    acc = lax.dot_general(h_scr[...], w_ref[...], (((1,), (1,)), ((), ())),
                          preferred_element_type=f32) * cs_ref[pl.program_id(1)]
    for c in range(acc.shape[1] // HEAD_DIM):
        qkv_ref[c] = acc[:, c * HEAD_DIM:(c + 1) * HEAD_DIM].astype(bf16)


def _in_proj(x2, g, w_in_t, layer, w_fg_t, col_scale, tn):
    T, D = x2.shape
    N = col_scale.shape[0] * tn
    tm = _pick(T, 1024)
    assert N <= w_in_t.shape[1]
    nh = w_fg_t.shape[0]
    return pl.pallas_call(
        _in_proj_kernel,
        grid=(T // tm, N // tn),
        in_specs=[
            pl.BlockSpec(memory_space=pltpu.SMEM),
            pl.BlockSpec((tm, D), lambda i, j: (i, 0)),
            pl.BlockSpec((1, D), lambda i, j: (0, 0)),
            pl.BlockSpec((None, tn, D), lambda i, j: (layer, j, 0)),
            pl.BlockSpec((nh, D), lambda i, j: (0, 0)),
        ],
        out_specs=[
            pl.BlockSpec((tn // HEAD_DIM, tm, HEAD_DIM), lambda i, j: (j, i, 0)),
            pl.BlockSpec((nh, tm), lambda i, j: (0, i)),
        ],
        out_shape=[
            jax.ShapeDtypeStruct((N // HEAD_DIM, T, HEAD_DIM), bf16),
            jax.ShapeDtypeStruct((nh, T), f32),
        ],
        scratch_shapes=[pltpu.VMEM((tm, D), bf16)],
        compiler_params=_params("arbitrary", "arbitrary"),
        name="in_proj",
    )(col_scale, x2, g, w_in_t, w_fg_t)


def _fox_decay_kernel(f_ref, b_ref, o_ref):
    lf = jax.nn.log_sigmoid(f_ref[...] + b_ref[...])
    S = lf.shape[1]
    lane = lax.broadcasted_iota(jnp.int32, lf.shape, 1)
    c = lf
    sh = 1
    while sh < S:
        c = c + jnp.where(lane >= sh, pltpu.roll(c, sh, axis=1), 0.0)
        sh *= 2
    rest = -c * LOG2E
    parts = []
    for _ in range(KEY_BIAS_PARTS):
        part = rest.astype(bf16).astype(f32)
        parts.append(part)
        rest = rest - part
    nh = lf.shape[0]
    cols = jnp.concatenate(parts + [jnp.zeros((LANES - KEY_BIAS_PARTS * nh, S), f32)], axis=0)
    o_ref[0] = jnp.transpose(cols).astype(bf16)


def _fox_decay(f_t, b_forget, B, S):
    nh = f_t.shape[0]
    assert KEY_BIAS_PARTS * nh <= LANES
    return pl.pallas_call(
        _fox_decay_kernel,
        grid=(B,),
        in_specs=[
            pl.BlockSpec((nh, S), lambda b: (0, b)),
            pl.BlockSpec((nh, 1), lambda b: (0, 0)),
        ],
        out_specs=pl.BlockSpec((1, S, LANES), lambda b: (b, 0, 0)),
        out_shape=jax.ShapeDtypeStruct((B, S, LANES), bf16),
        compiler_params=_params("arbitrary"),
        name="fox_decay",
    )(f_t, b_forget)


def _qk(q, k):
    return lax.dot_general(q, k, (((1,), (1,)), ((), ())), preferred_element_type=f32)


def _softmax_init(t):
    return jnp.full((t, 1), NEG, f32), jnp.zeros((t, 2 * HEAD_DIM), f32)


def _softmax_step2(carry, s2, v_ones):
    m, acc = carry
    m_new = jnp.maximum(m, jnp.max(s2, axis=-1, keepdims=True))
    p = jnp.exp2(s2 - m_new).astype(bf16)
    acc = jnp.exp2(m - m_new) * acc + jnp.dot(p, v_ones, preferred_element_type=f32)
    return m_new, acc


def _softmax_result(carry):
    _, acc = carry
    return acc[:, :HEAD_DIM] / acc[:, HEAD_DIM:]


def _fox_attn_kernel(q_ref, k_ref, v_ref, kb_ref, o_ref, qaug_scr, kaug_scr, vo_scr, *, t):
    h = pl.program_id(1)
    S = q_ref.shape[1]

    @pl.when(h == 0)
    def _():
        kaug_scr[:, HEAD_DIM:] = kb_ref[0]
        vo_scr[:, HEAD_DIM:] = jnp.ones((S, HEAD_DIM), vo_scr.dtype)

    lane = lax.broadcasted_iota(jnp.int32, (1, LANES), 1)
    mine = jnp.logical_and(lane % N_FOX_HEADS == h, lane < KEY_BIAS_PARTS * N_FOX_HEADS)
    qaug_scr[:, :HEAD_DIM] = q_ref[0]
    qaug_scr[:, HEAD_DIM:] = jnp.broadcast_to(jnp.where(mine, 1.0, 0.0), (S, LANES)).astype(bf16)
    kaug_scr[:, :HEAD_DIM] = k_ref[0]
    vo_scr[:, :HEAD_DIM] = v_ref[0]
    row = lax.broadcasted_iota(jnp.int32, (t, t), 0)
    col = lax.broadcasted_iota(jnp.int32, (t, t), 1)
    for qi in range(S // t):
        qa = qaug_scr[qi * t:(qi + 1) * t, :]
        carry = _softmax_init(t)
        for kb in range(qi + 1):
            keys = slice(kb * t, (kb + 1) * t)
            s2 = _qk(qa, kaug_scr[keys, :])
            if kb == qi:
                s2 = jnp.where(col <= row, s2, NEG)
            carry = _softmax_step2(carry, s2, vo_scr[keys, :])
        o_ref[qi * t:(qi + 1) * t, :] = _softmax_result(carry).astype(o_ref.dtype)


def _fox_attn(qkv, key_bias, B, S, slab0):
    T = B * S
    H = N_FOX_HEADS
    t = _pick(S, 512)
    kern = functools.partial(_fox_attn_kernel, t=t)
    return pl.pallas_call(
        kern,
        grid=(B, H),
        in_specs=[
            pl.BlockSpec((1, S, HEAD_DIM), lambda b, h: (slab0 + h, b, 0)),
            pl.BlockSpec((1, S, HEAD_DIM), lambda b, h: (slab0 + H + h, b, 0)),
            pl.BlockSpec((1, S, HEAD_DIM), lambda b, h: (slab0 + 2 * H + h, b, 0)),
            pl.BlockSpec((1, S, LANES), lambda b, h: (b, 0, 0)),
        ],
        out_specs=pl.BlockSpec((S, HEAD_DIM), lambda b, h: (b, h)),
        out_shape=jax.ShapeDtypeStruct((T, H * HEAD_DIM), bf16),
        scratch_shapes=[pltpu.VMEM((S, 2 * HEAD_DIM), bf16)] * 3,
        compiler_params=_params("arbitrary", "arbitrary"),
        name="fox_attn",
    )(qkv, qkv, qkv, key_bias)


def _rel_bucket_table(n):
    max_exact = REL_BUCKETS // 2
    d = np.arange(n)
    ratio = np.log(np.maximum(d, 1).astype(np.float32) / np.float32(max_exact)) / np.float32(
        math.log(REL_MAX_DIST / max_exact))
    large = max_exact + (ratio * np.float32(REL_BUCKETS - max_exact)).astype(np.int32)
    large = np.minimum(large, REL_BUCKETS - 1)
    return np.where(d < max_exact, d, large).astype(np.int32)


def _moba_attn_kernel(rel_ref, q_ref, k_ref, v_ref, avg_ref, hot_ref, bk_ref, o_ref, bias_scr, kaug_scr,
                      qaug_scr, vo_scr):
    h = pl.program_id(0)
    blk = MOBA_BLOCK
    t = 2 * blk
    S = q_ref.shape[1]

    @pl.when(pl.program_id(1) == 0)
    def _():
        far_bias = rel_ref[h, REL_BUCKETS - 1]
        row = lax.broadcasted_iota(jnp.int32, (blk, blk), 0)
        col = lax.broadcasted_iota(jnp.int32, (blk, blk), 1)
        tiles = []
        for which in range(2):
            bk = bk_ref[which]
            tile = jnp.zeros((blk, blk), f32)
            for b in range(REL_BUCKETS - 1):
                tile = jnp.where(bk == b, (rel_ref[h, b] - far_bias) * LOG2E, tile)
            tiles.append(jnp.where(col <= row, tile, NEG) if which == 0 else tile)
        own_tile, prev_tile = tiles
        zero_tile = jnp.zeros((blk, blk), f32)
        for which, quads in enumerate((((zero_tile, prev_tile), (zero_tile, zero_tile)),
                                       ((own_tile, zero_tile), (prev_tile, own_tile)))):
            for r in range(2):
                for c in range(2):
                    bias_scr[which, r * blk:(r + 1) * blk, c * blk:(c + 1) * blk] = quads[r][c]
        kaug_scr[:, HEAD_DIM:] = hot_ref[...]
        vo_scr[:, HEAD_DIM:] = jnp.ones((S, HEAD_DIM), vo_scr.dtype)

    kaug_scr[:, :HEAD_DIM] = k_ref[0]
    vo_scr[:, :HEAD_DIM] = v_ref[0]
    kmean = jnp.dot(avg_ref[...], k_ref[0], preferred_element_type=f32).astype(bf16)

    nbp = avg_ref.shape[0]
    block = lax.broadcasted_iota(jnp.int32, (nbp, S), 0)
    own = lax.broadcasted_iota(jnp.int32, (nbp, S), 1) // blk
    valid = block < own
    gate = jnp.where(valid, _qk(kmean, q_ref[0]), NEG)
    sel = block == own
    for _ in range(MOBA_TOPK):
        mx = jnp.max(gate, axis=0, keepdims=True)
        first = jnp.min(jnp.where(gate == mx, block, nbp), axis=0, keepdims=True)
        pick = block == first
        sel = jnp.logical_or(sel, jnp.logical_and(pick, valid))
        gate = jnp.where(pick, -jnp.inf, gate)
    chosen = jnp.concatenate([jnp.where(sel, 0.0, NEG), jnp.full((LANES - nbp, S), NEG, f32)], axis=0)
    qaug_scr[:, :HEAD_DIM] = q_ref[0]
    qaug_scr[:, HEAD_DIM:] = jnp.transpose(chosen).astype(bf16)

    for j in range(S // t):
        rows = slice(j * t, (j + 1) * t)
        qa = qaug_scr[rows, :]
        carry = _softmax_step2(_softmax_init(t), _qk(qa, kaug_scr[rows, :]) + bias_scr[1], vo_scr[rows, :])
        if j > 0:
            keys = slice((j - 1) * t, j * t)
            carry = _softmax_step2(carry, _qk(qa, kaug_scr[keys, :]) + bias_scr[0], vo_scr[keys, :])
        for c in range(j - 1):
            keys = slice(c * t, (c + 1) * t)
            carry = _softmax_step2(carry, _qk(qa, kaug_scr[keys, :]), vo_scr[keys, :])
        o_ref[rows, :] = _softmax_result(carry).astype(o_ref.dtype)


def _moba_attn(qkv, rel_bias, B, S, slab0):
    T = B * S
    H = N_MOBA_HEADS
    blk = MOBA_BLOCK
    assert S % (2 * blk) == 0 and S // blk <= LANES
    nb = S // blk
    nbp = -(-nb // SUBLANES) * SUBLANES
    avg = np.zeros((nbp, S), np.float32)
    hot = np.zeros((S, LANES), np.float32)
    for n in range(nb):
        avg[n, n * blk:(n + 1) * blk] = 1.0 / blk
        hot[n * blk:(n + 1) * blk, n] = 1.0
    table = _rel_bucket_table(2 * blk)
    dist = np.arange(blk)[:, None] - np.arange(blk)[None, :]
    buckets = np.stack([table[np.maximum(dist, 0)], table[dist + blk]]).astype(np.int32)
    grid_spec = pltpu.PrefetchScalarGridSpec(
        num_scalar_prefetch=1,
        grid=(H, B),
        in_specs=[
            pl.BlockSpec((1, S, HEAD_DIM), lambda h, b, r: (slab0 + h, b, 0)),
            pl.BlockSpec((1, S, HEAD_DIM), lambda h, b, r: (slab0 + H + h, b, 0)),
            pl.BlockSpec((1, S, HEAD_DIM), lambda h, b, r: (slab0 + 2 * H + h, b, 0)),
            pl.BlockSpec((nbp, S), lambda h, b, r: (0, 0)),
            pl.BlockSpec((S, LANES), lambda h, b, r: (0, 0)),
            pl.BlockSpec((2, blk, blk), lambda h, b, r: (0, 0, 0)),
        ],
        out_specs=pl.BlockSpec((S, HEAD_DIM), lambda h, b, r: (b, h)),
        scratch_shapes=[pltpu.VMEM((2, 2 * blk, 2 * blk), f32)] + [pltpu.VMEM((S, 2 * HEAD_DIM), bf16)] * 3,
    )
    return pl.pallas_call(
        _moba_attn_kernel,
        grid_spec=grid_spec,
        out_shape=jax.ShapeDtypeStruct((T, H * HEAD_DIM), bf16),
        compiler_params=_params("arbitrary", "arbitrary"),
        name="moba_attn",
    )(rel_bias, qkv, qkv, qkv, jnp.asarray(avg, bf16), jnp.asarray(hot, bf16), jnp.asarray(buckets))


def _rms(x, g):
    return x * lax.rsqrt(jnp.mean(x * x, axis=-1, keepdims=True) + EPS) * g


def _first_max(x, lane):
    mx = jnp.max(x, axis=-1, keepdims=True)
    return mx, jnp.min(jnp.where(x == mx, lane, LANES), axis=-1, keepdims=True)


def _route(logits, lane, grp=None):
    gl = jnp.where(lane < N_GROUPS, logits, -jnp.inf)
    gmax, top = _first_max(gl, lane)
    grp = top if grp is None else grp
    pg = jnp.exp(gl - gmax)
    p_top_group = jnp.sum(jnp.where(lane == grp, pg, 0.0), axis=-1, keepdims=True) / jnp.sum(
        pg, axis=-1, keepdims=True)
    lo = N_GROUPS + grp * EXPERTS_PER_GROUP
    in_grp = jnp.logical_and(lane >= lo, lane < lo + EXPERTS_PER_GROUP)
    el = jnp.where(in_grp, logits, -jnp.inf)
    pe = jnp.exp(el - jnp.max(el, axis=-1, keepdims=True))
    p_exp = jnp.where(in_grp, pe / jnp.sum(pe, axis=-1, keepdims=True), -1.0)
    p1, i1 = _first_max(p_exp, lane)
    p2, i2 = _first_max(jnp.where(lane == i1, -1.0, p_exp), lane)
    tot = p1 + p2
    combine = jnp.where(lane == i1, p1 / tot, jnp.where(lane == i2, p2 / tot, 0.0)) * p_top_group
    return grp, jnp.minimum(i1, i2) - lo, jnp.maximum(i1, i2) - lo, combine


def _out_proj_kernel(fo_ref, mo_ref, x_ref, gf_ref, gm_ref, w_ref, gffn_ref, wr_ref, br_ref,
                     x1_ref, route_ref, count_ref, count_scr, x1_scr):
    i = pl.program_id(0)

    @pl.when(i == 0)
    def _():
        count_scr[...] = jnp.zeros_like(count_scr)
        x1_scr[1] = jnp.zeros(x1_scr.shape[1:], x1_scr.dtype)

    tm, D = x_ref.shape
    chunk = D // OUT_PROJ_CHUNKS

    fo = _rms(fo_ref[...].astype(f32), gf_ref[...]).astype(bf16)
    mo = _rms(mo_ref[...].astype(f32), gm_ref[...]).astype(bf16)
    mix = jnp.concatenate([fo, mo], axis=-1)

    def project(c):
        cols = slice(c * chunk, (c + 1) * chunk)
        x1_scr[i % 2, :, cols] = x_ref[:, cols] + jnp.dot(mix, w_ref[:, cols], preferred_element_type=f32)

    x1 = x1_scr[(i + 1) % 2]
    h2 = _rms(x1, gffn_ref[...]).astype(bf16)
    logits = jnp.dot(h2, wr_ref[...], preferred_element_type=f32) + br_ref[...]
    project(0)
    lane = lax.broadcasted_iota(jnp.int32, logits.shape, 1)
    grp, ea, eb, combine = _route(logits, lane)
    x1_ref[:, :D] = x1
    x1_ref[:, D:] = combine
    project(1)
    pair = (ea * (2 * EXPERTS_PER_GROUP - 1 - ea)) // 2 + eb - ea - 1
    cls = grp * EXPERT_PAIRS + pair
    onehot = jnp.where(lane == cls, 1.0, 0.0)
    earlier = lax.broadcasted_iota(jnp.int32, (tm, tm), 1) < lax.broadcasted_iota(jnp.int32, (tm, tm), 0)
    counts = count_scr[...]
    before = jnp.dot(jnp.where(earlier, 1.0, 0.0).astype(bf16), onehot.astype(bf16),
                     preferred_element_type=f32) + counts
    rank = jnp.sum(onehot * before, axis=-1, keepdims=True).astype(jnp.int32)
    route_ref[...] = jnp.where(lane == 0, cls, jnp.where(lane == 1, rank, 0))
    counts = counts + jnp.where(i > 0, jnp.sum(onehot, axis=0, keepdims=True), 0.0)
    count_scr[...] = counts
    count_ref[...] = counts.astype(jnp.int32)
    for c in range(2, OUT_PROJ_CHUNKS):
        project(c)


def _out_proj(fo, mo, x2, gf, gm, w_out, g_ffn, w_r, b_r):
    T, D = x2.shape
    Wf = fo.shape[1]
    Wm = mo.shape[1]
    tm = _pick(T, 512)
    n = T // tm
    const = lambda i: (0, 0)
    rows_in = lambda i: (jnp.minimum(i, n - 1), 0)
    rows = lambda i: (jnp.maximum(i - 1, 0), 0)
    return pl.pallas_call(
        _out_proj_kernel,
        grid=(n + 1,),
        in_specs=[
            pl.BlockSpec((tm, Wf), rows_in),
            pl.BlockSpec((tm, Wm), rows_in),
            pl.BlockSpec((tm, D), rows_in),
            pl.BlockSpec((1, Wf), const),
            pl.BlockSpec((1, Wm), const),
            pl.BlockSpec((Wf + Wm, D), const),
            pl.BlockSpec((1, D), const),
            pl.BlockSpec((D, ROUTER_COLS), const),
            pl.BlockSpec((1, ROUTER_COLS), const),
        ],
        out_specs=[
            pl.BlockSpec((tm, D + ROUTER_COLS), rows),
            pl.BlockSpec((tm, ROUTER_COLS), rows),
            pl.BlockSpec((1, ROUTER_COLS), const),
        ],
        out_shape=[
            jax.ShapeDtypeStruct((T, D + ROUTER_COLS), f32),
            jax.ShapeDtypeStruct((T, ROUTER_COLS), jnp.int32),
            jax.ShapeDtypeStruct((1, ROUTER_COLS), jnp.int32),
        ],
        scratch_shapes=[pltpu.VMEM((1, ROUTER_COLS), f32), pltpu.VMEM((2, tm, D), f32)],
        compiler_params=_params("arbitrary"),
        name="out_proj",
    )(fo, mo, x2, gf, gm, w_out, g_ffn, w_r, b_r)


def _start_rows(row_copy, n):
    assert n % DMA_UNROLL == 0

    def body(g, carry):
        for k in range(DMA_UNROLL):
            row_copy(g * DMA_UNROLL + k).start()
        return carry

    lax.fori_loop(0, n // DMA_UNROLL, body, 0)


def _slot(cls_ref, rank_ref, base_ref, t):
    return base_ref[cls_ref[t]] + rank_ref[t]


def _dispatch_kernel(cls_ref, rank_ref, base_ref, fill_ref, x_ref, xs_ref, zero_scr, sem, zsem):
    i = pl.program_id(0)
    tm = x_ref.shape[0]

    zrows = zero_scr.shape[0]

    def row_copy(r):
        d = _slot(cls_ref, rank_ref, base_ref, i * tm + r)
        return pltpu.make_async_copy(x_ref.at[pl.ds(r, 1)], xs_ref.at[pl.ds(d, 1)], sem)

    def zero_row(r):
        return pltpu.make_async_copy(zero_scr.at[pl.ds(0, 1)], xs_ref.at[pl.ds(r, 1)], zsem)

    def zero_sublanes(c):
        dst = xs_ref.at[pl.ds(pl.multiple_of(c * SUBLANES, SUBLANES), SUBLANES)]
        return pltpu.make_async_copy(zero_scr.at[pl.ds(0, SUBLANES)], dst, zsem)

    def zero_chunk(c):
        return pltpu.make_async_copy(zero_scr, xs_ref.at[pl.ds(pl.multiple_of(c * zrows, zrows), zrows)], zsem)

    def start(copy):
        def body(r, carry):
            copy(r).start()
            return carry
        return body

    def wait(copy):
        def body(r, carry):
            copy(r).wait()
            return carry
        return body

    _start_rows(row_copy, tm)

    def zero_fill(act):
        def fill_class(c, carry):
            lo, hi = fill_ref[2 * c], fill_ref[2 * c + 1]
            mid = jnp.minimum((lo + SUBLANES - 1) // SUBLANES * SUBLANES, hi)
            lax.fori_loop(lo, mid, act(zero_row), 0)
            lax.fori_loop(mid // SUBLANES, hi // SUBLANES, act(zero_sublanes), 0)
            return carry

        lax.fori_loop(0, N_CLASSES, fill_class, 0)
        lo, hi = fill_ref[2 * N_CLASSES] // zrows, fill_ref[2 * N_CLASSES + 1] // zrows
        lax.fori_loop(lo, hi, act(zero_chunk), 0)

    @pl.when(i == 0)
    def _():
        zero_scr[...] = jnp.zeros_like(zero_scr)
        zero_fill(start)

    pltpu.make_async_copy(x_ref, xs_ref.at[pl.ds(0, tm)], sem).wait()

    @pl.when(i == 0)
    def _():
        zero_fill(wait)


def _dispatch(x1, cls, rank, base, fill, rows_out):
    T, D = x1.shape
    tm = _pick(T, ROW_COPY_TILE)
    grid_spec = pltpu.PrefetchScalarGridSpec(
        num_scalar_prefetch=4,
        grid=(T // tm,),
        in_specs=[pl.BlockSpec((tm, D), lambda i, *_: (i, 0))],
        out_specs=pl.BlockSpec(memory_space=pl.ANY),
        scratch_shapes=[pltpu.VMEM((ZERO_ROWS, D), f32), pltpu.SemaphoreType.DMA, pltpu.SemaphoreType.DMA],
    )
    return pl.pallas_call(
        _dispatch_kernel,
        grid_spec=grid_spec,
        out_shape=jax.ShapeDtypeStruct((rows_out, D), f32),
        compiler_params=_params("arbitrary"),
        name="moe_dispatch",
    )(cls, rank, base, fill, x1)


def _collect_kernel(cls_ref, rank_ref, base_ref, ys_ref, o_ref, sem):
    i = pl.program_id(0)
    tm = o_ref.shape[0]

    def row_copy(r):
        d = _slot(cls_ref, rank_ref, base_ref, i * tm + r)
        return pltpu.make_async_copy(ys_ref.at[pl.ds(d, 1)], o_ref.at[pl.ds(r, 1)], sem)

    _start_rows(row_copy, tm)
    pltpu.make_async_copy(ys_ref.at[pl.ds(0, tm)], o_ref, sem).wait()


def _collect(ys, cls, rank, base, T):
    D = ys.shape[1]
    tm = _pick(T, ROW_COPY_TILE)
    grid_spec = pltpu.PrefetchScalarGridSpec(
        num_scalar_prefetch=3,
        grid=(T // tm,),
        in_specs=[pl.BlockSpec(memory_space=pl.ANY)],
        out_specs=pl.BlockSpec((tm, D), lambda i, *_: (i, 0)),
        scratch_shapes=[pltpu.SemaphoreType.DMA],
    )
    return pl.pallas_call(
        _collect_kernel,
        grid_spec=grid_spec,
        out_shape=jax.ShapeDtypeStruct((T, D), f32),
        compiler_params=_params("arbitrary"),
        name="moe_collect",
    )(cls, rank, base, ys)


def _moe_kernel(ea_ref, eb_ref, nu_ref, xs_ref, gffn_ref, wga_ref, wua_ref, wda_ref,
                wgb_ref, wub_ref, wdb_ref, gfin_ref, o_ref):
    i = pl.program_id(0)
    D = o_ref.shape[1]

    @pl.when(i < nu_ref[0])
    def _():
        x = xs_ref[:, :D]
        combine = xs_ref[:, D:]
        h = _rms(x, gffn_ref[...]).astype(bf16)
        lane = lax.broadcasted_iota(jnp.int32, combine.shape, 1)
        y = jnp.zeros(x.shape, f32)
        for e_ref, wg_ref, wu_ref, wd_ref in ((ea_ref, wga_ref, wua_ref, wda_ref),
                                               (eb_ref, wgb_ref, wub_ref, wdb_ref)):
            c = jnp.sum(jnp.where(lane == N_GROUPS + e_ref[i], combine, 0.0), axis=-1, keepdims=True)
            gate = jnp.dot(h, wg_ref[0], preferred_element_type=f32)
            up = jnp.dot(h, wu_ref[0], preferred_element_type=f32)
            hid = (jax.nn.silu(gate) * up * c).astype(bf16)
            y = y + jnp.dot(hid, wd_ref[0], preferred_element_type=f32)
        o_ref[...] = _rms(x + y, gfin_ref[...])

    @pl.when(i >= nu_ref[0])
    def _():
        o_ref[...] = jnp.zeros_like(o_ref)


def _moe(xs, tile_ea, tile_eb, n_used, g_ffn, w_gate, w_up, w_down, g_final, ts):
    R = xs.shape[0]
    E, D, F = w_gate.shape
    const = lambda i, ea, eb, nu: (0, 0)
    rows = lambda i, ea, eb, nu: (jnp.minimum(i, nu[0] - 1), 0)
    first = lambda i, ea, eb, nu: (ea[i], 0, 0)
    second = lambda i, ea, eb, nu: (eb[i], 0, 0)
    grid_spec = pltpu.PrefetchScalarGridSpec(
        num_scalar_prefetch=3,
        grid=(R // ts,),
        in_specs=[
            pl.BlockSpec((ts, D + ROUTER_COLS), rows),
            pl.BlockSpec((1, D), const),
            pl.BlockSpec((1, D, F), first),
            pl.BlockSpec((1, D, F), first),
            pl.BlockSpec((1, F, D), first),
            pl.BlockSpec((1, D, F), second),
            pl.BlockSpec((1, D, F), second),
            pl.BlockSpec((1, F, D), second),
            pl.BlockSpec((1, D), const),
        ],
        out_specs=pl.BlockSpec((ts, D), lambda i, ea, eb, nu: (i, 0)),
    )
    return pl.pallas_call(
        _moe_kernel,
        grid_spec=grid_spec,
        out_shape=jax.ShapeDtypeStruct((R, D), f32),
        compiler_params=_params("arbitrary"),
        name="moe",
    )(tile_ea, tile_eb, n_used, xs, g_ffn, w_gate, w_up, w_down, w_gate, w_up, w_down, g_final)


def kernel(x, attn_norm_g, w_in, b_forget, fox_out_norm_g, moba_out_norm_g, rel_bias, w_out, ffn_norm_g,
           w_group_router, b_group_router, w_expert_router, b_expert_router, w_gate, w_up, w_down,
           final_norm_g):
    B, S, D = x.shape
    T = B * S
    depth = w_in.shape[0]
    fox_w = N_FOX_HEADS * HEAD_DIM
    moba_w = N_MOBA_HEADS * HEAD_DIM
    qkv_w = 3 * (fox_w + moba_w)
    assert w_in.shape[2] == qkv_w + N_FOX_HEADS
    assert N_GROUPS + N_EXPERTS <= ROUTER_COLS

    x2 = x.reshape(T, D)
    out = None
    for l in range(depth):
        w_in_t = jnp.swapaxes(w_in, 1, 2).astype(bf16)
        w_fg_t = w_in_t[l, qkv_w:, :]
        assert fox_w == moba_w
        col_scale = np.ones((qkv_w // fox_w,), np.float32)
        col_scale[[0, 3]] = HEAD_DIM ** -0.5 * LOG2E
        qkv, f_t = _in_proj(x2, attn_norm_g[l].reshape(1, D), w_in_t, l, w_fg_t, jnp.asarray(col_scale),
                            fox_w)
        key_bias = _fox_decay(f_t, b_forget[l].reshape(N_FOX_HEADS, 1), B, S)
        fo = _fox_attn(qkv, key_bias, B, S, 0)
        mo = _moba_attn(qkv, rel_bias, B, S, 3 * N_FOX_HEADS)

        w_r = jnp.concatenate([w_group_router[l], w_expert_router[l].reshape(D, N_EXPERTS)], axis=1)
        w_r = jnp.pad(w_r, ((0, 0), (0, ROUTER_COLS - w_r.shape[1]))).astype(bf16)
        b_r = jnp.concatenate([b_group_router[l], b_expert_router[l].reshape(N_EXPERTS)])
        b_r = jnp.pad(b_r, (0, ROUTER_COLS - b_r.shape[0])).reshape(1, ROUTER_COLS)
        g_ffn = ffn_norm_g[l].reshape(1, D)
        x1, route, count = _out_proj(fo, mo, x2, fox_out_norm_g[l].reshape(1, fox_w),
                                     moba_out_norm_g[l].reshape(1, moba_w), w_out[l].astype(bf16),
                                     g_ffn, w_r, b_r)
        ts = MOE_TILE
        cls, rank = route[:, 0], route[:, 1]
        counts = count[0, :N_CLASSES]
        padded = (counts + ts - 1) // ts * ts
        ends = jnp.cumsum(padded)
        base = ends - padded
        rows_out = (T // ts + N_CLASSES) * ts
        fill = jnp.concatenate([jnp.stack([base + counts, ends], axis=1).reshape(-1),
                                jnp.stack([ends[-1], jnp.asarray(rows_out, ends.dtype)])])
        tile_start = jnp.arange(rows_out // ts, dtype=jnp.int32) * ts
        tile_cls = jnp.minimum(jnp.sum(tile_start[:, None] >= ends[None, :], axis=1), N_CLASSES - 1)
        pairs = [(a, b) for a in range(EXPERTS_PER_GROUP) for b in range(a + 1, EXPERTS_PER_GROUP)]
        first_of_cls = np.array([g * EXPERTS_PER_GROUP + a for g in range(N_GROUPS) for a, _ in pairs], np.int32)
        second_of_cls = np.array([g * EXPERTS_PER_GROUP + b for g in range(N_GROUPS) for _, b in pairs], np.int32)
        tile_ea = jnp.asarray(first_of_cls)[tile_cls]
        tile_eb = jnp.asarray(second_of_cls)[tile_cls]
        n_used = (ends[-1:] // ts).astype(jnp.int32)

        assert l == depth - 1, "the fused residual + final norm epilogue expects a single layer"
        xs = _dispatch(x1, cls, rank, base, fill, rows_out)
        ys = _moe(xs, tile_ea, tile_eb, n_used, g_ffn, w_gate[l].astype(bf16),
                  w_up[l].astype(bf16), w_down[l].astype(bf16), final_norm_g.reshape(1, D), ts)
        out = _collect(ys, cls, rank, base, T)
        x2 = out
    return out.reshape(B, S, D)
```

```python
import functools
import math

import numpy as np
import jax
import jax.numpy as jnp
from jax import lax
from jax.experimental import pallas as pl
from jax.experimental.pallas import tpu as pltpu

HEAD_DIM = 128
N_FOX_HEADS = 8
N_MOBA_HEADS = 8
MOBA_BLOCK = 256
MOBA_TOPK = 3
REL_BUCKETS = 32
REL_MAX_DIST = 128
N_GROUPS = 4
EXPERTS_PER_GROUP = 4
N_EXPERTS = N_GROUPS * EXPERTS_PER_GROUP
EXPERT_PAIRS = EXPERTS_PER_GROUP * (EXPERTS_PER_GROUP - 1) // 2
N_CLASSES = N_GROUPS * EXPERT_PAIRS
EPS = 1e-6
NEG = -1e30
LOG2E = math.log2(math.e)
LANES = 128
SUBLANES = 8
ROUTER_COLS = LANES
OUT_PROJ_CHUNKS = 4
MOE_TILE = 256
KEY_BIAS_PARTS = 3
ZERO_ROWS = 64
IN_PROJ_COLS = 2048
ROW_COPY_TILE = 1024
DMA_UNROLL = 8
VMEM_LIMIT = 56 * 1024 * 1024

f32 = jnp.float32
bf16 = jnp.bfloat16


def _params(*sem):
    return pltpu.CompilerParams(dimension_semantics=sem, vmem_limit_bytes=VMEM_LIMIT)


def _pick(n, pref):
    t = min(n, pref)
    assert n % t == 0, (n, pref)
    return t


def _in_proj_kernel(cs_ref, x_ref, g_ref, w_ref, wfg_ref, qkv_ref, f_ref, h_scr, *, group_slabs):
    @pl.when(pl.program_id(1) == 0)
    def _():
        x = x_ref[...]
        y = x * lax.rsqrt(jnp.mean(x * x, axis=-1, keepdims=True) + EPS)
        hb = (y * g_ref[...]).astype(bf16)
        h_scr[...] = hb
        f_ref[...] = lax.dot_general(wfg_ref[...], hb, (((1,), (1,)), ((), ())),
                                     preferred_element_type=f32)

    acc = lax.dot_general(h_scr[...], w_ref[...], (((1,), (1,)), ((), ())), preferred_element_type=f32)
    slabs = acc.shape[1] // HEAD_DIM
    first_group = pl.program_id(1) * (slabs // group_slabs)
    for c in range(slabs):
        scale = cs_ref[first_group + c // group_slabs]
        qkv_ref[c] = (acc[:, c * HEAD_DIM:(c + 1) * HEAD_DIM] * scale).astype(bf16)


def _in_proj(x2, g, w_in_t, layer, w_fg_t, col_scale, group_cols):
    T, D = x2.shape
    N = col_scale.shape[0] * group_cols
    tm = _pick(T, 1024)
    tn = _pick(N, IN_PROJ_COLS)
    assert N <= w_in_t.shape[1] and tn % group_cols == 0
    nh = w_fg_t.shape[0]
    kern = functools.partial(_in_proj_kernel, group_slabs=group_cols // HEAD_DIM)
    return pl.pallas_call(
        kern,
        grid=(T // tm, N // tn),
        in_specs=[
            pl.BlockSpec(memory_space=pltpu.SMEM),
            pl.BlockSpec((tm, D), lambda i, j: (i, 0)),
            pl.BlockSpec((1, D), lambda i, j: (0, 0)),
            pl.BlockSpec((None, tn, D), lambda i, j: (layer, j, 0)),
            pl.BlockSpec((nh, D), lambda i, j: (0, 0)),
        ],
        out_specs=[
            pl.BlockSpec((tn // HEAD_DIM, tm, HEAD_DIM), lambda i, j: (j, i, 0)),
            pl.BlockSpec((nh, tm), lambda i, j: (0, i)),
        ],
        out_shape=[
            jax.ShapeDtypeStruct((N // HEAD_DIM, T, HEAD_DIM), bf16),
            jax.ShapeDtypeStruct((nh, T), f32),
        ],
        scratch_shapes=[pltpu.VMEM((tm, D), bf16)],
        compiler_params=_params("arbitrary", "arbitrary"),
        name="in_proj",
    )(col_scale, x2, g, w_in_t, w_fg_t)


def _fox_decay_kernel(f_ref, b_ref, o_ref):
    lf = jax.nn.log_sigmoid(f_ref[...] + b_ref[...])
    S = lf.shape[1]
    lane = lax.broadcasted_iota(jnp.int32, lf.shape, 1)
    c = lf
    sh = 1
    while sh < S:
        c = c + jnp.where(lane >= sh, pltpu.roll(c, sh, axis=1), 0.0)
        sh *= 2
    rest = -c * LOG2E
    parts = []
    for _ in range(KEY_BIAS_PARTS):
        part = rest.astype(bf16).astype(f32)
        parts.append(part)
        rest = rest - part
    nh = lf.shape[0]
    cols = jnp.concatenate(parts + [jnp.zeros((LANES - KEY_BIAS_PARTS * nh, S), f32)], axis=0)
    o_ref[0] = jnp.transpose(cols).astype(bf16)


def _fox_decay(f_t, b_forget, B, S):
    nh = f_t.shape[0]
    assert KEY_BIAS_PARTS * nh <= LANES
    return pl.pallas_call(
        _fox_decay_kernel,
        grid=(B,),
        in_specs=[
            pl.BlockSpec((nh, S), lambda b: (0, b)),
            pl.BlockSpec((nh, 1), lambda b: (0, 0)),
        ],
        out_specs=pl.BlockSpec((1, S, LANES), lambda b: (b, 0, 0)),
        out_shape=jax.ShapeDtypeStruct((B, S, LANES), bf16),
        compiler_params=_params("arbitrary"),
        name="fox_decay",
    )(f_t, b_forget)


def _qk(q, k):
    return lax.dot_general(q, k, (((1,), (1,)), ((), ())), preferred_element_type=f32)


def _softmax_init(t):
    return jnp.full((t, 1), NEG, f32), jnp.zeros((t, 2 * HEAD_DIM), f32)


def _softmax_step2(carry, s2, v_ones):
    m, acc = carry
    m_new = jnp.maximum(m, jnp.max(s2, axis=-1, keepdims=True))
    p = jnp.exp2(s2 - m_new).astype(bf16)
    acc = jnp.exp2(m - m_new) * acc + jnp.dot(p, v_ones, preferred_element_type=f32)
    return m_new, acc


def _softmax_result(carry):
    _, acc = carry
    return acc[:, :HEAD_DIM] / acc[:, HEAD_DIM:]


def _fox_attn_kernel(q_ref, k_ref, v_ref, kb_ref, o_ref, qaug_scr, kaug_scr, vo_scr, *, t):
    h = pl.program_id(1)
    S = q_ref.shape[1]

    @pl.when(h == 0)
    def _():
        kaug_scr[:, HEAD_DIM:] = kb_ref[0]
        vo_scr[:, HEAD_DIM:] = jnp.ones((S, HEAD_DIM), vo_scr.dtype)

    lane = lax.broadcasted_iota(jnp.int32, (1, LANES), 1)
    mine = jnp.logical_and(lane % N_FOX_HEADS == h, lane < KEY_BIAS_PARTS * N_FOX_HEADS)
    qaug_scr[:, :HEAD_DIM] = q_ref[0]
    qaug_scr[:, HEAD_DIM:] = jnp.broadcast_to(jnp.where(mine, 1.0, 0.0), (S, LANES)).astype(bf16)
    kaug_scr[:, :HEAD_DIM] = k_ref[0]
    vo_scr[:, :HEAD_DIM] = v_ref[0]
    row = lax.broadcasted_iota(jnp.int32, (t, t), 0)
    col = lax.broadcasted_iota(jnp.int32, (t, t), 1)
    for qi in range(S // t):
        qa = qaug_scr[qi * t:(qi + 1) * t, :]
        carry = _softmax_init(t)
        for kb in range(qi + 1):
            keys = slice(kb * t, (kb + 1) * t)
            s2 = _qk(qa, kaug_scr[keys, :])
            if kb == qi:
                s2 = jnp.where(col <= row, s2, NEG)
            carry = _softmax_step2(carry, s2, vo_scr[keys, :])
        o_ref[qi * t:(qi + 1) * t, :] = _softmax_result(carry).astype(o_ref.dtype)


def _fox_attn(qkv, key_bias, B, S, slab0):
    T = B * S
    H = N_FOX_HEADS
    t = _pick(S, 512)
    kern = functools.partial(_fox_attn_kernel, t=t)
    return pl.pallas_call(
        kern,
        grid=(B, H),
        in_specs=[
            pl.BlockSpec((1, S, HEAD_DIM), lambda b, h: (slab0 + h, b, 0)),
            pl.BlockSpec((1, S, HEAD_DIM), lambda b, h: (slab0 + H + h, b, 0)),
            pl.BlockSpec((1, S, HEAD_DIM), lambda b, h: (slab0 + 2 * H + h, b, 0)),
            pl.BlockSpec((1, S, LANES), lambda b, h: (b, 0, 0)),
        ],
        out_specs=pl.BlockSpec((S, HEAD_DIM), lambda b, h: (b, h)),
        out_shape=jax.ShapeDtypeStruct((T, H * HEAD_DIM), bf16),
        scratch_shapes=[pltpu.VMEM((S, 2 * HEAD_DIM), bf16)] * 3,
        compiler_params=_params("arbitrary", "arbitrary"),
        name="fox_attn",
    )(qkv, qkv, qkv, key_bias)


def _rel_bucket_table(n):
    max_exact = REL_BUCKETS // 2
    d = np.arange(n)
    ratio = np.log(np.maximum(d, 1).astype(np.float32) / np.float32(max_exact)) / np.float32(
        math.log(REL_MAX_DIST / max_exact))
    large = max_exact + (ratio * np.float32(REL_BUCKETS - max_exact)).astype(np.int32)
    large = np.minimum(large, REL_BUCKETS - 1)
    return np.where(d < max_exact, d, large).astype(np.int32)


def _moba_attn_kernel(rel_ref, q_ref, k_ref, v_ref, avg_ref, hot_ref, bk_ref, o_ref, bias_scr, kaug_scr,
                      qaug_scr, vo_scr):
    h = pl.program_id(0)
    blk = MOBA_BLOCK
    t = 2 * blk
    S = q_ref.shape[1]

    @pl.when(pl.program_id(1) == 0)
    def _():
        far_bias = rel_ref[h, REL_BUCKETS - 1]
        row = lax.broadcasted_iota(jnp.int32, (blk, blk), 0)
        col = lax.broadcasted_iota(jnp.int32, (blk, blk), 1)
        tiles = []
        for which in range(2):
            bk = bk_ref[which]
            tile = jnp.zeros((blk, blk), f32)
            for b in range(REL_BUCKETS - 1):
                tile = jnp.where(bk == b, (rel_ref[h, b] - far_bias) * LOG2E, tile)
            tiles.append(jnp.where(col <= row, tile, NEG) if which == 0 else tile)
        own_tile, prev_tile = tiles
        zero_tile = jnp.zeros((blk, blk), f32)
        for which, quads in enumerate((((zero_tile, prev_tile), (zero_tile, zero_tile)),
                                       ((own_tile, zero_tile), (prev_tile, own_tile)))):
            for r in range(2):
                for c in range(2):
                    bias_scr[which, r * blk:(r + 1) * blk, c * blk:(c + 1) * blk] = quads[r][c]
        kaug_scr[:, HEAD_DIM:] = hot_ref[...]
        vo_scr[:, HEAD_DIM:] = jnp.ones((S, HEAD_DIM), vo_scr.dtype)

    kaug_scr[:, :HEAD_DIM] = k_ref[0]
    vo_scr[:, :HEAD_DIM] = v_ref[0]
    kmean = jnp.dot(avg_ref[...], k_ref[0], preferred_element_type=f32).astype(bf16)

    nbp = avg_ref.shape[0]
    block = lax.broadcasted_iota(jnp.int32, (nbp, S), 0)
    own = lax.broadcasted_iota(jnp.int32, (nbp, S), 1) // blk
    valid = block < own
    gate = jnp.where(valid, _qk(kmean, q_ref[0]), NEG)
    sel = block == own
    for _ in range(MOBA_TOPK):
        mx = jnp.max(gate, axis=0, keepdims=True)
        first = jnp.min(jnp.where(gate == mx, block, nbp), axis=0, keepdims=True)
        pick = block == first
        sel = jnp.logical_or(sel, jnp.logical_and(pick, valid))
        gate = jnp.where(pick, -jnp.inf, gate)
    chosen = jnp.concatenate([jnp.where(sel, 0.0, NEG), jnp.full((LANES - nbp, S), NEG, f32)], axis=0)
    qaug_scr[:, :HEAD_DIM] = q_ref[0]
    qaug_scr[:, HEAD_DIM:] = jnp.transpose(chosen).astype(bf16)

    for j in range(S // t):
        rows = slice(j * t, (j + 1) * t)
        qa = qaug_scr[rows, :]
        carry = _softmax_step2(_softmax_init(t), _qk(qa, kaug_scr[rows, :]) + bias_scr[1], vo_scr[rows, :])
        if j > 0:
            keys = slice((j - 1) * t, j * t)
            carry = _softmax_step2(carry, _qk(qa, kaug_scr[keys, :]) + bias_scr[0], vo_scr[keys, :])
        for c in range(j - 1):
            keys = slice(c * t, (c + 1) * t)
            carry = _softmax_step2(carry, _qk(qa, kaug_scr[keys, :]), vo_scr[keys, :])
        o_ref[rows, :] = _softmax_result(carry).astype(o_ref.dtype)


def _moba_attn(qkv, rel_bias, B, S, slab0):
    T = B * S
    H = N_MOBA_HEADS
    blk = MOBA_BLOCK
    assert S % (2 * blk) == 0 and S // blk <= LANES
    nb = S // blk
    nbp = -(-nb // SUBLANES) * SUBLANES
    avg = np.zeros((nbp, S), np.float32)
    hot = np.zeros((S, LANES), np.float32)
    for n in range(nb):
        avg[n, n * blk:(n + 1) * blk] = 1.0 / blk
        hot[n * blk:(n + 1) * blk, n] = 1.0
    table = _rel_bucket_table(2 * blk)
    dist = np.arange(blk)[:, None] - np.arange(blk)[None, :]
    buckets = np.stack([table[np.maximum(dist, 0)], table[dist + blk]]).astype(np.int32)
    grid_spec = pltpu.PrefetchScalarGridSpec(
        num_scalar_prefetch=1,
        grid=(H, B),
        in_specs=[
            pl.BlockSpec((1, S, HEAD_DIM), lambda h, b, r: (slab0 + h, b, 0)),
            pl.BlockSpec((1, S, HEAD_DIM), lambda h, b, r: (slab0 + H + h, b, 0)),
            pl.BlockSpec((1, S, HEAD_DIM), lambda h, b, r: (slab0 + 2 * H + h, b, 0)),
            pl.BlockSpec((nbp, S), lambda h, b, r: (0, 0)),
            pl.BlockSpec((S, LANES), lambda h, b, r: (0, 0)),
            pl.BlockSpec((2, blk, blk), lambda h, b, r: (0, 0, 0)),
        ],
        out_specs=pl.BlockSpec((S, HEAD_DIM), lambda h, b, r: (b, h)),
        scratch_shapes=[pltpu.VMEM((2, 2 * blk, 2 * blk), f32)] + [pltpu.VMEM((S, 2 * HEAD_DIM), bf16)] * 3,
    )
    return pl.pallas_call(
        _moba_attn_kernel,
        grid_spec=grid_spec,
        out_shape=jax.ShapeDtypeStruct((T, H * HEAD_DIM), bf16),
        compiler_params=_params("arbitrary", "arbitrary"),
        name="moba_attn",
    )(rel_bias, qkv, qkv, qkv, jnp.asarray(avg, bf16), jnp.asarray(hot, bf16), jnp.asarray(buckets))


def _rms(x, g):
    return x * lax.rsqrt(jnp.mean(x * x, axis=-1, keepdims=True) + EPS) * g


def _first_max(x, lane):
    mx = jnp.max(x, axis=-1, keepdims=True)
    return mx, jnp.min(jnp.where(x == mx, lane, LANES), axis=-1, keepdims=True)


def _route(logits, lane, grp=None):
    gl = jnp.where(lane < N_GROUPS, logits, -jnp.inf)
    gmax, top = _first_max(gl, lane)
    grp = top if grp is None else grp
    pg = jnp.exp(gl - gmax)
    p_top_group = jnp.sum(jnp.where(lane == grp, pg, 0.0), axis=-1, keepdims=True) / jnp.sum(
        pg, axis=-1, keepdims=True)
    lo = N_GROUPS + grp * EXPERTS_PER_GROUP
    in_grp = jnp.logical_and(lane >= lo, lane < lo + EXPERTS_PER_GROUP)
    el = jnp.where(in_grp, logits, -jnp.inf)
    pe = jnp.exp(el - jnp.max(el, axis=-1, keepdims=True))
    p_exp = jnp.where(in_grp, pe / jnp.sum(pe, axis=-1, keepdims=True), -1.0)
    p1, i1 = _first_max(p_exp, lane)
    p2, i2 = _first_max(jnp.where(lane == i1, -1.0, p_exp), lane)
    tot = p1 + p2
    combine = jnp.where(lane == i1, p1 / tot, jnp.where(lane == i2, p2 / tot, 0.0)) * p_top_group
    return grp, jnp.minimum(i1, i2) - lo, jnp.maximum(i1, i2) - lo, combine


def _out_proj_kernel(fo_ref, mo_ref, x_ref, gf_ref, gm_ref, w_ref, gffn_ref, wr_ref, br_ref,
                     x1_ref, route_ref, count_ref, count_scr, x1_scr):
    i = pl.program_id(0)

    @pl.when(i == 0)
    def _():
        count_scr[...] = jnp.zeros_like(count_scr)
        x1_scr[1] = jnp.zeros(x1_scr.shape[1:], x1_scr.dtype)

    tm, D = x_ref.shape
    chunk = D // OUT_PROJ_CHUNKS

    fo = _rms(fo_ref[...].astype(f32), gf_ref[...]).astype(bf16)
    mo = _rms(mo_ref[...].astype(f32), gm_ref[...]).astype(bf16)
    mix = jnp.concatenate([fo, mo], axis=-1)

    def project(c):
        cols = slice(c * chunk, (c + 1) * chunk)
        x1_scr[i % 2, :, cols] = x_ref[:, cols] + jnp.dot(mix, w_ref[:, cols], preferred_element_type=f32)

    x1 = x1_scr[(i + 1) % 2]
    h2 = _rms(x1, gffn_ref[...]).astype(bf16)
    logits = jnp.dot(h2, wr_ref[...], preferred_element_type=f32) + br_ref[...]
    project(0)
    lane = lax.broadcasted_iota(jnp.int32, logits.shape, 1)
    grp, ea, eb, combine = _route(logits, lane)
    x1_ref[:, :D] = x1
    x1_ref[:, D:] = combine
    project(1)
    pair = (ea * (2 * EXPERTS_PER_GROUP - 1 - ea)) // 2 + eb - ea - 1
    cls = grp * EXPERT_PAIRS + pair
    onehot = jnp.where(lane == cls, 1.0, 0.0)
    earlier = lax.broadcasted_iota(jnp.int32, (tm, tm), 1) < lax.broadcasted_iota(jnp.int32, (tm, tm), 0)
    counts = count_scr[...]
    before = jnp.dot(jnp.where(earlier, 1.0, 0.0).astype(bf16), onehot.astype(bf16),
                     preferred_element_type=f32) + counts
    rank = jnp.sum(onehot * before, axis=-1, keepdims=True).astype(jnp.int32)
    route_ref[...] = jnp.where(lane == 0, cls, jnp.where(lane == 1, rank, 0))
    counts = counts + jnp.where(i > 0, jnp.sum(onehot, axis=0, keepdims=True), 0.0)
    count_scr[...] = counts
    count_ref[...] = counts.astype(jnp.int32)
    for c in range(2, OUT_PROJ_CHUNKS):
        project(c)


def _out_proj(fo, mo, x2, gf, gm, w_out, g_ffn, w_r, b_r):
    T, D = x2.shape
    Wf = fo.shape[1]
    Wm = mo.shape[1]
    tm = _pick(T, 512)
    n = T // tm
    const = lambda i: (0, 0)
    rows_in = lambda i: (jnp.minimum(i, n - 1), 0)
    rows = lambda i: (jnp.maximum(i - 1, 0), 0)
    return pl.pallas_call(
        _out_proj_kernel,
        grid=(n + 1,),
        in_specs=[
            pl.BlockSpec((tm, Wf), rows_in),
            pl.BlockSpec((tm, Wm), rows_in),
            pl.BlockSpec((tm, D), rows_in),
            pl.BlockSpec((1, Wf), const),
            pl.BlockSpec((1, Wm), const),
            pl.BlockSpec((Wf + Wm, D), const),
            pl.BlockSpec((1, D), const),
            pl.BlockSpec((D, ROUTER_COLS), const),
            pl.BlockSpec((1, ROUTER_COLS), const),
        ],
        out_specs=[
            pl.BlockSpec((tm, D + ROUTER_COLS), rows),
            pl.BlockSpec((tm, ROUTER_COLS), rows),
            pl.BlockSpec((1, ROUTER_COLS), const),
        ],
        out_shape=[
            jax.ShapeDtypeStruct((T, D + ROUTER_COLS), f32),
            jax.ShapeDtypeStruct((T, ROUTER_COLS), jnp.int32),
            jax.ShapeDtypeStruct((1, ROUTER_COLS), jnp.int32),
        ],
        scratch_shapes=[pltpu.VMEM((1, ROUTER_COLS), f32), pltpu.VMEM((2, tm, D), f32)],
        compiler_params=_params("arbitrary"),
        name="out_proj",
    )(fo, mo, x2, gf, gm, w_out, g_ffn, w_r, b_r)


def _start_rows(row_copy, n):
    assert n % DMA_UNROLL == 0

    def body(g, carry):
        for k in range(DMA_UNROLL):
            row_copy(g * DMA_UNROLL + k).start()
        return carry

    lax.fori_loop(0, n // DMA_UNROLL, body, 0)


def _slot(cls_ref, rank_ref, base_ref, t):
    return base_ref[cls_ref[t]] + rank_ref[t]


def _dispatch_kernel(cls_ref, rank_ref, base_ref, fill_ref, x_ref, xs_ref, zero_scr, sem, zsem):
    i = pl.program_id(0)
    tm = x_ref.shape[0]

    zrows = zero_scr.shape[0]

    def row_copy(r):
        d = _slot(cls_ref, rank_ref, base_ref, i * tm + r)
        return pltpu.make_async_copy(x_ref.at[pl.ds(r, 1)], xs_ref.at[pl.ds(d, 1)], sem)

    def zero_row(r):
        return pltpu.make_async_copy(zero_scr.at[pl.ds(0, 1)], xs_ref.at[pl.ds(r, 1)], zsem)

    def zero_sublanes(c):
        dst = xs_ref.at[pl.ds(pl.multiple_of(c * SUBLANES, SUBLANES), SUBLANES)]
        return pltpu.make_async_copy(zero_scr.at[pl.ds(0, SUBLANES)], dst, zsem)

    def zero_chunk(c):
        return pltpu.make_async_copy(zero_scr, xs_ref.at[pl.ds(pl.multiple_of(c * zrows, zrows), zrows)], zsem)

    def start(copy):
        def body(r, carry):
            copy(r).start()
            return carry
        return body

    def wait(copy):
        def body(r, carry):
            copy(r).wait()
            return carry
        return body

    _start_rows(row_copy, tm)

    def zero_fill(act):
        def fill_class(c, carry):
            lo, hi = fill_ref[2 * c], fill_ref[2 * c + 1]
            mid = jnp.minimum((lo + SUBLANES - 1) // SUBLANES * SUBLANES, hi)
            lax.fori_loop(lo, mid, act(zero_row), 0)
            lax.fori_loop(mid // SUBLANES, hi // SUBLANES, act(zero_sublanes), 0)
            return carry

        lax.fori_loop(0, N_CLASSES, fill_class, 0)
        lo, hi = fill_ref[2 * N_CLASSES] // zrows, fill_ref[2 * N_CLASSES + 1] // zrows
        lax.fori_loop(lo, hi, act(zero_chunk), 0)

    @pl.when(i == 0)
    def _():
        zero_scr[...] = jnp.zeros_like(zero_scr)
        zero_fill(start)

    pltpu.make_async_copy(x_ref, xs_ref.at[pl.ds(0, tm)], sem).wait()

    @pl.when(i == 0)
    def _():
        zero_fill(wait)


def _dispatch(x1, cls, rank, base, fill, rows_out):
    T, D = x1.shape
    tm = _pick(T, ROW_COPY_TILE)
    grid_spec = pltpu.PrefetchScalarGridSpec(
        num_scalar_prefetch=4,
        grid=(T // tm,),
        in_specs=[pl.BlockSpec((tm, D), lambda i, *_: (i, 0))],
        out_specs=pl.BlockSpec(memory_space=pl.ANY),
        scratch_shapes=[pltpu.VMEM((ZERO_ROWS, D), f32), pltpu.SemaphoreType.DMA, pltpu.SemaphoreType.DMA],
    )
    return pl.pallas_call(
        _dispatch_kernel,
        grid_spec=grid_spec,
        out_shape=jax.ShapeDtypeStruct((rows_out, D), f32),
        compiler_params=_params("arbitrary"),
        name="moe_dispatch",
    )(cls, rank, base, fill, x1)


def _collect_kernel(cls_ref, rank_ref, base_ref, ys_ref, o_ref, sem):
    i = pl.program_id(0)
    tm = o_ref.shape[0]

    def row_copy(r):
        d = _slot(cls_ref, rank_ref, base_ref, i * tm + r)
        return pltpu.make_async_copy(ys_ref.at[pl.ds(d, 1)], o_ref.at[pl.ds(r, 1)], sem)

    _start_rows(row_copy, tm)
    pltpu.make_async_copy(ys_ref.at[pl.ds(0, tm)], o_ref, sem).wait()


def _collect(ys, cls, rank, base, T):
    D = ys.shape[1]
    tm = _pick(T, ROW_COPY_TILE)
    grid_spec = pltpu.PrefetchScalarGridSpec(
        num_scalar_prefetch=3,
        grid=(T // tm,),
        in_specs=[pl.BlockSpec(memory_space=pl.ANY)],
        out_specs=pl.BlockSpec((tm, D), lambda i, *_: (i, 0)),
        scratch_shapes=[pltpu.SemaphoreType.DMA],
    )
    return pl.pallas_call(
        _collect_kernel,
        grid_spec=grid_spec,
        out_shape=jax.ShapeDtypeStruct((T, D), f32),
        compiler_params=_params("arbitrary"),
        name="moe_collect",
    )(cls, rank, base, ys)


def _moe_kernel(ea_ref, eb_ref, nu_ref, xs_ref, gffn_ref, wga_ref, wua_ref, wda_ref,
                wgb_ref, wub_ref, wdb_ref, gfin_ref, o_ref):
    i = pl.program_id(0)
    D = o_ref.shape[1]

    @pl.when(i < nu_ref[0])
    def _():
        x = xs_ref[:, :D]
        combine = xs_ref[:, D:]
        h = _rms(x, gffn_ref[...]).astype(bf16)
        lane = lax.broadcasted_iota(jnp.int32, combine.shape, 1)
        y = jnp.zeros(x.shape, f32)
        for e_ref, wg_ref, wu_ref, wd_ref in ((ea_ref, wga_ref, wua_ref, wda_ref),
                                               (eb_ref, wgb_ref, wub_ref, wdb_ref)):
            c = jnp.sum(jnp.where(lane == N_GROUPS + e_ref[i], combine, 0.0), axis=-1, keepdims=True)
            gate = jnp.dot(h, wg_ref[0], preferred_element_type=f32)
            up = jnp.dot(h, wu_ref[0], preferred_element_type=f32)
            hid = (jax.nn.silu(gate) * up * c).astype(bf16)
            y = y + jnp.dot(hid, wd_ref[0], preferred_element_type=f32)
        o_ref[...] = _rms(x + y, gfin_ref[...])

    @pl.when(i >= nu_ref[0])
    def _():
        o_ref[...] = jnp.zeros_like(o_ref)


def _moe(xs, tile_ea, tile_eb, n_used, g_ffn, w_gate, w_up, w_down, g_final, ts):
    R = xs.shape[0]
    E, D, F = w_gate.shape
    const = lambda i, ea, eb, nu: (0, 0)
    rows = lambda i, ea, eb, nu: (jnp.minimum(i, nu[0] - 1), 0)
    first = lambda i, ea, eb, nu: (ea[i], 0, 0)
    second = lambda i, ea, eb, nu: (eb[i], 0, 0)
    grid_spec = pltpu.PrefetchScalarGridSpec(
        num_scalar_prefetch=3,
        grid=(R // ts,),
        in_specs=[
            pl.BlockSpec((ts, D + ROUTER_COLS), rows),
            pl.BlockSpec((1, D), const),
            pl.BlockSpec((1, D, F), first),
            pl.BlockSpec((1, D, F), first),
            pl.BlockSpec((1, F, D), first),
            pl.BlockSpec((1, D, F), second),
            pl.BlockSpec((1, D, F), second),
            pl.BlockSpec((1, F, D), second),
            pl.BlockSpec((1, D), const),
        ],
        out_specs=pl.BlockSpec((ts, D), lambda i, ea, eb, nu: (i, 0)),
    )
    return pl.pallas_call(
        _moe_kernel,
        grid_spec=grid_spec,
        out_shape=jax.ShapeDtypeStruct((R, D), f32),
        compiler_params=_params("arbitrary"),
        name="moe",
    )(tile_ea, tile_eb, n_used, xs, g_ffn, w_gate, w_up, w_down, w_gate, w_up, w_down, g_final)


def kernel(x, attn_norm_g, w_in, b_forget, fox_out_norm_g, moba_out_norm_g, rel_bias, w_out, ffn_norm_g,
           w_group_router, b_group_router, w_expert_router, b_expert_router, w_gate, w_up, w_down,
           final_norm_g):
    B, S, D = x.shape
    T = B * S
    depth = w_in.shape[0]
    fox_w = N_FOX_HEADS * HEAD_DIM
    moba_w = N_MOBA_HEADS * HEAD_DIM
    qkv_w = 3 * (fox_w + moba_w)
    assert w_in.shape[2] == qkv_w + N_FOX_HEADS
    assert N_GROUPS + N_EXPERTS <= ROUTER_COLS

    x2 = x.reshape(T, D)
    out = None
    for l in range(depth):
        w_in_t = jnp.swapaxes(w_in, 1, 2).astype(bf16)
        w_fg_t = w_in_t[l, qkv_w:, :]
        assert fox_w == moba_w
        col_scale = np.ones((qkv_w // fox_w,), np.float32)
        col_scale[[0, 3]] = HEAD_DIM ** -0.5 * LOG2E
        qkv, f_t = _in_proj(x2, attn_norm_g[l].reshape(1, D), w_in_t, l, w_fg_t, jnp.asarray(col_scale),
                            fox_w)
        key_bias = _fox_decay(f_t, b_forget[l].reshape(N_FOX_HEADS, 1), B, S)
        fo = _fox_attn(qkv, key_bias, B, S, 0)
        mo = _moba_attn(qkv, rel_bias, B, S, 3 * N_FOX_HEADS)

        w_r = jnp.concatenate([w_group_router[l], w_expert_router[l].reshape(D, N_EXPERTS)], axis=1)
        w_r = jnp.pad(w_r, ((0, 0), (0, ROUTER_COLS - w_r.shape[1]))).astype(bf16)
        b_r = jnp.concatenate([b_group_router[l], b_expert_router[l].reshape(N_EXPERTS)])
        b_r = jnp.pad(b_r, (0, ROUTER_COLS - b_r.shape[0])).reshape(1, ROUTER_COLS)
        g_ffn = ffn_norm_g[l].reshape(1, D)
        x1, route, count = _out_proj(fo, mo, x2, fox_out_norm_g[l].reshape(1, fox_w),
                                     moba_out_norm_g[l].reshape(1, moba_w), w_out[l].astype(bf16),
                                     g_ffn, w_r, b_r)
        ts = MOE_TILE
        cls, rank = route[:, 0], route[:, 1]
        counts = count[0, :N_CLASSES]
        padded = (counts + ts - 1) // ts * ts
        ends = jnp.cumsum(padded)
        base = ends - padded
        rows_out = (T // ts + N_CLASSES) * ts
        fill = jnp.concatenate([jnp.stack([base + counts, ends], axis=1).reshape(-1),
                                jnp.stack([ends[-1], jnp.asarray(rows_out, ends.dtype)])])
        tile_start = jnp.arange(rows_out // ts, dtype=jnp.int32) * ts
        tile_cls = jnp.minimum(jnp.sum(tile_start[:, None] >= ends[None, :], axis=1), N_CLASSES - 1)
        pairs = [(a, b) for a in range(EXPERTS_PER_GROUP) for b in range(a + 1, EXPERTS_PER_GROUP)]
        first_of_cls = np.array([g * EXPERTS_PER_GROUP + a for g in range(N_GROUPS) for a, _ in pairs], np.int32)
        second_of_cls = np.array([g * EXPERTS_PER_GROUP + b for g in range(N_GROUPS) for _, b in pairs], np.int32)
        tile_ea = jnp.asarray(first_of_cls)[tile_cls]
        tile_eb = jnp.asarray(second_of_cls)[tile_cls]
        n_used = (ends[-1:] // ts).astype(jnp.int32)

        assert l == depth - 1, "the fused residual + final norm epilogue expects a single layer"
        xs = _dispatch(x1, cls, rank, base, fill, rows_out)
        ys = _moe(xs, tile_ea, tile_eb, n_used, g_ffn, w_gate[l].astype(bf16),
                  w_up[l].astype(bf16), w_down[l].astype(bf16), final_norm_g.reshape(1, D), ts)
        out = _collect(ys, cls, rank, base, T)
        x2 = out
    return out.reshape(B, S, D)
```
